```python
import functools
import jax, jax.numpy as jnp
from jax import lax
import numpy as np

D_MODEL = 1024
BATCH = 1
SEQ = 16384
DEPTH = 2
DEC_BATCH = 128
DEC_SEQ = 1
PAST_LEN = 16384
PAGE_SIZE = 128

ATT_HEADS = 8
ATT_KV_HEADS = 2
ATT_HEAD_DIM = 64
ATT_GROUP = ATT_HEADS // ATT_KV_HEADS
ATT_WIDTH = ATT_HEADS * ATT_HEAD_DIM
ATT_KV_WIDTH = ATT_KV_HEADS * ATT_HEAD_DIM
WINDOW = 128
RET_HEADS = 4
RET_DK = 128
RET_DV = 256
RET_QK_WIDTH = RET_HEADS * RET_DK
RET_WIDTH = RET_HEADS * RET_DV
RET_CHUNK = 128
ROPE_THETA = 10000.0
ALPHA = (2.0 * DEPTH) ** 0.25
BETA = (8.0 * DEPTH) ** -0.25
LN_EPS = 1e-5
RMS_EPS = 1e-6
SPLITS = (ATT_WIDTH, ATT_KV_WIDTH, ATT_KV_WIDTH, ATT_WIDTH,
          RET_QK_WIDTH, RET_QK_WIDTH, RET_WIDTH, RET_WIDTH,
          D_MODEL, D_MODEL)
IN_COLS = sum(SPLITS)

kernel_name = "hybrid_swa_sink_retention_adaln_deepnorm_step"


def rope(x, pos):
    half = x.shape[-1] // 2
    freqs = ROPE_THETA ** (-jnp.arange(half, dtype=jnp.float32) / half)
    ang = pos.astype(jnp.float32)[:, None] * freqs[None, :]
    cos = jnp.cos(ang)[:, None, :]
    sin = jnp.sin(ang)[:, None, :]
    xf = x.astype(jnp.float32)
    x1, x2 = xf[..., :half], xf[..., half:]
    return jnp.concatenate([x1 * cos - x2 * sin, x2 * cos + x1 * sin], axis=-1).astype(x.dtype)


def layer_norm(x, g, b):
    xf = x.astype(jnp.float32)
    mu = xf.mean(-1, keepdims=True)
    var = jnp.square(xf - mu).mean(-1, keepdims=True)
    y = (xf - mu) * lax.rsqrt(var + LN_EPS) * g.astype(jnp.float32) + b.astype(jnp.float32)
    return y.astype(x.dtype)


def sink_attention(q, k, v, qpos, kpos, sinks):
    logits = jnp.einsum('...qhgd,...khd->...hgqk', q.astype(jnp.float32), k.astype(jnp.float32))
    logits = logits * (ATT_HEAD_DIM ** -0.5)
    delta = qpos[..., :, None] - kpos[..., None, :]
    allowed = (delta >= 0) & (delta <= WINDOW) & (kpos[..., None, :] >= 0)
    logits = jnp.where(allowed[..., None, None, :, :], logits, -jnp.inf)
    sink = sinks.astype(jnp.float32).reshape(ATT_KV_HEADS, ATT_GROUP, 1, 1)
    m = jnp.maximum(logits.max(-1, keepdims=True), sink)
    p = jnp.exp(logits - m)
    denom = p.sum(-1, keepdims=True) + jnp.exp(sink - m)
    o = jnp.einsum('...hgqk,...khd->...qhgd', p / denom, v.astype(jnp.float32))
    return o.astype(q.dtype)


def window_attention_prompt(q, k, v, sinks):
    B, L, H, d = q.shape
    nb = L // WINDOW
    qb = q.reshape(B, nb, WINDOW, ATT_KV_HEADS, ATT_GROUP, d)
    kb = k.reshape(B, nb, WINDOW, ATT_KV_HEADS, d)
    vb = v.reshape(B, nb, WINDOW, ATT_KV_HEADS, d)

    def with_prev(t):
        prev = jnp.concatenate([jnp.zeros_like(t[:, :1]), t[:, :-1]], axis=1)
        return jnp.concatenate([prev, t], axis=2)

    pos = jnp.arange(L).reshape(nb, WINDOW)
    kpos = jnp.concatenate([pos - WINDOW, pos], axis=1)
    o = sink_attention(qb, with_prev(kb), with_prev(vb), pos, kpos, sinks)
    return o.reshape(B, L, H * d), k[:, L - WINDOW:], v[:, L - WINDOW:]


def window_attention_sample(q, k, v, sinks, k_buf, v_buf):
    B, T, H, d = q.shape
    W = k_buf.shape[1]
    qg = q.reshape(B, T, ATT_KV_HEADS, ATT_GROUP, d)
    kc = jnp.concatenate([k_buf.astype(k.dtype), k], axis=1)
    vc = jnp.concatenate([v_buf.astype(v.dtype), v], axis=1)
    qpos = PAST_LEN + jnp.arange(T)
    kpos = jnp.concatenate([PAST_LEN - W + jnp.arange(W), qpos])
    o = sink_attention(qg, kc, vc, qpos, kpos, sinks)
    return o.reshape(B, T, H * d), kc[:, T:], vc[:, T:]


def retention(q, k, v, s0):
    B, L, H, dk = q.shape
    dv = v.shape[-1]
    C = RET_CHUNK if L % RET_CHUNK == 0 else L
    nc = L // C
    log_gamma = jnp.log(1.0 - 2.0 ** (-5.0 - jnp.arange(H, dtype=jnp.float32)))
    idx = jnp.arange(C, dtype=jnp.float32)
    diff = idx[:, None] - idx[None, :]
    dmask = jnp.where(diff >= 0, jnp.exp(jnp.maximum(diff, 0.0)[None] * log_gamma[:, None, None]), 0.0)
    q_dec = jnp.exp((idx + 1.0)[None, :, None] * log_gamma[:, None, None])
    k_dec = jnp.exp((C - 1.0 - idx)[None, :, None] * log_gamma[:, None, None])
    chunk_dec = jnp.exp(C * log_gamma)[:, None, None]

    def to_chunks(t):
        return t.astype(jnp.float32).reshape(B, nc, C, H, t.shape[-1]).transpose(1, 0, 3, 2, 4)

    def step(S, inp):
        qc, kc, vc = inp
        inner = jnp.einsum('bhid,bhjd->bhij', qc, kc) * dmask
        o = jnp.einsum('bhij,bhje->bhie', inner, vc) + jnp.einsum('bhid,bhde->bhie', qc * q_dec, S)
        S = S * chunk_dec + jnp.einsum('bhjd,bhje->bhde', kc * k_dec, vc)
        return S, o

    s_fin, o = lax.scan(step, s0.astype(jnp.float32), (to_chunks(q), to_chunks(k), to_chunks(v)))
    o = o.transpose(1, 0, 3, 2, 4).reshape(B, L, H, dv)
    return o, s_fin


def decoder_layer(x, c, pos, s0, attend, sinks, w_in, w_cond, b_cond, w_pa, w_pr, w_out, ln_g, ln_b):
    B, L, _ = x.shape
    mod = jax.nn.silu(c) @ w_cond + b_cond
    shift, scale, gate = jnp.split(mod, 3, axis=-1)
    h = x * (1.0 + scale[:, None, :]) + shift[:, None, :]
    parts = jnp.split(h @ w_in, np.cumsum(SPLITS)[:-1].tolist(), axis=-1)
    aq, ak, av, ag, rq, rk, rv, rg, mga, mgr = parts
    q = rope(aq.reshape(B, L, ATT_HEADS, ATT_HEAD_DIM), pos)
    k = rope(ak.reshape(B, L, ATT_KV_HEADS, ATT_HEAD_DIM), pos)
    v = av.reshape(B, L, ATT_KV_HEADS, ATT_HEAD_DIM)
    o_att, k_new, v_new = attend(q, k, v, sinks)
    q_r = rope(rq.reshape(B, L, RET_HEADS, RET_DK), pos)
    k_r = rope(rk.reshape(B, L, RET_HEADS, RET_DK), pos) * (RET_DK ** -0.5)
    v_r = rv.reshape(B, L, RET_HEADS, RET_DV)
    o_ret, s_new = retention(q_r, k_r, v_r, s0)
    o_ret = o_ret * lax.rsqrt(jnp.square(o_ret).mean(-1, keepdims=True) + RMS_EPS)
    o_ret = o_ret.reshape(B, L, RET_WIDTH).astype(x.dtype)
    z_a = (o_att * jax.nn.silu(ag)) @ w_pa
    z_r = (o_ret * jax.nn.silu(rg)) @ w_pr
    z = jax.nn.sigmoid(mga) * z_a + jax.nn.sigmoid(mgr) * z_r
    y = layer_norm(ALPHA * x + gate[:, None, :] * (z @ w_out), ln_g, ln_b)
    return y, k_new, v_new, s_new


def setup_inputs(seed: int = 0) -> dict:
    key = jax.random.key(seed)
    ks = jax.random.split(key, 18)

    def nrm(k, shape, s):
        return jax.random.normal(k, shape, jnp.float32) * s

    win = min(WINDOW, PAST_LEN)
    return {
        "x_prompt": nrm(ks[0], (BATCH, SEQ, D_MODEL), 1.0),
        "x_sample": nrm(ks[1], (DEC_BATCH, DEC_SEQ, D_MODEL), 1.0),
        "c_prompt": nrm(ks[2], (BATCH, D_MODEL), 1.0),
        "c_sample": nrm(ks[3], (DEC_BATCH, D_MODEL), 1.0),
        "cache_k": nrm(ks[4], (DEPTH, DEC_BATCH, win, ATT_KV_HEADS, ATT_HEAD_DIM), 1.0),
        "cache_v": nrm(ks[5], (DEPTH, DEC_BATCH, win, ATT_KV_HEADS, ATT_HEAD_DIM), 1.0),
        "state_ret": nrm(ks[6], (DEPTH, DEC_BATCH, RET_HEADS, RET_DK, RET_DV), 0.1),
        "w_in": nrm(ks[7], (DEPTH, D_MODEL, IN_COLS), D_MODEL ** -0.5),
        "attn_sinks": nrm(ks[8], (DEPTH, ATT_HEADS), 1.0),
        "w_cond": nrm(ks[9], (DEPTH, D_MODEL, 3 * D_MODEL), D_MODEL ** -0.5),
        "b_cond": nrm(ks[10], (DEPTH, 3 * D_MODEL), 0.02),
        "w_proj_attn": nrm(ks[11], (DEPTH, ATT_WIDTH, D_MODEL), BETA * ATT_WIDTH ** -0.5),
        "w_proj_ret": nrm(ks[12], (DEPTH, RET_WIDTH, D_MODEL), BETA * RET_WIDTH ** -0.5),
        "w_out": nrm(ks[13], (DEPTH, D_MODEL, D_MODEL), BETA * D_MODEL ** -0.5),
        "ln_g": 1.0 + nrm(ks[14], (DEPTH, D_MODEL), 0.05),
        "ln_b": nrm(ks[15], (DEPTH, D_MODEL), 0.02),
    }


def reference(x_prompt, x_sample, c_prompt, c_sample, cache_k, cache_v, state_ret,
              w_in, attn_sinks, w_cond, b_cond, w_proj_attn, w_proj_ret, w_out, ln_g, ln_b):
    B, L, _ = x_prompt.shape
    DB, T, _ = x_sample.shape
    pos_prompt = jnp.arange(L)
    pos_sample = PAST_LEN + jnp.arange(T)
    s_zero = jnp.zeros((B, RET_HEADS, RET_DK, RET_DV), jnp.float32)
    yp, ys = x_prompt, x_sample
    kp, vp, sp, ksm, vsm, ssm = [], [], [], [], [], []
    for l in range(DEPTH):
        yp, k_new, v_new, s_new = decoder_layer(
            yp, c_prompt, pos_prompt, s_zero, window_attention_prompt, attn_sinks[l],
            w_in[l], w_cond[l], b_cond[l], w_proj_attn[l], w_proj_ret[l], w_out[l], ln_g[l], ln_b[l])
        kp.append(k_new)
        vp.append(v_new)
        sp.append(s_new.astype(x_prompt.dtype))
        attend_sample = functools.partial(window_attention_sample, k_buf=cache_k[l], v_buf=cache_v[l])
        ys, k_new, v_new, s_new = decoder_layer(
            ys, c_sample, pos_sample, state_ret[l], attend_sample, attn_sinks[l],
            w_in[l], w_cond[l], b_cond[l], w_proj_attn[l], w_proj_ret[l], w_out[l], ln_g[l], ln_b[l])
        ksm.append(k_new.astype(cache_k.dtype))
        vsm.append(v_new.astype(cache_v.dtype))
        ssm.append(s_new.astype(state_ret.dtype))
    return (yp, ys, jnp.stack(kp), jnp.stack(vp), jnp.stack(sp), jnp.stack(ksm), jnp.stack(vsm), jnp.stack(ssm))
```

```python
import functools

import jax
import jax.numpy as jnp
import numpy as np
from jax import lax
from jax.experimental import pallas as pl
from jax.experimental.pallas import tpu as pltpu

D_MODEL = 1024
DEPTH = 2
PAST_LEN = 16384
ATT_HEADS = 8
ATT_KV_HEADS = 2
ATT_HEAD_DIM = 64
ATT_GROUP = ATT_HEADS // ATT_KV_HEADS
ATT_WIDTH = ATT_HEADS * ATT_HEAD_DIM
ATT_KV_WIDTH = ATT_KV_HEADS * ATT_HEAD_DIM
WINDOW = 128
RET_HEADS = 4
RET_DK = 128
RET_DV = 256
RET_QK_WIDTH = RET_HEADS * RET_DK
RET_WIDTH = RET_HEADS * RET_DV
RET_CHUNK = 128
ROPE_THETA = 10000.0
ALPHA = (2.0 * DEPTH) ** 0.25
LN_EPS = 1e-5
RMS_EPS = 1e-6
ATT_SCALE = ATT_HEAD_DIM ** -0.5
RET_K_SCALE = RET_DK ** -0.5

C_AQ = 0
C_AK = C_AQ + ATT_WIDTH
C_AV = C_AK + ATT_KV_WIDTH
C_AG = C_AV + ATT_KV_WIDTH
C_RQ = C_AG + ATT_WIDTH
C_RK = C_RQ + RET_QK_WIDTH
C_RV = C_RK + RET_QK_WIDTH
C_RG = C_RV + RET_WIDTH
C_MA = C_RG + RET_WIDTH
C_MR = C_MA + D_MODEL
IN_COLS = C_MR + D_MODEL

LANES = 128
SUBLANES = 8
VMEM_LIMIT_BYTES = 56 * 1024 * 1024

PROMPT_ROWS = 256
SAMPLE_BLOCK = 8

BF16 = jnp.bfloat16
F32 = jnp.float32
NT = (((1,), (1,)), ((), ()))


def _sigmoid(x):
    return 1.0 / (1.0 + jnp.exp(-x))


def _silu(x):
    return x * _sigmoid(x)


def _dot(a, b):
    return jnp.dot(a, b, preferred_element_type=F32)


def _rope_attn_tile(x, cos, sin_signed, first_half):
    rot = jnp.where(first_half, pltpu.roll(x, LANES - 32, 1), pltpu.roll(x, 32, 1))
    return x * cos + rot * sin_signed


def _rope_ret_tile(x, cos, sin_signed):
    return x * cos + pltpu.roll(x, 64, 1) * sin_signed


def _lane_iota(shape):
    return lax.broadcasted_iota(jnp.int32, shape, len(shape) - 1)


def _cond_kernel(c_ref, w_ref, b_ref, o_ref):
    a = _silu(c_ref[...]).astype(BF16)
    o_ref[...] = _dot(a, w_ref[...].astype(BF16)) + b_ref[...]


def _cond_call(c_all, w_cond, b_cond):
    rows = c_all.shape[0]
    tn = 768
    return pl.pallas_call(
        _cond_kernel,
        out_shape=jax.ShapeDtypeStruct((DEPTH, rows, 3 * D_MODEL), F32),
        grid=(DEPTH, 3 * D_MODEL // tn),
        in_specs=[
            pl.BlockSpec((rows, D_MODEL), lambda l, j: (0, 0)),
            pl.BlockSpec((None, D_MODEL, tn), lambda l, j: (l, 0, j)),
            pl.BlockSpec((None, 1, tn), lambda l, j: (l, 0, j)),
        ],
        out_specs=pl.BlockSpec((None, rows, tn), lambda l, j: (l, 0, j)),
        compiler_params=pltpu.CompilerParams(
            dimension_semantics=("arbitrary", "arbitrary"), vmem_limit_bytes=VMEM_LIMIT_BYTES),
        name="cond_mod",
    )(c_all, w_cond, b_cond.reshape(DEPTH, 1, 3 * D_MODEL))


def _prompt_kernel(sinks_ref, cdec_ref,
                   x_ref, mod_ref, ca_ref, sa_ref, cr_ref, sr_ref,
                   win_ref, wpa_ref, wpr_ref, wout_ref, lng_ref, lnb_ref,
                   dmask_ref, qdec_ref, kdec_ref,
                   y_ref, knew_ref, vnew_ref, snew_ref,
                   hb_scr, q_scr, kvar_scr, vvar_scr, ga_scr, a_scr,
                   rq_scr, rk_scr, rv_scr, rg_scr, r_scr, s_scr):
    step = pl.program_id(0)
    tm = x_ref.shape[0]
    nsub = tm // WINDOW

    @pl.when(step == 0)
    def _():
        kvar_scr[:, 0:WINDOW, :] = jnp.zeros((4, WINDOW, LANES), BF16)
        vvar_scr[:, 0:WINDOW, :] = jnp.zeros((4, WINDOW, LANES), BF16)
        s_scr[...] = jnp.zeros(s_scr.shape, F32)

    x = x_ref[...]
    shift = mod_ref[:, 0:D_MODEL]
    scale = mod_ref[:, D_MODEL:2 * D_MODEL]
    gate_c = mod_ref[:, 2 * D_MODEL:3 * D_MODEL]
    hb_scr[...] = (x * (1.0 + scale) + shift).astype(BF16)

    def proj(c0, c1):
        return _dot(hb_scr[...], win_ref[:, c0:c1])

    lane = _lane_iota((tm, LANES))
    first_half32 = (lane & 32) == 0
    lo64 = lane < 64
    ca = ca_ref[...]
    sa = sa_ref[...]
    cr = cr_ref[...]
    sr = sr_ref[...]

    kv = proj(C_AK, C_AG)
    k_rot = _rope_attn_tile(kv[:, 0:LANES], ca, sa, first_half32)
    v_raw = kv[:, LANES:2 * LANES]
    knew_ref[...] = k_rot[tm - WINDOW:, :]
    vnew_ref[...] = v_raw[tm - WINDOW:, :]

    def store_variants(scr, t):
        swapped = pltpu.roll(t, 64, 1)
        zero = jnp.zeros_like(t)
        scr[0, WINDOW:WINDOW + tm, :] = jnp.where(lo64, t, zero).astype(BF16)
        scr[1, WINDOW:WINDOW + tm, :] = jnp.where(lo64, zero, swapped).astype(BF16)
        scr[2, WINDOW:WINDOW + tm, :] = jnp.where(lo64, swapped, zero).astype(BF16)
        scr[3, WINDOW:WINDOW + tm, :] = jnp.where(lo64, zero, t).astype(BF16)

    store_variants(kvar_scr, k_rot)
    store_variants(vvar_scr, v_raw)

    qp = proj(C_AQ, C_AK)
    for t in range(ATT_WIDTH // LANES):
        qt = _rope_attn_tile(qp[:, t * LANES:(t + 1) * LANES], ca, sa, first_half32)
        q_scr[:, t * LANES:(t + 1) * LANES] = (qt * ATT_SCALE).astype(BF16)
    ga_scr[...] = _silu(proj(C_AG, C_RQ))

    row = lax.broadcasted_iota(jnp.int32, (WINDOW, 2 * WINDOW), 0)
    col = lax.broadcasted_iota(jnp.int32, (WINDOW, 2 * WINDOW), 1)
    in_window = col <= row + WINDOW
    mask_std = (col >= row) & in_window
    off = jnp.where(step > 0, 0, 4 * WINDOW)
    mask_first = ((col >= row + off) | (col >= WINDOW)) & in_window
    lo64_w = _lane_iota((WINDOW, LANES)) < 64

    for i in range(nsub):
        r0 = i * WINDOW
        mask = mask_first if i == 0 else mask_std
        for t in range(ATT_WIDTH // LANES):
            g = t // 2
            qt = q_scr[r0:r0 + WINDOW, t * LANES:(t + 1) * LANES]
            o_acc = None
            recips = []
            for p in range(2):
                head = 2 * t + p
                keys = kvar_scr[2 * g + p, r0:r0 + 2 * WINDOW, :]
                s = lax.dot_general(qt, keys, NT, preferred_element_type=F32)
                s = jnp.where(mask, s, -jnp.inf)
                sink = sinks_ref[head]
                m = jnp.maximum(jnp.max(s, axis=-1, keepdims=True), sink)
                pe = jnp.exp(s - m)
                denom = jnp.sum(pe, axis=-1, keepdims=True) + jnp.exp(sink - m)
                recips.append(1.0 / denom)
                vals = vvar_scr[2 * g + p, r0:r0 + 2 * WINDOW, :]
                o = _dot(pe.astype(BF16), vals)
                o_acc = o if o_acc is None else o_acc + o
            o_t = o_acc * jnp.where(lo64_w, recips[0], recips[1])
            gt = ga_scr[r0:r0 + WINDOW, t * LANES:(t + 1) * LANES]
            a_scr[r0:r0 + WINDOW, t * LANES:(t + 1) * LANES] = (o_t * gt).astype(BF16)

    kvar_scr[:, 0:WINDOW, :] = kvar_scr[:, tm:tm + WINDOW, :]
    vvar_scr[:, 0:WINDOW, :] = vvar_scr[:, tm:tm + WINDOW, :]

    rqp = proj(C_RQ, C_RK)
    rkp = proj(C_RK, C_RV)
    for h in range(RET_HEADS):
        sl = slice(h * RET_DK, (h + 1) * RET_DK)
        rq_scr[:, sl] = _rope_ret_tile(rqp[:, sl], cr, sr)
        rk_scr[:, sl] = _rope_ret_tile(rkp[:, sl], cr, sr) * RET_K_SCALE
    rv_scr[...] = proj(C_RV, C_RG).astype(BF16)
    rg_scr[...] = _silu(proj(C_RG, C_MA))

    for c in range(nsub):
        r0 = c * RET_CHUNK
        for h in range(RET_HEADS):
            qh = rq_scr[r0:r0 + RET_CHUNK, h * RET_DK:(h + 1) * RET_DK]
            kh = rk_scr[r0:r0 + RET_CHUNK, h * RET_DK:(h + 1) * RET_DK]
            vh = rv_scr[r0:r0 + RET_CHUNK, h * RET_DV:(h + 1) * RET_DV]
            inner = lax.dot_general(qh.astype(BF16), kh.astype(BF16), NT,
                                    preferred_element_type=F32) * dmask_ref[h]
            s_old = s_scr[h]
            o = _dot(inner.astype(BF16), vh) + _dot((qh * qdec_ref[h]).astype(BF16), s_old.astype(BF16))
            kd = (kh * kdec_ref[h]).astype(BF16)
            s_scr[h] = s_old * cdec_ref[h] + lax.dot_general(
                kd, vh, (((0,), (0,)), ((), ())), preferred_element_type=F32)
            ms = jnp.mean(o * o, axis=-1, keepdims=True)
            on = o * lax.rsqrt(ms + RMS_EPS)
            gt = rg_scr[r0:r0 + RET_CHUNK, h * RET_DV:(h + 1) * RET_DV]
            r_scr[r0:r0 + RET_CHUNK, h * RET_DV:(h + 1) * RET_DV] = (on * gt).astype(BF16)

    @pl.when(step == pl.num_programs(0) - 1)
    def _():
        snew_ref[...] = s_scr[...]

    za = _dot(a_scr[...], wpa_ref[...])
    zr = _dot(r_scr[...], wpr_ref[...])
    z = _sigmoid(proj(C_MA, C_MR)) * za + _sigmoid(proj(C_MR, IN_COLS)) * zr
    u = _dot(z.astype(BF16), wout_ref[...])
    t = ALPHA * x_ref[...] + gate_c * u
    mu = jnp.mean(t, axis=-1, keepdims=True)
    d = t - mu
    var = jnp.mean(d * d, axis=-1, keepdims=True)
    y_ref[...] = d * lax.rsqrt(var + LN_EPS) * lng_ref[...] + lnb_ref[...]


def _const_spec(shape):
    nd = len(shape)
    return pl.BlockSpec(shape, lambda i: (0,) * nd, pipeline_mode=pl.Buffered(1))


def _prompt_layer(x, mod_p, tabs, w_in_b, w_pa_b, w_pr_b, w_out_b, ln_g, ln_b, sinks, ret_tabs):
    seq = x.shape[0]
    tm = PROMPT_ROWS
    ca, sa, cr, sr = tabs
    dmask, qdec, kdec, cdec = ret_tabs
    smem = pl.BlockSpec(memory_space=pltpu.SMEM)
    row_spec = lambda w: pl.BlockSpec((tm, w), lambda i: (i, 0))
    return pl.pallas_call(
        _prompt_kernel,
        out_shape=(
            jax.ShapeDtypeStruct((seq, D_MODEL), F32),
            jax.ShapeDtypeStruct((WINDOW, ATT_KV_WIDTH), F32),
            jax.ShapeDtypeStruct((WINDOW, ATT_KV_WIDTH), F32),
            jax.ShapeDtypeStruct((RET_HEADS, RET_DK, RET_DV), F32),
        ),
        grid=(seq // tm,),
        in_specs=[
            smem, smem,
            row_spec(D_MODEL),
            _const_spec((1, 3 * D_MODEL)),
            row_spec(LANES), row_spec(LANES), row_spec(LANES), row_spec(LANES),
            _const_spec((D_MODEL, IN_COLS)),
            _const_spec((ATT_WIDTH, D_MODEL)),
            _const_spec((RET_WIDTH, D_MODEL)),
            _const_spec((D_MODEL, D_MODEL)),
            _const_spec((1, D_MODEL)), _const_spec((1, D_MODEL)),
            _const_spec((RET_HEADS, RET_CHUNK, RET_CHUNK)),
            _const_spec((RET_HEADS, RET_CHUNK, RET_DK)),
            _const_spec((RET_HEADS, RET_CHUNK, RET_DK)),
        ],
        out_specs=(
            row_spec(D_MODEL),
            pl.BlockSpec((WINDOW, ATT_KV_WIDTH), lambda i: (0, 0)),
            pl.BlockSpec((WINDOW, ATT_KV_WIDTH), lambda i: (0, 0)),
            pl.BlockSpec((RET_HEADS, RET_DK, RET_DV), lambda i: (0, 0, 0)),
        ),
        scratch_shapes=[
            pltpu.VMEM((tm, D_MODEL), BF16),
            pltpu.VMEM((tm, ATT_WIDTH), BF16),
            pltpu.VMEM((4, WINDOW + tm, LANES), BF16),
            pltpu.VMEM((4, WINDOW + tm, LANES), BF16),
            pltpu.VMEM((tm, ATT_WIDTH), F32),
            pltpu.VMEM((tm, ATT_WIDTH), BF16),
            pltpu.VMEM((tm, RET_QK_WIDTH), F32),
            pltpu.VMEM((tm, RET_QK_WIDTH), F32),
            pltpu.VMEM((tm, RET_WIDTH), BF16),
            pltpu.VMEM((tm, RET_WIDTH), F32),
            pltpu.VMEM((tm, RET_WIDTH), BF16),
            pltpu.VMEM((RET_HEADS, RET_DK, RET_DV), F32),
        ],
        compiler_params=pltpu.CompilerParams(
            dimension_semantics=("arbitrary",), vmem_limit_bytes=VMEM_LIMIT_BYTES),
        name="prompt_layer",
    )(sinks, cdec, x, mod_p, ca, sa, cr, sr, w_in_b, w_pa_b, w_pr_b, w_out_b,
      ln_g.reshape(1, D_MODEL), ln_b.reshape(1, D_MODEL), dmask, qdec, kdec)


def _sample_proj_kernel(x_ref, mod_ref, ca_ref, sa_ref, cr_ref, sr_ref, win_ref,
                        qkv_ref, ga_ref, ret_ref, rg_ref, mg_ref):
    rows = x_ref.shape[0]
    x = x_ref[...]
    shift = mod_ref[:, 0:D_MODEL]
    scale = mod_ref[:, D_MODEL:2 * D_MODEL]
    hb = (x * (1.0 + scale) + shift).astype(BF16)

    def proj(c0, c1):
        return _dot(hb, win_ref[:, c0:c1])

    lane = _lane_iota((rows, LANES))
    first_half32 = (lane & 32) == 0
    ca = ca_ref[...]
    sa = sa_ref[...]
    cr = cr_ref[...]
    sr = sr_ref[...]

    qkv = proj(C_AQ, C_AG)
    for t in range(ATT_WIDTH // LANES):
        qt = _rope_attn_tile(qkv[:, t * LANES:(t + 1) * LANES], ca, sa, first_half32)
        qkv_ref[:, t * LANES:(t + 1) * LANES] = qt * ATT_SCALE
    qkv_ref[:, C_AK:C_AV] = _rope_attn_tile(qkv[:, C_AK:C_AV], ca, sa, first_half32)
    qkv_ref[:, C_AV:C_AG] = qkv[:, C_AV:C_AG]
    ga_ref[...] = _silu(proj(C_AG, C_RQ))

    rqk = proj(C_RQ, C_RV)
    for h in range(RET_HEADS):
        sl = slice(h * RET_DK, (h + 1) * RET_DK)
        ret_ref[:, sl] = _rope_ret_tile(rqk[:, sl], cr, sr)
        sk = slice(RET_QK_WIDTH + h * RET_DK, RET_QK_WIDTH + (h + 1) * RET_DK)
        ret_ref[:, sk] = _rope_ret_tile(rqk[:, sk], cr, sr) * RET_K_SCALE
    ret_ref[:, 2 * RET_QK_WIDTH:] = proj(C_RV, C_RG)
    rg_ref[...] = _silu(proj(C_RG, C_MA))
    mg_ref[...] = _sigmoid(proj(C_MA, IN_COLS))


def _sample_proj(x, mod_s, tabs, w_in_b):
    rows = x.shape[0]
    ca, sa, cr, sr = tabs
    widths = (C_AG, ATT_WIDTH, 2 * RET_QK_WIDTH + RET_WIDTH, RET_WIDTH, 2 * D_MODEL)
    return pl.pallas_call(
        _sample_proj_kernel,
        out_shape=tuple(jax.ShapeDtypeStruct((rows, w), F32) for w in widths),
        compiler_params=pltpu.CompilerParams(vmem_limit_bytes=VMEM_LIMIT_BYTES),
        name="sample_proj",
    )(x, mod_s, ca, sa, cr, sr, w_in_b)


def _sample_mix_kernel(sinks_ref, gam_ref, qkv_ref, ret_ref, ck_ref, cv_ref, st_ref, *rest):
    oatt_ref, oret_ref, nk_ref, nv_ref, ns_ref = rest[-5:]
    nb = qkv_ref.shape[0]

    rowi = lax.broadcasted_iota(jnp.int32, (ATT_HEADS, LANES), 0)
    lanei = lax.broadcasted_iota(jnp.int32, (ATT_HEADS, LANES), 1)
    lane_group = lanei // ATT_HEAD_DIM
    sink_col = jnp.zeros((ATT_HEADS, 1), F32)
    rowc = lax.broadcasted_iota(jnp.int32, (ATT_HEADS, 1), 0)
    for h in range(ATT_HEADS):
        sink_col = jnp.where(rowc == h, sinks_ref[h], sink_col)
    lo64_row = _lane_iota((1, LANES)) < 64
    dr = lax.broadcasted_iota(jnp.int32, (RET_DK, RET_DK), 0)
    dc = lax.broadcasted_iota(jnp.int32, (RET_DK, RET_DK), 1)
    diag = dr == dc
    ones_row = jnp.ones((RET_DK, 1), F32)

    def body(b, carry):
        qkv = qkv_ref[b]
        kmat = ck_ref[b]
        vmat = cv_ref[b]
        k_new = qkv[:, C_AK:C_AV]
        v_new = qkv[:, C_AV:C_AG]

        qmat = jnp.zeros((ATT_HEADS, LANES), F32)
        for h in range(ATT_HEADS):
            t, p, g = h // 2, h % 2, h // ATT_GROUP
            tile = qkv[:, t * LANES:(t + 1) * LANES]
            src = tile if p == g else pltpu.roll(tile, 64, 1)
            qmat = jnp.where((rowi == h) & (lane_group == g), jnp.broadcast_to(src, (ATT_HEADS, LANES)), qmat)

        s_c = lax.dot_general(qmat.astype(BF16), kmat.astype(BF16), NT, preferred_element_type=F32)
        s_self = jnp.sum(qmat * k_new, axis=-1, keepdims=True)
        m = jnp.maximum(jnp.maximum(jnp.max(s_c, axis=-1, keepdims=True), s_self), sink_col)
        p_c = jnp.exp(s_c - m)
        p_self = jnp.exp(s_self - m)
        denom = jnp.sum(p_c, axis=-1, keepdims=True) + p_self + jnp.exp(sink_col - m)
        o = (_dot(p_c.astype(BF16), vmat.astype(BF16)) + p_self * v_new) / denom
        o_sw = pltpu.roll(o, 64, 1)
        for t in range(ATT_WIDTH // LANES):
            g = t // 2
            first = (o if g == 0 else o_sw)[2 * t:2 * t + 1, :]
            second = (o_sw if g == 0 else o)[2 * t + 1:2 * t + 2, :]
            oatt_ref[b, :, t * LANES:(t + 1) * LANES] = jnp.where(lo64_row, first, second)

        nk_ref[b, 0:WINDOW - 1, :] = kmat[1:WINDOW, :]
        nk_ref[b, WINDOW - 1:WINDOW, :] = k_new
        nv_ref[b, 0:WINDOW - 1, :] = vmat[1:WINDOW, :]
        nv_ref[b, WINDOW - 1:WINDOW, :] = v_new

        ret = ret_ref[b]
        for h in range(RET_HEADS):
            qh = ret[:, h * RET_DK:(h + 1) * RET_DK]
            kh = ret[:, RET_QK_WIDTH + h * RET_DK:RET_QK_WIDTH + (h + 1) * RET_DK]
            vh = ret[:, 2 * RET_QK_WIDTH + h * RET_DV:2 * RET_QK_WIDTH + (h + 1) * RET_DV]
            kdiag = jnp.where(diag, jnp.broadcast_to(kh, (RET_DK, RET_DK)), 0.0).astype(BF16)
            vfull = jnp.broadcast_to(vh, (RET_DK, RET_DV)).astype(BF16)
            s_new = st_ref[b, h] * gam_ref[h] + _dot(kdiag, vfull)
            ns_ref[b, h] = s_new
            q8 = jnp.broadcast_to(qh, (SUBLANES, RET_DK)).astype(BF16)
            oh = _dot(q8, s_new.astype(BF16))[0:1, :]
            ms = jnp.mean(oh * oh, axis=-1, keepdims=True)
            oret_ref[b, :, h * RET_DV:(h + 1) * RET_DV] = oh * lax.rsqrt(ms + RMS_EPS)
        return carry

    lax.fori_loop(0, nb, body, 0)


def _sample_mix(layer, qkv, ret, cache_k, cache_v, state, sinks, gam, prev):
    nbatch = qkv.shape[0]
    nb = SAMPLE_BLOCK
    win = cache_k.shape[2]
    smem = pl.BlockSpec(memory_space=pltpu.SMEM)
    any_spec = pl.BlockSpec(memory_space=pl.ANY)
    in_specs = [
        smem, smem,
        pl.BlockSpec((nb, 1, C_AG), lambda i: (i, 0, 0)),
        pl.BlockSpec((nb, 1, 2 * RET_QK_WIDTH + RET_WIDTH), lambda i: (i, 0, 0)),
        pl.BlockSpec((None, nb, win, ATT_KV_WIDTH), lambda i: (layer, i, 0, 0)),
        pl.BlockSpec((None, nb, win, ATT_KV_WIDTH), lambda i: (layer, i, 0, 0)),
        pl.BlockSpec((None, nb, RET_HEADS, RET_DK, RET_DV), lambda i: (layer, i, 0, 0, 0)),
    ]
    args = [sinks, gam, qkv.reshape(nbatch, 1, C_AG), ret.reshape(nbatch, 1, -1), cache_k, cache_v, state]
    aliases = {}
    if prev is not None:
        for j, arr in enumerate(prev):
            aliases[len(args)] = 2 + j
            args.append(arr)
            in_specs.append(any_spec)
    return pl.pallas_call(
        _sample_mix_kernel,
        out_shape=(
            jax.ShapeDtypeStruct((nbatch, 1, ATT_WIDTH), F32),
            jax.ShapeDtypeStruct((nbatch, 1, RET_WIDTH), F32),
            jax.ShapeDtypeStruct(cache_k.shape, F32),
            jax.ShapeDtypeStruct(cache_v.shape, F32),
            jax.ShapeDtypeStruct(state.shape, F32),
        ),
        grid=(nbatch // nb,),
        in_specs=in_specs,
        out_specs=(
            pl.BlockSpec((nb, 1, ATT_WIDTH), lambda i: (i, 0, 0)),
            pl.BlockSpec((nb, 1, RET_WIDTH), lambda i: (i, 0, 0)),
            pl.BlockSpec((None, nb, win, ATT_KV_WIDTH), lambda i: (layer, i, 0, 0)),
            pl.BlockSpec((None, nb, win, ATT_KV_WIDTH), lambda i: (layer, i, 0, 0)),
            pl.BlockSpec((None, nb, RET_HEADS, RET_DK, RET_DV), lambda i: (layer, i, 0, 0, 0)),
        ),
        input_output_aliases=aliases,
        compiler_params=pltpu.CompilerParams(
            dimension_semantics=("arbitrary",), vmem_limit_bytes=VMEM_LIMIT_BYTES),
        name="sample_mix",
    )(*args)


def _sample_out_kernel(x_ref, mod_ref, oatt_ref, ga_ref, oret_ref, rg_ref, mg_ref,
                       wpa_ref, wpr_ref, wout_ref, lng_ref, lnb_ref, y_ref):
    gate_c = mod_ref[:, 2 * D_MODEL:3 * D_MODEL]
    za = _dot((oatt_ref[...] * ga_ref[...]).astype(BF16), wpa_ref[...])
    zr = _dot((oret_ref[...] * rg_ref[...]).astype(BF16), wpr_ref[...])
    z = mg_ref[:, 0:D_MODEL] * za + mg_ref[:, D_MODEL:2 * D_MODEL] * zr
    u = _dot(z.astype(BF16), wout_ref[...])
    t = ALPHA * x_ref[...] + gate_c * u
    mu = jnp.mean(t, axis=-1, keepdims=True)
    d = t - mu
    var = jnp.mean(d * d, axis=-1, keepdims=True)
    y_ref[...] = d * lax.rsqrt(var + LN_EPS) * lng_ref[...] + lnb_ref[...]


def _sample_out(x, mod_s, oatt, ga, oret, rg, mg, w_pa_b, w_pr_b, w_out_b, ln_g, ln_b):
    return pl.pallas_call(
        _sample_out_kernel,
        out_shape=jax.ShapeDtypeStruct(x.shape, F32),
        compiler_params=pltpu.CompilerParams(vmem_limit_bytes=VMEM_LIMIT_BYTES),
        name="sample_out",
    )(x, mod_s, oatt, ga, oret, rg, mg, w_pa_b, w_pr_b, w_out_b,
      ln_g.reshape(1, D_MODEL), ln_b.reshape(1, D_MODEL))


def _rope_tables(pos):
    def cs(half):
        freqs = ROPE_THETA ** (-jnp.arange(half, dtype=F32) / half)
        ang = pos.astype(F32)[:, None] * freqs[None, :]
        return jnp.cos(ang), jnp.sin(ang)

    c32, s32 = cs(ATT_HEAD_DIM // 2)
    c64, s64 = cs(RET_DK // 2)
    ca = jnp.tile(c32, (1, 4))
    sa = jnp.tile(jnp.concatenate([-s32, s32], axis=1), (1, 2))
    cr = jnp.tile(c64, (1, 2))
    sr = jnp.concatenate([-s64, s64], axis=1)
    return ca, sa, cr, sr


def _retention_tables():
    c = RET_CHUNK
    log_gamma = jnp.log(1.0 - 2.0 ** (-5.0 - jnp.arange(RET_HEADS, dtype=F32)))
    idx = jnp.arange(c, dtype=F32)
    diff = idx[:, None] - idx[None, :]
    dmask = jnp.where(diff >= 0, jnp.exp(jnp.maximum(diff, 0.0)[None] * log_gamma[:, None, None]), 0.0)
    q_dec = jnp.exp((idx + 1.0)[None, :, None] * log_gamma[:, None, None])
    k_dec = jnp.exp((c - 1.0 - idx)[None, :, None] * log_gamma[:, None, None])
    chunk_dec = jnp.exp(c * log_gamma)
    qdec = jnp.broadcast_to(q_dec, (RET_HEADS, c, RET_DK))
    kdec = jnp.broadcast_to(k_dec, (RET_HEADS, c, RET_DK))
    gamma1 = jnp.exp(1.0 * log_gamma)
    return (dmask, qdec, kdec, chunk_dec), gamma1


def kernel(x_prompt, x_sample, c_prompt, c_sample, cache_k, cache_v, state_ret, w_in, attn_sinks,
           w_cond, b_cond, w_proj_attn, w_proj_ret, w_out, ln_g, ln_b):
    seq = x_prompt.shape[1]
    nbatch = x_sample.shape[0]
    win = cache_k.shape[2]

    w_in_b = [w_in[l].astype(BF16) for l in range(DEPTH)]
    w_pa_b = [w_proj_attn[l].astype(BF16) for l in range(DEPTH)]
    w_pr_b = [w_proj_ret[l].astype(BF16) for l in range(DEPTH)]
    w_out_b = [w_out[l].astype(BF16) for l in range(DEPTH)]

    pad = (-(nbatch + 1)) % SUBLANES
    c_all = jnp.concatenate([c_sample, c_prompt, jnp.zeros((pad, D_MODEL), F32)], axis=0)
    mod = _cond_call(c_all, w_cond, b_cond)

    tabs_p = _rope_tables(jnp.arange(seq))
    tabs_s = _rope_tables(PAST_LEN + jnp.arange(1))
    ret_tabs, gamma1 = _retention_tables()

    ck = cache_k.reshape(DEPTH, nbatch, win, ATT_KV_WIDTH)
    cv = cache_v.reshape(DEPTH, nbatch, win, ATT_KV_WIDTH)

    yp = x_prompt[0]
    ys = x_sample[:, 0, :]
    kp, vp, sp = [], [], []
    prev = None
    for l in range(DEPTH):
        mod_s = mod[l, 0:nbatch]
        mod_p = mod[l, nbatch:nbatch + 1]
        yp, k_new, v_new, s_new = _prompt_layer(
            yp, mod_p, tabs_p, w_in_b[l], w_pa_b[l], w_pr_b[l], w_out_b[l], ln_g[l], ln_b[l],
            attn_sinks[l], ret_tabs)
        kp.append(k_new.reshape(1, WINDOW, ATT_KV_HEADS, ATT_HEAD_DIM))
        vp.append(v_new.reshape(1, WINDOW, ATT_KV_HEADS, ATT_HEAD_DIM))
        sp.append(s_new[None])

        qkv, ga, ret, rg, mg = _sample_proj(ys, mod_s, tabs_s, w_in_b[l])
        oatt, oret, nk, nv, ns = _sample_mix(l, qkv, ret, ck, cv, state_ret, attn_sinks[l], gamma1, prev)
        prev = (nk, nv, ns)
        ys = _sample_out(ys, mod_s, oatt.reshape(nbatch, ATT_WIDTH), ga, oret.reshape(nbatch, RET_WIDTH),
                         rg, mg, w_pa_b[l], w_pr_b[l], w_out_b[l], ln_g[l], ln_b[l])

    nk, nv, ns = prev
    return (yp[None], ys[:, None, :], jnp.stack(kp), jnp.stack(vp), jnp.stack(sp),
            nk.reshape(cache_k.shape), nv.reshape(cache_v.shape), ns)
```

```python
import functools

import jax
import jax.numpy as jnp
import numpy as np
from jax import lax
from jax.experimental import pallas as pl
from jax.experimental.pallas import tpu as pltpu

D_MODEL = 1024
DEPTH = 2
PAST_LEN = 16384
ATT_HEADS = 8
ATT_KV_HEADS = 2
ATT_HEAD_DIM = 64
ATT_GROUP = ATT_HEADS // ATT_KV_HEADS
ATT_WIDTH = ATT_HEADS * ATT_HEAD_DIM
ATT_KV_WIDTH = ATT_KV_HEADS * ATT_HEAD_DIM
WINDOW = 128
RET_HEADS = 4
RET_DK = 128
RET_DV = 256
RET_QK_WIDTH = RET_HEADS * RET_DK
RET_WIDTH = RET_HEADS * RET_DV
RET_CHUNK = 128
ROPE_THETA = 10000.0
ALPHA = (2.0 * DEPTH) ** 0.25
LN_EPS = 1e-5
RMS_EPS = 1e-6
ATT_SCALE = ATT_HEAD_DIM ** -0.5
RET_K_SCALE = RET_DK ** -0.5

C_AQ = 0
C_AK = C_AQ + ATT_WIDTH
C_AV = C_AK + ATT_KV_WIDTH
C_AG = C_AV + ATT_KV_WIDTH
C_RQ = C_AG + ATT_WIDTH
C_RK = C_RQ + RET_QK_WIDTH
C_RV = C_RK + RET_QK_WIDTH
C_RG = C_RV + RET_WIDTH
C_MA = C_RG + RET_WIDTH
C_MR = C_MA + D_MODEL
IN_COLS = C_MR + D_MODEL

LANES = 128
SUBLANES = 8
VMEM_LIMIT_BYTES = 56 * 1024 * 1024

PROMPT_ROWS = 256
SAMPLE_BLOCK = 8

BF16 = jnp.bfloat16
F32 = jnp.float32
NT = (((1,), (1,)), ((), ()))

_LOG_GAMMA = np.log(1.0 - 2.0 ** (-5.0 - np.arange(RET_HEADS, dtype=np.float64)))
CHUNK_DECAY = tuple(float(v) for v in np.exp(RET_CHUNK * _LOG_GAMMA))
TOKEN_DECAY = tuple(float(v) for v in np.exp(_LOG_GAMMA))


def _sigmoid(x):
    return 1.0 / (1.0 + jnp.exp(-x))


def _silu(x):
    return x * _sigmoid(x)


def _dot(a, b):
    return jnp.dot(a, b, preferred_element_type=F32)


def _rope_attn_tile(x, cos, sin_signed, first_half):
    rot = jnp.where(first_half, pltpu.roll(x, LANES - 32, 1), pltpu.roll(x, 32, 1))
    return x * cos + rot * sin_signed


def _rope_ret_tile(x, cos, sin_signed):
    return x * cos + pltpu.roll(x, 64, 1) * sin_signed


def _lane_iota(shape):
    return lax.broadcasted_iota(jnp.int32, shape, len(shape) - 1)


def _layer_spec(shape, layer):
    nd = len(shape)
    return pl.BlockSpec((None,) + tuple(shape), lambda i: (layer,) + (0,) * nd,
                        pipeline_mode=pl.Buffered(1))


def _const_spec(shape):
    nd = len(shape)
    return pl.BlockSpec(tuple(shape), lambda i: (0,) * nd, pipeline_mode=pl.Buffered(1))


def _rope_lane_patterns():
    lane = np.arange(LANES)
    f_att = ROPE_THETA ** (-(lane % 32) / 32.0)
    s_att = np.where(lane % 64 < 32, -1.0, 1.0)
    f_ret = ROPE_THETA ** (-(lane % 64) / 64.0)
    s_ret = np.where(lane < 64, -1.0, 1.0)
    return (f_att, s_att), (f_ret, s_ret)


def _prompt_rope_tables(seq, tm):
    starts = np.arange(seq // tm, dtype=np.float64)[:, None] * tm
    offs = np.arange(tm, dtype=np.float64)[:, None]
    base, within = [], []
    for freq, sign in _rope_lane_patterns():
        base += [np.cos(starts * freq), np.sin(starts * freq)]
        c, s = np.cos(offs * freq), np.sin(offs * freq)
        within += [c, s, sign * c, sign * s]
    return (jnp.asarray(np.stack(base, axis=1), F32),
            jnp.asarray(np.stack(within, axis=0), F32))


def _sample_rope_table(pos):
    rows = []
    for freq, sign in _rope_lane_patterns():
        rows += [np.cos(pos * freq), sign * np.sin(pos * freq)]
    return jnp.asarray(np.stack(rows, axis=0), F32)


def _retention_tables():
    c = RET_CHUNK
    idx = np.arange(c, dtype=np.float64)
    diff = idx[:, None] - idx[None, :]
    lg = _LOG_GAMMA[:, None, None]
    dmask = np.where(diff >= 0, np.exp(np.maximum(diff, 0.0)[None] * lg), 0.0)
    qdec = np.broadcast_to(np.exp((idx + 1.0)[None, :, None] * lg), (RET_HEADS, c, RET_DK))
    kdec = np.broadcast_to(np.exp((c - 1.0 - idx)[None, :, None] * lg), (RET_HEADS, c, RET_DK))
    return jnp.asarray(dmask, F32), jnp.asarray(qdec, F32), jnp.asarray(kdec, F32)


def _cond_kernel(c_ref, w_ref, b_ref, o_ref):
    a = _silu(c_ref[...]).astype(BF16)
    o_ref[...] = _dot(a, w_ref[...].astype(BF16)) + b_ref[...]


def _cond_call(c_all, w_cond, b_cond):
    rows = c_all.shape[0]
    tn = 768
    return pl.pallas_call(
        _cond_kernel,
        out_shape=jax.ShapeDtypeStruct((DEPTH, rows, 3 * D_MODEL), F32),
        grid=(DEPTH, 3 * D_MODEL // tn),
        in_specs=[
            pl.BlockSpec((rows, D_MODEL), lambda l, j: (0, 0)),
            pl.BlockSpec((None, D_MODEL, tn), lambda l, j: (l, 0, j)),
            pl.BlockSpec((None, 1, tn), lambda l, j: (l, 0, j)),
        ],
        out_specs=pl.BlockSpec((None, rows, tn), lambda l, j: (l, 0, j)),
        compiler_params=pltpu.CompilerParams(
            dimension_semantics=("arbitrary", "arbitrary"), vmem_limit_bytes=VMEM_LIMIT_BYTES),
        name="cond_mod",
    )(c_all, w_cond, b_cond.reshape(DEPTH, 1, 3 * D_MODEL))


def _prompt_kernel(sinks_ref, x_ref, mod_ref, base_ref, rtab_ref,
                   win_ref, wpa_ref, wpr_ref, wout_ref, lng_ref, lnb_ref,
                   dmask_ref, qdec_ref, kdec_ref,
                   y_ref, knew_ref, vnew_ref, snew_ref,
                   tab_scr, hb_scr, q_scr, kvar_scr, vvar_scr, ga_scr, a_scr,
                   rq_scr, rk_scr, rv_scr, rg_scr, r_scr, s_scr, *, layer):
    step = pl.program_id(0)
    tm = x_ref.shape[0]
    nsub = tm // WINDOW

    @pl.when(step == 0)
    def _():
        kvar_scr[:, 0:WINDOW, :] = jnp.zeros((4, WINDOW, LANES), BF16)
        vvar_scr[:, 0:WINDOW, :] = jnp.zeros((4, WINDOW, LANES), BF16)
        s_scr[...] = jnp.zeros(s_scr.shape, F32)

    for fam in range(2):
        cb = base_ref[2 * fam:2 * fam + 1, :]
        sb = base_ref[2 * fam + 1:2 * fam + 2, :]
        tab_scr[2 * fam] = cb * rtab_ref[4 * fam] - sb * rtab_ref[4 * fam + 1]
        tab_scr[2 * fam + 1] = sb * rtab_ref[4 * fam + 2] + cb * rtab_ref[4 * fam + 3]

    x = x_ref[...]
    shift = mod_ref[0:1, 0:D_MODEL]
    scale = mod_ref[0:1, D_MODEL:2 * D_MODEL]
    hb_scr[...] = (x * (1.0 + scale) + shift).astype(BF16)

    def proj(c0, c1):
        return _dot(hb_scr[...], win_ref[:, c0:c1])

    lane = _lane_iota((tm, LANES))
    first_half32 = (lane & 32) == 0
    lo64 = lane < 64

    def rope_attn(t):
        return _rope_attn_tile(t, tab_scr[0], tab_scr[1], first_half32)

    def rope_ret(t):
        return _rope_ret_tile(t, tab_scr[2], tab_scr[3])

    kv = proj(C_AK, C_AG)
    k_rot = rope_attn(kv[:, 0:LANES])
    v_raw = kv[:, LANES:2 * LANES]
    knew_ref[...] = k_rot[tm - WINDOW:, :]
    vnew_ref[...] = v_raw[tm - WINDOW:, :]

    def store_variants(scr, t):
        swapped = pltpu.roll(t, 64, 1)
        zero = jnp.zeros_like(t)
        scr[0, WINDOW:WINDOW + tm, :] = jnp.where(lo64, t, zero).astype(BF16)
        scr[1, WINDOW:WINDOW + tm, :] = jnp.where(lo64, zero, swapped).astype(BF16)
        scr[2, WINDOW:WINDOW + tm, :] = jnp.where(lo64, swapped, zero).astype(BF16)
        scr[3, WINDOW:WINDOW + tm, :] = jnp.where(lo64, zero, t).astype(BF16)

    store_variants(kvar_scr, k_rot)
    store_variants(vvar_scr, v_raw)

    qp = proj(C_AQ, C_AK)
    for t in range(ATT_WIDTH // LANES):
        qt = rope_attn(qp[:, t * LANES:(t + 1) * LANES])
        q_scr[:, t * LANES:(t + 1) * LANES] = (qt * ATT_SCALE).astype(BF16)
    ga_scr[...] = _silu(proj(C_AG, C_RQ))

    row = lax.broadcasted_iota(jnp.int32, (WINDOW, 2 * WINDOW), 0)
    col = lax.broadcasted_iota(jnp.int32, (WINDOW, 2 * WINDOW), 1)
    in_window = col <= row + WINDOW
    mask_std = (col >= row) & in_window
    off = jnp.where(step > 0, 0, 4 * WINDOW)
    mask_first = ((col >= row + off) | (col >= WINDOW)) & in_window
    lo64_w = _lane_iota((WINDOW, LANES)) < 64

    for i in range(nsub):
        r0 = i * WINDOW
        mask = mask_first if i == 0 else mask_std
        for t in range(ATT_WIDTH // LANES):
            g = t // 2
            qt = q_scr[r0:r0 + WINDOW, t * LANES:(t + 1) * LANES]
            o_acc = None
            recips = []
            for p in range(2):
                head = 2 * t + p
                keys = kvar_scr[2 * g + p, r0:r0 + 2 * WINDOW, :]
                s = lax.dot_general(qt, keys, NT, preferred_element_type=F32)
                s = jnp.where(mask, s, -jnp.inf)
                sink = sinks_ref[layer, head]
                m = jnp.maximum(jnp.max(s, axis=-1, keepdims=True), sink)
                pe = jnp.exp(s - m)
                denom = jnp.sum(pe, axis=-1, keepdims=True) + jnp.exp(sink - m)
                recips.append(1.0 / denom)
                vals = vvar_scr[2 * g + p, r0:r0 + 2 * WINDOW, :]
                o = _dot(pe.astype(BF16), vals)
                o_acc = o if o_acc is None else o_acc + o
            o_t = o_acc * jnp.where(lo64_w, recips[0], recips[1])
            gt = ga_scr[r0:r0 + WINDOW, t * LANES:(t + 1) * LANES]
            a_scr[r0:r0 + WINDOW, t * LANES:(t + 1) * LANES] = (o_t * gt).astype(BF16)

    kvar_scr[:, 0:WINDOW, :] = kvar_scr[:, tm:tm + WINDOW, :]
    vvar_scr[:, 0:WINDOW, :] = vvar_scr[:, tm:tm + WINDOW, :]

    rqp = proj(C_RQ, C_RK)
    rkp = proj(C_RK, C_RV)
    for h in range(RET_HEADS):
        sl = slice(h * RET_DK, (h + 1) * RET_DK)
        rq_scr[:, sl] = rope_ret(rqp[:, sl])
        rk_scr[:, sl] = rope_ret(rkp[:, sl]) * RET_K_SCALE
    rv_scr[...] = proj(C_RV, C_RG).astype(BF16)
    rg_scr[...] = _silu(proj(C_RG, C_MA))

    for c in range(nsub):
        r0 = c * RET_CHUNK
        for h in range(RET_HEADS):
            qh = rq_scr[r0:r0 + RET_CHUNK, h * RET_DK:(h + 1) * RET_DK]
            kh = rk_scr[r0:r0 + RET_CHUNK, h * RET_DK:(h + 1) * RET_DK]
            vh = rv_scr[r0:r0 + RET_CHUNK, h * RET_DV:(h + 1) * RET_DV]
            inner = lax.dot_general(qh.astype(BF16), kh.astype(BF16), NT,
                                    preferred_element_type=F32) * dmask_ref[h]
            s_old = s_scr[h]
            o = _dot(inner.astype(BF16), vh) + _dot((qh * qdec_ref[h]).astype(BF16), s_old.astype(BF16))
            kd = (kh * kdec_ref[h]).astype(BF16)
            s_scr[h] = s_old * CHUNK_DECAY[h] + lax.dot_general(
                kd, vh, (((0,), (0,)), ((), ())), preferred_element_type=F32)
            ms = jnp.mean(o * o, axis=-1, keepdims=True)
            on = o * lax.rsqrt(ms + RMS_EPS)
            gt = rg_scr[r0:r0 + RET_CHUNK, h * RET_DV:(h + 1) * RET_DV]
            r_scr[r0:r0 + RET_CHUNK, h * RET_DV:(h + 1) * RET_DV] = (on * gt).astype(BF16)

    @pl.when(step == pl.num_programs(0) - 1)
    def _():
        snew_ref[...] = s_scr[...]

    za = _dot(a_scr[...], wpa_ref[...])
    zr = _dot(r_scr[...], wpr_ref[...])
    z = _sigmoid(proj(C_MA, C_MR)) * za + _sigmoid(proj(C_MR, IN_COLS)) * zr
    u = _dot(z.astype(BF16), wout_ref[...])
    gate_c = mod_ref[0:1, 2 * D_MODEL:3 * D_MODEL]
    t = ALPHA * x_ref[...] + gate_c * u
    mu = jnp.mean(t, axis=-1, keepdims=True)
    d = t - mu
    var = jnp.mean(d * d, axis=-1, keepdims=True)
    y_ref[...] = d * lax.rsqrt(var + LN_EPS) * lng_ref[...] + lnb_ref[...]


def _prompt_layer(layer, x, mod, mod_row_block, base, rtab, w_in_b, w_pa_b, w_pr_b, w_out_b,
                  ln_g, ln_b, sinks, ret_tabs):
    seq = x.shape[0]
    tm = PROMPT_ROWS
    dmask, qdec, kdec = ret_tabs
    smem = pl.BlockSpec(memory_space=pltpu.SMEM)
    row_spec = lambda w: pl.BlockSpec((tm, w), lambda i: (i, 0))
    return pl.pallas_call(
        functools.partial(_prompt_kernel, layer=layer),
        out_shape=(
            jax.ShapeDtypeStruct((seq, D_MODEL), F32),
            jax.ShapeDtypeStruct((WINDOW, ATT_KV_WIDTH), F32),
            jax.ShapeDtypeStruct((WINDOW, ATT_KV_WIDTH), F32),
            jax.ShapeDtypeStruct((RET_HEADS, RET_DK, RET_DV), F32),
        ),
        grid=(seq // tm,),
        in_specs=[
            smem,
            row_spec(D_MODEL),
            pl.BlockSpec((None, SUBLANES, 3 * D_MODEL), lambda i: (layer, mod_row_block, 0),
                         pipeline_mode=pl.Buffered(1)),
            pl.BlockSpec((None, 4, LANES), lambda i: (i, 0, 0)),
            _const_spec((8, tm, LANES)),
            _layer_spec((D_MODEL, IN_COLS), layer),
            _layer_spec((ATT_WIDTH, D_MODEL), layer),
            _layer_spec((RET_WIDTH, D_MODEL), layer),
            _layer_spec((D_MODEL, D_MODEL), layer),
            _layer_spec((1, D_MODEL), layer), _layer_spec((1, D_MODEL), layer),
            _const_spec((RET_HEADS, RET_CHUNK, RET_CHUNK)),
            _const_spec((RET_HEADS, RET_CHUNK, RET_DK)),
            _const_spec((RET_HEADS, RET_CHUNK, RET_DK)),
        ],
        out_specs=(
            row_spec(D_MODEL),
            pl.BlockSpec((WINDOW, ATT_KV_WIDTH), lambda i: (0, 0)),
            pl.BlockSpec((WINDOW, ATT_KV_WIDTH), lambda i: (0, 0)),
            pl.BlockSpec((RET_HEADS, RET_DK, RET_DV), lambda i: (0, 0, 0)),
        ),
        scratch_shapes=[
            pltpu.VMEM((4, tm, LANES), F32),
            pltpu.VMEM((tm, D_MODEL), BF16),
            pltpu.VMEM((tm, ATT_WIDTH), BF16),
            pltpu.VMEM((4, WINDOW + tm, LANES), BF16),
            pltpu.VMEM((4, WINDOW + tm, LANES), BF16),
            pltpu.VMEM((tm, ATT_WIDTH), F32),
            pltpu.VMEM((tm, ATT_WIDTH), BF16),
            pltpu.VMEM((tm, RET_QK_WIDTH), F32),
            pltpu.VMEM((tm, RET_QK_WIDTH), F32),
            pltpu.VMEM((tm, RET_WIDTH), BF16),
            pltpu.VMEM((tm, RET_WIDTH), F32),
            pltpu.VMEM((tm, RET_WIDTH), BF16),
            pltpu.VMEM((RET_HEADS, RET_DK, RET_DV), F32),
        ],
        compiler_params=pltpu.CompilerParams(
            dimension_semantics=("arbitrary",), vmem_limit_bytes=VMEM_LIMIT_BYTES),
        name="prompt_layer",
    )(sinks, x, mod, base, rtab, w_in_b, w_pa_b, w_pr_b, w_out_b, ln_g, ln_b, dmask, qdec, kdec)


def _sample_proj_kernel(x_ref, mod_ref, tab_ref, win_ref,
                        qkv_ref, ga_ref, ret_ref, rg_ref, mg_ref):
    rows = x_ref.shape[0]
    x = x_ref[...]
    shift = mod_ref[:, 0:D_MODEL]
    scale = mod_ref[:, D_MODEL:2 * D_MODEL]
    hb = (x * (1.0 + scale) + shift).astype(BF16)

    def proj(c0, c1):
        return _dot(hb, win_ref[:, c0:c1])

    lane = _lane_iota((rows, LANES))
    first_half32 = (lane & 32) == 0
    ca = tab_ref[0:1, :]
    sa = tab_ref[1:2, :]
    cr = tab_ref[2:3, :]
    sr = tab_ref[3:4, :]

    qkv = proj(C_AQ, C_AG)
    for t in range(ATT_WIDTH // LANES):
        qt = _rope_attn_tile(qkv[:, t * LANES:(t + 1) * LANES], ca, sa, first_half32)
        qkv_ref[:, t * LANES:(t + 1) * LANES] = qt * ATT_SCALE
    qkv_ref[:, C_AK:C_AV] = _rope_attn_tile(qkv[:, C_AK:C_AV], ca, sa, first_half32)
    qkv_ref[:, C_AV:C_AG] = qkv[:, C_AV:C_AG]
    ga_ref[...] = _silu(proj(C_AG, C_RQ))

    rqk = proj(C_RQ, C_RV)
    for h in range(RET_HEADS):
        sl = slice(h * RET_DK, (h + 1) * RET_DK)
        ret_ref[:, sl] = _rope_ret_tile(rqk[:, sl], cr, sr)
        sk = slice(RET_QK_WIDTH + h * RET_DK, RET_QK_WIDTH + (h + 1) * RET_DK)
        ret_ref[:, sk] = _rope_ret_tile(rqk[:, sk], cr, sr) * RET_K_SCALE
    ret_ref[:, 2 * RET_QK_WIDTH:] = proj(C_RV, C_RG)
    rg_ref[...] = _silu(proj(C_RG, C_MA))
    mg_ref[...] = _sigmoid(proj(C_MA, IN_COLS))


def _sample_proj(layer, x, mod, tab, w_in_b):
    rows = x.shape[0]
    widths = (C_AG, ATT_WIDTH, 2 * RET_QK_WIDTH + RET_WIDTH, RET_WIDTH, 2 * D_MODEL)
    return pl.pallas_call(
        _sample_proj_kernel,
        out_shape=tuple(jax.ShapeDtypeStruct((rows, w), F32) for w in widths),
        grid=(1,),
        in_specs=[
            _const_spec((rows, D_MODEL)),
            _layer_spec((rows, 3 * D_MODEL), layer),
            _const_spec((4, LANES)),
            _layer_spec((D_MODEL, IN_COLS), layer),
        ],
        out_specs=tuple(pl.BlockSpec((rows, w), lambda i: (0, 0)) for w in widths),
        compiler_params=pltpu.CompilerParams(
            dimension_semantics=("arbitrary",), vmem_limit_bytes=VMEM_LIMIT_BYTES),
        name="sample_proj",
    )(x, mod, tab, w_in_b)


def _sample_mix_kernel(sinks_ref, qkv_ref, ret_ref, ck_ref, cv_ref, st_ref, *rest, layer):
    oatt_ref, oret_ref, nk_ref, nv_ref, ns_ref = rest[-5:]
    nb = qkv_ref.shape[0]

    rowi = lax.broadcasted_iota(jnp.int32, (ATT_HEADS, LANES), 0)
    lanei = lax.broadcasted_iota(jnp.int32, (ATT_HEADS, LANES), 1)
    lane_group = lanei // ATT_HEAD_DIM
    sink_col = jnp.zeros((ATT_HEADS, 1), F32)
    rowc = lax.broadcasted_iota(jnp.int32, (ATT_HEADS, 1), 0)
    for h in range(ATT_HEADS):
        sink_col = jnp.where(rowc == h, sinks_ref[layer, h], sink_col)
    lo64_row = _lane_iota((1, LANES)) < 64
    dr = lax.broadcasted_iota(jnp.int32, (RET_DK, RET_DK), 0)
    dc = lax.broadcasted_iota(jnp.int32, (RET_DK, RET_DK), 1)
    diag = dr == dc


    qmats, logits = [], []
    for b in range(nb):
        qkv = qkv_ref[b]
        qmat = jnp.zeros((ATT_HEADS, LANES), F32)
        for h in range(ATT_HEADS):
            t, p, g = h // 2, h % 2, h // ATT_GROUP
            tile = qkv[:, t * LANES:(t + 1) * LANES]
            src = tile if p == g else pltpu.roll(tile, 64, 1)
            qmat = jnp.where((rowi == h) & (lane_group == g), jnp.broadcast_to(src, (ATT_HEADS, LANES)), qmat)
        qmats.append(qmat)
        logits.append(lax.dot_general(qmat.astype(BF16), ck_ref[b].astype(BF16), NT,
                                      preferred_element_type=F32))

    for b in range(nb):
        ret = ret_ref[b]
        for h in range(RET_HEADS):
            kh = ret[:, RET_QK_WIDTH + h * RET_DK:RET_QK_WIDTH + (h + 1) * RET_DK]
            vh = ret[:, 2 * RET_QK_WIDTH + h * RET_DV:2 * RET_QK_WIDTH + (h + 1) * RET_DV]
            kdiag = jnp.where(diag, jnp.broadcast_to(kh, (RET_DK, RET_DK)), 0.0).astype(BF16)
            vfull = jnp.broadcast_to(vh, (RET_DK, RET_DV)).astype(BF16)
            ns_ref[b, h] = st_ref[b, h] * TOKEN_DECAY[h] + _dot(kdiag, vfull)

    for b in range(nb):
        qkv = qkv_ref[b]
        k_new = qkv[:, C_AK:C_AV]
        v_new = qkv[:, C_AV:C_AG]
        s_c = logits[b]
        s_self = jnp.sum(qmats[b] * k_new, axis=-1, keepdims=True)
        m = jnp.maximum(jnp.maximum(jnp.max(s_c, axis=-1, keepdims=True), s_self), sink_col)
        p_c = jnp.exp(s_c - m)
        p_self = jnp.exp(s_self - m)
        denom = jnp.sum(p_c, axis=-1, keepdims=True) + p_self + jnp.exp(sink_col - m)
        o = (_dot(p_c.astype(BF16), cv_ref[b].astype(BF16)) + p_self * v_new) / denom
        o_sw = pltpu.roll(o, 64, 1)
        for t in range(ATT_WIDTH // LANES):
            g = t // 2
            first = (o if g == 0 else o_sw)[2 * t:2 * t + 1, :]
            second = (o_sw if g == 0 else o)[2 * t + 1:2 * t + 2, :]
            oatt_ref[b, :, t * LANES:(t + 1) * LANES] = jnp.where(lo64_row, first, second)

        nk_ref[b, 0:WINDOW - 1, :] = ck_ref[b, 1:WINDOW, :]
        nk_ref[b, WINDOW - 1:WINDOW, :] = k_new
        nv_ref[b, 0:WINDOW - 1, :] = cv_ref[b, 1:WINDOW, :]
        nv_ref[b, WINDOW - 1:WINDOW, :] = v_new

    for b in range(nb):
        ret = ret_ref[b]
        for h in range(RET_HEADS):
            qh = ret[:, h * RET_DK:(h + 1) * RET_DK]
            q8 = jnp.broadcast_to(qh, (SUBLANES, RET_DK)).astype(BF16)
            oh = _dot(q8, ns_ref[b, h].astype(BF16))[0:1, :]
            ms = jnp.mean(oh * oh, axis=-1, keepdims=True)
            oret_ref[b, :, h * RET_DV:(h + 1) * RET_DV] = oh * lax.rsqrt(ms + RMS_EPS)


def _sample_mix(layer, qkv, ret, cache_k, cache_v, state, sinks, prev):
    nbatch = qkv.shape[0]
    nb = SAMPLE_BLOCK
    win = cache_k.shape[2]
    smem = pl.BlockSpec(memory_space=pltpu.SMEM)
    any_spec = pl.BlockSpec(memory_space=pl.ANY)
    in_specs = [
        smem,
        pl.BlockSpec((nb, 1, C_AG), lambda i: (i, 0, 0)),
        pl.BlockSpec((nb, 1, 2 * RET_QK_WIDTH + RET_WIDTH), lambda i: (i, 0, 0)),
        pl.BlockSpec((None, nb, win, ATT_KV_WIDTH), lambda i: (layer, i, 0, 0)),
        pl.BlockSpec((None, nb, win, ATT_KV_WIDTH), lambda i: (layer, i, 0, 0)),
        pl.BlockSpec((None, nb, RET_HEADS, RET_DK, RET_DV), lambda i: (layer, i, 0, 0, 0)),
    ]
    args = [sinks, qkv.reshape(nbatch, 1, C_AG), ret.reshape(nbatch, 1, -1), cache_k, cache_v, state]
    aliases = {}
    if prev is not None:
        for j, arr in enumerate(prev):
            aliases[len(args)] = 2 + j
            args.append(arr)
            in_specs.append(any_spec)
    return pl.pallas_call(
        functools.partial(_sample_mix_kernel, layer=layer),
        out_shape=(
            jax.ShapeDtypeStruct((nbatch, 1, ATT_WIDTH), F32),
            jax.ShapeDtypeStruct((nbatch, 1, RET_WIDTH), F32),
            jax.ShapeDtypeStruct(cache_k.shape, F32),
            jax.ShapeDtypeStruct(cache_v.shape, F32),
            jax.ShapeDtypeStruct(state.shape, F32),
        ),
        grid=(nbatch // nb,),
        in_specs=in_specs,
        out_specs=(
            pl.BlockSpec((nb, 1, ATT_WIDTH), lambda i: (i, 0, 0)),
            pl.BlockSpec((nb, 1, RET_WIDTH), lambda i: (i, 0, 0)),
            pl.BlockSpec((None, nb, win, ATT_KV_WIDTH), lambda i: (layer, i, 0, 0)),
            pl.BlockSpec((None, nb, win, ATT_KV_WIDTH), lambda i: (layer, i, 0, 0)),
            pl.BlockSpec((None, nb, RET_HEADS, RET_DK, RET_DV), lambda i: (layer, i, 0, 0, 0)),
        ),
        input_output_aliases=aliases,
        compiler_params=pltpu.CompilerParams(
            dimension_semantics=("arbitrary",), vmem_limit_bytes=VMEM_LIMIT_BYTES),
        name="sample_mix",
    )(*args)


def _sample_out_kernel(x_ref, mod_ref, oatt_ref, ga_ref, oret_ref, rg_ref, mg_ref,
                       wpa_ref, wpr_ref, wout_ref, lng_ref, lnb_ref, y_ref):
    gate_c = mod_ref[:, 2 * D_MODEL:3 * D_MODEL]
    za = _dot((oatt_ref[...] * ga_ref[...]).astype(BF16), wpa_ref[...])
    zr = _dot((oret_ref[...] * rg_ref[...]).astype(BF16), wpr_ref[...])
    z = mg_ref[:, 0:D_MODEL] * za + mg_ref[:, D_MODEL:2 * D_MODEL] * zr
    u = _dot(z.astype(BF16), wout_ref[...])
    t = ALPHA * x_ref[...] + gate_c * u
    mu = jnp.mean(t, axis=-1, keepdims=True)
    d = t - mu
    var = jnp.mean(d * d, axis=-1, keepdims=True)
    y_ref[...] = d * lax.rsqrt(var + LN_EPS) * lng_ref[...] + lnb_ref[...]


def _sample_out(layer, x, mod, oatt, ga, oret, rg, mg, w_pa_b, w_pr_b, w_out_b, ln_g, ln_b):
    rows = x.shape[0]
    return pl.pallas_call(
        _sample_out_kernel,
        out_shape=jax.ShapeDtypeStruct(x.shape, F32),
        grid=(1,),
        in_specs=[
            _const_spec((rows, D_MODEL)),
            _layer_spec((rows, 3 * D_MODEL), layer),
            _const_spec((rows, ATT_WIDTH)), _const_spec((rows, ATT_WIDTH)),
            _const_spec((rows, RET_WIDTH)), _const_spec((rows, RET_WIDTH)),
            _const_spec((rows, 2 * D_MODEL)),
            _layer_spec((ATT_WIDTH, D_MODEL), layer),
            _layer_spec((RET_WIDTH, D_MODEL), layer),
            _layer_spec((D_MODEL, D_MODEL), layer),
            _layer_spec((1, D_MODEL), layer), _layer_spec((1, D_MODEL), layer),
        ],
        out_specs=pl.BlockSpec((rows, D_MODEL), lambda i: (0, 0)),
        compiler_params=pltpu.CompilerParams(
            dimension_semantics=("arbitrary",), vmem_limit_bytes=VMEM_LIMIT_BYTES),
        name="sample_out",
    )(x, mod, oatt, ga, oret, rg, mg, w_pa_b, w_pr_b, w_out_b, ln_g, ln_b)


def kernel(x_prompt, x_sample, c_prompt, c_sample, cache_k, cache_v, state_ret, w_in, attn_sinks,
           w_cond, b_cond, w_proj_attn, w_proj_ret, w_out, ln_g, ln_b):
    seq = x_prompt.shape[1]
    nbatch = x_sample.shape[0]
    win = cache_k.shape[2]
    assert seq % PROMPT_ROWS == 0 and nbatch % SAMPLE_BLOCK == 0 and nbatch % SUBLANES == 0

    w_in_b = w_in.astype(BF16)
    w_pa_b = w_proj_attn.astype(BF16)
    w_pr_b = w_proj_ret.astype(BF16)
    w_out_b = w_out.astype(BF16)
    ln_g3 = ln_g.reshape(DEPTH, 1, D_MODEL)
    ln_b3 = ln_b.reshape(DEPTH, 1, D_MODEL)

    c_all = jnp.concatenate([c_sample, c_prompt, jnp.zeros((SUBLANES - 1, D_MODEL), F32)], axis=0)
    mod = _cond_call(c_all, w_cond, b_cond)
    prompt_mod_block = nbatch // SUBLANES

    base, rtab = _prompt_rope_tables(seq, PROMPT_ROWS)
    stab = _sample_rope_table(float(PAST_LEN))
    ret_tabs = _retention_tables()

    ck = cache_k.reshape(DEPTH, nbatch, win, ATT_KV_WIDTH)
    cv = cache_v.reshape(DEPTH, nbatch, win, ATT_KV_WIDTH)

    yp = x_prompt[0]
    ys = x_sample[:, 0, :]
    kp, vp, sp = [], [], []
    prev = None
    for l in range(DEPTH):
        yp, k_new, v_new, s_new = _prompt_layer(
            l, yp, mod, prompt_mod_block, base, rtab, w_in_b, w_pa_b, w_pr_b, w_out_b,
            ln_g3, ln_b3, attn_sinks, ret_tabs)
        kp.append(k_new.reshape(1, WINDOW, ATT_KV_HEADS, ATT_HEAD_DIM))
        vp.append(v_new.reshape(1, WINDOW, ATT_KV_HEADS, ATT_HEAD_DIM))
        sp.append(s_new[None])

        qkv, ga, ret, rg, mg = _sample_proj(l, ys, mod, stab, w_in_b)
        oatt, oret, nk, nv, ns = _sample_mix(l, qkv, ret, ck, cv, state_ret, attn_sinks, prev)
        prev = (nk, nv, ns)
        ys = _sample_out(l, ys, mod, oatt.reshape(nbatch, ATT_WIDTH), ga, oret.reshape(nbatch, RET_WIDTH),
                         rg, mg, w_pa_b, w_pr_b, w_out_b, ln_g3, ln_b3)

    nk, nv, ns = prev
    return (yp[None], ys[:, None, :], jnp.stack(kp), jnp.stack(vp), jnp.stack(sp),
            nk.reshape(cache_k.shape), nv.reshape(cache_v.shape), ns)
```

```python
import functools

import jax
import jax.numpy as jnp
import numpy as np
from jax import lax
from jax.experimental import pallas as pl
from jax.experimental.pallas import tpu as pltpu

D_MODEL = 1024
DEPTH = 2
PAST_LEN = 16384
ATT_HEADS = 8
ATT_KV_HEADS = 2
ATT_HEAD_DIM = 64
ATT_GROUP = ATT_HEADS // ATT_KV_HEADS
ATT_WIDTH = ATT_HEADS * ATT_HEAD_DIM
ATT_KV_WIDTH = ATT_KV_HEADS * ATT_HEAD_DIM
WINDOW = 128
RET_HEADS = 4
RET_DK = 128
RET_DV = 256
RET_QK_WIDTH = RET_HEADS * RET_DK
RET_WIDTH = RET_HEADS * RET_DV
RET_CHUNK = 128
ROPE_THETA = 10000.0
ALPHA = (2.0 * DEPTH) ** 0.25
LN_EPS = 1e-5
RMS_EPS = 1e-6
ATT_SCALE = ATT_HEAD_DIM ** -0.5
RET_K_SCALE = RET_DK ** -0.5

C_AQ = 0
C_AK = C_AQ + ATT_WIDTH
C_AV = C_AK + ATT_KV_WIDTH
C_AG = C_AV + ATT_KV_WIDTH
C_RQ = C_AG + ATT_WIDTH
C_RK = C_RQ + RET_QK_WIDTH
C_RV = C_RK + RET_QK_WIDTH
C_RG = C_RV + RET_WIDTH
C_MA = C_RG + RET_WIDTH
C_MR = C_MA + D_MODEL
IN_COLS = C_MR + D_MODEL

LANES = 128
SUBLANES = 8
VMEM_LIMIT_BYTES = 56 * 1024 * 1024

PROMPT_ROWS = 256
SAMPLE_BLOCK = 8

BF16 = jnp.bfloat16
F32 = jnp.float32
NT = (((1,), (1,)), ((), ()))

_LOG_GAMMA = np.log(1.0 - 2.0 ** (-5.0 - np.arange(RET_HEADS, dtype=np.float64)))
CHUNK_DECAY = tuple(float(v) for v in np.exp(RET_CHUNK * _LOG_GAMMA))
TOKEN_DECAY = tuple(float(v) for v in np.exp(_LOG_GAMMA))


def _sigmoid(x):
    return 0.5 * jnp.tanh(0.5 * x) + 0.5


def _silu(x):
    return x * _sigmoid(x)


def _dot(a, b):
    return jnp.dot(a, b, preferred_element_type=F32)


def _rope_attn_tile(x, cos, sin_signed, first_half):
    rot = jnp.where(first_half, pltpu.roll(x, LANES - 32, 1), pltpu.roll(x, 32, 1))
    return x * cos + rot * sin_signed


def _rope_ret_tile(x, cos, sin_signed):
    return x * cos + pltpu.roll(x, 64, 1) * sin_signed


def _lane_iota(shape):
    return lax.broadcasted_iota(jnp.int32, shape, len(shape) - 1)


def _layer_spec(shape, layer):
    nd = len(shape)
    return pl.BlockSpec((None,) + tuple(shape), lambda i: (layer,) + (0,) * nd,
                        pipeline_mode=pl.Buffered(1))


def _const_spec(shape):
    nd = len(shape)
    return pl.BlockSpec(tuple(shape), lambda i: (0,) * nd, pipeline_mode=pl.Buffered(1))


def _rope_lane_patterns():
    lane = np.arange(LANES)
    f_att = ROPE_THETA ** (-(lane % 32) / 32.0)
    s_att = np.where(lane % 64 < 32, -1.0, 1.0)
    f_ret = ROPE_THETA ** (-(lane % 64) / 64.0)
    s_ret = np.where(lane < 64, -1.0, 1.0)
    return (f_att, s_att), (f_ret, s_ret)


def _prompt_rope_tables(seq, tm):
    starts = np.arange(seq // tm, dtype=np.float64)[:, None] * tm
    offs = np.arange(tm, dtype=np.float64)[:, None]
    base, within = [], []
    for freq, sign in _rope_lane_patterns():
        base += [np.cos(starts * freq), np.sin(starts * freq)]
        c, s = np.cos(offs * freq), np.sin(offs * freq)
        within += [c, s, sign * c, sign * s]
    return (jnp.asarray(np.stack(base, axis=1), F32),
            jnp.asarray(np.stack(within, axis=0), F32))


def _sample_rope_table(pos):
    rows = []
    for freq, sign in _rope_lane_patterns():
        rows += [np.cos(pos * freq), sign * np.sin(pos * freq)]
    return jnp.asarray(np.stack(rows, axis=0), F32)


def _retention_tables():
    c = RET_CHUNK
    idx = np.arange(c, dtype=np.float64)
    diff = idx[:, None] - idx[None, :]
    lg = _LOG_GAMMA[:, None, None]
    dmask = np.where(diff >= 0, np.exp(np.maximum(diff, 0.0)[None] * lg), 0.0)
    qdec = np.broadcast_to(np.exp((idx + 1.0)[None, :, None] * lg), (RET_HEADS, c, RET_DK))
    kdec = np.broadcast_to(np.exp((c - 1.0 - idx)[None, :, None] * lg), (RET_HEADS, c, RET_DK))
    return jnp.asarray(dmask, F32), jnp.asarray(qdec, F32), jnp.asarray(kdec, F32)


def _cond_kernel(c_ref, w_ref, b_ref, o_ref):
    a = _silu(c_ref[...]).astype(BF16)
    o_ref[...] = _dot(a, w_ref[...].astype(BF16)) + b_ref[...]


def _cond_call(c_all, w_cond, b_cond):
    rows = c_all.shape[0]
    tn = 768
    return pl.pallas_call(
        _cond_kernel,
        out_shape=jax.ShapeDtypeStruct((DEPTH, rows, 3 * D_MODEL), F32),
        grid=(DEPTH, 3 * D_MODEL // tn),
        in_specs=[
            pl.BlockSpec((rows, D_MODEL), lambda l, j: (0, 0)),
            pl.BlockSpec((None, D_MODEL, tn), lambda l, j: (l, 0, j)),
            pl.BlockSpec((None, 1, tn), lambda l, j: (l, 0, j)),
        ],
        out_specs=pl.BlockSpec((None, rows, tn), lambda l, j: (l, 0, j)),
        compiler_params=pltpu.CompilerParams(
            dimension_semantics=("arbitrary", "arbitrary"), vmem_limit_bytes=VMEM_LIMIT_BYTES),
        name="cond_mod",
    )(c_all, w_cond, b_cond.reshape(DEPTH, 1, 3 * D_MODEL))


def _prompt_kernel(sinks_ref, x_ref, mod_ref, base_ref, rtab_ref,
                   win_ref, wpa_ref, wpr_ref, wout_ref, lng_ref, lnb_ref,
                   dmask_ref, qdec_ref, kdec_ref,
                   y_ref, knew_ref, vnew_ref, snew_ref,
                   tab_scr, hb_scr, q_scr, kvar_scr, vvar_scr, ga_scr, a_scr,
                   rq_scr, rk_scr, rv_scr, rg_scr, r_scr, s_scr,
                   ma_scr, mr_scr, z_scr, zr_scr, *, layer):
    step = pl.program_id(0)
    tm = x_ref.shape[0]
    nsub = tm // WINDOW

    @pl.when(step == 0)
    def _():
        kvar_scr[:, 0:WINDOW, :] = jnp.zeros((4, WINDOW, LANES), BF16)
        vvar_scr[:, 0:WINDOW, :] = jnp.zeros((4, WINDOW, LANES), BF16)
        s_scr[...] = jnp.zeros(s_scr.shape, F32)

    for fam in range(2):
        cb = base_ref[2 * fam:2 * fam + 1, :]
        sb = base_ref[2 * fam + 1:2 * fam + 2, :]
        tab_scr[2 * fam] = cb * rtab_ref[4 * fam] - sb * rtab_ref[4 * fam + 1]
        tab_scr[2 * fam + 1] = sb * rtab_ref[4 * fam + 2] + cb * rtab_ref[4 * fam + 3]

    x = x_ref[...]
    shift = mod_ref[0:1, 0:D_MODEL]
    scale = mod_ref[0:1, D_MODEL:2 * D_MODEL]
    hb_scr[...] = (x * (1.0 + scale) + shift).astype(BF16)

    def proj(c0, c1):
        return _dot(hb_scr[...], win_ref[:, c0:c1])

    lane = _lane_iota((tm, LANES))
    first_half32 = (lane & 32) == 0
    lo64 = lane < 64

    def rope_attn(t):
        return _rope_attn_tile(t, tab_scr[0], tab_scr[1], first_half32)

    def rope_ret(t):
        return _rope_ret_tile(t, tab_scr[2], tab_scr[3])

    kv = proj(C_AK, C_AG)
    k_rot = rope_attn(kv[:, 0:LANES])
    v_raw = kv[:, LANES:2 * LANES]
    knew_ref[...] = k_rot[tm - WINDOW:, :]
    vnew_ref[...] = v_raw[tm - WINDOW:, :]

    def store_variants(scr, t):
        swapped = pltpu.roll(t, 64, 1)
        zero = jnp.zeros_like(t)
        scr[0, WINDOW:WINDOW + tm, :] = jnp.where(lo64, t, zero).astype(BF16)
        scr[1, WINDOW:WINDOW + tm, :] = jnp.where(lo64, zero, swapped).astype(BF16)
        scr[2, WINDOW:WINDOW + tm, :] = jnp.where(lo64, swapped, zero).astype(BF16)
        scr[3, WINDOW:WINDOW + tm, :] = jnp.where(lo64, zero, t).astype(BF16)

    store_variants(kvar_scr, k_rot)
    store_variants(vvar_scr, v_raw)

    qp = proj(C_AQ, C_AK)
    for t in range(ATT_WIDTH // LANES):
        qt = rope_attn(qp[:, t * LANES:(t + 1) * LANES])
        q_scr[:, t * LANES:(t + 1) * LANES] = (qt * ATT_SCALE).astype(BF16)
    ga_scr[...] = _silu(proj(C_AG, C_RQ))

    def chunk_rq():
        p = proj(C_RQ, C_RK)
        for h in range(RET_HEADS):
            sl = slice(h * RET_DK, (h + 1) * RET_DK)
            rq_scr[:, sl] = rope_ret(p[:, sl])

    def chunk_rk():
        p = proj(C_RK, C_RV)
        for h in range(RET_HEADS):
            sl = slice(h * RET_DK, (h + 1) * RET_DK)
            rk_scr[:, sl] = rope_ret(p[:, sl]) * RET_K_SCALE

    def half_chunk(c0, dst, fn, half):
        w = dst.shape[1] // 2
        def run():
            dst[:, half * w:(half + 1) * w] = fn(proj(c0 + half * w, c0 + (half + 1) * w))
        return run

    to_bf16 = lambda v: v.astype(BF16)
    att_companions = [
        chunk_rq, chunk_rk,
        half_chunk(C_RV, rv_scr, to_bf16, 0), half_chunk(C_RV, rv_scr, to_bf16, 1),
        half_chunk(C_RG, rg_scr, _silu, 0), half_chunk(C_RG, rg_scr, _silu, 1),
        half_chunk(C_MA, ma_scr, _sigmoid, 0), half_chunk(C_MA, ma_scr, _sigmoid, 1),
    ]

    row = lax.broadcasted_iota(jnp.int32, (WINDOW, 2 * WINDOW), 0)
    col = lax.broadcasted_iota(jnp.int32, (WINDOW, 2 * WINDOW), 1)
    in_window = col <= row + WINDOW
    mask_std = (col >= row) & in_window
    off = jnp.where(step > 0, 0, 4 * WINDOW)
    mask_first = ((col >= row + off) | (col >= WINDOW)) & in_window
    lo64_w = _lane_iota((WINDOW, LANES)) < 64

    def attention_logits(i, t):
        r0 = i * WINDOW
        g = t // 2
        qt = q_scr[r0:r0 + WINDOW, t * LANES:(t + 1) * LANES]
        return [lax.dot_general(qt, kvar_scr[2 * g + p, r0:r0 + 2 * WINDOW, :], NT,
                                preferred_element_type=F32) for p in range(2)]

    def attention_values(i, t, logits):
        r0 = i * WINDOW
        mask = mask_first if i == 0 else mask_std
        g = t // 2
        o_acc = None
        recips = []
        for p in range(2):
            head = 2 * t + p
            s = jnp.where(mask, logits[p], -jnp.inf)
            sink = sinks_ref[layer, head]
            m = jnp.maximum(jnp.max(s, axis=-1, keepdims=True), sink)
            pe = jnp.exp(s - m)
            denom = jnp.sum(pe, axis=-1, keepdims=True) + jnp.exp(sink - m)
            recips.append(1.0 / denom)
            vals = vvar_scr[2 * g + p, r0:r0 + 2 * WINDOW, :]
            o = _dot(pe.astype(BF16), vals)
            o_acc = o if o_acc is None else o_acc + o
        o_t = o_acc * jnp.where(lo64_w, recips[0], recips[1])
        gt = ga_scr[r0:r0 + WINDOW, t * LANES:(t + 1) * LANES]
        a_scr[r0:r0 + WINDOW, t * LANES:(t + 1) * LANES] = (o_t * gt).astype(BF16)

    n = 0
    for i in range(nsub):
        for t in range(ATT_WIDTH // LANES):
            logits = attention_logits(i, t)
            if n < len(att_companions):
                att_companions[n]()
            n += 1
            attention_values(i, t, logits)
    for run in att_companions[n:]:
        run()

    kvar_scr[:, 0:WINDOW, :] = kvar_scr[:, tm:tm + WINDOW, :]
    vvar_scr[:, 0:WINDOW, :] = vvar_scr[:, tm:tm + WINDOW, :]

    def ret_slices(c, h):
        rows = slice(c * RET_CHUNK, (c + 1) * RET_CHUNK)
        return rows, slice(h * RET_DK, (h + 1) * RET_DK), slice(h * RET_DV, (h + 1) * RET_DV)

    def retention_scores_and_state(c):
        inners, s_olds = [], []
        for h in range(RET_HEADS):
            rows, sk, sv = ret_slices(c, h)
            qh = rq_scr[rows, sk]
            kh = rk_scr[rows, sk]
            vh = rv_scr[rows, sv]
            inners.append(lax.dot_general(qh.astype(BF16), kh.astype(BF16), NT, preferred_element_type=F32))
            s_old = s_scr[h]
            s_olds.append(s_old.astype(BF16))
            kd = (kh * kdec_ref[h]).astype(BF16)
            s_scr[h] = s_old * CHUNK_DECAY[h] + lax.dot_general(
                kd, vh, (((0,), (0,)), ((), ())), preferred_element_type=F32)
        return inners, s_olds

    def retention_outputs(c, inners, s_olds):
        for h in range(RET_HEADS):
            rows, sk, sv = ret_slices(c, h)
            qh = rq_scr[rows, sk]
            lhs = jnp.concatenate([(inners[h] * dmask_ref[h]).astype(BF16),
                                   (qh * qdec_ref[h]).astype(BF16)], axis=1)
            rhs = jnp.concatenate([rv_scr[rows, sv], s_olds[h]], axis=0)
            o = _dot(lhs, rhs)
            ms = jnp.mean(o * o, axis=-1, keepdims=True)
            on = o * lax.rsqrt(ms + RMS_EPS)
            r_scr[rows, sv] = (on * rg_scr[rows, sv]).astype(BF16)

    def chunk_za(half):
        w = D_MODEL // 2
        def run():
            sl = slice(half * w, (half + 1) * w)
            z_scr[:, sl] = ma_scr[:, sl] * _dot(a_scr[...], wpa_ref[:, sl])
        return run

    ret_companions = [half_chunk(C_MR, mr_scr, _sigmoid, 0), half_chunk(C_MR, mr_scr, _sigmoid, 1)]
    n = 0
    for c in range(nsub):
        inners, s_olds = retention_scores_and_state(c)
        if n < len(ret_companions):
            ret_companions[n]()
        n += 1
        retention_outputs(c, inners, s_olds)
    for run in ret_companions[n:]:
        run()
    chunk_za(0)()
    chunk_za(1)()
    zr_scr[...] = _dot(r_scr[...], wpr_ref[...])

    @pl.when(step == pl.num_programs(0) - 1)
    def _():
        snew_ref[...] = s_scr[...]

    gate_c = mod_ref[0:1, 2 * D_MODEL:3 * D_MODEL]
    lng = lng_ref[...]
    lnb = lnb_ref[...]
    for i in range(nsub):
        rows = slice(i * WINDOW, (i + 1) * WINDOW)
        z = z_scr[rows, :] + mr_scr[rows, :] * zr_scr[rows, :]
        u = _dot(z.astype(BF16), wout_ref[...])
        t = ALPHA * x_ref[rows, :] + gate_c * u
        mu = jnp.mean(t, axis=-1, keepdims=True)
        d = t - mu
        var = jnp.mean(d * d, axis=-1, keepdims=True)
        y_ref[rows, :] = d * lax.rsqrt(var + LN_EPS) * lng + lnb


def _prompt_layer(layer, x, mod, mod_row_block, base, rtab, w_in_b, w_pa_b, w_pr_b, w_out_b,
                  ln_g, ln_b, sinks, ret_tabs):
    seq = x.shape[0]
    tm = PROMPT_ROWS
    dmask, qdec, kdec = ret_tabs
    smem = pl.BlockSpec(memory_space=pltpu.SMEM)
    row_spec = lambda w: pl.BlockSpec((tm, w), lambda i: (i, 0))
    return pl.pallas_call(
        functools.partial(_prompt_kernel, layer=layer),
        out_shape=(
            jax.ShapeDtypeStruct((seq, D_MODEL), F32),
            jax.ShapeDtypeStruct((WINDOW, ATT_KV_WIDTH), F32),
            jax.ShapeDtypeStruct((WINDOW, ATT_KV_WIDTH), F32),
            jax.ShapeDtypeStruct((RET_HEADS, RET_DK, RET_DV), F32),
        ),
        grid=(seq // tm,),
        in_specs=[
            smem,
            row_spec(D_MODEL),
            pl.BlockSpec((None, SUBLANES, 3 * D_MODEL), lambda i: (layer, mod_row_block, 0),
                         pipeline_mode=pl.Buffered(1)),
            pl.BlockSpec((None, 4, LANES), lambda i: (i, 0, 0)),
            _const_spec((8, tm, LANES)),
            _layer_spec((D_MODEL, IN_COLS), layer),
            _layer_spec((ATT_WIDTH, D_MODEL), layer),
            _layer_spec((RET_WIDTH, D_MODEL), layer),
            _layer_spec((D_MODEL, D_MODEL), layer),
            _layer_spec((1, D_MODEL), layer), _layer_spec((1, D_MODEL), layer),
            _const_spec((RET_HEADS, RET_CHUNK, RET_CHUNK)),
            _const_spec((RET_HEADS, RET_CHUNK, RET_DK)),
            _const_spec((RET_HEADS, RET_CHUNK, RET_DK)),
        ],
        out_specs=(
            row_spec(D_MODEL),
            pl.BlockSpec((WINDOW, ATT_KV_WIDTH), lambda i: (0, 0)),
            pl.BlockSpec((WINDOW, ATT_KV_WIDTH), lambda i: (0, 0)),
            pl.BlockSpec((RET_HEADS, RET_DK, RET_DV), lambda i: (0, 0, 0)),
        ),
        scratch_shapes=[
            pltpu.VMEM((4, tm, LANES), F32),
            pltpu.VMEM((tm, D_MODEL), BF16),
            pltpu.VMEM((tm, ATT_WIDTH), BF16),
            pltpu.VMEM((4, WINDOW + tm, LANES), BF16),
            pltpu.VMEM((4, WINDOW + tm, LANES), BF16),
            pltpu.VMEM((tm, ATT_WIDTH), F32),
            pltpu.VMEM((tm, ATT_WIDTH), BF16),
            pltpu.VMEM((tm, RET_QK_WIDTH), F32),
            pltpu.VMEM((tm, RET_QK_WIDTH), F32),
            pltpu.VMEM((tm, RET_WIDTH), BF16),
            pltpu.VMEM((tm, RET_WIDTH), F32),
            pltpu.VMEM((tm, RET_WIDTH), BF16),
            pltpu.VMEM((RET_HEADS, RET_DK, RET_DV), F32),
            pltpu.VMEM((tm, D_MODEL), F32),
            pltpu.VMEM((tm, D_MODEL), F32),
            pltpu.VMEM((tm, D_MODEL), F32),
            pltpu.VMEM((tm, D_MODEL), F32),
        ],
        compiler_params=pltpu.CompilerParams(
            dimension_semantics=("arbitrary",), vmem_limit_bytes=VMEM_LIMIT_BYTES),
        name="prompt_layer",
    )(sinks, x, mod, base, rtab, w_in_b, w_pa_b, w_pr_b, w_out_b, ln_g, ln_b, dmask, qdec, kdec)


def _sample_proj_kernel(x_ref, mod_ref, tab_ref, win_ref,
                        qkv_ref, ga_ref, ret_ref, rg_ref, mg_ref):
    rows = x_ref.shape[0]
    x = x_ref[...]
    shift = mod_ref[:, 0:D_MODEL]
    scale = mod_ref[:, D_MODEL:2 * D_MODEL]
    hb = (x * (1.0 + scale) + shift).astype(BF16)

    def proj(c0, c1):
        return _dot(hb, win_ref[:, c0:c1])

    lane = _lane_iota((rows, LANES))
    first_half32 = (lane & 32) == 0
    ca = tab_ref[0:1, :]
    sa = tab_ref[1:2, :]
    cr = tab_ref[2:3, :]
    sr = tab_ref[3:4, :]

    qkv = proj(C_AQ, C_AG)
    for t in range(ATT_WIDTH // LANES):
        qt = _rope_attn_tile(qkv[:, t * LANES:(t + 1) * LANES], ca, sa, first_half32)
        qkv_ref[:, t * LANES:(t + 1) * LANES] = qt * ATT_SCALE
    qkv_ref[:, C_AK:C_AV] = _rope_attn_tile(qkv[:, C_AK:C_AV], ca, sa, first_half32)
    qkv_ref[:, C_AV:C_AG] = qkv[:, C_AV:C_AG]
    ga_ref[...] = _silu(proj(C_AG, C_RQ))

    rqk = proj(C_RQ, C_RV)
    for h in range(RET_HEADS):
        sl = slice(h * RET_DK, (h + 1) * RET_DK)
        ret_ref[:, sl] = _rope_ret_tile(rqk[:, sl], cr, sr)
        sk = slice(RET_QK_WIDTH + h * RET_DK, RET_QK_WIDTH + (h + 1) * RET_DK)
        ret_ref[:, sk] = _rope_ret_tile(rqk[:, sk], cr, sr) * RET_K_SCALE
    ret_ref[:, 2 * RET_QK_WIDTH:] = proj(C_RV, C_RG)
    rg_ref[...] = _silu(proj(C_RG, C_MA))
    mg_ref[...] = _sigmoid(proj(C_MA, IN_COLS))


def _sample_proj(layer, x, mod, tab, w_in_b):
    rows = x.shape[0]
    widths = (C_AG, ATT_WIDTH, 2 * RET_QK_WIDTH + RET_WIDTH, RET_WIDTH, 2 * D_MODEL)
    return pl.pallas_call(
        _sample_proj_kernel,
        out_shape=tuple(jax.ShapeDtypeStruct((rows, w), F32) for w in widths),
        grid=(1,),
        in_specs=[
            _const_spec((rows, D_MODEL)),
            _layer_spec((rows, 3 * D_MODEL), layer),
            _const_spec((4, LANES)),
            _layer_spec((D_MODEL, IN_COLS), layer),
        ],
        out_specs=tuple(pl.BlockSpec((rows, w), lambda i: (0, 0)) for w in widths),
        compiler_params=pltpu.CompilerParams(
            dimension_semantics=("arbitrary",), vmem_limit_bytes=VMEM_LIMIT_BYTES),
        name="sample_proj",
    )(x, mod, tab, w_in_b)


def _sample_mix_kernel(sinks_ref, qkv_ref, ret_ref, ck_ref, cv_ref, st_ref, *rest, layer):
    oatt_ref, oret_ref, nk_ref, nv_ref, ns_ref = rest[-5:]
    nb = qkv_ref.shape[0]

    rowi = lax.broadcasted_iota(jnp.int32, (ATT_HEADS, LANES), 0)
    lanei = lax.broadcasted_iota(jnp.int32, (ATT_HEADS, LANES), 1)
    lane_group = lanei // ATT_HEAD_DIM
    sink_col = jnp.zeros((ATT_HEADS, 1), F32)
    rowc = lax.broadcasted_iota(jnp.int32, (ATT_HEADS, 1), 0)
    for h in range(ATT_HEADS):
        sink_col = jnp.where(rowc == h, sinks_ref[layer, h], sink_col)
    lo64_row = _lane_iota((1, LANES)) < 64
    dr = lax.broadcasted_iota(jnp.int32, (RET_DK, RET_DK), 0)
    dc = lax.broadcasted_iota(jnp.int32, (RET_DK, RET_DK), 1)
    diag = dr == dc


    qmats, logits = [], []
    for b in range(nb):
        qkv = qkv_ref[b]
        qmat = jnp.zeros((ATT_HEADS, LANES), F32)
        for h in range(ATT_HEADS):
            t, p, g = h // 2, h % 2, h // ATT_GROUP
            tile = qkv[:, t * LANES:(t + 1) * LANES]
            src = tile if p == g else pltpu.roll(tile, 64, 1)
            qmat = jnp.where((rowi == h) & (lane_group == g), jnp.broadcast_to(src, (ATT_HEADS, LANES)), qmat)
        qmats.append(qmat)
        logits.append(lax.dot_general(qmat.astype(BF16), ck_ref[b].astype(BF16), NT,
                                      preferred_element_type=F32))

    for b in range(nb):
        ret = ret_ref[b]
        for h in range(RET_HEADS):
            kh = ret[:, RET_QK_WIDTH + h * RET_DK:RET_QK_WIDTH + (h + 1) * RET_DK]
            vh = ret[:, 2 * RET_QK_WIDTH + h * RET_DV:2 * RET_QK_WIDTH + (h + 1) * RET_DV]
            kdiag = jnp.where(diag, jnp.broadcast_to(kh, (RET_DK, RET_DK)), 0.0).astype(BF16)
            vfull = jnp.broadcast_to(vh, (RET_DK, RET_DV)).astype(BF16)
            ns_ref[b, h] = st_ref[b, h] * TOKEN_DECAY[h] + _dot(kdiag, vfull)

    for b in range(nb):
        qkv = qkv_ref[b]
        k_new = qkv[:, C_AK:C_AV]
        v_new = qkv[:, C_AV:C_AG]
        s_c = logits[b]
        s_self = jnp.sum(qmats[b] * k_new, axis=-1, keepdims=True)
        m = jnp.maximum(jnp.maximum(jnp.max(s_c, axis=-1, keepdims=True), s_self), sink_col)
        p_c = jnp.exp(s_c - m)
        p_self = jnp.exp(s_self - m)
        denom = jnp.sum(p_c, axis=-1, keepdims=True) + p_self + jnp.exp(sink_col - m)
        o = (_dot(p_c.astype(BF16), cv_ref[b].astype(BF16)) + p_self * v_new) / denom
        o_sw = pltpu.roll(o, 64, 1)
        for t in range(ATT_WIDTH // LANES):
            g = t // 2
            first = (o if g == 0 else o_sw)[2 * t:2 * t + 1, :]
            second = (o_sw if g == 0 else o)[2 * t + 1:2 * t + 2, :]
            oatt_ref[b, :, t * LANES:(t + 1) * LANES] = jnp.where(lo64_row, first, second)

        nk_ref[b, 0:WINDOW - 1, :] = ck_ref[b, 1:WINDOW, :]
        nk_ref[b, WINDOW - 1:WINDOW, :] = k_new
        nv_ref[b, 0:WINDOW - 1, :] = cv_ref[b, 1:WINDOW, :]
        nv_ref[b, WINDOW - 1:WINDOW, :] = v_new

    for b in range(nb):
        ret = ret_ref[b]
        for h in range(RET_HEADS):
            qh = ret[:, h * RET_DK:(h + 1) * RET_DK]
            q8 = jnp.broadcast_to(qh, (SUBLANES, RET_DK)).astype(BF16)
            oh = _dot(q8, ns_ref[b, h].astype(BF16))[0:1, :]
            ms = jnp.mean(oh * oh, axis=-1, keepdims=True)
            oret_ref[b, :, h * RET_DV:(h + 1) * RET_DV] = oh * lax.rsqrt(ms + RMS_EPS)


def _sample_mix(layer, qkv, ret, cache_k, cache_v, state, sinks, prev):
    nbatch = qkv.shape[0]
    nb = SAMPLE_BLOCK
    win = cache_k.shape[2]
    smem = pl.BlockSpec(memory_space=pltpu.SMEM)
    any_spec = pl.BlockSpec(memory_space=pl.ANY)
    in_specs = [
        smem,
        pl.BlockSpec((nb, 1, C_AG), lambda i: (i, 0, 0)),
        pl.BlockSpec((nb, 1, 2 * RET_QK_WIDTH + RET_WIDTH), lambda i: (i, 0, 0)),
        pl.BlockSpec((None, nb, win, ATT_KV_WIDTH), lambda i: (layer, i, 0, 0)),
        pl.BlockSpec((None, nb, win, ATT_KV_WIDTH), lambda i: (layer, i, 0, 0)),
        pl.BlockSpec((None, nb, RET_HEADS, RET_DK, RET_DV), lambda i: (layer, i, 0, 0, 0)),
    ]
    args = [sinks, qkv.reshape(nbatch, 1, C_AG), ret.reshape(nbatch, 1, -1), cache_k, cache_v, state]
    aliases = {}
    if prev is not None:
        for j, arr in enumerate(prev):
            aliases[len(args)] = 2 + j
            args.append(arr)
            in_specs.append(any_spec)
    return pl.pallas_call(
        functools.partial(_sample_mix_kernel, layer=layer),
        out_shape=(
            jax.ShapeDtypeStruct((nbatch, 1, ATT_WIDTH), F32),
            jax.ShapeDtypeStruct((nbatch, 1, RET_WIDTH), F32),
            jax.ShapeDtypeStruct(cache_k.shape, F32),
            jax.ShapeDtypeStruct(cache_v.shape, F32),
            jax.ShapeDtypeStruct(state.shape, F32),
        ),
        grid=(nbatch // nb,),
        in_specs=in_specs,
        out_specs=(
            pl.BlockSpec((nb, 1, ATT_WIDTH), lambda i: (i, 0, 0)),
            pl.BlockSpec((nb, 1, RET_WIDTH), lambda i: (i, 0, 0)),
            pl.BlockSpec((None, nb, win, ATT_KV_WIDTH), lambda i: (layer, i, 0, 0)),
            pl.BlockSpec((None, nb, win, ATT_KV_WIDTH), lambda i: (layer, i, 0, 0)),
            pl.BlockSpec((None, nb, RET_HEADS, RET_DK, RET_DV), lambda i: (layer, i, 0, 0, 0)),
        ),
        input_output_aliases=aliases,
        compiler_params=pltpu.CompilerParams(
            dimension_semantics=("arbitrary",), vmem_limit_bytes=VMEM_LIMIT_BYTES),
        name="sample_mix",
    )(*args)


def _sample_out_kernel(x_ref, mod_ref, oatt_ref, ga_ref, oret_ref, rg_ref, mg_ref,
                       wpa_ref, wpr_ref, wout_ref, lng_ref, lnb_ref, y_ref):
    gate_c = mod_ref[:, 2 * D_MODEL:3 * D_MODEL]
    za = _dot((oatt_ref[...] * ga_ref[...]).astype(BF16), wpa_ref[...])
    zr = _dot((oret_ref[...] * rg_ref[...]).astype(BF16), wpr_ref[...])
    z = mg_ref[:, 0:D_MODEL] * za + mg_ref[:, D_MODEL:2 * D_MODEL] * zr
    u = _dot(z.astype(BF16), wout_ref[...])
    t = ALPHA * x_ref[...] + gate_c * u
    mu = jnp.mean(t, axis=-1, keepdims=True)
    d = t - mu
    var = jnp.mean(d * d, axis=-1, keepdims=True)
    y_ref[...] = d * lax.rsqrt(var + LN_EPS) * lng_ref[...] + lnb_ref[...]


def _sample_out(layer, x, mod, oatt, ga, oret, rg, mg, w_pa_b, w_pr_b, w_out_b, ln_g, ln_b):
    rows = x.shape[0]
    return pl.pallas_call(
        _sample_out_kernel,
        out_shape=jax.ShapeDtypeStruct(x.shape, F32),
        grid=(1,),
        in_specs=[
            _const_spec((rows, D_MODEL)),
            _layer_spec((rows, 3 * D_MODEL), layer),
            _const_spec((rows, ATT_WIDTH)), _const_spec((rows, ATT_WIDTH)),
            _const_spec((rows, RET_WIDTH)), _const_spec((rows, RET_WIDTH)),
            _const_spec((rows, 2 * D_MODEL)),
            _layer_spec((ATT_WIDTH, D_MODEL), layer),
            _layer_spec((RET_WIDTH, D_MODEL), layer),
            _layer_spec((D_MODEL, D_MODEL), layer),
            _layer_spec((1, D_MODEL), layer), _layer_spec((1, D_MODEL), layer),
        ],
        out_specs=pl.BlockSpec((rows, D_MODEL), lambda i: (0, 0)),
        compiler_params=pltpu.CompilerParams(
            dimension_semantics=("arbitrary",), vmem_limit_bytes=VMEM_LIMIT_BYTES),
        name="sample_out",
    )(x, mod, oatt, ga, oret, rg, mg, w_pa_b, w_pr_b, w_out_b, ln_g, ln_b)


def kernel(x_prompt, x_sample, c_prompt, c_sample, cache_k, cache_v, state_ret, w_in, attn_sinks,
           w_cond, b_cond, w_proj_attn, w_proj_ret, w_out, ln_g, ln_b):
    seq = x_prompt.shape[1]
    nbatch = x_sample.shape[0]
    win = cache_k.shape[2]
    assert seq % PROMPT_ROWS == 0 and nbatch % SAMPLE_BLOCK == 0 and nbatch % SUBLANES == 0

    w_in_b = w_in.astype(BF16)
    w_pa_b = w_proj_attn.astype(BF16)
    w_pr_b = w_proj_ret.astype(BF16)
    w_out_b = w_out.astype(BF16)
    ln_g3 = ln_g.reshape(DEPTH, 1, D_MODEL)
    ln_b3 = ln_b.reshape(DEPTH, 1, D_MODEL)

    c_all = jnp.concatenate([c_sample, c_prompt, jnp.zeros((SUBLANES - 1, D_MODEL), F32)], axis=0)
    mod = _cond_call(c_all, w_cond, b_cond)
    prompt_mod_block = nbatch // SUBLANES

    base, rtab = _prompt_rope_tables(seq, PROMPT_ROWS)
    stab = _sample_rope_table(float(PAST_LEN))
    ret_tabs = _retention_tables()

    ck = cache_k.reshape(DEPTH, nbatch, win, ATT_KV_WIDTH)
    cv = cache_v.reshape(DEPTH, nbatch, win, ATT_KV_WIDTH)

    yp = x_prompt[0]
    ys = x_sample[:, 0, :]
    kp, vp, sp = [], [], []
    prev = None
    for l in range(DEPTH):
        yp, k_new, v_new, s_new = _prompt_layer(
            l, yp, mod, prompt_mod_block, base, rtab, w_in_b, w_pa_b, w_pr_b, w_out_b,
            ln_g3, ln_b3, attn_sinks, ret_tabs)
        kp.append(k_new.reshape(1, WINDOW, ATT_KV_HEADS, ATT_HEAD_DIM))
        vp.append(v_new.reshape(1, WINDOW, ATT_KV_HEADS, ATT_HEAD_DIM))
        sp.append(s_new[None])

        qkv, ga, ret, rg, mg = _sample_proj(l, ys, mod, stab, w_in_b)
        oatt, oret, nk, nv, ns = _sample_mix(l, qkv, ret, ck, cv, state_ret, attn_sinks, prev)
        prev = (nk, nv, ns)
        ys = _sample_out(l, ys, mod, oatt.reshape(nbatch, ATT_WIDTH), ga, oret.reshape(nbatch, RET_WIDTH),
                         rg, mg, w_pa_b, w_pr_b, w_out_b, ln_g3, ln_b3)

    nk, nv, ns = prev
    return (yp[None], ys[:, None, :], jnp.stack(kp), jnp.stack(vp), jnp.stack(sp),
            nk.reshape(cache_k.shape), nv.reshape(cache_v.shape), ns)
```

```python
import functools

import jax
import jax.numpy as jnp
import numpy as np
from jax import lax
from jax.experimental import pallas as pl
from jax.experimental.pallas import tpu as pltpu

D_MODEL = 1024
DEPTH = 2
PAST_LEN = 16384
ATT_HEADS = 8
ATT_KV_HEADS = 2
ATT_HEAD_DIM = 64
ATT_GROUP = ATT_HEADS // ATT_KV_HEADS
ATT_WIDTH = ATT_HEADS * ATT_HEAD_DIM
ATT_KV_WIDTH = ATT_KV_HEADS * ATT_HEAD_DIM
WINDOW = 128
RET_HEADS = 4
RET_DK = 128
RET_DV = 256
RET_QK_WIDTH = RET_HEADS * RET_DK
RET_WIDTH = RET_HEADS * RET_DV
RET_CHUNK = 128
ROPE_THETA = 10000.0
ALPHA = (2.0 * DEPTH) ** 0.25
LN_EPS = 1e-5
RMS_EPS = 1e-6
ATT_SCALE = ATT_HEAD_DIM ** -0.5
RET_K_SCALE = RET_DK ** -0.5

C_AQ = 0
C_AK = C_AQ + ATT_WIDTH
C_AV = C_AK + ATT_KV_WIDTH
C_AG = C_AV + ATT_KV_WIDTH
C_RQ = C_AG + ATT_WIDTH
C_RK = C_RQ + RET_QK_WIDTH
C_RV = C_RK + RET_QK_WIDTH
C_RG = C_RV + RET_WIDTH
C_MA = C_RG + RET_WIDTH
C_MR = C_MA + D_MODEL
IN_COLS = C_MR + D_MODEL

LANES = 128
SUBLANES = 8
VMEM_LIMIT_BYTES = 56 * 1024 * 1024

PROMPT_ROWS = 256

BF16 = jnp.bfloat16
F32 = jnp.float32
NT = (((1,), (1,)), ((), ()))

_LOG_GAMMA = np.log(1.0 - 2.0 ** (-5.0 - np.arange(RET_HEADS, dtype=np.float64)))
CHUNK_DECAY = tuple(float(v) for v in np.exp(RET_CHUNK * _LOG_GAMMA))
TOKEN_DECAY = tuple(float(v) for v in np.exp(_LOG_GAMMA))


def _sigmoid(x):
    return 0.5 * jnp.tanh(0.5 * x) + 0.5


def _silu(x):
    return x * _sigmoid(x)


def _dot(a, b):
    return jnp.dot(a, b, preferred_element_type=F32)


def _rope_attn_tile(x, cos, sin_signed, first_half):
    rot = jnp.where(first_half, pltpu.roll(x, LANES - 32, 1), pltpu.roll(x, 32, 1))
    return x * cos + rot * sin_signed


def _rope_ret_tile(x, cos, sin_signed):
    return x * cos + pltpu.roll(x, 64, 1) * sin_signed


def _lane_iota(shape):
    return lax.broadcasted_iota(jnp.int32, shape, len(shape) - 1)


def _layer_spec(shape, layer):
    nd = len(shape)
    return pl.BlockSpec((None,) + tuple(shape), lambda i: (layer,) + (0,) * nd,
                        pipeline_mode=pl.Buffered(1))


def _const_spec(shape):
    nd = len(shape)
    return pl.BlockSpec(tuple(shape), lambda i: (0,) * nd, pipeline_mode=pl.Buffered(1))


def _rope_lane_patterns():
    lane = np.arange(LANES)
    f_att = ROPE_THETA ** (-(lane % 32) / 32.0)
    s_att = np.where(lane % 64 < 32, -1.0, 1.0)
    f_ret = ROPE_THETA ** (-(lane % 64) / 64.0)
    s_ret = np.where(lane < 64, -1.0, 1.0)
    return (f_att, s_att), (f_ret, s_ret)


def _prompt_rope_tables(seq, tm):
    starts = np.arange(seq // tm, dtype=np.float64)[:, None] * tm
    offs = np.arange(tm, dtype=np.float64)[:, None]
    base, within = [], []
    for freq, sign in _rope_lane_patterns():
        base += [np.cos(starts * freq), np.sin(starts * freq)]
        c, s = np.cos(offs * freq), np.sin(offs * freq)
        within += [c, s, sign * c, sign * s]
    return (jnp.asarray(np.stack(base, axis=1), F32),
            jnp.asarray(np.stack(within, axis=0), F32))


def _sample_rope_table(pos):
    rows = []
    for freq, sign in _rope_lane_patterns():
        rows += [np.cos(pos * freq), sign * np.sin(pos * freq)]
    return jnp.asarray(np.stack(rows, axis=0), F32)


def _retention_tables():
    c = RET_CHUNK
    idx = np.arange(c, dtype=np.float64)
    diff = idx[:, None] - idx[None, :]
    lg = _LOG_GAMMA[:, None, None]
    dmask = np.where(diff >= 0, np.exp(np.maximum(diff, 0.0)[None] * lg), 0.0)
    qdec = np.broadcast_to(np.exp((idx + 1.0)[None, :, None] * lg), (RET_HEADS, c, RET_DK))
    kdec = np.broadcast_to(np.exp((c - 1.0 - idx)[None, :, None] * lg), (RET_HEADS, c, RET_DK))
    return jnp.asarray(dmask, F32), jnp.asarray(qdec, F32), jnp.asarray(kdec, F32)


def _cond_kernel(c_ref, w_ref, b_ref, o_ref):
    a = _silu(c_ref[...]).astype(BF16)
    o_ref[...] = _dot(a, w_ref[...].astype(BF16)) + b_ref[...]


def _cond_call(c_all, w_cond, b_cond):
    rows = c_all.shape[0]
    tn = 768
    return pl.pallas_call(
        _cond_kernel,
        out_shape=jax.ShapeDtypeStruct((DEPTH, rows, 3 * D_MODEL), F32),
        grid=(DEPTH, 3 * D_MODEL // tn),
        in_specs=[
            pl.BlockSpec((rows, D_MODEL), lambda l, j: (0, 0)),
            pl.BlockSpec((None, D_MODEL, tn), lambda l, j: (l, 0, j)),
            pl.BlockSpec((None, 1, tn), lambda l, j: (l, 0, j)),
        ],
        out_specs=pl.BlockSpec((None, rows, tn), lambda l, j: (l, 0, j)),
        compiler_params=pltpu.CompilerParams(
            dimension_semantics=("arbitrary", "arbitrary"), vmem_limit_bytes=VMEM_LIMIT_BYTES),
        name="cond_mod",
    )(c_all, w_cond, b_cond.reshape(DEPTH, 1, 3 * D_MODEL))


N_PROMPT_INPUTS = 19
N_PROMPT_OUTPUTS = 9


def _prompt_kernel(*refs, layer, n_aliased):
    (sinks_ref, x_ref, mod_ref, base_ref, rtab_ref,
     win_ref, wpa_ref, wpr_ref, wout_ref, lng_ref, lnb_ref,
     dmask_ref, qdec_ref, kdec_ref,
     sqkv_ref, sret_ref, ck_ref, cv_ref, st_ref) = refs[:N_PROMPT_INPUTS]
    refs = refs[N_PROMPT_INPUTS + n_aliased:]
    (y_ref, knew_ref, vnew_ref, snew_ref,
     oatt_ref, oret_ref, nk_ref, nv_ref, ns_ref) = refs[:N_PROMPT_OUTPUTS]
    (tab_scr, hb_scr, q_scr, kvar_scr, vvar_scr, ga_scr, a_scr,
     rq_scr, rk_scr, rv_scr, rg_scr, r_scr, s_scr,
     ma_scr, mr_scr, z_scr, zr_scr) = refs[N_PROMPT_OUTPUTS:]
    step = pl.program_id(0)
    tm = x_ref.shape[0]
    nsub = tm // WINDOW

    @pl.when(step == 0)
    def _():
        kvar_scr[:, 0:WINDOW, :] = jnp.zeros((4, WINDOW, LANES), BF16)
        vvar_scr[:, 0:WINDOW, :] = jnp.zeros((4, WINDOW, LANES), BF16)
        s_scr[...] = jnp.zeros(s_scr.shape, F32)

    for fam in range(2):
        cb = base_ref[2 * fam:2 * fam + 1, :]
        sb = base_ref[2 * fam + 1:2 * fam + 2, :]
        tab_scr[2 * fam] = cb * rtab_ref[4 * fam] - sb * rtab_ref[4 * fam + 1]
        tab_scr[2 * fam + 1] = sb * rtab_ref[4 * fam + 2] + cb * rtab_ref[4 * fam + 3]

    x = x_ref[...]
    shift = mod_ref[0:1, 0:D_MODEL]
    scale = mod_ref[0:1, D_MODEL:2 * D_MODEL]
    hb_scr[...] = (x * (1.0 + scale) + shift).astype(BF16)

    def proj(c0, c1):
        return _dot(hb_scr[...], win_ref[:, c0:c1])

    lane = _lane_iota((tm, LANES))
    first_half32 = (lane & 32) == 0
    lo64 = lane < 64

    def rope_attn(t):
        return _rope_attn_tile(t, tab_scr[0], tab_scr[1], first_half32)

    def rope_ret(t):
        return _rope_ret_tile(t, tab_scr[2], tab_scr[3])

    kv = proj(C_AK, C_AG)
    k_rot = rope_attn(kv[:, 0:LANES])
    v_raw = kv[:, LANES:2 * LANES]
    knew_ref[...] = k_rot[tm - WINDOW:, :]
    vnew_ref[...] = v_raw[tm - WINDOW:, :]

    def store_variants(scr, t):
        swapped = pltpu.roll(t, 64, 1)
        zero = jnp.zeros_like(t)
        scr[0, WINDOW:WINDOW + tm, :] = jnp.where(lo64, t, zero).astype(BF16)
        scr[1, WINDOW:WINDOW + tm, :] = jnp.where(lo64, zero, swapped).astype(BF16)
        scr[2, WINDOW:WINDOW + tm, :] = jnp.where(lo64, swapped, zero).astype(BF16)
        scr[3, WINDOW:WINDOW + tm, :] = jnp.where(lo64, zero, t).astype(BF16)

    store_variants(kvar_scr, k_rot)
    store_variants(vvar_scr, v_raw)

    qp = proj(C_AQ, C_AK)
    for t in range(ATT_WIDTH // LANES):
        qt = rope_attn(qp[:, t * LANES:(t + 1) * LANES])
        q_scr[:, t * LANES:(t + 1) * LANES] = (qt * ATT_SCALE).astype(BF16)
    ga_scr[...] = _silu(proj(C_AG, C_RQ))

    def chunk_rq():
        p = proj(C_RQ, C_RK)
        for h in range(RET_HEADS):
            sl = slice(h * RET_DK, (h + 1) * RET_DK)
            rq_scr[:, sl] = rope_ret(p[:, sl])

    def chunk_rk():
        p = proj(C_RK, C_RV)
        for h in range(RET_HEADS):
            sl = slice(h * RET_DK, (h + 1) * RET_DK)
            rk_scr[:, sl] = rope_ret(p[:, sl]) * RET_K_SCALE

    def half_chunk(c0, dst, fn, half):
        w = dst.shape[1] // 2
        def run():
            dst[:, half * w:(half + 1) * w] = fn(proj(c0 + half * w, c0 + (half + 1) * w))
        return run

    to_bf16 = lambda v: v.astype(BF16)
    att_companions = [
        half_chunk(C_RG, rg_scr, _silu, 0), chunk_rq,
        half_chunk(C_RG, rg_scr, _silu, 1), chunk_rk,
        half_chunk(C_MA, ma_scr, _sigmoid, 0), half_chunk(C_RV, rv_scr, to_bf16, 0),
        half_chunk(C_MA, ma_scr, _sigmoid, 1), half_chunk(C_RV, rv_scr, to_bf16, 1),
    ]

    row = lax.broadcasted_iota(jnp.int32, (WINDOW, 2 * WINDOW), 0)
    col = lax.broadcasted_iota(jnp.int32, (WINDOW, 2 * WINDOW), 1)
    in_window = col <= row + WINDOW
    mask_std = (col >= row) & in_window
    off = jnp.where(step > 0, 0, 4 * WINDOW)
    mask_first = ((col >= row + off) | (col >= WINDOW)) & in_window
    lo64_w = _lane_iota((WINDOW, LANES)) < 64

    def attention_logits(i, t):
        r0 = i * WINDOW
        g = t // 2
        qt = q_scr[r0:r0 + WINDOW, t * LANES:(t + 1) * LANES]
        return [lax.dot_general(qt, kvar_scr[2 * g + p, r0:r0 + 2 * WINDOW, :], NT,
                                preferred_element_type=F32) for p in range(2)]

    def attention_values(i, t, logits):
        r0 = i * WINDOW
        mask = mask_first if i == 0 else mask_std
        g = t // 2
        o_acc = None
        recips = []
        for p in range(2):
            head = 2 * t + p
            s = jnp.where(mask, logits[p], -jnp.inf)
            sink = sinks_ref[layer, head]
            m = jnp.maximum(jnp.max(s, axis=-1, keepdims=True), sink)
            pe = jnp.exp(s - m)
            denom = jnp.sum(pe, axis=-1, keepdims=True) + jnp.exp(sink - m)
            recips.append(1.0 / denom)
            vals = vvar_scr[2 * g + p, r0:r0 + 2 * WINDOW, :]
            o = _dot(pe.astype(BF16), vals)
            o_acc = o if o_acc is None else o_acc + o
        o_t = o_acc * jnp.where(lo64_w, recips[0], recips[1])
        gt = ga_scr[r0:r0 + WINDOW, t * LANES:(t + 1) * LANES]
        a_scr[r0:r0 + WINDOW, t * LANES:(t + 1) * LANES] = (o_t * gt).astype(BF16)

    n = 0
    for i in range(nsub):
        for t in range(ATT_WIDTH // LANES):
            logits = attention_logits(i, t)
            if n < len(att_companions):
                att_companions[n]()
            n += 1
            attention_values(i, t, logits)
    for run in att_companions[n:]:
        run()

    kvar_scr[:, 0:WINDOW, :] = kvar_scr[:, tm:tm + WINDOW, :]
    vvar_scr[:, 0:WINDOW, :] = vvar_scr[:, tm:tm + WINDOW, :]

    def ret_slices(c, h):
        rows = slice(c * RET_CHUNK, (c + 1) * RET_CHUNK)
        return rows, slice(h * RET_DK, (h + 1) * RET_DK), slice(h * RET_DV, (h + 1) * RET_DV)

    def retention_scores_and_state(c):
        inners, s_olds = [], []
        for h in range(RET_HEADS):
            rows, sk, sv = ret_slices(c, h)
            qh = rq_scr[rows, sk]
            kh = rk_scr[rows, sk]
            vh = rv_scr[rows, sv]
            inners.append(lax.dot_general(qh.astype(BF16), kh.astype(BF16), NT, preferred_element_type=F32))
            s_old = s_scr[h]
            s_olds.append(s_old.astype(BF16))
            kd = (kh * kdec_ref[h]).astype(BF16)
            s_scr[h] = s_old * CHUNK_DECAY[h] + lax.dot_general(
                kd, vh, (((0,), (0,)), ((), ())), preferred_element_type=F32)
        return inners, s_olds

    def retention_outputs(c, inners, s_olds):
        for h in range(RET_HEADS):
            rows, sk, sv = ret_slices(c, h)
            qh = rq_scr[rows, sk]
            lhs = jnp.concatenate([(inners[h] * dmask_ref[h]).astype(BF16),
                                   (qh * qdec_ref[h]).astype(BF16)], axis=1)
            rhs = jnp.concatenate([rv_scr[rows, sv], s_olds[h]], axis=0)
            o = _dot(lhs, rhs)
            ms = jnp.mean(o * o, axis=-1, keepdims=True)
            on = o * lax.rsqrt(ms + RMS_EPS)
            r_scr[rows, sv] = (on * rg_scr[rows, sv]).astype(BF16)

    def chunk_za(half):
        w = D_MODEL // 2
        def run():
            sl = slice(half * w, (half + 1) * w)
            z_scr[:, sl] = ma_scr[:, sl] * _dot(a_scr[...], wpa_ref[:, sl])
        return run

    ret_companions = [half_chunk(C_MR, mr_scr, _sigmoid, 0), half_chunk(C_MR, mr_scr, _sigmoid, 1)]
    n = 0
    for c in range(nsub):
        inners, s_olds = retention_scores_and_state(c)
        if n < len(ret_companions):
            ret_companions[n]()
        n += 1
        retention_outputs(c, inners, s_olds)
    for run in ret_companions[n:]:
        run()
    sample_qmats, sample_logits = _sample_scores_and_state(sqkv_ref, sret_ref, ck_ref, st_ref, ns_ref)
    chunk_za(0)()
    chunk_za(1)()
    zr_scr[...] = _dot(r_scr[...], wpr_ref[...])

    @pl.when(step == pl.num_programs(0) - 1)
    def _():
        snew_ref[...] = s_scr[...]

    gate_c = mod_ref[0:1, 2 * D_MODEL:3 * D_MODEL]
    lng = lng_ref[...]
    lnb = lnb_ref[...]
    for i in range(nsub):
        rows = slice(i * WINDOW, (i + 1) * WINDOW)
        z = z_scr[rows, :] + mr_scr[rows, :] * zr_scr[rows, :]
        u = _dot(z.astype(BF16), wout_ref[...])
        t = ALPHA * x_ref[rows, :] + gate_c * u
        mu = jnp.mean(t, axis=-1, keepdims=True)
        d = t - mu
        var = jnp.mean(d * d, axis=-1, keepdims=True)
        y_ref[rows, :] = d * lax.rsqrt(var + LN_EPS) * lng + lnb

    _sample_outputs(sinks_ref, layer, sqkv_ref, sret_ref, ck_ref, cv_ref, sample_qmats, sample_logits,
                    oatt_ref, oret_ref, nk_ref, nv_ref, ns_ref)


def _fused_layer(layer, x, mod, mod_row_block, base, rtab, w_in_b, w_pa_b, w_pr_b, w_out_b,
                 ln_g, ln_b, sinks, ret_tabs, sample_qkv, sample_ret, cache_k, cache_v, state, shared_out):
    seq = x.shape[0]
    tm = PROMPT_ROWS
    steps = seq // tm
    nbatch = sample_qkv.shape[0]
    nb = nbatch // steps
    win = cache_k.shape[2]
    dmask, qdec, kdec = ret_tabs
    smem = pl.BlockSpec(memory_space=pltpu.SMEM)
    row_spec = lambda w: pl.BlockSpec((tm, w), lambda i: (i, 0))
    seq_spec = lambda w: pl.BlockSpec((nb, 1, w), lambda i: (i, 0, 0))
    cache_spec = pl.BlockSpec((None, nb, win, ATT_KV_WIDTH), lambda i: (layer, i, 0, 0))
    state_spec = pl.BlockSpec((None, nb, RET_HEADS, RET_DK, RET_DV), lambda i: (layer, i, 0, 0, 0))
    in_specs = [
        smem,
        row_spec(D_MODEL),
        pl.BlockSpec((None, SUBLANES, 3 * D_MODEL), lambda i: (layer, mod_row_block, 0),
                     pipeline_mode=pl.Buffered(1)),
        pl.BlockSpec((None, 4, LANES), lambda i: (i, 0, 0)),
        _const_spec((8, tm, LANES)),
        _layer_spec((D_MODEL, IN_COLS), layer),
        _layer_spec((ATT_WIDTH, D_MODEL), layer),
        _layer_spec((RET_WIDTH, D_MODEL), layer),
        _layer_spec((D_MODEL, D_MODEL), layer),
        _layer_spec((1, D_MODEL), layer), _layer_spec((1, D_MODEL), layer),
        _const_spec((RET_HEADS, RET_CHUNK, RET_CHUNK)),
        _const_spec((RET_HEADS, RET_CHUNK, RET_DK)),
        _const_spec((RET_HEADS, RET_CHUNK, RET_DK)),
        seq_spec(C_AG), seq_spec(2 * RET_QK_WIDTH + RET_WIDTH),
        cache_spec, cache_spec, state_spec,
    ]
    args = [sinks, x, mod, base, rtab, w_in_b, w_pa_b, w_pr_b, w_out_b, ln_g, ln_b, dmask, qdec, kdec,
            sample_qkv.reshape(nbatch, 1, C_AG), sample_ret.reshape(nbatch, 1, -1), cache_k, cache_v, state]
    assert len(args) == N_PROMPT_INPUTS
    aliases = {}
    if shared_out is not None:
        for j, arr in enumerate(shared_out):
            aliases[len(args)] = N_PROMPT_OUTPUTS - len(shared_out) + j
            args.append(arr)
            in_specs.append(pl.BlockSpec(memory_space=pl.ANY))
    return pl.pallas_call(
        functools.partial(_prompt_kernel, layer=layer, n_aliased=len(aliases)),
        out_shape=(
            jax.ShapeDtypeStruct((seq, D_MODEL), F32),
            jax.ShapeDtypeStruct((WINDOW, ATT_KV_WIDTH), F32),
            jax.ShapeDtypeStruct((WINDOW, ATT_KV_WIDTH), F32),
            jax.ShapeDtypeStruct((RET_HEADS, RET_DK, RET_DV), F32),
            jax.ShapeDtypeStruct((nbatch, 1, ATT_WIDTH), F32),
            jax.ShapeDtypeStruct((nbatch, 1, RET_WIDTH), F32),
            jax.ShapeDtypeStruct(cache_k.shape, F32),
            jax.ShapeDtypeStruct(cache_v.shape, F32),
            jax.ShapeDtypeStruct(state.shape, F32),
        ),
        grid=(steps,),
        in_specs=in_specs,
        out_specs=(
            row_spec(D_MODEL),
            pl.BlockSpec((WINDOW, ATT_KV_WIDTH), lambda i: (0, 0)),
            pl.BlockSpec((WINDOW, ATT_KV_WIDTH), lambda i: (0, 0)),
            pl.BlockSpec((RET_HEADS, RET_DK, RET_DV), lambda i: (0, 0, 0)),
            seq_spec(ATT_WIDTH), seq_spec(RET_WIDTH),
            cache_spec, cache_spec, state_spec,
        ),
        input_output_aliases=aliases,
        scratch_shapes=[
            pltpu.VMEM((4, tm, LANES), F32),
            pltpu.VMEM((tm, D_MODEL), BF16),
            pltpu.VMEM((tm, ATT_WIDTH), BF16),
            pltpu.VMEM((4, WINDOW + tm, LANES), BF16),
            pltpu.VMEM((4, WINDOW + tm, LANES), BF16),
            pltpu.VMEM((tm, ATT_WIDTH), F32),
            pltpu.VMEM((tm, ATT_WIDTH), BF16),
            pltpu.VMEM((tm, RET_QK_WIDTH), F32),
            pltpu.VMEM((tm, RET_QK_WIDTH), F32),
            pltpu.VMEM((tm, RET_WIDTH), BF16),
            pltpu.VMEM((tm, RET_WIDTH), F32),
            pltpu.VMEM((tm, RET_WIDTH), BF16),
            pltpu.VMEM((RET_HEADS, RET_DK, RET_DV), F32),
            pltpu.VMEM((tm, D_MODEL), F32),
            pltpu.VMEM((tm, D_MODEL), F32),
            pltpu.VMEM((tm, D_MODEL), F32),
            pltpu.VMEM((tm, D_MODEL), F32),
        ],
        compiler_params=pltpu.CompilerParams(
            dimension_semantics=("arbitrary",), vmem_limit_bytes=VMEM_LIMIT_BYTES),
        name="fused_layer",
    )(*args)


def _sample_proj_kernel(x_ref, mod_ref, tab_ref, win_ref,
                        qkv_ref, ga_ref, ret_ref, rg_ref, mg_ref):
    rows = x_ref.shape[0]
    x = x_ref[...]
    shift = mod_ref[:, 0:D_MODEL]
    scale = mod_ref[:, D_MODEL:2 * D_MODEL]
    hb = (x * (1.0 + scale) + shift).astype(BF16)

    def proj(c0, c1):
        return _dot(hb, win_ref[:, c0:c1])

    lane = _lane_iota((rows, LANES))
    first_half32 = (lane & 32) == 0
    ca = tab_ref[0:1, :]
    sa = tab_ref[1:2, :]
    cr = tab_ref[2:3, :]
    sr = tab_ref[3:4, :]

    qkv = proj(C_AQ, C_AG)
    for t in range(ATT_WIDTH // LANES):
        qt = _rope_attn_tile(qkv[:, t * LANES:(t + 1) * LANES], ca, sa, first_half32)
        qkv_ref[:, t * LANES:(t + 1) * LANES] = qt * ATT_SCALE
    qkv_ref[:, C_AK:C_AV] = _rope_attn_tile(qkv[:, C_AK:C_AV], ca, sa, first_half32)
    qkv_ref[:, C_AV:C_AG] = qkv[:, C_AV:C_AG]
    ga_ref[...] = _silu(proj(C_AG, C_RQ))

    rqk = proj(C_RQ, C_RV)
    for h in range(RET_HEADS):
        sl = slice(h * RET_DK, (h + 1) * RET_DK)
        ret_ref[:, sl] = _rope_ret_tile(rqk[:, sl], cr, sr)
        sk = slice(RET_QK_WIDTH + h * RET_DK, RET_QK_WIDTH + (h + 1) * RET_DK)
        ret_ref[:, sk] = _rope_ret_tile(rqk[:, sk], cr, sr) * RET_K_SCALE
    ret_ref[:, 2 * RET_QK_WIDTH:] = proj(C_RV, C_RG)
    rg_ref[...] = _silu(proj(C_RG, C_MA))
    mg_ref[...] = _sigmoid(proj(C_MA, IN_COLS))


def _sample_proj(layer, x, mod, tab, w_in_b):
    rows = x.shape[0]
    widths = (C_AG, ATT_WIDTH, 2 * RET_QK_WIDTH + RET_WIDTH, RET_WIDTH, 2 * D_MODEL)
    return pl.pallas_call(
        _sample_proj_kernel,
        out_shape=tuple(jax.ShapeDtypeStruct((rows, w), F32) for w in widths),
        grid=(1,),
        in_specs=[
            _const_spec((rows, D_MODEL)),
            _layer_spec((rows, 3 * D_MODEL), layer),
            _const_spec((4, LANES)),
            _layer_spec((D_MODEL, IN_COLS), layer),
        ],
        out_specs=tuple(pl.BlockSpec((rows, w), lambda i: (0, 0)) for w in widths),
        compiler_params=pltpu.CompilerParams(
            dimension_semantics=("arbitrary",), vmem_limit_bytes=VMEM_LIMIT_BYTES),
        name="sample_proj",
    )(x, mod, tab, w_in_b)


def _sample_scores_and_state(qkv_ref, ret_ref, ck_ref, st_ref, ns_ref):
    nb = qkv_ref.shape[0]
    rowi = lax.broadcasted_iota(jnp.int32, (ATT_HEADS, LANES), 0)
    lanei = lax.broadcasted_iota(jnp.int32, (ATT_HEADS, LANES), 1)
    lane_group = lanei // ATT_HEAD_DIM
    dr = lax.broadcasted_iota(jnp.int32, (RET_DK, RET_DK), 0)
    dc = lax.broadcasted_iota(jnp.int32, (RET_DK, RET_DK), 1)
    diag = dr == dc

    qmats, logits = [], []
    for b in range(nb):
        qkv = qkv_ref[b]
        qmat = jnp.zeros((ATT_HEADS, LANES), F32)
        for h in range(ATT_HEADS):
            t, p, g = h // 2, h % 2, h // ATT_GROUP
            tile = qkv[:, t * LANES:(t + 1) * LANES]
            src = tile if p == g else pltpu.roll(tile, 64, 1)
            qmat = jnp.where((rowi == h) & (lane_group == g), jnp.broadcast_to(src, (ATT_HEADS, LANES)), qmat)
        qmats.append(qmat)
        logits.append(lax.dot_general(qmat.astype(BF16), ck_ref[b].astype(BF16), NT,
                                      preferred_element_type=F32))

    for b in range(nb):
        ret = ret_ref[b]
        for h in range(RET_HEADS):
            kh = ret[:, RET_QK_WIDTH + h * RET_DK:RET_QK_WIDTH + (h + 1) * RET_DK]
            vh = ret[:, 2 * RET_QK_WIDTH + h * RET_DV:2 * RET_QK_WIDTH + (h + 1) * RET_DV]
            kdiag = jnp.where(diag, jnp.broadcast_to(kh, (RET_DK, RET_DK)), 0.0).astype(BF16)
            vfull = jnp.broadcast_to(vh, (RET_DK, RET_DV)).astype(BF16)
            ns_ref[b, h] = st_ref[b, h] * TOKEN_DECAY[h] + _dot(kdiag, vfull)
    return qmats, logits


def _sample_outputs(sinks_ref, layer, qkv_ref, ret_ref, ck_ref, cv_ref, qmats, logits,
                    oatt_ref, oret_ref, nk_ref, nv_ref, ns_ref):
    nb = qkv_ref.shape[0]
    sink_col = jnp.zeros((ATT_HEADS, 1), F32)
    rowc = lax.broadcasted_iota(jnp.int32, (ATT_HEADS, 1), 0)
    for h in range(ATT_HEADS):
        sink_col = jnp.where(rowc == h, sinks_ref[layer, h], sink_col)
    lo64_row = _lane_iota((1, LANES)) < 64

    for b in range(nb):
        qkv = qkv_ref[b]
        k_new = qkv[:, C_AK:C_AV]
        v_new = qkv[:, C_AV:C_AG]
        s_c = logits[b]
        s_self = jnp.sum(qmats[b] * k_new, axis=-1, keepdims=True)
        m = jnp.maximum(jnp.maximum(jnp.max(s_c, axis=-1, keepdims=True), s_self), sink_col)
        p_c = jnp.exp(s_c - m)
        p_self = jnp.exp(s_self - m)
        denom = jnp.sum(p_c, axis=-1, keepdims=True) + p_self + jnp.exp(sink_col - m)
        o = (_dot(p_c.astype(BF16), cv_ref[b].astype(BF16)) + p_self * v_new) / denom
        o_sw = pltpu.roll(o, 64, 1)
        for t in range(ATT_WIDTH // LANES):
            g = t // 2
            first = (o if g == 0 else o_sw)[2 * t:2 * t + 1, :]
            second = (o_sw if g == 0 else o)[2 * t + 1:2 * t + 2, :]
            oatt_ref[b, :, t * LANES:(t + 1) * LANES] = jnp.where(lo64_row, first, second)

        nk_ref[b, 0:WINDOW - 1, :] = ck_ref[b, 1:WINDOW, :]
        nk_ref[b, WINDOW - 1:WINDOW, :] = k_new
        nv_ref[b, 0:WINDOW - 1, :] = cv_ref[b, 1:WINDOW, :]
        nv_ref[b, WINDOW - 1:WINDOW, :] = v_new

    for b in range(nb):
        ret = ret_ref[b]
        for h in range(RET_HEADS):
            qh = ret[:, h * RET_DK:(h + 1) * RET_DK]
            q8 = jnp.broadcast_to(qh, (SUBLANES, RET_DK)).astype(BF16)
            oh = _dot(q8, ns_ref[b, h].astype(BF16))[0:1, :]
            ms = jnp.mean(oh * oh, axis=-1, keepdims=True)
            oret_ref[b, :, h * RET_DV:(h + 1) * RET_DV] = oh * lax.rsqrt(ms + RMS_EPS)


def _sample_out_kernel(x_ref, mod_ref, oatt_ref, ga_ref, oret_ref, rg_ref, mg_ref,
                       wpa_ref, wpr_ref, wout_ref, lng_ref, lnb_ref, y_ref):
    gate_c = mod_ref[:, 2 * D_MODEL:3 * D_MODEL]
    za = _dot((oatt_ref[...] * ga_ref[...]).astype(BF16), wpa_ref[...])
    zr = _dot((oret_ref[...] * rg_ref[...]).astype(BF16), wpr_ref[...])
    z = mg_ref[:, 0:D_MODEL] * za + mg_ref[:, D_MODEL:2 * D_MODEL] * zr
    u = _dot(z.astype(BF16), wout_ref[...])
    t = ALPHA * x_ref[...] + gate_c * u
    mu = jnp.mean(t, axis=-1, keepdims=True)
    d = t - mu
    var = jnp.mean(d * d, axis=-1, keepdims=True)
    y_ref[...] = d * lax.rsqrt(var + LN_EPS) * lng_ref[...] + lnb_ref[...]


def _sample_out(layer, x, mod, oatt, ga, oret, rg, mg, w_pa_b, w_pr_b, w_out_b, ln_g, ln_b):
    rows = x.shape[0]
    return pl.pallas_call(
        _sample_out_kernel,
        out_shape=jax.ShapeDtypeStruct(x.shape, F32),
        grid=(1,),
        in_specs=[
            _const_spec((rows, D_MODEL)),
            _layer_spec((rows, 3 * D_MODEL), layer),
            _const_spec((rows, ATT_WIDTH)), _const_spec((rows, ATT_WIDTH)),
            _const_spec((rows, RET_WIDTH)), _const_spec((rows, RET_WIDTH)),
            _const_spec((rows, 2 * D_MODEL)),
            _layer_spec((ATT_WIDTH, D_MODEL), layer),
            _layer_spec((RET_WIDTH, D_MODEL), layer),
            _layer_spec((D_MODEL, D_MODEL), layer),
            _layer_spec((1, D_MODEL), layer), _layer_spec((1, D_MODEL), layer),
        ],
        out_specs=pl.BlockSpec((rows, D_MODEL), lambda i: (0, 0)),
        compiler_params=pltpu.CompilerParams(
            dimension_semantics=("arbitrary",), vmem_limit_bytes=VMEM_LIMIT_BYTES),
        name="sample_out",
    )(x, mod, oatt, ga, oret, rg, mg, w_pa_b, w_pr_b, w_out_b, ln_g, ln_b)


def kernel(x_prompt, x_sample, c_prompt, c_sample, cache_k, cache_v, state_ret, w_in, attn_sinks,
           w_cond, b_cond, w_proj_attn, w_proj_ret, w_out, ln_g, ln_b):
    seq = x_prompt.shape[1]
    nbatch = x_sample.shape[0]
    win = cache_k.shape[2]
    assert seq % PROMPT_ROWS == 0 and nbatch % (seq // PROMPT_ROWS) == 0 and nbatch % SUBLANES == 0

    w_in_b = w_in.astype(BF16)
    w_pa_b = w_proj_attn.astype(BF16)
    w_pr_b = w_proj_ret.astype(BF16)
    w_out_b = w_out.astype(BF16)
    ln_g3 = ln_g.reshape(DEPTH, 1, D_MODEL)
    ln_b3 = ln_b.reshape(DEPTH, 1, D_MODEL)

    c_all = jnp.concatenate([c_sample, c_prompt, jnp.zeros((SUBLANES - 1, D_MODEL), F32)], axis=0)
    mod = _cond_call(c_all, w_cond, b_cond)
    prompt_mod_block = nbatch // SUBLANES

    base, rtab = _prompt_rope_tables(seq, PROMPT_ROWS)
    stab = _sample_rope_table(float(PAST_LEN))
    ret_tabs = _retention_tables()

    ck = cache_k.reshape(DEPTH, nbatch, win, ATT_KV_WIDTH)
    cv = cache_v.reshape(DEPTH, nbatch, win, ATT_KV_WIDTH)

    yp = x_prompt[0]
    ys = x_sample[:, 0, :]
    kp, vp, sp = [], [], []
    prev = None
    for l in range(DEPTH):
        qkv, ga, ret, rg, mg = _sample_proj(l, ys, mod, stab, w_in_b)
        yp, k_new, v_new, s_new, oatt, oret, nk, nv, ns = _fused_layer(
            l, yp, mod, prompt_mod_block, base, rtab, w_in_b, w_pa_b, w_pr_b, w_out_b,
            ln_g3, ln_b3, attn_sinks, ret_tabs, qkv, ret, ck, cv, state_ret, prev)
        kp.append(k_new.reshape(1, WINDOW, ATT_KV_HEADS, ATT_HEAD_DIM))
        vp.append(v_new.reshape(1, WINDOW, ATT_KV_HEADS, ATT_HEAD_DIM))
        sp.append(s_new[None])
        prev = (nk, nv, ns)
        ys = _sample_out(l, ys, mod, oatt.reshape(nbatch, ATT_WIDTH), ga, oret.reshape(nbatch, RET_WIDTH),
                         rg, mg, w_pa_b, w_pr_b, w_out_b, ln_g3, ln_b3)

    nk, nv, ns = prev
    return (yp[None], ys[:, None, :], jnp.stack(kp), jnp.stack(vp), jnp.stack(sp),
            nk.reshape(cache_k.shape), nv.reshape(cache_v.shape), ns)
```

```python
import functools

import jax
import jax.numpy as jnp
import numpy as np
from jax import lax
from jax.experimental import pallas as pl
from jax.experimental.pallas import tpu as pltpu

D_MODEL = 1024
DEPTH = 2
PAST_LEN = 16384
ATT_HEADS = 8
ATT_KV_HEADS = 2
ATT_HEAD_DIM = 64
ATT_GROUP = ATT_HEADS // ATT_KV_HEADS
ATT_WIDTH = ATT_HEADS * ATT_HEAD_DIM
ATT_KV_WIDTH = ATT_KV_HEADS * ATT_HEAD_DIM
WINDOW = 128
RET_HEADS = 4
RET_DK = 128
RET_DV = 256
RET_QK_WIDTH = RET_HEADS * RET_DK
RET_WIDTH = RET_HEADS * RET_DV
RET_CHUNK = 128
ROPE_THETA = 10000.0
ALPHA = (2.0 * DEPTH) ** 0.25
LN_EPS = 1e-5
RMS_EPS = 1e-6
ATT_SCALE = ATT_HEAD_DIM ** -0.5
RET_K_SCALE = RET_DK ** -0.5

C_AQ = 0
C_AK = C_AQ + ATT_WIDTH
C_AV = C_AK + ATT_KV_WIDTH
C_AG = C_AV + ATT_KV_WIDTH
C_RQ = C_AG + ATT_WIDTH
C_RK = C_RQ + RET_QK_WIDTH
C_RV = C_RK + RET_QK_WIDTH
C_RG = C_RV + RET_WIDTH
C_MA = C_RG + RET_WIDTH
C_MR = C_MA + D_MODEL
IN_COLS = C_MR + D_MODEL

LANES = 128
SUBLANES = 8
VMEM_LIMIT_BYTES = 56 * 1024 * 1024

PROMPT_ROWS = 256

BF16 = jnp.bfloat16
F32 = jnp.float32
NT = (((1,), (1,)), ((), ()))

_LOG_GAMMA = np.log(1.0 - 2.0 ** (-5.0 - np.arange(RET_HEADS, dtype=np.float64)))
CHUNK_DECAY = tuple(float(v) for v in np.exp(RET_CHUNK * _LOG_GAMMA))
TOKEN_DECAY = tuple(float(v) for v in np.exp(_LOG_GAMMA))


def _sigmoid(x):
    return 0.5 * jnp.tanh(0.5 * x) + 0.5


def _silu(x):
    return x * _sigmoid(x)


def _dot(a, b):
    return jnp.dot(a, b, preferred_element_type=F32)


def _rope_attn_tile(x, cos, sin_signed, first_half):
    rot = jnp.where(first_half, pltpu.roll(x, LANES - 32, 1), pltpu.roll(x, 32, 1))
    return x * cos + rot * sin_signed


def _rope_ret_tile(x, cos, sin_signed):
    return x * cos + pltpu.roll(x, 64, 1) * sin_signed


def _lane_iota(shape):
    return lax.broadcasted_iota(jnp.int32, shape, len(shape) - 1)


def _layer_spec(shape, layer):
    nd = len(shape)
    return pl.BlockSpec((None,) + tuple(shape), lambda i: (layer,) + (0,) * nd,
                        pipeline_mode=pl.Buffered(1))


def _const_spec(shape):
    nd = len(shape)
    return pl.BlockSpec(tuple(shape), lambda i: (0,) * nd, pipeline_mode=pl.Buffered(1))


def _rope_lane_patterns():
    lane = np.arange(LANES)
    f_att = ROPE_THETA ** (-(lane % 32) / 32.0)
    s_att = np.where(lane % 64 < 32, -1.0, 1.0)
    f_ret = ROPE_THETA ** (-(lane % 64) / 64.0)
    s_ret = np.where(lane < 64, -1.0, 1.0)
    return (f_att, s_att), (f_ret, s_ret)


def _prompt_rope_tables(seq, tm):
    starts = np.arange(seq // tm, dtype=np.float64)[:, None] * tm
    offs = np.arange(tm, dtype=np.float64)[:, None]
    base, within = [], []
    for freq, sign in _rope_lane_patterns():
        base += [np.cos(starts * freq), np.sin(starts * freq)]
        c, s = np.cos(offs * freq), np.sin(offs * freq)
        within += [c, s, sign * c, sign * s]
    return (jnp.asarray(np.stack(base, axis=1), F32),
            jnp.asarray(np.stack(within, axis=0), F32))


def _sample_rope_table(pos):
    rows = []
    for freq, sign in _rope_lane_patterns():
        rows += [np.cos(pos * freq), sign * np.sin(pos * freq)]
    return jnp.asarray(np.stack(rows, axis=0), F32)


def _retention_tables():
    c = RET_CHUNK
    idx = np.arange(c, dtype=np.float64)
    diff = idx[:, None] - idx[None, :]
    lg = _LOG_GAMMA[:, None, None]
    dmask = np.where(diff >= 0, np.exp(np.maximum(diff, 0.0)[None] * lg), 0.0)
    qdec = np.broadcast_to(np.exp((idx + 1.0)[None, :, None] * lg), (RET_HEADS, c, RET_DK))
    kdec = np.broadcast_to(np.exp((c - 1.0 - idx)[None, :, None] * lg), (RET_HEADS, c, RET_DK))
    return jnp.asarray(dmask, F32), jnp.asarray(qdec, F32), jnp.asarray(kdec, F32)


def _cond_kernel(c_ref, w_ref, b_ref, o_ref):
    a = _silu(c_ref[...]).astype(BF16)
    o_ref[...] = _dot(a, w_ref[...].astype(BF16)) + b_ref[...]


def _cond_call(c_all, w_cond, b_cond):
    rows = c_all.shape[0]
    tn = 768
    return pl.pallas_call(
        _cond_kernel,
        out_shape=jax.ShapeDtypeStruct((DEPTH, rows, 3 * D_MODEL), F32),
        grid=(DEPTH, 3 * D_MODEL // tn),
        in_specs=[
            pl.BlockSpec((rows, D_MODEL), lambda l, j: (0, 0)),
            pl.BlockSpec((None, D_MODEL, tn), lambda l, j: (l, 0, j)),
            pl.BlockSpec((None, 1, tn), lambda l, j: (l, 0, j)),
        ],
        out_specs=pl.BlockSpec((None, rows, tn), lambda l, j: (l, 0, j)),
        compiler_params=pltpu.CompilerParams(
            dimension_semantics=("arbitrary", "arbitrary"), vmem_limit_bytes=VMEM_LIMIT_BYTES),
        name="cond_mod",
    )(c_all, w_cond, b_cond.reshape(DEPTH, 1, 3 * D_MODEL))


N_PROMPT_INPUTS = 19
N_PROMPT_OUTPUTS = 9


def _prompt_kernel(*refs, layer, n_aliased):
    (sinks_ref, x_ref, mod_ref, base_ref, rtab_ref,
     win_ref, wpa_ref, wpr_ref, wout_ref, lng_ref, lnb_ref,
     dmask_ref, qdec_ref, kdec_ref,
     sqkv_ref, sret_ref, ck_ref, cv_ref, st_ref) = refs[:N_PROMPT_INPUTS]
    refs = refs[N_PROMPT_INPUTS + n_aliased:]
    (y_ref, knew_ref, vnew_ref, snew_ref,
     oatt_ref, oret_ref, nk_ref, nv_ref, ns_ref) = refs[:N_PROMPT_OUTPUTS]
    (tab_scr, hb_scr, q_scr, kvar_scr, vvar_scr, ga_scr, a_scr,
     rq_scr, rk_scr, rv_scr, rg_scr, r_scr, s_scr,
     ma_scr, mr_scr) = refs[N_PROMPT_OUTPUTS:]
    step = pl.program_id(0)
    tm = x_ref.shape[0]
    nsub = tm // WINDOW

    @pl.when(step == 0)
    def _():
        kvar_scr[:, 0:WINDOW, :] = jnp.zeros((4, WINDOW, LANES), BF16)
        vvar_scr[:, 0:WINDOW, :] = jnp.zeros((4, WINDOW, LANES), BF16)
        s_scr[...] = jnp.zeros(s_scr.shape, F32)

    x = x_ref[...]
    shift = mod_ref[0:1, 0:D_MODEL]
    scale = mod_ref[0:1, D_MODEL:2 * D_MODEL]
    hb_scr[...] = (x * (1.0 + scale) + shift).astype(BF16)

    for fam in range(2):
        cb = base_ref[2 * fam:2 * fam + 1, :]
        sb = base_ref[2 * fam + 1:2 * fam + 2, :]
        tab_scr[2 * fam] = cb * rtab_ref[4 * fam] - sb * rtab_ref[4 * fam + 1]
        tab_scr[2 * fam + 1] = sb * rtab_ref[4 * fam + 2] + cb * rtab_ref[4 * fam + 3]

    def proj(c0, c1):
        return _dot(hb_scr[...], win_ref[:, c0:c1])

    lane = _lane_iota((tm, LANES))
    first_half32 = (lane & 32) == 0
    lo64 = lane < 64

    def rope_attn(t):
        return _rope_attn_tile(t, tab_scr[0], tab_scr[1], first_half32)

    def rope_ret(t):
        return _rope_ret_tile(t, tab_scr[2], tab_scr[3])

    kv = proj(C_AK, C_AG)
    k_rot = rope_attn(kv[:, 0:LANES])
    v_raw = kv[:, LANES:2 * LANES]
    knew_ref[...] = k_rot[tm - WINDOW:, :]
    vnew_ref[...] = v_raw[tm - WINDOW:, :]

    def store_variants(scr, t):
        swapped = pltpu.roll(t, 64, 1)
        zero = jnp.zeros_like(t)
        scr[0, WINDOW:WINDOW + tm, :] = jnp.where(lo64, t, zero).astype(BF16)
        scr[1, WINDOW:WINDOW + tm, :] = jnp.where(lo64, zero, swapped).astype(BF16)
        scr[2, WINDOW:WINDOW + tm, :] = jnp.where(lo64, swapped, zero).astype(BF16)
        scr[3, WINDOW:WINDOW + tm, :] = jnp.where(lo64, zero, t).astype(BF16)

    store_variants(kvar_scr, k_rot)
    store_variants(vvar_scr, v_raw)

    qp = proj(C_AQ, C_AK)
    for t in range(ATT_WIDTH // LANES):
        qt = rope_attn(qp[:, t * LANES:(t + 1) * LANES])
        q_scr[:, t * LANES:(t + 1) * LANES] = (qt * ATT_SCALE).astype(BF16)
    ga_scr[...] = proj(C_AG, C_RQ)

    def chunk_rq():
        p = proj(C_RQ, C_RK)
        for h in range(RET_HEADS):
            sl = slice(h * RET_DK, (h + 1) * RET_DK)
            rq_scr[:, sl] = rope_ret(p[:, sl])

    def chunk_rk():
        p = proj(C_RK, C_RV)
        for h in range(RET_HEADS):
            sl = slice(h * RET_DK, (h + 1) * RET_DK)
            rk_scr[:, sl] = rope_ret(p[:, sl]) * RET_K_SCALE

    def half_chunk(c0, dst, fn, half):
        w = dst.shape[1] // 2
        def run():
            dst[:, half * w:(half + 1) * w] = fn(proj(c0 + half * w, c0 + (half + 1) * w))
        return run

    to_bf16 = lambda v: v.astype(BF16)
    raw = lambda v: v
    att_companions = [
        chunk_rq, chunk_rk,
        half_chunk(C_RV, rv_scr, to_bf16, 0), half_chunk(C_RV, rv_scr, to_bf16, 1),
        half_chunk(C_RG, rg_scr, raw, 0), half_chunk(C_RG, rg_scr, raw, 1),
        half_chunk(C_MA, ma_scr, raw, 0), half_chunk(C_MA, ma_scr, raw, 1),
    ]

    row = lax.broadcasted_iota(jnp.int32, (WINDOW, 2 * WINDOW), 0)
    col = lax.broadcasted_iota(jnp.int32, (WINDOW, 2 * WINDOW), 1)
    in_window = col <= row + WINDOW
    mask_std = (col >= row) & in_window
    off = jnp.where(step > 0, 0, 4 * WINDOW)
    mask_first = ((col >= row + off) | (col >= WINDOW)) & in_window
    lo64_w = _lane_iota((WINDOW, LANES)) < 64

    def attention_logits(i, t):
        r0 = i * WINDOW
        g = t // 2
        qt = q_scr[r0:r0 + WINDOW, t * LANES:(t + 1) * LANES]
        return [lax.dot_general(qt, kvar_scr[2 * g + p, r0:r0 + 2 * WINDOW, :], NT,
                                preferred_element_type=F32) for p in range(2)]

    def attention_values(i, t, logits):
        r0 = i * WINDOW
        mask = mask_first if i == 0 else mask_std
        g = t // 2
        o_acc = None
        recips = []
        for p in range(2):
            head = 2 * t + p
            s = jnp.where(mask, logits[p], -jnp.inf)
            sink = sinks_ref[layer, head]
            m = jnp.maximum(jnp.max(s, axis=-1, keepdims=True), sink)
            pe = jnp.exp(s - m)
            denom = jnp.sum(pe, axis=-1, keepdims=True) + jnp.exp(sink - m)
            recips.append(1.0 / denom)
            vals = vvar_scr[2 * g + p, r0:r0 + 2 * WINDOW, :]
            o = _dot(pe.astype(BF16), vals)
            o_acc = o if o_acc is None else o_acc + o
        o_t = o_acc * jnp.where(lo64_w, recips[0], recips[1])
        a_scr[r0:r0 + WINDOW, t * LANES:(t + 1) * LANES] = o_t

    n = 0
    for i in range(nsub):
        for t in range(ATT_WIDTH // LANES):
            logits = attention_logits(i, t)
            if n < len(att_companions):
                att_companions[n]()
            n += 1
            attention_values(i, t, logits)
    for run in att_companions[n:]:
        run()

    kvar_scr[:, 0:WINDOW, :] = kvar_scr[:, tm:tm + WINDOW, :]
    vvar_scr[:, 0:WINDOW, :] = vvar_scr[:, tm:tm + WINDOW, :]

    def ret_slices(c, h):
        rows = slice(c * RET_CHUNK, (c + 1) * RET_CHUNK)
        return rows, slice(h * RET_DK, (h + 1) * RET_DK), slice(h * RET_DV, (h + 1) * RET_DV)

    def retention_scores_and_state(c):
        inners, s_olds = [], []
        for h in range(RET_HEADS):
            rows, sk, sv = ret_slices(c, h)
            qh = rq_scr[rows, sk]
            kh = rk_scr[rows, sk]
            vh = rv_scr[rows, sv]
            inners.append(lax.dot_general(qh.astype(BF16), kh.astype(BF16), NT, preferred_element_type=F32))
            s_old = s_scr[h]
            s_olds.append(s_old.astype(BF16))
            kd = (kh * kdec_ref[h]).astype(BF16)
            s_scr[h] = s_old * CHUNK_DECAY[h] + lax.dot_general(
                kd, vh, (((0,), (0,)), ((), ())), preferred_element_type=F32)
        return inners, s_olds

    def retention_outputs(c, inners, s_olds):
        for h in range(RET_HEADS):
            rows, sk, sv = ret_slices(c, h)
            qh = rq_scr[rows, sk]
            lhs = jnp.concatenate([(inners[h] * dmask_ref[h]).astype(BF16),
                                   (qh * qdec_ref[h]).astype(BF16)], axis=1)
            rhs = jnp.concatenate([rv_scr[rows, sv], s_olds[h]], axis=0)
            o = _dot(lhs, rhs)
            ms = jnp.mean(o * o, axis=-1, keepdims=True)
            on = o * lax.rsqrt(ms + RMS_EPS)
            r_scr[rows, sv] = on

    ret_companions = [half_chunk(C_MR, mr_scr, raw, 0), half_chunk(C_MR, mr_scr, raw, 1)]
    n = 0
    for c in range(nsub):
        inners, s_olds = retention_scores_and_state(c)
        if n < len(ret_companions):
            ret_companions[n]()
        n += 1
        retention_outputs(c, inners, s_olds)
    for run in ret_companions[n:]:
        run()
    sample_qmats, sample_logits = _sample_scores_and_state(sqkv_ref, sret_ref, ck_ref, st_ref, ns_ref)

    @pl.when(step == pl.num_programs(0) - 1)
    def _():
        snew_ref[...] = s_scr[...]

    gate_c = mod_ref[0:1, 2 * D_MODEL:3 * D_MODEL]
    lng = lng_ref[...]
    lnb = lnb_ref[...]
    windows = [slice(i * WINDOW, (i + 1) * WINDOW) for i in range(nsub)]
    branch = []
    for rows in windows:
        za = _dot((a_scr[rows, :] * _silu(ga_scr[rows, :])).astype(BF16), wpa_ref[...])
        zr = _dot((r_scr[rows, :] * _silu(rg_scr[rows, :])).astype(BF16), wpr_ref[...])
        branch.append((za, zr))
    for rows, (za, zr) in zip(windows, branch):
        z = _sigmoid(ma_scr[rows, :]) * za + _sigmoid(mr_scr[rows, :]) * zr
        u = _dot(z.astype(BF16), wout_ref[...])
        t = ALPHA * x_ref[rows, :] + gate_c * u
        mu = jnp.mean(t, axis=-1, keepdims=True)
        d = t - mu
        var = jnp.mean(d * d, axis=-1, keepdims=True)
        y_ref[rows, :] = d * lax.rsqrt(var + LN_EPS) * lng + lnb

    _sample_outputs(sinks_ref, layer, sqkv_ref, sret_ref, ck_ref, cv_ref, sample_qmats, sample_logits,
                    oatt_ref, oret_ref, nk_ref, nv_ref, ns_ref)


def _fused_layer(layer, x, mod, mod_row_block, base, rtab, w_in_b, w_pa_b, w_pr_b, w_out_b,
                 ln_g, ln_b, sinks, ret_tabs, sample_qkv, sample_ret, cache_k, cache_v, state, shared_out):
    seq = x.shape[0]
    tm = PROMPT_ROWS
    steps = seq // tm
    nbatch = sample_qkv.shape[0]
    nb = nbatch // steps
    win = cache_k.shape[2]
    dmask, qdec, kdec = ret_tabs
    smem = pl.BlockSpec(memory_space=pltpu.SMEM)
    row_spec = lambda w: pl.BlockSpec((tm, w), lambda i: (i, 0))
    seq_spec = lambda w: pl.BlockSpec((nb, 1, w), lambda i: (i, 0, 0))
    cache_spec = pl.BlockSpec((None, nb, win, ATT_KV_WIDTH), lambda i: (layer, i, 0, 0))
    state_spec = pl.BlockSpec((None, nb, RET_HEADS, RET_DK, RET_DV), lambda i: (layer, i, 0, 0, 0))
    in_specs = [
        smem,
        row_spec(D_MODEL),
        pl.BlockSpec((None, SUBLANES, 3 * D_MODEL), lambda i: (layer, mod_row_block, 0),
                     pipeline_mode=pl.Buffered(1)),
        pl.BlockSpec((None, 4, LANES), lambda i: (i, 0, 0)),
        _const_spec((8, tm, LANES)),
        _layer_spec((D_MODEL, IN_COLS), layer),
        _layer_spec((ATT_WIDTH, D_MODEL), layer),
        _layer_spec((RET_WIDTH, D_MODEL), layer),
        _layer_spec((D_MODEL, D_MODEL), layer),
        _layer_spec((1, D_MODEL), layer), _layer_spec((1, D_MODEL), layer),
        _const_spec((RET_HEADS, RET_CHUNK, RET_CHUNK)),
        _const_spec((RET_HEADS, RET_CHUNK, RET_DK)),
        _const_spec((RET_HEADS, RET_CHUNK, RET_DK)),
        seq_spec(C_AG), seq_spec(2 * RET_QK_WIDTH + RET_WIDTH),
        cache_spec, cache_spec, state_spec,
    ]
    args = [sinks, x, mod, base, rtab, w_in_b, w_pa_b, w_pr_b, w_out_b, ln_g, ln_b, dmask, qdec, kdec,
            sample_qkv.reshape(nbatch, 1, C_AG), sample_ret.reshape(nbatch, 1, -1), cache_k, cache_v, state]
    assert len(args) == N_PROMPT_INPUTS
    aliases = {}
    if shared_out is not None:
        for j, arr in enumerate(shared_out):
            aliases[len(args)] = N_PROMPT_OUTPUTS - len(shared_out) + j
            args.append(arr)
            in_specs.append(pl.BlockSpec(memory_space=pl.ANY))
    return pl.pallas_call(
        functools.partial(_prompt_kernel, layer=layer, n_aliased=len(aliases)),
        out_shape=(
            jax.ShapeDtypeStruct((seq, D_MODEL), F32),
            jax.ShapeDtypeStruct((WINDOW, ATT_KV_WIDTH), F32),
            jax.ShapeDtypeStruct((WINDOW, ATT_KV_WIDTH), F32),
            jax.ShapeDtypeStruct((RET_HEADS, RET_DK, RET_DV), F32),
            jax.ShapeDtypeStruct((nbatch, 1, ATT_WIDTH), F32),
            jax.ShapeDtypeStruct((nbatch, 1, RET_WIDTH), F32),
            jax.ShapeDtypeStruct(cache_k.shape, F32),
            jax.ShapeDtypeStruct(cache_v.shape, F32),
            jax.ShapeDtypeStruct(state.shape, F32),
        ),
        grid=(steps,),
        in_specs=in_specs,
        out_specs=(
            row_spec(D_MODEL),
            pl.BlockSpec((WINDOW, ATT_KV_WIDTH), lambda i: (0, 0)),
            pl.BlockSpec((WINDOW, ATT_KV_WIDTH), lambda i: (0, 0)),
            pl.BlockSpec((RET_HEADS, RET_DK, RET_DV), lambda i: (0, 0, 0)),
            seq_spec(ATT_WIDTH), seq_spec(RET_WIDTH),
            cache_spec, cache_spec, state_spec,
        ),
        input_output_aliases=aliases,
        scratch_shapes=[
            pltpu.VMEM((4, tm, LANES), F32),
            pltpu.VMEM((tm, D_MODEL), BF16),
            pltpu.VMEM((tm, ATT_WIDTH), BF16),
            pltpu.VMEM((4, WINDOW + tm, LANES), BF16),
            pltpu.VMEM((4, WINDOW + tm, LANES), BF16),
            pltpu.VMEM((tm, ATT_WIDTH), F32),
            pltpu.VMEM((tm, ATT_WIDTH), F32),
            pltpu.VMEM((tm, RET_QK_WIDTH), F32),
            pltpu.VMEM((tm, RET_QK_WIDTH), F32),
            pltpu.VMEM((tm, RET_WIDTH), BF16),
            pltpu.VMEM((tm, RET_WIDTH), F32),
            pltpu.VMEM((tm, RET_WIDTH), F32),
            pltpu.VMEM((RET_HEADS, RET_DK, RET_DV), F32),
            pltpu.VMEM((tm, D_MODEL), F32),
            pltpu.VMEM((tm, D_MODEL), F32),
        ],
        compiler_params=pltpu.CompilerParams(
            dimension_semantics=("arbitrary",), vmem_limit_bytes=VMEM_LIMIT_BYTES),
        name="fused_layer",
    )(*args)


def _sample_proj_kernel(x_ref, mod_ref, tab_ref, win_ref,
                        qkv_ref, ga_ref, ret_ref, rg_ref, mg_ref):
    rows = x_ref.shape[0]
    x = x_ref[...]
    shift = mod_ref[:, 0:D_MODEL]
    scale = mod_ref[:, D_MODEL:2 * D_MODEL]
    hb = (x * (1.0 + scale) + shift).astype(BF16)

    def proj(c0, c1):
        return _dot(hb, win_ref[:, c0:c1])

    lane = _lane_iota((rows, LANES))
    first_half32 = (lane & 32) == 0
    ca = tab_ref[0:1, :]
    sa = tab_ref[1:2, :]
    cr = tab_ref[2:3, :]
    sr = tab_ref[3:4, :]

    qkv = proj(C_AQ, C_AG)
    for t in range(ATT_WIDTH // LANES):
        qt = _rope_attn_tile(qkv[:, t * LANES:(t + 1) * LANES], ca, sa, first_half32)
        qkv_ref[:, t * LANES:(t + 1) * LANES] = qt * ATT_SCALE
    qkv_ref[:, C_AK:C_AV] = _rope_attn_tile(qkv[:, C_AK:C_AV], ca, sa, first_half32)
    qkv_ref[:, C_AV:C_AG] = qkv[:, C_AV:C_AG]
    ga_ref[...] = _silu(proj(C_AG, C_RQ))

    rqk = proj(C_RQ, C_RV)
    for h in range(RET_HEADS):
        sl = slice(h * RET_DK, (h + 1) * RET_DK)
        ret_ref[:, sl] = _rope_ret_tile(rqk[:, sl], cr, sr)
        sk = slice(RET_QK_WIDTH + h * RET_DK, RET_QK_WIDTH + (h + 1) * RET_DK)
        ret_ref[:, sk] = _rope_ret_tile(rqk[:, sk], cr, sr) * RET_K_SCALE
    ret_ref[:, 2 * RET_QK_WIDTH:] = proj(C_RV, C_RG)
    rg_ref[...] = _silu(proj(C_RG, C_MA))
    mg_ref[...] = _sigmoid(proj(C_MA, IN_COLS))


def _sample_proj(layer, x, mod, tab, w_in_b):
    rows = x.shape[0]
    widths = (C_AG, ATT_WIDTH, 2 * RET_QK_WIDTH + RET_WIDTH, RET_WIDTH, 2 * D_MODEL)
    return pl.pallas_call(
        _sample_proj_kernel,
        out_shape=tuple(jax.ShapeDtypeStruct((rows, w), F32) for w in widths),
        grid=(1,),
        in_specs=[
            _const_spec((rows, D_MODEL)),
            _layer_spec((rows, 3 * D_MODEL), layer),
            _const_spec((4, LANES)),
            _layer_spec((D_MODEL, IN_COLS), layer),
        ],
        out_specs=tuple(pl.BlockSpec((rows, w), lambda i: (0, 0)) for w in widths),
        compiler_params=pltpu.CompilerParams(
            dimension_semantics=("arbitrary",), vmem_limit_bytes=VMEM_LIMIT_BYTES),
        name="sample_proj",
    )(x, mod, tab, w_in_b)


def _sample_scores_and_state(qkv_ref, ret_ref, ck_ref, st_ref, ns_ref):
    nb = qkv_ref.shape[0]
    rowi = lax.broadcasted_iota(jnp.int32, (ATT_HEADS, LANES), 0)
    lanei = lax.broadcasted_iota(jnp.int32, (ATT_HEADS, LANES), 1)
    lane_group = lanei // ATT_HEAD_DIM
    dr = lax.broadcasted_iota(jnp.int32, (RET_DK, RET_DK), 0)
    dc = lax.broadcasted_iota(jnp.int32, (RET_DK, RET_DK), 1)
    diag = dr == dc

    qmats, logits = [], []
    for b in range(nb):
        qkv = qkv_ref[b]
        qmat = jnp.zeros((ATT_HEADS, LANES), F32)
        for h in range(ATT_HEADS):
            t, p, g = h // 2, h % 2, h // ATT_GROUP
            tile = qkv[:, t * LANES:(t + 1) * LANES]
            src = tile if p == g else pltpu.roll(tile, 64, 1)
            qmat = jnp.where((rowi == h) & (lane_group == g), jnp.broadcast_to(src, (ATT_HEADS, LANES)), qmat)
        qmats.append(qmat)
        logits.append(lax.dot_general(qmat.astype(BF16), ck_ref[b].astype(BF16), NT,
                                      preferred_element_type=F32))

    for b in range(nb):
        ret = ret_ref[b]
        for h in range(RET_HEADS):
            kh = ret[:, RET_QK_WIDTH + h * RET_DK:RET_QK_WIDTH + (h + 1) * RET_DK]
            vh = ret[:, 2 * RET_QK_WIDTH + h * RET_DV:2 * RET_QK_WIDTH + (h + 1) * RET_DV]
            kdiag = jnp.where(diag, jnp.broadcast_to(kh, (RET_DK, RET_DK)), 0.0).astype(BF16)
            vfull = jnp.broadcast_to(vh, (RET_DK, RET_DV)).astype(BF16)
            ns_ref[b, h] = st_ref[b, h] * TOKEN_DECAY[h] + _dot(kdiag, vfull)
    return qmats, logits


def _sample_outputs(sinks_ref, layer, qkv_ref, ret_ref, ck_ref, cv_ref, qmats, logits,
                    oatt_ref, oret_ref, nk_ref, nv_ref, ns_ref):
    nb = qkv_ref.shape[0]
    sink_col = jnp.zeros((ATT_HEADS, 1), F32)
    rowc = lax.broadcasted_iota(jnp.int32, (ATT_HEADS, 1), 0)
    for h in range(ATT_HEADS):
        sink_col = jnp.where(rowc == h, sinks_ref[layer, h], sink_col)
    lo64_row = _lane_iota((1, LANES)) < 64

    for b in range(nb):
        qkv = qkv_ref[b]
        k_new = qkv[:, C_AK:C_AV]
        v_new = qkv[:, C_AV:C_AG]
        s_c = logits[b]
        s_self = jnp.sum(qmats[b] * k_new, axis=-1, keepdims=True)
        m = jnp.maximum(jnp.maximum(jnp.max(s_c, axis=-1, keepdims=True), s_self), sink_col)
        p_c = jnp.exp(s_c - m)
        p_self = jnp.exp(s_self - m)
        denom = jnp.sum(p_c, axis=-1, keepdims=True) + p_self + jnp.exp(sink_col - m)
        o = (_dot(p_c.astype(BF16), cv_ref[b].astype(BF16)) + p_self * v_new) / denom
        o_sw = pltpu.roll(o, 64, 1)
        for t in range(ATT_WIDTH // LANES):
            g = t // 2
            first = (o if g == 0 else o_sw)[2 * t:2 * t + 1, :]
            second = (o_sw if g == 0 else o)[2 * t + 1:2 * t + 2, :]
            oatt_ref[b, :, t * LANES:(t + 1) * LANES] = jnp.where(lo64_row, first, second)

        nk_ref[b, 0:WINDOW - 1, :] = ck_ref[b, 1:WINDOW, :]
        nk_ref[b, WINDOW - 1:WINDOW, :] = k_new
        nv_ref[b, 0:WINDOW - 1, :] = cv_ref[b, 1:WINDOW, :]
        nv_ref[b, WINDOW - 1:WINDOW, :] = v_new

    for b in range(nb):
        ret = ret_ref[b]
        for h in range(RET_HEADS):
            qh = ret[:, h * RET_DK:(h + 1) * RET_DK]
            q8 = jnp.broadcast_to(qh, (SUBLANES, RET_DK)).astype(BF16)
            oh = _dot(q8, ns_ref[b, h].astype(BF16))[0:1, :]
            ms = jnp.mean(oh * oh, axis=-1, keepdims=True)
            oret_ref[b, :, h * RET_DV:(h + 1) * RET_DV] = oh * lax.rsqrt(ms + RMS_EPS)


def _sample_out_kernel(x_ref, mod_ref, oatt_ref, ga_ref, oret_ref, rg_ref, mg_ref,
                       wpa_ref, wpr_ref, wout_ref, lng_ref, lnb_ref, y_ref):
    gate_c = mod_ref[:, 2 * D_MODEL:3 * D_MODEL]
    za = _dot((oatt_ref[...] * ga_ref[...]).astype(BF16), wpa_ref[...])
    zr = _dot((oret_ref[...] * rg_ref[...]).astype(BF16), wpr_ref[...])
    z = mg_ref[:, 0:D_MODEL] * za + mg_ref[:, D_MODEL:2 * D_MODEL] * zr
    u = _dot(z.astype(BF16), wout_ref[...])
    t = ALPHA * x_ref[...] + gate_c * u
    mu = jnp.mean(t, axis=-1, keepdims=True)
    d = t - mu
    var = jnp.mean(d * d, axis=-1, keepdims=True)
    y_ref[...] = d * lax.rsqrt(var + LN_EPS) * lng_ref[...] + lnb_ref[...]


def _sample_out(layer, x, mod, oatt, ga, oret, rg, mg, w_pa_b, w_pr_b, w_out_b, ln_g, ln_b):
    rows = x.shape[0]
    return pl.pallas_call(
        _sample_out_kernel,
        out_shape=jax.ShapeDtypeStruct(x.shape, F32),
        grid=(1,),
        in_specs=[
            _const_spec((rows, D_MODEL)),
            _layer_spec((rows, 3 * D_MODEL), layer),
            _const_spec((rows, ATT_WIDTH)), _const_spec((rows, ATT_WIDTH)),
            _const_spec((rows, RET_WIDTH)), _const_spec((rows, RET_WIDTH)),
            _const_spec((rows, 2 * D_MODEL)),
            _layer_spec((ATT_WIDTH, D_MODEL), layer),
            _layer_spec((RET_WIDTH, D_MODEL), layer),
            _layer_spec((D_MODEL, D_MODEL), layer),
            _layer_spec((1, D_MODEL), layer), _layer_spec((1, D_MODEL), layer),
        ],
        out_specs=pl.BlockSpec((rows, D_MODEL), lambda i: (0, 0)),
        compiler_params=pltpu.CompilerParams(
            dimension_semantics=("arbitrary",), vmem_limit_bytes=VMEM_LIMIT_BYTES),
        name="sample_out",
    )(x, mod, oatt, ga, oret, rg, mg, w_pa_b, w_pr_b, w_out_b, ln_g, ln_b)


def kernel(x_prompt, x_sample, c_prompt, c_sample, cache_k, cache_v, state_ret, w_in, attn_sinks,
           w_cond, b_cond, w_proj_attn, w_proj_ret, w_out, ln_g, ln_b):
    seq = x_prompt.shape[1]
    nbatch = x_sample.shape[0]
    win = cache_k.shape[2]
    assert seq % PROMPT_ROWS == 0 and nbatch % (seq // PROMPT_ROWS) == 0 and nbatch % SUBLANES == 0

    w_in_b = w_in.astype(BF16)
    w_pa_b = w_proj_attn.astype(BF16)
    w_pr_b = w_proj_ret.astype(BF16)
    w_out_b = w_out.astype(BF16)
    ln_g3 = ln_g.reshape(DEPTH, 1, D_MODEL)
    ln_b3 = ln_b.reshape(DEPTH, 1, D_MODEL)

    c_all = jnp.concatenate([c_sample, c_prompt, jnp.zeros((SUBLANES - 1, D_MODEL), F32)], axis=0)
    mod = _cond_call(c_all, w_cond, b_cond)
    prompt_mod_block = nbatch // SUBLANES

    base, rtab = _prompt_rope_tables(seq, PROMPT_ROWS)
    stab = _sample_rope_table(float(PAST_LEN))
    ret_tabs = _retention_tables()

    ck = cache_k.reshape(DEPTH, nbatch, win, ATT_KV_WIDTH)
    cv = cache_v.reshape(DEPTH, nbatch, win, ATT_KV_WIDTH)

    yp = x_prompt[0]
    ys = x_sample[:, 0, :]
    kp, vp, sp = [], [], []
    prev = None
    for l in range(DEPTH):
        qkv, ga, ret, rg, mg = _sample_proj(l, ys, mod, stab, w_in_b)
        yp, k_new, v_new, s_new, oatt, oret, nk, nv, ns = _fused_layer(
            l, yp, mod, prompt_mod_block, base, rtab, w_in_b, w_pa_b, w_pr_b, w_out_b,
            ln_g3, ln_b3, attn_sinks, ret_tabs, qkv, ret, ck, cv, state_ret, prev)
        kp.append(k_new.reshape(1, WINDOW, ATT_KV_HEADS, ATT_HEAD_DIM))
        vp.append(v_new.reshape(1, WINDOW, ATT_KV_HEADS, ATT_HEAD_DIM))
        sp.append(s_new[None])
        prev = (nk, nv, ns)
        ys = _sample_out(l, ys, mod, oatt.reshape(nbatch, ATT_WIDTH), ga, oret.reshape(nbatch, RET_WIDTH),
                         rg, mg, w_pa_b, w_pr_b, w_out_b, ln_g3, ln_b3)

    nk, nv, ns = prev
    return (yp[None], ys[:, None, :], jnp.stack(kp), jnp.stack(vp), jnp.stack(sp),
            nk.reshape(cache_k.shape), nv.reshape(cache_v.shape), ns)
```

```python
import functools

import jax
import jax.numpy as jnp
import numpy as np
from jax import lax
from jax.experimental import pallas as pl
from jax.experimental.pallas import tpu as pltpu

D_MODEL = 1024
DEPTH = 2
PAST_LEN = 16384
ATT_HEADS = 8
ATT_KV_HEADS = 2
ATT_HEAD_DIM = 64
ATT_GROUP = ATT_HEADS // ATT_KV_HEADS
ATT_WIDTH = ATT_HEADS * ATT_HEAD_DIM
ATT_KV_WIDTH = ATT_KV_HEADS * ATT_HEAD_DIM
WINDOW = 128
RET_HEADS = 4
RET_DK = 128
RET_DV = 256
RET_QK_WIDTH = RET_HEADS * RET_DK
RET_WIDTH = RET_HEADS * RET_DV
RET_CHUNK = 128
ROPE_THETA = 10000.0
ALPHA = (2.0 * DEPTH) ** 0.25
LN_EPS = 1e-5
RMS_EPS = 1e-6
ATT_SCALE = ATT_HEAD_DIM ** -0.5
RET_K_SCALE = RET_DK ** -0.5

C_AQ = 0
C_AK = C_AQ + ATT_WIDTH
C_AV = C_AK + ATT_KV_WIDTH
C_AG = C_AV + ATT_KV_WIDTH
C_RQ = C_AG + ATT_WIDTH
C_RK = C_RQ + RET_QK_WIDTH
C_RV = C_RK + RET_QK_WIDTH
C_RG = C_RV + RET_WIDTH
C_MA = C_RG + RET_WIDTH
C_MR = C_MA + D_MODEL
IN_COLS = C_MR + D_MODEL

LANES = 128
SUBLANES = 8
VMEM_LIMIT_BYTES = 56 * 1024 * 1024

PROMPT_ROWS = 256

BF16 = jnp.bfloat16
F32 = jnp.float32
NT = (((1,), (1,)), ((), ()))

_LOG_GAMMA = np.log(1.0 - 2.0 ** (-5.0 - np.arange(RET_HEADS, dtype=np.float64)))
CHUNK_DECAY = tuple(float(v) for v in np.exp(RET_CHUNK * _LOG_GAMMA))
TOKEN_DECAY = tuple(float(v) for v in np.exp(_LOG_GAMMA))


def _sigmoid(x):
    return 0.5 * jnp.tanh(0.5 * x) + 0.5


def _silu(x):
    return x * _sigmoid(x)


def _dot(a, b):
    return jnp.dot(a, b, preferred_element_type=F32)


def _rope_attn_tile(x, cos, sin_signed, first_half):
    rot = jnp.where(first_half, pltpu.roll(x, LANES - 32, 1), pltpu.roll(x, 32, 1))
    return x * cos + rot * sin_signed


def _rope_ret_tile(x, cos, sin_signed):
    return x * cos + pltpu.roll(x, 64, 1) * sin_signed


def _lane_iota(shape):
    return lax.broadcasted_iota(jnp.int32, shape, len(shape) - 1)


def _layer_spec(shape, layer):
    nd = len(shape)
    return pl.BlockSpec((None,) + tuple(shape), lambda i: (layer,) + (0,) * nd,
                        pipeline_mode=pl.Buffered(1))


def _const_spec(shape):
    nd = len(shape)
    return pl.BlockSpec(tuple(shape), lambda i: (0,) * nd, pipeline_mode=pl.Buffered(1))


def _rope_lane_patterns():
    lane = np.arange(LANES)
    f_att = ROPE_THETA ** (-(lane % 32) / 32.0)
    s_att = np.where(lane % 64 < 32, -1.0, 1.0)
    f_ret = ROPE_THETA ** (-(lane % 64) / 64.0)
    s_ret = np.where(lane < 64, -1.0, 1.0)
    return (f_att, s_att), (f_ret, s_ret)


def _prompt_rope_tables(seq, tm):
    starts = np.arange(seq // tm, dtype=np.float64)[:, None] * tm
    offs = np.arange(tm, dtype=np.float64)[:, None]
    base, within = [], []
    for freq, sign in _rope_lane_patterns():
        base += [np.cos(starts * freq), np.sin(starts * freq)]
        c, s = np.cos(offs * freq), np.sin(offs * freq)
        within += [c, s, sign * c, sign * s]
    return (jnp.asarray(np.stack(base, axis=1), F32),
            jnp.asarray(np.stack(within, axis=0), F32))


def _sample_rope_table(pos):
    rows = []
    for freq, sign in _rope_lane_patterns():
        rows += [np.cos(pos * freq), sign * np.sin(pos * freq)]
    return jnp.asarray(np.stack(rows, axis=0), F32)


def _retention_tables():
    c = RET_CHUNK
    idx = np.arange(c, dtype=np.float64)
    diff = idx[:, None] - idx[None, :]
    lg = _LOG_GAMMA[:, None, None]
    dmask = np.where(diff >= 0, np.exp(np.maximum(diff, 0.0)[None] * lg), 0.0)
    qdec = np.broadcast_to(np.exp((idx + 1.0)[None, :, None] * lg), (RET_HEADS, c, RET_DK))
    kdec = np.broadcast_to(np.exp((c - 1.0 - idx)[None, :, None] * lg), (RET_HEADS, c, RET_DK))
    return jnp.asarray(dmask, F32), jnp.asarray(qdec, F32), jnp.asarray(kdec, F32)


def _cond_kernel(c_ref, w_ref, b_ref, o_ref):
    a = _silu(c_ref[...]).astype(BF16)
    o_ref[...] = _dot(a, w_ref[...].astype(BF16)) + b_ref[...]


def _cond_call(c_all, w_cond, b_cond):
    rows = c_all.shape[0]
    tn = 768
    return pl.pallas_call(
        _cond_kernel,
        out_shape=jax.ShapeDtypeStruct((DEPTH, rows, 3 * D_MODEL), F32),
        grid=(DEPTH, 3 * D_MODEL // tn),
        in_specs=[
            pl.BlockSpec((rows, D_MODEL), lambda l, j: (0, 0)),
            pl.BlockSpec((None, D_MODEL, tn), lambda l, j: (l, 0, j)),
            pl.BlockSpec((None, 1, tn), lambda l, j: (l, 0, j)),
        ],
        out_specs=pl.BlockSpec((None, rows, tn), lambda l, j: (l, 0, j)),
        compiler_params=pltpu.CompilerParams(
            dimension_semantics=("arbitrary", "arbitrary"), vmem_limit_bytes=VMEM_LIMIT_BYTES),
        name="cond_mod",
    )(c_all, w_cond, b_cond.reshape(DEPTH, 1, 3 * D_MODEL))


N_PROMPT_INPUTS = 19
N_PROMPT_OUTPUTS = 9


def _prompt_kernel(*refs, layer, n_aliased):
    (sinks_ref, x_ref, mod_ref, base_ref, rtab_ref,
     win_ref, wpa_ref, wpr_ref, wout_ref, lng_ref, lnb_ref,
     dmask_ref, qdec_ref, kdec_ref,
     sqkv_ref, sret_ref, ck_ref, cv_ref, st_ref) = refs[:N_PROMPT_INPUTS]
    refs = refs[N_PROMPT_INPUTS + n_aliased:]
    (y_ref, knew_ref, vnew_ref, snew_ref,
     oatt_ref, oret_ref, nk_ref, nv_ref, ns_ref) = refs[:N_PROMPT_OUTPUTS]
    (tab_scr, hb_scr, q_scr, kvar_scr, vvar_scr, ga_scr, a_scr,
     rq_scr, rk_scr, rv_scr, rg_scr, r_scr, s_scr,
     ma_scr, mr_scr) = refs[N_PROMPT_OUTPUTS:]
    step = pl.program_id(0)
    tm = x_ref.shape[0]
    nsub = tm // WINDOW

    @pl.when(step == 0)
    def _():
        kvar_scr[:, 0:WINDOW, :] = jnp.zeros((4, WINDOW, LANES), BF16)
        vvar_scr[:, 0:WINDOW, :] = jnp.zeros((4, WINDOW, LANES), BF16)
        s_scr[...] = jnp.zeros(s_scr.shape, F32)

    x = x_ref[...]
    shift = mod_ref[0:1, 0:D_MODEL]
    scale = mod_ref[0:1, D_MODEL:2 * D_MODEL]
    hb_scr[...] = (x * (1.0 + scale) + shift).astype(BF16)

    for fam in range(2):
        cb = base_ref[2 * fam:2 * fam + 1, :]
        sb = base_ref[2 * fam + 1:2 * fam + 2, :]
        tab_scr[2 * fam] = cb * rtab_ref[4 * fam] - sb * rtab_ref[4 * fam + 1]
        tab_scr[2 * fam + 1] = sb * rtab_ref[4 * fam + 2] + cb * rtab_ref[4 * fam + 3]

    def proj(c0, c1):
        return _dot(hb_scr[...], win_ref[:, c0:c1])

    lane = _lane_iota((tm, LANES))
    first_half32 = (lane & 32) == 0
    lo64 = lane < 64

    def rope_attn(t):
        return _rope_attn_tile(t, tab_scr[0], tab_scr[1], first_half32)

    def rope_ret(t):
        return _rope_ret_tile(t, tab_scr[2], tab_scr[3])

    kv = proj(C_AK, C_AG)
    k_rot = rope_attn(kv[:, 0:LANES])
    v_raw = kv[:, LANES:2 * LANES]
    knew_ref[...] = k_rot[tm - WINDOW:, :]
    vnew_ref[...] = v_raw[tm - WINDOW:, :]

    def store_variants(scr, t):
        swapped = pltpu.roll(t, 64, 1)
        zero = jnp.zeros_like(t)
        scr[0, WINDOW:WINDOW + tm, :] = jnp.where(lo64, t, zero).astype(BF16)
        scr[1, WINDOW:WINDOW + tm, :] = jnp.where(lo64, zero, swapped).astype(BF16)
        scr[2, WINDOW:WINDOW + tm, :] = jnp.where(lo64, swapped, zero).astype(BF16)
        scr[3, WINDOW:WINDOW + tm, :] = jnp.where(lo64, zero, t).astype(BF16)

    store_variants(kvar_scr, k_rot)
    store_variants(vvar_scr, v_raw)

    qp = proj(C_AQ, C_AK)
    for t in range(ATT_WIDTH // LANES):
        qt = rope_attn(qp[:, t * LANES:(t + 1) * LANES])
        q_scr[:, t * LANES:(t + 1) * LANES] = (qt * ATT_SCALE).astype(BF16)
    ga_scr[...] = proj(C_AG, C_RQ)

    def chunk_rq():
        p = proj(C_RQ, C_RK)
        for h in range(RET_HEADS):
            sl = slice(h * RET_DK, (h + 1) * RET_DK)
            rq_scr[:, sl] = rope_ret(p[:, sl])

    def chunk_rk():
        p = proj(C_RK, C_RV)
        for h in range(RET_HEADS):
            sl = slice(h * RET_DK, (h + 1) * RET_DK)
            rk_scr[:, sl] = rope_ret(p[:, sl]) * RET_K_SCALE

    def half_chunk(c0, dst, fn, half):
        w = dst.shape[1] // 2
        def run():
            dst[:, half * w:(half + 1) * w] = fn(proj(c0 + half * w, c0 + (half + 1) * w))
        return run

    to_bf16 = lambda v: v.astype(BF16)
    raw = lambda v: v
    att_companions = [
        chunk_rq, chunk_rk,
        half_chunk(C_RV, rv_scr, to_bf16, 0), half_chunk(C_RV, rv_scr, to_bf16, 1),
        half_chunk(C_RG, rg_scr, raw, 0), half_chunk(C_RG, rg_scr, raw, 1),
        half_chunk(C_MA, ma_scr, raw, 0), half_chunk(C_MA, ma_scr, raw, 1),
    ]

    row = lax.broadcasted_iota(jnp.int32, (WINDOW, 2 * WINDOW), 0)
    col = lax.broadcasted_iota(jnp.int32, (WINDOW, 2 * WINDOW), 1)
    in_window = col <= row + WINDOW
    mask_std = (col >= row) & in_window
    off = jnp.where(step > 0, 0, 4 * WINDOW)
    mask_first = ((col >= row + off) | (col >= WINDOW)) & in_window
    lo64_w = _lane_iota((WINDOW, LANES)) < 64

    def attention_logits(i, t):
        r0 = i * WINDOW
        g = t // 2
        qt = q_scr[r0:r0 + WINDOW, t * LANES:(t + 1) * LANES]
        return [lax.dot_general(qt, kvar_scr[2 * g + p, r0:r0 + 2 * WINDOW, :], NT,
                                preferred_element_type=F32) for p in range(2)]

    def attention_values(i, t, logits):
        r0 = i * WINDOW
        mask = mask_first if i == 0 else mask_std
        g = t // 2
        o_acc = None
        recips = []
        for p in range(2):
            head = 2 * t + p
            s = jnp.where(mask, logits[p], -jnp.inf)
            sink = sinks_ref[layer, head]
            m = jnp.maximum(jnp.max(s, axis=-1, keepdims=True), sink)
            pe = jnp.exp(s - m)
            denom = jnp.sum(pe, axis=-1, keepdims=True) + jnp.exp(sink - m)
            recips.append(1.0 / denom)
            vals = vvar_scr[2 * g + p, r0:r0 + 2 * WINDOW, :]
            o = _dot(pe.astype(BF16), vals)
            o_acc = o if o_acc is None else o_acc + o
        o_t = o_acc * jnp.where(lo64_w, recips[0], recips[1])
        a_scr[r0:r0 + WINDOW, t * LANES:(t + 1) * LANES] = o_t

    n = 0
    for i in range(nsub):
        for t in range(ATT_WIDTH // LANES):
            logits = attention_logits(i, t)
            if n < len(att_companions):
                att_companions[n]()
            n += 1
            attention_values(i, t, logits)
    for run in att_companions[n:]:
        run()

    kvar_scr[:, 0:WINDOW, :] = kvar_scr[:, tm:tm + WINDOW, :]
    vvar_scr[:, 0:WINDOW, :] = vvar_scr[:, tm:tm + WINDOW, :]

    def ret_slices(c, h):
        rows = slice(c * RET_CHUNK, (c + 1) * RET_CHUNK)
        return rows, slice(h * RET_DK, (h + 1) * RET_DK), slice(h * RET_DV, (h + 1) * RET_DV)

    def retention_scores_and_state(c):
        inners, s_olds = [], []
        for h in range(RET_HEADS):
            rows, sk, sv = ret_slices(c, h)
            qh = rq_scr[rows, sk]
            kh = rk_scr[rows, sk]
            vh = rv_scr[rows, sv]
            inners.append(lax.dot_general(qh.astype(BF16), kh.astype(BF16), NT, preferred_element_type=F32))
            s_old = s_scr[h]
            s_olds.append(s_old.astype(BF16))
            kd = (kh * kdec_ref[h]).astype(BF16)
            s_scr[h] = s_old * CHUNK_DECAY[h] + lax.dot_general(
                kd, vh, (((0,), (0,)), ((), ())), preferred_element_type=F32)
        return inners, s_olds

    def retention_outputs(c, inners, s_olds):
        for h in range(RET_HEADS):
            rows, sk, sv = ret_slices(c, h)
            qh = rq_scr[rows, sk]
            lhs = jnp.concatenate([(inners[h] * dmask_ref[h]).astype(BF16),
                                   (qh * qdec_ref[h]).astype(BF16)], axis=1)
            rhs = jnp.concatenate([rv_scr[rows, sv], s_olds[h]], axis=0)
            o = _dot(lhs, rhs)
            ms = jnp.mean(o * o, axis=-1, keepdims=True)
            on = o * lax.rsqrt(ms + RMS_EPS)
            r_scr[rows, sv] = on

    ret_companions = [half_chunk(C_MR, mr_scr, raw, 0), half_chunk(C_MR, mr_scr, raw, 1)]
    n = 0
    for c in range(nsub):
        inners, s_olds = retention_scores_and_state(c)
        if n < len(ret_companions):
            ret_companions[n]()
        n += 1
        retention_outputs(c, inners, s_olds)
    for run in ret_companions[n:]:
        run()
    sample_qmats, sample_logits = _sample_scores_and_state(sqkv_ref, sret_ref, ck_ref, st_ref, ns_ref)

    @pl.when(step == pl.num_programs(0) - 1)
    def _():
        snew_ref[...] = s_scr[...]

    gate_c = mod_ref[0:1, 2 * D_MODEL:3 * D_MODEL]
    lng = lng_ref[...]
    lnb = lnb_ref[...]
    windows = [slice(i * WINDOW, (i + 1) * WINDOW) for i in range(nsub)]
    branch = []
    for rows in windows:
        za = _dot((a_scr[rows, :] * _silu(ga_scr[rows, :])).astype(BF16), wpa_ref[...])
        zr = _dot((r_scr[rows, :] * _silu(rg_scr[rows, :])).astype(BF16), wpr_ref[...])
        branch.append((za, zr))
    for rows, (za, zr) in zip(windows, branch):
        z = _sigmoid(ma_scr[rows, :]) * za + _sigmoid(mr_scr[rows, :]) * zr
        u = _dot(z.astype(BF16), wout_ref[...])
        t = ALPHA * x_ref[rows, :] + gate_c * u
        mu = jnp.mean(t, axis=-1, keepdims=True)
        d = t - mu
        var = jnp.mean(d * d, axis=-1, keepdims=True)
        y_ref[rows, :] = d * lax.rsqrt(var + LN_EPS) * lng + lnb

    _sample_outputs(sinks_ref, layer, sqkv_ref, sret_ref, ck_ref, cv_ref, sample_qmats, sample_logits,
                    oatt_ref, oret_ref, nk_ref, nv_ref, ns_ref)


def _fused_layer(layer, x, mod, mod_row_block, base, rtab, w_in_b, w_pa_b, w_pr_b, w_out_b,
                 ln_g, ln_b, sinks, ret_tabs, sample_qkv, sample_ret, cache_k, cache_v, state, shared_out):
    seq = x.shape[0]
    tm = PROMPT_ROWS
    steps = seq // tm
    nbatch = sample_qkv.shape[0]
    nb = nbatch // steps
    win = cache_k.shape[3]
    assert win == WINDOW == LANES
    dmask, qdec, kdec = ret_tabs
    smem = pl.BlockSpec(memory_space=pltpu.SMEM)
    row_spec = lambda w: pl.BlockSpec((tm, w), lambda i: (i, 0))
    seq_spec = lambda w: pl.BlockSpec((nb, 1, w), lambda i: (i, 0, 0))
    cache_spec = pl.BlockSpec((None, nb, ATT_KV_WIDTH, win), lambda i: (layer, i, 0, 0))
    state_spec = pl.BlockSpec((None, nb, RET_HEADS, RET_DK, RET_DV), lambda i: (layer, i, 0, 0, 0))
    in_specs = [
        smem,
        row_spec(D_MODEL),
        pl.BlockSpec((None, SUBLANES, 3 * D_MODEL), lambda i: (layer, mod_row_block, 0),
                     pipeline_mode=pl.Buffered(1)),
        pl.BlockSpec((None, 4, LANES), lambda i: (i, 0, 0)),
        _const_spec((8, tm, LANES)),
        _layer_spec((D_MODEL, IN_COLS), layer),
        _layer_spec((ATT_WIDTH, D_MODEL), layer),
        _layer_spec((RET_WIDTH, D_MODEL), layer),
        _layer_spec((D_MODEL, D_MODEL), layer),
        _layer_spec((1, D_MODEL), layer), _layer_spec((1, D_MODEL), layer),
        _const_spec((RET_HEADS, RET_CHUNK, RET_CHUNK)),
        _const_spec((RET_HEADS, RET_CHUNK, RET_DK)),
        _const_spec((RET_HEADS, RET_CHUNK, RET_DK)),
        seq_spec(C_AG), seq_spec(2 * RET_QK_WIDTH + RET_WIDTH),
        cache_spec, cache_spec, state_spec,
    ]
    args = [sinks, x, mod, base, rtab, w_in_b, w_pa_b, w_pr_b, w_out_b, ln_g, ln_b, dmask, qdec, kdec,
            sample_qkv.reshape(nbatch, 1, C_AG), sample_ret.reshape(nbatch, 1, -1), cache_k, cache_v, state]
    assert len(args) == N_PROMPT_INPUTS
    aliases = {}
    if shared_out is not None:
        for j, arr in enumerate(shared_out):
            aliases[len(args)] = N_PROMPT_OUTPUTS - len(shared_out) + j
            args.append(arr)
            in_specs.append(pl.BlockSpec(memory_space=pl.ANY))
    return pl.pallas_call(
        functools.partial(_prompt_kernel, layer=layer, n_aliased=len(aliases)),
        out_shape=(
            jax.ShapeDtypeStruct((seq, D_MODEL), F32),
            jax.ShapeDtypeStruct((WINDOW, ATT_KV_WIDTH), F32),
            jax.ShapeDtypeStruct((WINDOW, ATT_KV_WIDTH), F32),
            jax.ShapeDtypeStruct((RET_HEADS, RET_DK, RET_DV), F32),
            jax.ShapeDtypeStruct((nbatch, 1, ATT_WIDTH), F32),
            jax.ShapeDtypeStruct((nbatch, 1, RET_WIDTH), F32),
            jax.ShapeDtypeStruct(cache_k.shape, F32),
            jax.ShapeDtypeStruct(cache_v.shape, F32),
            jax.ShapeDtypeStruct(state.shape, F32),
        ),
        grid=(steps,),
        in_specs=in_specs,
        out_specs=(
            row_spec(D_MODEL),
            pl.BlockSpec((WINDOW, ATT_KV_WIDTH), lambda i: (0, 0)),
            pl.BlockSpec((WINDOW, ATT_KV_WIDTH), lambda i: (0, 0)),
            pl.BlockSpec((RET_HEADS, RET_DK, RET_DV), lambda i: (0, 0, 0)),
            seq_spec(ATT_WIDTH), seq_spec(RET_WIDTH),
            cache_spec, cache_spec, state_spec,
        ),
        input_output_aliases=aliases,
        scratch_shapes=[
            pltpu.VMEM((4, tm, LANES), F32),
            pltpu.VMEM((tm, D_MODEL), BF16),
            pltpu.VMEM((tm, ATT_WIDTH), BF16),
            pltpu.VMEM((4, WINDOW + tm, LANES), BF16),
            pltpu.VMEM((4, WINDOW + tm, LANES), BF16),
            pltpu.VMEM((tm, ATT_WIDTH), F32),
            pltpu.VMEM((tm, ATT_WIDTH), F32),
            pltpu.VMEM((tm, RET_QK_WIDTH), F32),
            pltpu.VMEM((tm, RET_QK_WIDTH), F32),
            pltpu.VMEM((tm, RET_WIDTH), BF16),
            pltpu.VMEM((tm, RET_WIDTH), F32),
            pltpu.VMEM((tm, RET_WIDTH), F32),
            pltpu.VMEM((RET_HEADS, RET_DK, RET_DV), F32),
            pltpu.VMEM((tm, D_MODEL), F32),
            pltpu.VMEM((tm, D_MODEL), F32),
        ],
        compiler_params=pltpu.CompilerParams(
            dimension_semantics=("arbitrary",), vmem_limit_bytes=VMEM_LIMIT_BYTES),
        name="fused_layer",
    )(*args)


def _sample_proj_kernel(x_ref, mod_ref, tab_ref, win_ref,
                        qkv_ref, ga_ref, ret_ref, rg_ref, mg_ref):
    rows = x_ref.shape[0]
    x = x_ref[...]
    shift = mod_ref[:, 0:D_MODEL]
    scale = mod_ref[:, D_MODEL:2 * D_MODEL]
    hb = (x * (1.0 + scale) + shift).astype(BF16)

    def proj(c0, c1):
        return _dot(hb, win_ref[:, c0:c1])

    lane = _lane_iota((rows, LANES))
    first_half32 = (lane & 32) == 0
    ca = tab_ref[0:1, :]
    sa = tab_ref[1:2, :]
    cr = tab_ref[2:3, :]
    sr = tab_ref[3:4, :]

    qkv = proj(C_AQ, C_AG)
    for t in range(ATT_WIDTH // LANES):
        qt = _rope_attn_tile(qkv[:, t * LANES:(t + 1) * LANES], ca, sa, first_half32)
        qkv_ref[:, t * LANES:(t + 1) * LANES] = qt * ATT_SCALE
    qkv_ref[:, C_AK:C_AV] = _rope_attn_tile(qkv[:, C_AK:C_AV], ca, sa, first_half32)
    qkv_ref[:, C_AV:C_AG] = qkv[:, C_AV:C_AG]
    ga_ref[...] = _silu(proj(C_AG, C_RQ))

    rqk = proj(C_RQ, C_RV)
    for h in range(RET_HEADS):
        sl = slice(h * RET_DK, (h + 1) * RET_DK)
        ret_ref[:, sl] = _rope_ret_tile(rqk[:, sl], cr, sr)
        sk = slice(RET_QK_WIDTH + h * RET_DK, RET_QK_WIDTH + (h + 1) * RET_DK)
        ret_ref[:, sk] = _rope_ret_tile(rqk[:, sk], cr, sr) * RET_K_SCALE
    ret_ref[:, 2 * RET_QK_WIDTH:] = proj(C_RV, C_RG)
    rg_ref[...] = _silu(proj(C_RG, C_MA))
    mg_ref[...] = _sigmoid(proj(C_MA, IN_COLS))


def _sample_proj(layer, x, mod, tab, w_in_b):
    rows = x.shape[0]
    widths = (C_AG, ATT_WIDTH, 2 * RET_QK_WIDTH + RET_WIDTH, RET_WIDTH, 2 * D_MODEL)
    return pl.pallas_call(
        _sample_proj_kernel,
        out_shape=tuple(jax.ShapeDtypeStruct((rows, w), F32) for w in widths),
        grid=(1,),
        in_specs=[
            _const_spec((rows, D_MODEL)),
            _layer_spec((rows, 3 * D_MODEL), layer),
            _const_spec((4, LANES)),
            _layer_spec((D_MODEL, IN_COLS), layer),
        ],
        out_specs=tuple(pl.BlockSpec((rows, w), lambda i: (0, 0)) for w in widths),
        compiler_params=pltpu.CompilerParams(
            dimension_semantics=("arbitrary",), vmem_limit_bytes=VMEM_LIMIT_BYTES),
        name="sample_proj",
    )(x, mod, tab, w_in_b)


def _sample_scores_and_state(qkv_ref, ret_ref, ck_ref, st_ref, ns_ref):
    nb = qkv_ref.shape[0]
    rowi = lax.broadcasted_iota(jnp.int32, (ATT_HEADS, LANES), 0)
    lanei = lax.broadcasted_iota(jnp.int32, (ATT_HEADS, LANES), 1)
    lane_group = lanei // ATT_HEAD_DIM
    dr = lax.broadcasted_iota(jnp.int32, (RET_DK, RET_DK), 0)
    dc = lax.broadcasted_iota(jnp.int32, (RET_DK, RET_DK), 1)
    diag = dr == dc

    qmats, logits = [], []
    for b in range(nb):
        qkv = qkv_ref[b]
        qmat = jnp.zeros((ATT_HEADS, LANES), F32)
        for h in range(ATT_HEADS):
            t, p, g = h // 2, h % 2, h // ATT_GROUP
            tile = qkv[:, t * LANES:(t + 1) * LANES]
            src = tile if p == g else pltpu.roll(tile, 64, 1)
            qmat = jnp.where((rowi == h) & (lane_group == g), jnp.broadcast_to(src, (ATT_HEADS, LANES)), qmat)
        qmats.append(qmat)
        logits.append(_dot(qmat.astype(BF16), ck_ref[b].astype(BF16)))

    for b in range(nb):
        ret = ret_ref[b]
        for h in range(RET_HEADS):
            kh = ret[:, RET_QK_WIDTH + h * RET_DK:RET_QK_WIDTH + (h + 1) * RET_DK]
            vh = ret[:, 2 * RET_QK_WIDTH + h * RET_DV:2 * RET_QK_WIDTH + (h + 1) * RET_DV]
            kdiag = jnp.where(diag, jnp.broadcast_to(kh, (RET_DK, RET_DK)), 0.0).astype(BF16)
            vfull = jnp.broadcast_to(vh, (RET_DK, RET_DV)).astype(BF16)
            ns_ref[b, h] = st_ref[b, h] * TOKEN_DECAY[h] + _dot(kdiag, vfull)
    return qmats, logits


def _sample_outputs(sinks_ref, layer, qkv_ref, ret_ref, ck_ref, cv_ref, qmats, logits,
                    oatt_ref, oret_ref, nk_ref, nv_ref, ns_ref):
    nb = qkv_ref.shape[0]
    sink_col = jnp.zeros((ATT_HEADS, 1), F32)
    rowc = lax.broadcasted_iota(jnp.int32, (ATT_HEADS, 1), 0)
    for h in range(ATT_HEADS):
        sink_col = jnp.where(rowc == h, sinks_ref[layer, h], sink_col)
    lo64_row = _lane_iota((1, LANES)) < 64
    dr = lax.broadcasted_iota(jnp.int32, (LANES, LANES), 0)
    dc = lax.broadcasted_iota(jnp.int32, (LANES, LANES), 1)
    diag = dr == dc
    last_lane = dc == WINDOW - 1

    for b in range(nb):
        qkv = qkv_ref[b]
        k_new = qkv[:, C_AK:C_AV]
        v_new = qkv[:, C_AV:C_AG]
        s_c = logits[b]
        s_self = jnp.sum(qmats[b] * k_new, axis=-1, keepdims=True)
        m = jnp.maximum(jnp.maximum(jnp.max(s_c, axis=-1, keepdims=True), s_self), sink_col)
        p_c = jnp.exp(s_c - m)
        p_self = jnp.exp(s_self - m)
        denom = jnp.sum(p_c, axis=-1, keepdims=True) + p_self + jnp.exp(sink_col - m)
        o = (lax.dot_general(p_c.astype(BF16), cv_ref[b].astype(BF16), NT, preferred_element_type=F32)
             + p_self * v_new) / denom
        o_sw = pltpu.roll(o, 64, 1)
        for t in range(ATT_WIDTH // LANES):
            g = t // 2
            first = (o if g == 0 else o_sw)[2 * t:2 * t + 1, :]
            second = (o_sw if g == 0 else o)[2 * t + 1:2 * t + 2, :]
            oatt_ref[b, :, t * LANES:(t + 1) * LANES] = jnp.where(lo64_row, first, second)

        for new_row, src_ref, dst_ref in ((k_new, ck_ref, nk_ref), (v_new, cv_ref, nv_ref)):
            new_col = jnp.sum(jnp.where(diag, jnp.broadcast_to(new_row, (LANES, LANES)), 0.0),
                              axis=1, keepdims=True)
            shifted = pltpu.roll(src_ref[b], WINDOW - 1, 1)
            dst_ref[b] = jnp.where(last_lane, new_col, shifted)

    for b in range(nb):
        ret = ret_ref[b]
        for h in range(RET_HEADS):
            qh = ret[:, h * RET_DK:(h + 1) * RET_DK]
            q8 = jnp.broadcast_to(qh, (SUBLANES, RET_DK)).astype(BF16)
            oh = _dot(q8, ns_ref[b, h].astype(BF16))[0:1, :]
            ms = jnp.mean(oh * oh, axis=-1, keepdims=True)
            oret_ref[b, :, h * RET_DV:(h + 1) * RET_DV] = oh * lax.rsqrt(ms + RMS_EPS)


def _sample_out_kernel(x_ref, mod_ref, oatt_ref, ga_ref, oret_ref, rg_ref, mg_ref,
                       wpa_ref, wpr_ref, wout_ref, lng_ref, lnb_ref, y_ref):
    gate_c = mod_ref[:, 2 * D_MODEL:3 * D_MODEL]
    za = _dot((oatt_ref[...] * ga_ref[...]).astype(BF16), wpa_ref[...])
    zr = _dot((oret_ref[...] * rg_ref[...]).astype(BF16), wpr_ref[...])
    z = mg_ref[:, 0:D_MODEL] * za + mg_ref[:, D_MODEL:2 * D_MODEL] * zr
    u = _dot(z.astype(BF16), wout_ref[...])
    t = ALPHA * x_ref[...] + gate_c * u
    mu = jnp.mean(t, axis=-1, keepdims=True)
    d = t - mu
    var = jnp.mean(d * d, axis=-1, keepdims=True)
    y_ref[...] = d * lax.rsqrt(var + LN_EPS) * lng_ref[...] + lnb_ref[...]


def _sample_out(layer, x, mod, oatt, ga, oret, rg, mg, w_pa_b, w_pr_b, w_out_b, ln_g, ln_b):
    rows = x.shape[0]
    return pl.pallas_call(
        _sample_out_kernel,
        out_shape=jax.ShapeDtypeStruct(x.shape, F32),
        grid=(1,),
        in_specs=[
            _const_spec((rows, D_MODEL)),
            _layer_spec((rows, 3 * D_MODEL), layer),
            _const_spec((rows, ATT_WIDTH)), _const_spec((rows, ATT_WIDTH)),
            _const_spec((rows, RET_WIDTH)), _const_spec((rows, RET_WIDTH)),
            _const_spec((rows, 2 * D_MODEL)),
            _layer_spec((ATT_WIDTH, D_MODEL), layer),
            _layer_spec((RET_WIDTH, D_MODEL), layer),
            _layer_spec((D_MODEL, D_MODEL), layer),
            _layer_spec((1, D_MODEL), layer), _layer_spec((1, D_MODEL), layer),
        ],
        out_specs=pl.BlockSpec((rows, D_MODEL), lambda i: (0, 0)),
        compiler_params=pltpu.CompilerParams(
            dimension_semantics=("arbitrary",), vmem_limit_bytes=VMEM_LIMIT_BYTES),
        name="sample_out",
    )(x, mod, oatt, ga, oret, rg, mg, w_pa_b, w_pr_b, w_out_b, ln_g, ln_b)


def kernel(x_prompt, x_sample, c_prompt, c_sample, cache_k, cache_v, state_ret, w_in, attn_sinks,
           w_cond, b_cond, w_proj_attn, w_proj_ret, w_out, ln_g, ln_b):
    seq = x_prompt.shape[1]
    nbatch = x_sample.shape[0]
    win = cache_k.shape[2]
    assert seq % PROMPT_ROWS == 0 and nbatch % (seq // PROMPT_ROWS) == 0 and nbatch % SUBLANES == 0

    w_in_b = w_in.astype(BF16)
    w_pa_b = w_proj_attn.astype(BF16)
    w_pr_b = w_proj_ret.astype(BF16)
    w_out_b = w_out.astype(BF16)
    ln_g3 = ln_g.reshape(DEPTH, 1, D_MODEL)
    ln_b3 = ln_b.reshape(DEPTH, 1, D_MODEL)

    c_all = jnp.concatenate([c_sample, c_prompt, jnp.zeros((SUBLANES - 1, D_MODEL), F32)], axis=0)
    mod = _cond_call(c_all, w_cond, b_cond)
    prompt_mod_block = nbatch // SUBLANES

    base, rtab = _prompt_rope_tables(seq, PROMPT_ROWS)
    stab = _sample_rope_table(float(PAST_LEN))
    ret_tabs = _retention_tables()

    def feature_major(c):
        return c.transpose(0, 1, 3, 4, 2).reshape(DEPTH, nbatch, ATT_KV_WIDTH, win)

    def window_major(c):
        return c.reshape(DEPTH, nbatch, ATT_KV_HEADS, ATT_HEAD_DIM, win).transpose(0, 1, 4, 2, 3)

    ck = feature_major(cache_k)
    cv = feature_major(cache_v)

    yp = x_prompt[0]
    ys = x_sample[:, 0, :]
    kp, vp, sp = [], [], []
    prev = None
    for l in range(DEPTH):
        qkv, ga, ret, rg, mg = _sample_proj(l, ys, mod, stab, w_in_b)
        yp, k_new, v_new, s_new, oatt, oret, nk, nv, ns = _fused_layer(
            l, yp, mod, prompt_mod_block, base, rtab, w_in_b, w_pa_b, w_pr_b, w_out_b,
            ln_g3, ln_b3, attn_sinks, ret_tabs, qkv, ret, ck, cv, state_ret, prev)
        kp.append(k_new.reshape(1, WINDOW, ATT_KV_HEADS, ATT_HEAD_DIM))
        vp.append(v_new.reshape(1, WINDOW, ATT_KV_HEADS, ATT_HEAD_DIM))
        sp.append(s_new[None])
        prev = (nk, nv, ns)
        ys = _sample_out(l, ys, mod, oatt.reshape(nbatch, ATT_WIDTH), ga, oret.reshape(nbatch, RET_WIDTH),
                         rg, mg, w_pa_b, w_pr_b, w_out_b, ln_g3, ln_b3)

    nk, nv, ns = prev
    return (yp[None], ys[:, None, :], jnp.stack(kp), jnp.stack(vp), jnp.stack(sp),
            window_major(nk), window_major(nv), ns)
```

```python
import functools

import jax
import jax.numpy as jnp
import numpy as np
from jax import lax
from jax.experimental import pallas as pl
from jax.experimental.pallas import tpu as pltpu

D_MODEL = 1024
DEPTH = 2
PAST_LEN = 16384
ATT_HEADS = 8
ATT_KV_HEADS = 2
ATT_HEAD_DIM = 64
ATT_GROUP = ATT_HEADS // ATT_KV_HEADS
ATT_WIDTH = ATT_HEADS * ATT_HEAD_DIM
ATT_KV_WIDTH = ATT_KV_HEADS * ATT_HEAD_DIM
WINDOW = 128
RET_HEADS = 4
RET_DK = 128
RET_DV = 256
RET_QK_WIDTH = RET_HEADS * RET_DK
RET_WIDTH = RET_HEADS * RET_DV
RET_CHUNK = 128
ROPE_THETA = 10000.0
ALPHA = (2.0 * DEPTH) ** 0.25
LN_EPS = 1e-5
RMS_EPS = 1e-6
ATT_SCALE = ATT_HEAD_DIM ** -0.5
RET_K_SCALE = RET_DK ** -0.5

C_AQ = 0
C_AK = C_AQ + ATT_WIDTH
C_AV = C_AK + ATT_KV_WIDTH
C_AG = C_AV + ATT_KV_WIDTH
C_RQ = C_AG + ATT_WIDTH
C_RK = C_RQ + RET_QK_WIDTH
C_RV = C_RK + RET_QK_WIDTH
C_RG = C_RV + RET_WIDTH
C_MA = C_RG + RET_WIDTH
C_MR = C_MA + D_MODEL
IN_COLS = C_MR + D_MODEL

LANES = 128
SUBLANES = 8
VMEM_LIMIT_BYTES = 56 * 1024 * 1024

PROMPT_ROWS = 256

BF16 = jnp.bfloat16
F32 = jnp.float32
NT = (((1,), (1,)), ((), ()))

_LOG_GAMMA = np.log(1.0 - 2.0 ** (-5.0 - np.arange(RET_HEADS, dtype=np.float64)))
CHUNK_DECAY = tuple(float(v) for v in np.exp(RET_CHUNK * _LOG_GAMMA))
TOKEN_DECAY = tuple(float(v) for v in np.exp(_LOG_GAMMA))


def _sigmoid(x):
    return 0.5 * jnp.tanh(0.5 * x) + 0.5


def _silu(x):
    return x * _sigmoid(x)


def _dot(a, b):
    return jnp.dot(a, b, preferred_element_type=F32)


def _rope_attn_tile(x, cos, sin_signed, first_half):
    rot = jnp.where(first_half, pltpu.roll(x, LANES - 32, 1), pltpu.roll(x, 32, 1))
    return x * cos + rot * sin_signed


def _rope_ret_tile(x, cos, sin_signed):
    return x * cos + pltpu.roll(x, 64, 1) * sin_signed


def _lane_iota(shape):
    return lax.broadcasted_iota(jnp.int32, shape, len(shape) - 1)


def _layer_spec(shape, layer):
    nd = len(shape)
    return pl.BlockSpec((None,) + tuple(shape), lambda i: (layer,) + (0,) * nd,
                        pipeline_mode=pl.Buffered(1))


def _const_spec(shape):
    nd = len(shape)
    return pl.BlockSpec(tuple(shape), lambda i: (0,) * nd, pipeline_mode=pl.Buffered(1))


def _rope_lane_patterns():
    lane = np.arange(LANES)
    f_att = ROPE_THETA ** (-(lane % 32) / 32.0)
    s_att = np.where(lane % 64 < 32, -1.0, 1.0)
    f_ret = ROPE_THETA ** (-(lane % 64) / 64.0)
    s_ret = np.where(lane < 64, -1.0, 1.0)
    return (f_att, s_att), (f_ret, s_ret)


def _prompt_rope_tables(seq, tm):
    starts = np.arange(seq // tm, dtype=np.float64)[:, None] * tm
    offs = np.arange(tm, dtype=np.float64)[:, None]
    base, within = [], []
    for freq, sign in _rope_lane_patterns():
        base += [np.cos(starts * freq), np.sin(starts * freq)]
        c, s = np.cos(offs * freq), np.sin(offs * freq)
        within += [c, s, sign * c, sign * s]
    return (jnp.asarray(np.stack(base, axis=1), F32),
            jnp.asarray(np.stack(within, axis=0), F32))


def _sample_rope_table(pos):
    rows = []
    for freq, sign in _rope_lane_patterns():
        rows += [np.cos(pos * freq), sign * np.sin(pos * freq)]
    return jnp.asarray(np.stack(rows, axis=0), F32)


def _retention_tables():
    c = RET_CHUNK
    idx = np.arange(c, dtype=np.float64)
    diff = idx[:, None] - idx[None, :]
    lg = _LOG_GAMMA[:, None, None]
    dmask = np.where(diff >= 0, np.exp(np.maximum(diff, 0.0)[None] * lg), 0.0)
    qdec = np.broadcast_to(np.exp((idx + 1.0)[None, :, None] * lg), (RET_HEADS, c, RET_DK))
    kdec = np.broadcast_to(np.exp((c - 1.0 - idx)[None, :, None] * lg), (RET_HEADS, c, RET_DK))
    return jnp.asarray(dmask, F32), jnp.asarray(qdec, F32), jnp.asarray(kdec, F32)


def _cond_kernel(c_ref, w_ref, b_ref, o_ref):
    a = _silu(c_ref[...]).astype(BF16)
    o_ref[...] = _dot(a, w_ref[...].astype(BF16)) + b_ref[...]


def _cond_call(c_all, w_cond, b_cond):
    rows = c_all.shape[0]
    tn = 768
    return pl.pallas_call(
        _cond_kernel,
        out_shape=jax.ShapeDtypeStruct((DEPTH, rows, 3 * D_MODEL), F32),
        grid=(DEPTH, 3 * D_MODEL // tn),
        in_specs=[
            pl.BlockSpec((rows, D_MODEL), lambda l, j: (0, 0)),
            pl.BlockSpec((None, D_MODEL, tn), lambda l, j: (l, 0, j)),
            pl.BlockSpec((None, 1, tn), lambda l, j: (l, 0, j)),
        ],
        out_specs=pl.BlockSpec((None, rows, tn), lambda l, j: (l, 0, j)),
        compiler_params=pltpu.CompilerParams(
            dimension_semantics=("arbitrary", "arbitrary"), vmem_limit_bytes=VMEM_LIMIT_BYTES),
        name="cond_mod",
    )(c_all, w_cond, b_cond.reshape(DEPTH, 1, 3 * D_MODEL))


N_PROMPT_INPUTS = 19
N_PROMPT_OUTPUTS = 9


def _prompt_kernel(*refs, layer, n_aliased):
    (sinks_ref, x_ref, mod_ref, base_ref, rtab_ref,
     win_ref, wpa_ref, wpr_ref, wout_ref, lng_ref, lnb_ref,
     dmask_ref, qdec_ref, kdec_ref,
     sqkv_ref, sret_ref, ck_ref, cv_ref, st_ref) = refs[:N_PROMPT_INPUTS]
    refs = refs[N_PROMPT_INPUTS + n_aliased:]
    (y_ref, knew_ref, vnew_ref, snew_ref,
     oatt_ref, oret_ref, nk_ref, nv_ref, ns_ref) = refs[:N_PROMPT_OUTPUTS]
    (tab_scr, hb_scr, q_scr, kvar_scr, vvar_scr, ga_scr, a_scr,
     rq_scr, rk_scr, rv_scr, rg_scr, r_scr, s_scr,
     ma_scr, mr_scr) = refs[N_PROMPT_OUTPUTS:]
    step = pl.program_id(0)
    tm = x_ref.shape[0]
    nsub = tm // WINDOW

    def proj(c0, c1):
        return _dot(hb_scr[...], win_ref[:, c0:c1])

    lane = _lane_iota((tm, LANES))
    first_half32 = (lane & 32) == 0
    lo64 = lane < 64

    def rope_attn(t):
        return _rope_attn_tile(t, tab_scr[0], tab_scr[1], first_half32)

    def rope_ret(t):
        return _rope_ret_tile(t, tab_scr[2], tab_scr[3])

    def store_variants(scr, t, fill):
        swapped = pltpu.roll(t, 64, 1)
        other = jnp.full_like(t, fill)
        scr[0, WINDOW:WINDOW + tm, :] = jnp.where(lo64, t, other).astype(BF16)
        scr[1, WINDOW:WINDOW + tm, :] = jnp.where(lo64, other, swapped).astype(BF16)
        scr[2, WINDOW:WINDOW + tm, :] = jnp.where(lo64, swapped, other).astype(BF16)
        scr[3, WINDOW:WINDOW + tm, :] = jnp.where(lo64, other, t).astype(BF16)

    def block_prepare(src_x_ref, src_base_ref):
        shift = mod_ref[0:1, 0:D_MODEL]
        scale = mod_ref[0:1, D_MODEL:2 * D_MODEL]
        hb_scr[...] = (src_x_ref[...] * (1.0 + scale) + shift).astype(BF16)
        for fam in range(2):
            cb = src_base_ref[2 * fam:2 * fam + 1, :]
            sb = src_base_ref[2 * fam + 1:2 * fam + 2, :]
            tab_scr[2 * fam] = cb * rtab_ref[4 * fam] - sb * rtab_ref[4 * fam + 1]
            tab_scr[2 * fam + 1] = sb * rtab_ref[4 * fam + 2] + cb * rtab_ref[4 * fam + 3]

    def block_head():
        kv = proj(C_AK, C_AG)
        k_rot = rope_attn(kv[:, 0:LANES])
        v_raw = kv[:, LANES:2 * LANES]
        knew_ref[...] = k_rot[tm - WINDOW:, :]
        vnew_ref[...] = v_raw[tm - WINDOW:, :]
        store_variants(kvar_scr, k_rot, 0.0)
        store_variants(vvar_scr, v_raw, 1.0)
        qp = proj(C_AQ, C_AK)
        for t in range(ATT_WIDTH // LANES):
            qt = rope_attn(qp[:, t * LANES:(t + 1) * LANES])
            q_scr[:, t * LANES:(t + 1) * LANES] = (qt * ATT_SCALE).astype(BF16)
        ga_scr[...] = proj(C_AG, C_RQ)

    @pl.when(step == 0)
    def _():
        kvar_scr[:, 0:WINDOW, :] = jnp.zeros((4, WINDOW, LANES), BF16)
        vvar_scr[:, 0:WINDOW, :] = jnp.zeros((4, WINDOW, LANES), BF16)
        s_scr[...] = jnp.zeros(s_scr.shape, F32)

    block_prepare(x_ref, base_ref)
    block_head()

    def chunk_rq():
        p = proj(C_RQ, C_RK)
        for h in range(RET_HEADS):
            sl = slice(h * RET_DK, (h + 1) * RET_DK)
            rq_scr[:, sl] = rope_ret(p[:, sl])

    def chunk_rk():
        p = proj(C_RK, C_RV)
        for h in range(RET_HEADS):
            sl = slice(h * RET_DK, (h + 1) * RET_DK)
            rk_scr[:, sl] = rope_ret(p[:, sl]) * RET_K_SCALE

    def half_chunk(c0, dst, fn, half):
        w = dst.shape[1] // 2
        def run():
            dst[:, half * w:(half + 1) * w] = fn(proj(c0 + half * w, c0 + (half + 1) * w))
        return run

    to_bf16 = lambda v: v.astype(BF16)
    raw = lambda v: v
    att_companions = [
        chunk_rq, chunk_rk,
        half_chunk(C_RV, rv_scr, to_bf16, 0), half_chunk(C_RV, rv_scr, to_bf16, 1),
        half_chunk(C_RG, rg_scr, raw, 0), half_chunk(C_RG, rg_scr, raw, 1),
        half_chunk(C_MA, ma_scr, raw, 0), half_chunk(C_MA, ma_scr, raw, 1),
    ]

    row = lax.broadcasted_iota(jnp.int32, (WINDOW, 2 * WINDOW), 0)
    col = lax.broadcasted_iota(jnp.int32, (WINDOW, 2 * WINDOW), 1)
    in_window = col <= row + WINDOW
    mask_std = (col >= row) & in_window
    off = jnp.where(step > 0, 0, 4 * WINDOW)
    mask_first = ((col >= row + off) | (col >= WINDOW)) & in_window
    lo64_w = _lane_iota((WINDOW, LANES)) < 64

    def attention_logits(i, t):
        r0 = i * WINDOW
        g = t // 2
        qt = q_scr[r0:r0 + WINDOW, t * LANES:(t + 1) * LANES]
        return [lax.dot_general(qt, kvar_scr[2 * g + p, r0:r0 + 2 * WINDOW, :], NT,
                                preferred_element_type=F32) for p in range(2)]

    def attention_values(i, t, logits):
        r0 = i * WINDOW
        mask = mask_first if i == 0 else mask_std
        g = t // 2
        outs, sink_terms = [], []
        for p in range(2):
            head = 2 * t + p
            s = jnp.where(mask, logits[p], -jnp.inf)
            sink = sinks_ref[layer, head]
            m = jnp.maximum(jnp.max(s, axis=-1, keepdims=True), sink)
            pe = jnp.exp(s - m).astype(BF16)
            sink_terms.append(jnp.exp(sink - m))
            outs.append(_dot(pe, vvar_scr[2 * g + p, r0:r0 + 2 * WINDOW, :]))
        weighted = jnp.where(lo64_w, outs[0], outs[1])
        row_sums = pltpu.roll(jnp.where(lo64_w, outs[1], outs[0]), 64, 1)
        denom = row_sums + jnp.where(lo64_w, sink_terms[0], sink_terms[1])
        a_scr[r0:r0 + WINDOW, t * LANES:(t + 1) * LANES] = weighted * (1.0 / denom)

    n = 0
    for i in range(nsub):
        for t in range(ATT_WIDTH // LANES):
            logits = attention_logits(i, t)
            if n < len(att_companions):
                att_companions[n]()
            n += 1
            attention_values(i, t, logits)
    for run in att_companions[n:]:
        run()

    kvar_scr[:, 0:WINDOW, :] = kvar_scr[:, tm:tm + WINDOW, :]
    vvar_scr[:, 0:WINDOW, :] = vvar_scr[:, tm:tm + WINDOW, :]

    def ret_slices(c, h):
        rows = slice(c * RET_CHUNK, (c + 1) * RET_CHUNK)
        return rows, slice(h * RET_DK, (h + 1) * RET_DK), slice(h * RET_DV, (h + 1) * RET_DV)

    def retention_scores_and_state(c):
        inners, s_olds = [], []
        for h in range(RET_HEADS):
            rows, sk, sv = ret_slices(c, h)
            qh = rq_scr[rows, sk]
            kh = rk_scr[rows, sk]
            vh = rv_scr[rows, sv]
            inners.append(lax.dot_general(qh.astype(BF16), kh.astype(BF16), NT, preferred_element_type=F32))
            s_old = s_scr[h]
            s_olds.append(s_old.astype(BF16))
            kd = (kh * kdec_ref[h]).astype(BF16)
            s_scr[h] = s_old * CHUNK_DECAY[h] + lax.dot_general(
                kd, vh, (((0,), (0,)), ((), ())), preferred_element_type=F32)
        return inners, s_olds

    def retention_outputs(c, inners, s_olds):
        for h in range(RET_HEADS):
            rows, sk, sv = ret_slices(c, h)
            qh = rq_scr[rows, sk]
            lhs = jnp.concatenate([(inners[h] * dmask_ref[h]).astype(BF16),
                                   (qh * qdec_ref[h]).astype(BF16)], axis=1)
            rhs = jnp.concatenate([rv_scr[rows, sv], s_olds[h]], axis=0)
            o = _dot(lhs, rhs)
            ms = jnp.mean(o * o, axis=-1, keepdims=True)
            on = o * lax.rsqrt(ms + RMS_EPS)
            r_scr[rows, sv] = on

    ret_companions = [half_chunk(C_MR, mr_scr, raw, 0), half_chunk(C_MR, mr_scr, raw, 1)]
    n = 0
    for c in range(nsub):
        inners, s_olds = retention_scores_and_state(c)
        if n < len(ret_companions):
            ret_companions[n]()
        n += 1
        retention_outputs(c, inners, s_olds)
    for run in ret_companions[n:]:
        run()
    sample_qmats, sample_logits = _sample_scores_and_state(sqkv_ref, sret_ref, ck_ref, st_ref, ns_ref)

    @pl.when(step == pl.num_programs(0) - 1)
    def _():
        snew_ref[...] = s_scr[...]

    gate_c = mod_ref[0:1, 2 * D_MODEL:3 * D_MODEL]
    lng = lng_ref[...]
    lnb = lnb_ref[...]
    windows = [slice(i * WINDOW, (i + 1) * WINDOW) for i in range(nsub)]
    branch = []
    for rows in windows:
        za = _dot((a_scr[rows, :] * _silu(ga_scr[rows, :])).astype(BF16), wpa_ref[...])
        zr = _dot((r_scr[rows, :] * _silu(rg_scr[rows, :])).astype(BF16), wpr_ref[...])
        branch.append((za, zr))
    for rows, (za, zr) in zip(windows, branch):
        z = _sigmoid(ma_scr[rows, :]) * za + _sigmoid(mr_scr[rows, :]) * zr
        u = _dot(z.astype(BF16), wout_ref[...])
        t = ALPHA * x_ref[rows, :] + gate_c * u
        mu = jnp.mean(t, axis=-1, keepdims=True)
        d = t - mu
        var = jnp.mean(d * d, axis=-1, keepdims=True)
        y_ref[rows, :] = d * lax.rsqrt(var + LN_EPS) * lng + lnb

    _sample_outputs(sinks_ref, layer, sqkv_ref, sret_ref, ck_ref, cv_ref, sample_qmats, sample_logits,
                    oatt_ref, oret_ref, nk_ref, nv_ref, ns_ref)


def _fused_layer(layer, x, mod, mod_row_block, base, rtab, w_in_b, w_pa_b, w_pr_b, w_out_b,
                 ln_g, ln_b, sinks, ret_tabs, sample_qkv, sample_ret, cache_k, cache_v, state, shared_out):
    seq = x.shape[0]
    tm = PROMPT_ROWS
    steps = seq // tm
    nbatch = sample_qkv.shape[0]
    nb = nbatch // steps
    win = cache_k.shape[3]
    assert win == WINDOW == LANES
    dmask, qdec, kdec = ret_tabs
    smem = pl.BlockSpec(memory_space=pltpu.SMEM)
    row_spec = lambda w: pl.BlockSpec((tm, w), lambda i: (i, 0))
    seq_spec = lambda w: pl.BlockSpec((nb, 1, w), lambda i: (i, 0, 0))
    cache_spec = pl.BlockSpec((None, nb, ATT_KV_WIDTH, win), lambda i: (layer, i, 0, 0))
    state_spec = pl.BlockSpec((None, nb, RET_HEADS, RET_DK, RET_DV), lambda i: (layer, i, 0, 0, 0))
    in_specs = [
        smem,
        row_spec(D_MODEL),
        pl.BlockSpec((None, SUBLANES, 3 * D_MODEL), lambda i: (layer, mod_row_block, 0),
                     pipeline_mode=pl.Buffered(1)),
        pl.BlockSpec((None, 4, LANES), lambda i: (i, 0, 0)),
        _const_spec((8, tm, LANES)),
        _layer_spec((D_MODEL, IN_COLS), layer),
        _layer_spec((ATT_WIDTH, D_MODEL), layer),
        _layer_spec((RET_WIDTH, D_MODEL), layer),
        _layer_spec((D_MODEL, D_MODEL), layer),
        _layer_spec((1, D_MODEL), layer), _layer_spec((1, D_MODEL), layer),
        _const_spec((RET_HEADS, RET_CHUNK, RET_CHUNK)),
        _const_spec((RET_HEADS, RET_CHUNK, RET_DK)),
        _const_spec((RET_HEADS, RET_CHUNK, RET_DK)),
        seq_spec(C_AG), seq_spec(2 * RET_QK_WIDTH + RET_WIDTH),
        cache_spec, cache_spec, state_spec,
    ]
    args = [sinks, x, mod, base, rtab, w_in_b, w_pa_b, w_pr_b, w_out_b, ln_g, ln_b, dmask, qdec, kdec,
            sample_qkv, sample_ret, cache_k, cache_v, state]
    assert len(args) == N_PROMPT_INPUTS
    aliases = {}
    if shared_out is not None:
        for j, arr in enumerate(shared_out):
            aliases[len(args)] = N_PROMPT_OUTPUTS - len(shared_out) + j
            args.append(arr)
            in_specs.append(pl.BlockSpec(memory_space=pl.ANY))
    return pl.pallas_call(
        functools.partial(_prompt_kernel, layer=layer, n_aliased=len(aliases)),
        out_shape=(
            jax.ShapeDtypeStruct((seq, D_MODEL), F32),
            jax.ShapeDtypeStruct((WINDOW, ATT_KV_WIDTH), F32),
            jax.ShapeDtypeStruct((WINDOW, ATT_KV_WIDTH), F32),
            jax.ShapeDtypeStruct((RET_HEADS, RET_DK, RET_DV), F32),
            jax.ShapeDtypeStruct((nbatch, 1, ATT_WIDTH), F32),
            jax.ShapeDtypeStruct((nbatch, 1, RET_WIDTH), F32),
            jax.ShapeDtypeStruct(cache_k.shape, F32),
            jax.ShapeDtypeStruct(cache_v.shape, F32),
            jax.ShapeDtypeStruct(state.shape, F32),
        ),
        grid=(steps,),
        in_specs=in_specs,
        out_specs=(
            row_spec(D_MODEL),
            pl.BlockSpec((WINDOW, ATT_KV_WIDTH), lambda i: (0, 0)),
            pl.BlockSpec((WINDOW, ATT_KV_WIDTH), lambda i: (0, 0)),
            pl.BlockSpec((RET_HEADS, RET_DK, RET_DV), lambda i: (0, 0, 0)),
            seq_spec(ATT_WIDTH), seq_spec(RET_WIDTH),
            cache_spec, cache_spec, state_spec,
        ),
        input_output_aliases=aliases,
        scratch_shapes=[
            pltpu.VMEM((4, tm, LANES), F32),
            pltpu.VMEM((tm, D_MODEL), BF16),
            pltpu.VMEM((tm, ATT_WIDTH), BF16),
            pltpu.VMEM((4, WINDOW + tm, LANES), BF16),
            pltpu.VMEM((4, WINDOW + tm, LANES), BF16),
            pltpu.VMEM((tm, ATT_WIDTH), F32),
            pltpu.VMEM((tm, ATT_WIDTH), F32),
            pltpu.VMEM((tm, RET_QK_WIDTH), F32),
            pltpu.VMEM((tm, RET_QK_WIDTH), F32),
            pltpu.VMEM((tm, RET_WIDTH), BF16),
            pltpu.VMEM((tm, RET_WIDTH), F32),
            pltpu.VMEM((tm, RET_WIDTH), F32),
            pltpu.VMEM((RET_HEADS, RET_DK, RET_DV), F32),
            pltpu.VMEM((tm, D_MODEL), F32),
            pltpu.VMEM((tm, D_MODEL), F32),
        ],
        compiler_params=pltpu.CompilerParams(
            dimension_semantics=("arbitrary",), vmem_limit_bytes=VMEM_LIMIT_BYTES),
        name="fused_layer",
    )(*args)


def _sample_proj_kernel(x_ref, mod_ref, tab_ref, win_ref,
                        qkv_ref, ga_ref, ret_ref, rg_ref, mg_ref):
    rows = x_ref.shape[0]
    x = x_ref[:, 0, :]
    shift = mod_ref[:, 0:D_MODEL]
    scale = mod_ref[:, D_MODEL:2 * D_MODEL]
    hb = (x * (1.0 + scale) + shift).astype(BF16)

    def proj(c0, c1):
        return _dot(hb, win_ref[:, c0:c1])

    lane = _lane_iota((rows, LANES))
    first_half32 = (lane & 32) == 0
    ca = tab_ref[0:1, :]
    sa = tab_ref[1:2, :]
    cr = tab_ref[2:3, :]
    sr = tab_ref[3:4, :]

    qkv = proj(C_AQ, C_AG)
    for t in range(ATT_WIDTH // LANES):
        qt = _rope_attn_tile(qkv[:, t * LANES:(t + 1) * LANES], ca, sa, first_half32)
        qkv_ref[:, 0, t * LANES:(t + 1) * LANES] = qt * ATT_SCALE
    qkv_ref[:, 0, C_AK:C_AV] = _rope_attn_tile(qkv[:, C_AK:C_AV], ca, sa, first_half32)
    qkv_ref[:, 0, C_AV:C_AG] = qkv[:, C_AV:C_AG]
    ga_ref[...] = _silu(proj(C_AG, C_RQ))

    rqk = proj(C_RQ, C_RV)
    for h in range(RET_HEADS):
        sl = slice(h * RET_DK, (h + 1) * RET_DK)
        ret_ref[:, 0, sl] = _rope_ret_tile(rqk[:, sl], cr, sr)
        sk = slice(RET_QK_WIDTH + h * RET_DK, RET_QK_WIDTH + (h + 1) * RET_DK)
        ret_ref[:, 0, sk] = _rope_ret_tile(rqk[:, sk], cr, sr) * RET_K_SCALE
    ret_ref[:, 0, 2 * RET_QK_WIDTH:] = proj(C_RV, C_RG)
    rg_ref[...] = _silu(proj(C_RG, C_MA))
    mg_ref[...] = _sigmoid(proj(C_MA, IN_COLS))


def _sample_proj(layer, x, mod, tab, w_in_b):
    rows = x.shape[0]
    shapes = ((rows, 1, C_AG), (rows, ATT_WIDTH), (rows, 1, 2 * RET_QK_WIDTH + RET_WIDTH),
              (rows, RET_WIDTH), (rows, 2 * D_MODEL))
    return pl.pallas_call(
        _sample_proj_kernel,
        out_shape=tuple(jax.ShapeDtypeStruct(s, F32) for s in shapes),
        grid=(1,),
        in_specs=[
            _const_spec((rows, 1, D_MODEL)),
            _layer_spec((rows, 3 * D_MODEL), layer),
            _const_spec((4, LANES)),
            _layer_spec((D_MODEL, IN_COLS), layer),
        ],
        out_specs=tuple(pl.BlockSpec(s, lambda i, nd=len(s): (0,) * nd) for s in shapes),
        compiler_params=pltpu.CompilerParams(
            dimension_semantics=("arbitrary",), vmem_limit_bytes=VMEM_LIMIT_BYTES),
        name="sample_proj",
    )(x, mod, tab, w_in_b)


def _sample_scores_and_state(qkv_ref, ret_ref, ck_ref, st_ref, ns_ref):
    nb = qkv_ref.shape[0]
    rowi = lax.broadcasted_iota(jnp.int32, (ATT_HEADS, LANES), 0)
    lanei = lax.broadcasted_iota(jnp.int32, (ATT_HEADS, LANES), 1)
    lane_group = lanei // ATT_HEAD_DIM
    dr = lax.broadcasted_iota(jnp.int32, (RET_DK, RET_DK), 0)
    dc = lax.broadcasted_iota(jnp.int32, (RET_DK, RET_DK), 1)
    diag = dr == dc

    qmats, logits = [], []
    for b in range(nb):
        qkv = qkv_ref[b]
        qmat = jnp.zeros((ATT_HEADS, LANES), F32)
        for h in range(ATT_HEADS):
            t, p, g = h // 2, h % 2, h // ATT_GROUP
            tile = qkv[:, t * LANES:(t + 1) * LANES]
            src = tile if p == g else pltpu.roll(tile, 64, 1)
            qmat = jnp.where((rowi == h) & (lane_group == g), jnp.broadcast_to(src, (ATT_HEADS, LANES)), qmat)
        qmats.append(qmat)
        logits.append(_dot(qmat.astype(BF16), ck_ref[b].astype(BF16)))

    for b in range(nb):
        ret = ret_ref[b]
        for h in range(RET_HEADS):
            kh = ret[:, RET_QK_WIDTH + h * RET_DK:RET_QK_WIDTH + (h + 1) * RET_DK]
            vh = ret[:, 2 * RET_QK_WIDTH + h * RET_DV:2 * RET_QK_WIDTH + (h + 1) * RET_DV]
            kdiag = jnp.where(diag, jnp.broadcast_to(kh, (RET_DK, RET_DK)), 0.0).astype(BF16)
            vfull = jnp.broadcast_to(vh, (RET_DK, RET_DV)).astype(BF16)
            ns_ref[b, h] = st_ref[b, h] * TOKEN_DECAY[h] + _dot(kdiag, vfull)
    return qmats, logits


def _sample_outputs(sinks_ref, layer, qkv_ref, ret_ref, ck_ref, cv_ref, qmats, logits,
                    oatt_ref, oret_ref, nk_ref, nv_ref, ns_ref):
    nb = qkv_ref.shape[0]
    sink_col = jnp.zeros((ATT_HEADS, 1), F32)
    rowc = lax.broadcasted_iota(jnp.int32, (ATT_HEADS, 1), 0)
    for h in range(ATT_HEADS):
        sink_col = jnp.where(rowc == h, sinks_ref[layer, h], sink_col)
    lo64_row = _lane_iota((1, LANES)) < 64
    dr = lax.broadcasted_iota(jnp.int32, (LANES, LANES), 0)
    dc = lax.broadcasted_iota(jnp.int32, (LANES, LANES), 1)
    diag = dr == dc
    last_lane = dc == WINDOW - 1

    for b in range(nb):
        qkv = qkv_ref[b]
        k_new = qkv[:, C_AK:C_AV]
        v_new = qkv[:, C_AV:C_AG]
        s_c = logits[b]
        s_self = jnp.sum(qmats[b] * k_new, axis=-1, keepdims=True)
        m = jnp.maximum(jnp.maximum(jnp.max(s_c, axis=-1, keepdims=True), s_self), sink_col)
        p_c = jnp.exp(s_c - m)
        p_self = jnp.exp(s_self - m)
        denom = jnp.sum(p_c, axis=-1, keepdims=True) + p_self + jnp.exp(sink_col - m)
        o = (lax.dot_general(p_c.astype(BF16), cv_ref[b].astype(BF16), NT, preferred_element_type=F32)
             + p_self * v_new) / denom
        o_sw = pltpu.roll(o, 64, 1)
        for t in range(ATT_WIDTH // LANES):
            g = t // 2
            first = (o if g == 0 else o_sw)[2 * t:2 * t + 1, :]
            second = (o_sw if g == 0 else o)[2 * t + 1:2 * t + 2, :]
            oatt_ref[b, :, t * LANES:(t + 1) * LANES] = jnp.where(lo64_row, first, second)

        for new_row, src_ref, dst_ref in ((k_new, ck_ref, nk_ref), (v_new, cv_ref, nv_ref)):
            new_col = jnp.sum(jnp.where(diag, jnp.broadcast_to(new_row, (LANES, LANES)), 0.0),
                              axis=1, keepdims=True)
            shifted = pltpu.roll(src_ref[b], WINDOW - 1, 1)
            dst_ref[b] = jnp.where(last_lane, new_col, shifted)

    for b in range(nb):
        ret = ret_ref[b]
        for h in range(RET_HEADS):
            qh = ret[:, h * RET_DK:(h + 1) * RET_DK]
            q8 = jnp.broadcast_to(qh, (SUBLANES, RET_DK)).astype(BF16)
            oh = _dot(q8, ns_ref[b, h].astype(BF16))[0:1, :]
            ms = jnp.mean(oh * oh, axis=-1, keepdims=True)
            oret_ref[b, :, h * RET_DV:(h + 1) * RET_DV] = oh * lax.rsqrt(ms + RMS_EPS)


def _sample_out_kernel(x_ref, mod_ref, oatt_ref, ga_ref, oret_ref, rg_ref, mg_ref,
                       wpa_ref, wpr_ref, wout_ref, lng_ref, lnb_ref, y_ref):
    gate_c = mod_ref[:, 2 * D_MODEL:3 * D_MODEL]
    za = _dot((oatt_ref[:, 0, :] * ga_ref[...]).astype(BF16), wpa_ref[...])
    zr = _dot((oret_ref[:, 0, :] * rg_ref[...]).astype(BF16), wpr_ref[...])
    z = mg_ref[:, 0:D_MODEL] * za + mg_ref[:, D_MODEL:2 * D_MODEL] * zr
    u = _dot(z.astype(BF16), wout_ref[...])
    t = ALPHA * x_ref[:, 0, :] + gate_c * u
    mu = jnp.mean(t, axis=-1, keepdims=True)
    d = t - mu
    var = jnp.mean(d * d, axis=-1, keepdims=True)
    y_ref[:, 0, :] = d * lax.rsqrt(var + LN_EPS) * lng_ref[...] + lnb_ref[...]


def _sample_out(layer, x, mod, oatt, ga, oret, rg, mg, w_pa_b, w_pr_b, w_out_b, ln_g, ln_b):
    rows = x.shape[0]
    return pl.pallas_call(
        _sample_out_kernel,
        out_shape=jax.ShapeDtypeStruct(x.shape, F32),
        grid=(1,),
        in_specs=[
            _const_spec((rows, 1, D_MODEL)),
            _layer_spec((rows, 3 * D_MODEL), layer),
            _const_spec((rows, 1, ATT_WIDTH)), _const_spec((rows, ATT_WIDTH)),
            _const_spec((rows, 1, RET_WIDTH)), _const_spec((rows, RET_WIDTH)),
            _const_spec((rows, 2 * D_MODEL)),
            _layer_spec((ATT_WIDTH, D_MODEL), layer),
            _layer_spec((RET_WIDTH, D_MODEL), layer),
            _layer_spec((D_MODEL, D_MODEL), layer),
            _layer_spec((1, D_MODEL), layer), _layer_spec((1, D_MODEL), layer),
        ],
        out_specs=pl.BlockSpec((rows, 1, D_MODEL), lambda i: (0, 0, 0)),
        compiler_params=pltpu.CompilerParams(
            dimension_semantics=("arbitrary",), vmem_limit_bytes=VMEM_LIMIT_BYTES),
        name="sample_out",
    )(x, mod, oatt, ga, oret, rg, mg, w_pa_b, w_pr_b, w_out_b, ln_g, ln_b)


def kernel(x_prompt, x_sample, c_prompt, c_sample, cache_k, cache_v, state_ret, w_in, attn_sinks,
           w_cond, b_cond, w_proj_attn, w_proj_ret, w_out, ln_g, ln_b):
    seq = x_prompt.shape[1]
    nbatch = x_sample.shape[0]
    win = cache_k.shape[2]
    assert seq % PROMPT_ROWS == 0 and nbatch % (seq // PROMPT_ROWS) == 0 and nbatch % SUBLANES == 0

    w_in_b = w_in.astype(BF16)
    w_pa_b = w_proj_attn.astype(BF16)
    w_pr_b = w_proj_ret.astype(BF16)
    w_out_b = w_out.astype(BF16)
    ln_g3 = ln_g.reshape(DEPTH, 1, D_MODEL)
    ln_b3 = ln_b.reshape(DEPTH, 1, D_MODEL)

    c_all = jnp.concatenate([c_sample, c_prompt, jnp.zeros((SUBLANES - 1, D_MODEL), F32)], axis=0)
    mod = _cond_call(c_all, w_cond, b_cond)
    prompt_mod_block = nbatch // SUBLANES

    base, rtab = _prompt_rope_tables(seq, PROMPT_ROWS)
    stab = _sample_rope_table(float(PAST_LEN))
    ret_tabs = _retention_tables()

    def feature_major(c):
        return c.transpose(0, 1, 3, 4, 2).reshape(DEPTH, nbatch, ATT_KV_WIDTH, win)

    def window_major(c):
        return c.reshape(DEPTH, nbatch, ATT_KV_HEADS, ATT_HEAD_DIM, win).transpose(0, 1, 4, 2, 3)

    ck = feature_major(cache_k)
    cv = feature_major(cache_v)

    yp = x_prompt[0]
    ys = x_sample
    kp, vp, sp = [], [], []
    prev = None
    for l in range(DEPTH):
        qkv, ga, ret, rg, mg = _sample_proj(l, ys, mod, stab, w_in_b)
        yp, k_new, v_new, s_new, oatt, oret, nk, nv, ns = _fused_layer(
            l, yp, mod, prompt_mod_block, base, rtab, w_in_b, w_pa_b, w_pr_b, w_out_b,
            ln_g3, ln_b3, attn_sinks, ret_tabs, qkv, ret, ck, cv, state_ret, prev)
        kp.append(k_new.reshape(1, WINDOW, ATT_KV_HEADS, ATT_HEAD_DIM))
        vp.append(v_new.reshape(1, WINDOW, ATT_KV_HEADS, ATT_HEAD_DIM))
        sp.append(s_new[None])
        prev = (nk, nv, ns)
        ys = _sample_out(l, ys, mod, oatt, ga, oret, rg, mg, w_pa_b, w_pr_b, w_out_b, ln_g3, ln_b3)

    nk, nv, ns = prev
    return (yp[None], ys, jnp.stack(kp), jnp.stack(vp), jnp.stack(sp),
            window_major(nk), window_major(nv), ns)
```

```python
import functools

import jax
import jax.numpy as jnp
import numpy as np
from jax import lax
from jax.experimental import pallas as pl
from jax.experimental.pallas import tpu as pltpu

D_MODEL = 1024
DEPTH = 2
PAST_LEN = 16384
ATT_HEADS = 8
ATT_KV_HEADS = 2
ATT_HEAD_DIM = 64
ATT_GROUP = ATT_HEADS // ATT_KV_HEADS
ATT_WIDTH = ATT_HEADS * ATT_HEAD_DIM
ATT_KV_WIDTH = ATT_KV_HEADS * ATT_HEAD_DIM
WINDOW = 128
RET_HEADS = 4
RET_DK = 128
RET_DV = 256
RET_QK_WIDTH = RET_HEADS * RET_DK
RET_WIDTH = RET_HEADS * RET_DV
RET_CHUNK = 128
ROPE_THETA = 10000.0
ALPHA = (2.0 * DEPTH) ** 0.25
LN_EPS = 1e-5
RMS_EPS = 1e-6
ATT_SCALE = ATT_HEAD_DIM ** -0.5
RET_K_SCALE = RET_DK ** -0.5

C_AQ = 0
C_AK = C_AQ + ATT_WIDTH
C_AV = C_AK + ATT_KV_WIDTH
C_AG = C_AV + ATT_KV_WIDTH
C_RQ = C_AG + ATT_WIDTH
C_RK = C_RQ + RET_QK_WIDTH
C_RV = C_RK + RET_QK_WIDTH
C_RG = C_RV + RET_WIDTH
C_MA = C_RG + RET_WIDTH
C_MR = C_MA + D_MODEL
IN_COLS = C_MR + D_MODEL

LANES = 128
SUBLANES = 8
BF16_SUBLANES = 16
VMEM_LIMIT_BYTES = 56 * 1024 * 1024

PROMPT_ROWS = 256

BF16 = jnp.bfloat16
F32 = jnp.float32
NT = (((1,), (1,)), ((), ()))

_LOG_GAMMA = np.log(1.0 - 2.0 ** (-5.0 - np.arange(RET_HEADS, dtype=np.float64)))
CHUNK_DECAY = tuple(float(v) for v in np.exp(RET_CHUNK * _LOG_GAMMA))
TOKEN_DECAY = tuple(float(v) for v in np.exp(_LOG_GAMMA))


def _sigmoid(x):
    return 0.5 * jnp.tanh(0.5 * x) + 0.5


def _silu(x):
    return x * _sigmoid(x)


def _dot(a, b):
    return jnp.dot(a, b, preferred_element_type=F32)


def _rope_attn_tile(x, cos, sin_signed, first_half):
    rot = jnp.where(first_half, pltpu.roll(x, LANES - 32, 1), pltpu.roll(x, 32, 1))
    return x * cos + rot * sin_signed


def _rope_ret_tile(x, cos, sin_signed):
    return x * cos + pltpu.roll(x, 64, 1) * sin_signed


def _lane_iota(shape):
    return lax.broadcasted_iota(jnp.int32, shape, len(shape) - 1)


def _layer_spec(shape, layer):
    nd = len(shape)
    return pl.BlockSpec((None,) + tuple(shape), lambda i: (layer,) + (0,) * nd,
                        pipeline_mode=pl.Buffered(1))


def _const_spec(shape):
    nd = len(shape)
    return pl.BlockSpec(tuple(shape), lambda i: (0,) * nd, pipeline_mode=pl.Buffered(1))


def _rope_lane_patterns():
    lane = np.arange(LANES)
    f_att = ROPE_THETA ** (-(lane % 32) / 32.0)
    s_att = np.where(lane % 64 < 32, -1.0, 1.0)
    f_ret = ROPE_THETA ** (-(lane % 64) / 64.0)
    s_ret = np.where(lane < 64, -1.0, 1.0)
    return (f_att, s_att), (f_ret, s_ret)


def _prompt_rope_tables(seq, tm):
    starts = np.arange(seq // tm, dtype=np.float64)[:, None] * tm
    offs = np.arange(tm, dtype=np.float64)[:, None]
    base, within = [], []
    for freq, sign in _rope_lane_patterns():
        base += [np.cos(starts * freq), np.sin(starts * freq)]
        c, s = np.cos(offs * freq), np.sin(offs * freq)
        within += [c, s, sign * c, sign * s]
    return (jnp.asarray(np.stack(base, axis=1), F32),
            jnp.asarray(np.stack(within, axis=0), F32))


def _sample_rope_table(pos):
    rows = []
    for freq, sign in _rope_lane_patterns():
        rows += [np.cos(pos * freq), sign * np.sin(pos * freq)]
    return jnp.asarray(np.stack(rows, axis=0), F32)


def _retention_tables():
    c = RET_CHUNK
    idx = np.arange(c, dtype=np.float64)
    diff = idx[:, None] - idx[None, :]
    lg = _LOG_GAMMA[:, None, None]
    dmask = np.where(diff >= 0, np.exp(np.maximum(diff, 0.0)[None] * lg), 0.0)
    qdec = np.broadcast_to(np.exp((idx + 1.0)[None, :, None] * lg), (RET_HEADS, c, RET_DK))
    kdec = np.broadcast_to(np.exp((c - 1.0 - idx)[None, :, None] * lg), (RET_HEADS, c, RET_DK))
    return jnp.asarray(dmask, F32), jnp.asarray(qdec, F32), jnp.asarray(kdec, F32)


def _cond_kernel(c_ref, w_ref, b_ref, o_ref):
    a = _silu(c_ref[...]).astype(BF16)
    o_ref[...] = _dot(a, w_ref[...].astype(BF16)) + b_ref[...]


def _cond_call(c_all, w_cond, b_cond):
    rows = c_all.shape[0]
    tn = 768
    return pl.pallas_call(
        _cond_kernel,
        out_shape=jax.ShapeDtypeStruct((DEPTH, rows, 3 * D_MODEL), F32),
        grid=(DEPTH, 3 * D_MODEL // tn),
        in_specs=[
            pl.BlockSpec((rows, D_MODEL), lambda l, j: (0, 0)),
            pl.BlockSpec((None, D_MODEL, tn), lambda l, j: (l, 0, j)),
            pl.BlockSpec((None, 1, tn), lambda l, j: (l, 0, j)),
        ],
        out_specs=pl.BlockSpec((None, rows, tn), lambda l, j: (l, 0, j)),
        compiler_params=pltpu.CompilerParams(
            dimension_semantics=("arbitrary", "arbitrary"), vmem_limit_bytes=VMEM_LIMIT_BYTES),
        name="cond_mod",
    )(c_all, w_cond, b_cond.reshape(DEPTH, 1, 3 * D_MODEL))


N_PROMPT_INPUTS = 19
N_PROMPT_OUTPUTS = 9
N_WEIGHTS = 4


def _prompt_kernel(*refs, layer, n_aliased, n_cast):
    (sinks_ref, x_ref, mod_ref, base_ref, rtab_ref,
     win_ref, wpa_ref, wpr_ref, wout_ref, lng_ref, lnb_ref,
     dmask_ref, qdec_ref, kdec_ref,
     sqkv_ref, sret_ref, ck_ref, cv_ref, st_ref) = refs[:N_PROMPT_INPUTS]
    cast_in = refs[N_PROMPT_INPUTS:N_PROMPT_INPUTS + n_cast]
    refs = refs[N_PROMPT_INPUTS + n_cast + n_aliased:]
    (y_ref, knew_ref, vnew_ref, snew_ref,
     oatt_ref, oret_ref, nk_ref, nv_ref, ns_ref) = refs[:N_PROMPT_OUTPUTS]
    cast_out = refs[N_PROMPT_OUTPUTS:N_PROMPT_OUTPUTS + n_cast]
    (tab_scr, hb_scr, q_scr, kvar_scr, vvar_scr, ga_scr, a_scr,
     rq_scr, rk_scr, rv_scr, rg_scr, r_scr, s_scr,
     ma_scr, mr_scr) = refs[N_PROMPT_OUTPUTS + n_cast:]
    step = pl.program_id(0)

    def cast_next_layer_weights():
        for src, dst in zip(cast_in, cast_out):
            dst[...] = src[...].astype(BF16)

    tm = x_ref.shape[0]
    nsub = tm // WINDOW

    def proj(c0, c1):
        return _dot(hb_scr[...], win_ref[:, c0:c1])

    lane = _lane_iota((tm, LANES))
    first_half32 = (lane & 32) == 0
    lo64 = lane < 64

    def rope_attn(t):
        return _rope_attn_tile(t, tab_scr[0], tab_scr[1], first_half32)

    def rope_ret(t):
        return _rope_ret_tile(t, tab_scr[2], tab_scr[3])

    def store_variants(scr, t, fill):
        swapped = pltpu.roll(t, 64, 1)
        other = jnp.full_like(t, fill)
        scr[0, WINDOW:WINDOW + tm, :] = jnp.where(lo64, t, other).astype(BF16)
        scr[1, WINDOW:WINDOW + tm, :] = jnp.where(lo64, other, swapped).astype(BF16)
        scr[2, WINDOW:WINDOW + tm, :] = jnp.where(lo64, swapped, other).astype(BF16)
        scr[3, WINDOW:WINDOW + tm, :] = jnp.where(lo64, other, t).astype(BF16)

    def block_prepare(src_x_ref, src_base_ref):
        shift = mod_ref[0:1, 0:D_MODEL]
        scale = mod_ref[0:1, D_MODEL:2 * D_MODEL]
        hb_scr[...] = (src_x_ref[...] * (1.0 + scale) + shift).astype(BF16)
        for fam in range(2):
            cb = src_base_ref[2 * fam:2 * fam + 1, :]
            sb = src_base_ref[2 * fam + 1:2 * fam + 2, :]
            tab_scr[2 * fam] = cb * rtab_ref[4 * fam] - sb * rtab_ref[4 * fam + 1]
            tab_scr[2 * fam + 1] = sb * rtab_ref[4 * fam + 2] + cb * rtab_ref[4 * fam + 3]

    def block_head():
        kv = proj(C_AK, C_AG)
        k_rot = rope_attn(kv[:, 0:LANES])
        v_raw = kv[:, LANES:2 * LANES]
        knew_ref[...] = k_rot[tm - WINDOW:, :]
        vnew_ref[...] = v_raw[tm - WINDOW:, :]
        store_variants(kvar_scr, k_rot, 0.0)
        store_variants(vvar_scr, v_raw, 1.0)
        qp = proj(C_AQ, C_AK)
        for t in range(ATT_WIDTH // LANES):
            qt = rope_attn(qp[:, t * LANES:(t + 1) * LANES])
            q_scr[:, t * LANES:(t + 1) * LANES] = (qt * ATT_SCALE).astype(BF16)
        ga_scr[...] = proj(C_AG, C_RQ)

    @pl.when(step == 0)
    def _():
        kvar_scr[:, 0:WINDOW, :] = jnp.zeros((4, WINDOW, LANES), BF16)
        vvar_scr[:, 0:WINDOW, :] = jnp.zeros((4, WINDOW, LANES), BF16)
        s_scr[...] = jnp.zeros(s_scr.shape, F32)

    block_prepare(x_ref, base_ref)
    block_head()

    def chunk_rq():
        p = proj(C_RQ, C_RK)
        for h in range(RET_HEADS):
            sl = slice(h * RET_DK, (h + 1) * RET_DK)
            rq_scr[:, sl] = rope_ret(p[:, sl])

    def chunk_rk():
        p = proj(C_RK, C_RV)
        for h in range(RET_HEADS):
            sl = slice(h * RET_DK, (h + 1) * RET_DK)
            rk_scr[:, sl] = rope_ret(p[:, sl]) * RET_K_SCALE

    def half_chunk(c0, dst, fn, half):
        w = dst.shape[1] // 2
        def run():
            dst[:, half * w:(half + 1) * w] = fn(proj(c0 + half * w, c0 + (half + 1) * w))
        return run

    to_bf16 = lambda v: v.astype(BF16)
    raw = lambda v: v
    att_companions = [
        chunk_rq, chunk_rk,
        half_chunk(C_RV, rv_scr, to_bf16, 0), half_chunk(C_RV, rv_scr, to_bf16, 1),
        half_chunk(C_RG, rg_scr, raw, 0), half_chunk(C_RG, rg_scr, raw, 1),
        half_chunk(C_MA, ma_scr, raw, 0), half_chunk(C_MA, ma_scr, raw, 1),
    ]

    row = lax.broadcasted_iota(jnp.int32, (WINDOW, 2 * WINDOW), 0)
    col = lax.broadcasted_iota(jnp.int32, (WINDOW, 2 * WINDOW), 1)
    in_window = col <= row + WINDOW
    mask_std = (col >= row) & in_window
    off = jnp.where(step > 0, 0, 4 * WINDOW)
    mask_first = ((col >= row + off) | (col >= WINDOW)) & in_window
    lo64_w = _lane_iota((WINDOW, LANES)) < 64

    def attention_logits(i, t):
        r0 = i * WINDOW
        g = t // 2
        qt = q_scr[r0:r0 + WINDOW, t * LANES:(t + 1) * LANES]
        return [lax.dot_general(qt, kvar_scr[2 * g + p, r0:r0 + 2 * WINDOW, :], NT,
                                preferred_element_type=F32) for p in range(2)]

    def attention_values(i, t, logits):
        r0 = i * WINDOW
        mask = mask_first if i == 0 else mask_std
        g = t // 2
        outs, sink_terms = [], []
        for p in range(2):
            head = 2 * t + p
            s = jnp.where(mask, logits[p], -jnp.inf)
            sink = sinks_ref[layer, head]
            m = jnp.maximum(jnp.max(s, axis=-1, keepdims=True), sink)
            pe = jnp.exp(s - m).astype(BF16)
            sink_terms.append(jnp.exp(sink - m))
            outs.append(_dot(pe, vvar_scr[2 * g + p, r0:r0 + 2 * WINDOW, :]))
        weighted = jnp.where(lo64_w, outs[0], outs[1])
        row_sums = pltpu.roll(jnp.where(lo64_w, outs[1], outs[0]), 64, 1)
        denom = row_sums + jnp.where(lo64_w, sink_terms[0], sink_terms[1])
        a_scr[r0:r0 + WINDOW, t * LANES:(t + 1) * LANES] = weighted * (1.0 / denom)

    n = 0
    for i in range(nsub):
        for t in range(ATT_WIDTH // LANES):
            logits = attention_logits(i, t)
            if n < len(att_companions):
                att_companions[n]()
            if n == 2:
                cast_next_layer_weights()
            n += 1
            attention_values(i, t, logits)
    for run in att_companions[n:]:
        run()

    kvar_scr[:, 0:WINDOW, :] = kvar_scr[:, tm:tm + WINDOW, :]
    vvar_scr[:, 0:WINDOW, :] = vvar_scr[:, tm:tm + WINDOW, :]

    def ret_slices(c, h):
        rows = slice(c * RET_CHUNK, (c + 1) * RET_CHUNK)
        return rows, slice(h * RET_DK, (h + 1) * RET_DK), slice(h * RET_DV, (h + 1) * RET_DV)

    def retention_scores_and_state(c):
        inners, s_olds = [], []
        for h in range(RET_HEADS):
            rows, sk, sv = ret_slices(c, h)
            qh = rq_scr[rows, sk]
            kh = rk_scr[rows, sk]
            vh = rv_scr[rows, sv]
            inners.append(lax.dot_general(qh.astype(BF16), kh.astype(BF16), NT, preferred_element_type=F32))
            s_old = s_scr[h]
            s_olds.append(s_old.astype(BF16))
            kd = (kh * kdec_ref[h]).astype(BF16)
            s_scr[h] = s_old * CHUNK_DECAY[h] + lax.dot_general(
                kd, vh, (((0,), (0,)), ((), ())), preferred_element_type=F32)
        return inners, s_olds

    def retention_outputs(c, inners, s_olds):
        for h in range(RET_HEADS):
            rows, sk, sv = ret_slices(c, h)
            qh = rq_scr[rows, sk]
            lhs = jnp.concatenate([(inners[h] * dmask_ref[h]).astype(BF16),
                                   (qh * qdec_ref[h]).astype(BF16)], axis=1)
            rhs = jnp.concatenate([rv_scr[rows, sv], s_olds[h]], axis=0)
            o = _dot(lhs, rhs)
            ms = jnp.mean(o * o, axis=-1, keepdims=True)
            on = o * lax.rsqrt(ms + RMS_EPS)
            r_scr[rows, sv] = on

    ret_companions = [half_chunk(C_MR, mr_scr, raw, 0), half_chunk(C_MR, mr_scr, raw, 1)]
    n = 0
    for c in range(nsub):
        inners, s_olds = retention_scores_and_state(c)
        if n < len(ret_companions):
            ret_companions[n]()
        n += 1
        retention_outputs(c, inners, s_olds)
    for run in ret_companions[n:]:
        run()
    sample_qmats, sample_logits = _sample_scores_and_state(sqkv_ref, sret_ref, ck_ref, st_ref, ns_ref)

    @pl.when(step == pl.num_programs(0) - 1)
    def _():
        snew_ref[...] = s_scr[...]

    gate_c = mod_ref[0:1, 2 * D_MODEL:3 * D_MODEL]
    lng = lng_ref[...]
    lnb = lnb_ref[...]
    windows = [slice(i * WINDOW, (i + 1) * WINDOW) for i in range(nsub)]
    branch = []
    for rows in windows:
        za = _dot((a_scr[rows, :] * _silu(ga_scr[rows, :])).astype(BF16), wpa_ref[...])
        zr = _dot((r_scr[rows, :] * _silu(rg_scr[rows, :])).astype(BF16), wpr_ref[...])
        branch.append((za, zr))
    for rows, (za, zr) in zip(windows, branch):
        z = _sigmoid(ma_scr[rows, :]) * za + _sigmoid(mr_scr[rows, :]) * zr
        u = _dot(z.astype(BF16), wout_ref[...])
        t = ALPHA * x_ref[rows, :] + gate_c * u
        mu = jnp.mean(t, axis=-1, keepdims=True)
        d = t - mu
        var = jnp.mean(d * d, axis=-1, keepdims=True)
        y_ref[rows, :] = d * lax.rsqrt(var + LN_EPS) * lng + lnb

    _sample_outputs(sinks_ref, layer, sqkv_ref, sret_ref, ck_ref, cv_ref, sample_qmats, sample_logits,
                    oatt_ref, oret_ref, nk_ref, nv_ref, ns_ref)


def _fused_layer(layer, x, mod, mod_row_block, base, rtab, weights_b, next_weights,
                 ln_g, ln_b, sinks, ret_tabs, sample_qkv, sample_ret, cache_k, cache_v, state, shared_out):
    w_in_b, w_pa_b, w_pr_b, w_out_b = weights_b
    seq = x.shape[0]
    tm = PROMPT_ROWS
    steps = seq // tm
    nbatch = sample_qkv.shape[0]
    nb = nbatch // steps
    win = cache_k.shape[3]
    assert win == WINDOW == LANES
    dmask, qdec, kdec = ret_tabs
    smem = pl.BlockSpec(memory_space=pltpu.SMEM)
    row_spec = lambda w: pl.BlockSpec((tm, w), lambda i: (i, 0))
    seq_spec = lambda w: pl.BlockSpec((nb, 1, w), lambda i: (i, 0, 0))
    cache_spec = pl.BlockSpec((None, nb, ATT_KV_WIDTH, win), lambda i: (layer, i, 0, 0))
    state_spec = pl.BlockSpec((None, nb, RET_HEADS, RET_DK, RET_DV), lambda i: (layer, i, 0, 0, 0))
    in_specs = [
        smem,
        row_spec(D_MODEL),
        pl.BlockSpec((None, SUBLANES, 3 * D_MODEL), lambda i: (layer, mod_row_block, 0),
                     pipeline_mode=pl.Buffered(1)),
        pl.BlockSpec((None, 4, LANES), lambda i: (i, 0, 0)),
        _const_spec((8, tm, LANES)),
        _const_spec((D_MODEL, IN_COLS)),
        _const_spec((ATT_WIDTH, D_MODEL)),
        _const_spec((RET_WIDTH, D_MODEL)),
        _const_spec((D_MODEL, D_MODEL)),
        _layer_spec((1, D_MODEL), layer), _layer_spec((1, D_MODEL), layer),
        _const_spec((RET_HEADS, RET_CHUNK, RET_CHUNK)),
        _const_spec((RET_HEADS, RET_CHUNK, RET_DK)),
        _const_spec((RET_HEADS, RET_CHUNK, RET_DK)),
        seq_spec(C_AG), seq_spec(2 * RET_QK_WIDTH + RET_WIDTH),
        cache_spec, cache_spec, state_spec,
    ]
    args = [sinks, x, mod, base, rtab, w_in_b, w_pa_b, w_pr_b, w_out_b, ln_g, ln_b, dmask, qdec, kdec,
            sample_qkv, sample_ret, cache_k, cache_v, state]
    assert len(args) == N_PROMPT_INPUTS
    cast_shapes, cast_specs = [], []
    if next_weights is not None:
        assert len(next_weights) == N_WEIGHTS
        for w in next_weights:
            _, rows, cols = w.shape
            slab = max(BF16_SUBLANES, rows // steps)
            nslab = rows // slab
            assert rows % slab == 0 and nslab <= steps
            args.append(w)
            in_specs.append(pl.BlockSpec((None, slab, cols),
                                         lambda i, n=nslab: (layer + 1, jnp.minimum(i, n - 1), 0)))
            cast_shapes.append(jax.ShapeDtypeStruct((rows, cols), BF16))
            cast_specs.append(pl.BlockSpec((slab, cols), lambda i, n=nslab: (jnp.minimum(i, n - 1), 0)))
    aliases = {}
    if shared_out is not None:
        for j, arr in enumerate(shared_out):
            aliases[len(args)] = N_PROMPT_OUTPUTS - len(shared_out) + j
            args.append(arr)
            in_specs.append(pl.BlockSpec(memory_space=pl.ANY))
    return pl.pallas_call(
        functools.partial(_prompt_kernel, layer=layer, n_aliased=len(aliases), n_cast=len(cast_shapes)),
        out_shape=(
            jax.ShapeDtypeStruct((seq, D_MODEL), F32),
            jax.ShapeDtypeStruct((WINDOW, ATT_KV_WIDTH), F32),
            jax.ShapeDtypeStruct((WINDOW, ATT_KV_WIDTH), F32),
            jax.ShapeDtypeStruct((RET_HEADS, RET_DK, RET_DV), F32),
            jax.ShapeDtypeStruct((nbatch, 1, ATT_WIDTH), F32),
            jax.ShapeDtypeStruct((nbatch, 1, RET_WIDTH), F32),
            jax.ShapeDtypeStruct(cache_k.shape, F32),
            jax.ShapeDtypeStruct(cache_v.shape, F32),
            jax.ShapeDtypeStruct(state.shape, F32),
        ) + tuple(cast_shapes),
        grid=(steps,),
        in_specs=in_specs,
        out_specs=(
            row_spec(D_MODEL),
            pl.BlockSpec((WINDOW, ATT_KV_WIDTH), lambda i: (0, 0)),
            pl.BlockSpec((WINDOW, ATT_KV_WIDTH), lambda i: (0, 0)),
            pl.BlockSpec((RET_HEADS, RET_DK, RET_DV), lambda i: (0, 0, 0)),
            seq_spec(ATT_WIDTH), seq_spec(RET_WIDTH),
            cache_spec, cache_spec, state_spec,
        ) + tuple(cast_specs),
        input_output_aliases=aliases,
        scratch_shapes=[
            pltpu.VMEM((4, tm, LANES), F32),
            pltpu.VMEM((tm, D_MODEL), BF16),
            pltpu.VMEM((tm, ATT_WIDTH), BF16),
            pltpu.VMEM((4, WINDOW + tm, LANES), BF16),
            pltpu.VMEM((4, WINDOW + tm, LANES), BF16),
            pltpu.VMEM((tm, ATT_WIDTH), F32),
            pltpu.VMEM((tm, ATT_WIDTH), F32),
            pltpu.VMEM((tm, RET_QK_WIDTH), F32),
            pltpu.VMEM((tm, RET_QK_WIDTH), F32),
            pltpu.VMEM((tm, RET_WIDTH), BF16),
            pltpu.VMEM((tm, RET_WIDTH), F32),
            pltpu.VMEM((tm, RET_WIDTH), F32),
            pltpu.VMEM((RET_HEADS, RET_DK, RET_DV), F32),
            pltpu.VMEM((tm, D_MODEL), F32),
            pltpu.VMEM((tm, D_MODEL), F32),
        ],
        compiler_params=pltpu.CompilerParams(
            dimension_semantics=("arbitrary",), vmem_limit_bytes=VMEM_LIMIT_BYTES),
        name="fused_layer",
    )(*args)


def _sample_proj_kernel(x_ref, mod_ref, tab_ref, win_ref,
                        qkv_ref, ga_ref, ret_ref, rg_ref, mg_ref):
    rows = x_ref.shape[0]
    x = x_ref[:, 0, :]
    shift = mod_ref[:, 0:D_MODEL]
    scale = mod_ref[:, D_MODEL:2 * D_MODEL]
    hb = (x * (1.0 + scale) + shift).astype(BF16)

    def proj(c0, c1):
        return _dot(hb, win_ref[:, c0:c1])

    lane = _lane_iota((rows, LANES))
    first_half32 = (lane & 32) == 0
    ca = tab_ref[0:1, :]
    sa = tab_ref[1:2, :]
    cr = tab_ref[2:3, :]
    sr = tab_ref[3:4, :]

    qkv = proj(C_AQ, C_AG)
    for t in range(ATT_WIDTH // LANES):
        qt = _rope_attn_tile(qkv[:, t * LANES:(t + 1) * LANES], ca, sa, first_half32)
        qkv_ref[:, 0, t * LANES:(t + 1) * LANES] = qt * ATT_SCALE
    qkv_ref[:, 0, C_AK:C_AV] = _rope_attn_tile(qkv[:, C_AK:C_AV], ca, sa, first_half32)
    qkv_ref[:, 0, C_AV:C_AG] = qkv[:, C_AV:C_AG]
    ga_ref[...] = _silu(proj(C_AG, C_RQ))

    rqk = proj(C_RQ, C_RV)
    for h in range(RET_HEADS):
        sl = slice(h * RET_DK, (h + 1) * RET_DK)
        ret_ref[:, 0, sl] = _rope_ret_tile(rqk[:, sl], cr, sr)
        sk = slice(RET_QK_WIDTH + h * RET_DK, RET_QK_WIDTH + (h + 1) * RET_DK)
        ret_ref[:, 0, sk] = _rope_ret_tile(rqk[:, sk], cr, sr) * RET_K_SCALE
    ret_ref[:, 0, 2 * RET_QK_WIDTH:] = proj(C_RV, C_RG)
    rg_ref[...] = _silu(proj(C_RG, C_MA))
    mg_ref[...] = _sigmoid(proj(C_MA, IN_COLS))


def _sample_proj(layer, x, mod, tab, w_in_b):
    rows = x.shape[0]
    shapes = ((rows, 1, C_AG), (rows, ATT_WIDTH), (rows, 1, 2 * RET_QK_WIDTH + RET_WIDTH),
              (rows, RET_WIDTH), (rows, 2 * D_MODEL))
    return pl.pallas_call(
        _sample_proj_kernel,
        out_shape=tuple(jax.ShapeDtypeStruct(s, F32) for s in shapes),
        grid=(1,),
        in_specs=[
            _const_spec((rows, 1, D_MODEL)),
            _layer_spec((rows, 3 * D_MODEL), layer),
            _const_spec((4, LANES)),
            _const_spec((D_MODEL, IN_COLS)),
        ],
        out_specs=tuple(pl.BlockSpec(s, lambda i, nd=len(s): (0,) * nd) for s in shapes),
        compiler_params=pltpu.CompilerParams(
            dimension_semantics=("arbitrary",), vmem_limit_bytes=VMEM_LIMIT_BYTES),
        name="sample_proj",
    )(x, mod, tab, w_in_b)


def _sample_scores_and_state(qkv_ref, ret_ref, ck_ref, st_ref, ns_ref):
    nb = qkv_ref.shape[0]
    rowi = lax.broadcasted_iota(jnp.int32, (ATT_HEADS, LANES), 0)
    lanei = lax.broadcasted_iota(jnp.int32, (ATT_HEADS, LANES), 1)
    lane_group = lanei // ATT_HEAD_DIM
    dr = lax.broadcasted_iota(jnp.int32, (RET_DK, RET_DK), 0)
    dc = lax.broadcasted_iota(jnp.int32, (RET_DK, RET_DK), 1)
    diag = dr == dc

    qmats, logits = [], []
    for b in range(nb):
        qkv = qkv_ref[b]
        qmat = jnp.zeros((ATT_HEADS, LANES), F32)
        for h in range(ATT_HEADS):
            t, p, g = h // 2, h % 2, h // ATT_GROUP
            tile = qkv[:, t * LANES:(t + 1) * LANES]
            src = tile if p == g else pltpu.roll(tile, 64, 1)
            qmat = jnp.where((rowi == h) & (lane_group == g), jnp.broadcast_to(src, (ATT_HEADS, LANES)), qmat)
        qmats.append(qmat)
        logits.append(_dot(qmat.astype(BF16), ck_ref[b].astype(BF16)))

    for b in range(nb):
        ret = ret_ref[b]
        for h in range(RET_HEADS):
            kh = ret[:, RET_QK_WIDTH + h * RET_DK:RET_QK_WIDTH + (h + 1) * RET_DK]
            vh = ret[:, 2 * RET_QK_WIDTH + h * RET_DV:2 * RET_QK_WIDTH + (h + 1) * RET_DV]
            kdiag = jnp.where(diag, jnp.broadcast_to(kh, (RET_DK, RET_DK)), 0.0).astype(BF16)
            vfull = jnp.broadcast_to(vh, (RET_DK, RET_DV)).astype(BF16)
            ns_ref[b, h] = st_ref[b, h] * TOKEN_DECAY[h] + _dot(kdiag, vfull)
    return qmats, logits


def _sample_outputs(sinks_ref, layer, qkv_ref, ret_ref, ck_ref, cv_ref, qmats, logits,
                    oatt_ref, oret_ref, nk_ref, nv_ref, ns_ref):
    nb = qkv_ref.shape[0]
    sink_col = jnp.zeros((ATT_HEADS, 1), F32)
    rowc = lax.broadcasted_iota(jnp.int32, (ATT_HEADS, 1), 0)
    for h in range(ATT_HEADS):
        sink_col = jnp.where(rowc == h, sinks_ref[layer, h], sink_col)
    lo64_row = _lane_iota((1, LANES)) < 64
    dr = lax.broadcasted_iota(jnp.int32, (LANES, LANES), 0)
    dc = lax.broadcasted_iota(jnp.int32, (LANES, LANES), 1)
    diag = dr == dc
    last_lane = dc == WINDOW - 1

    for b in range(nb):
        qkv = qkv_ref[b]
        k_new = qkv[:, C_AK:C_AV]
        v_new = qkv[:, C_AV:C_AG]
        s_c = logits[b]
        s_self = jnp.sum(qmats[b] * k_new, axis=-1, keepdims=True)
        m = jnp.maximum(jnp.maximum(jnp.max(s_c, axis=-1, keepdims=True), s_self), sink_col)
        p_c = jnp.exp(s_c - m)
        p_self = jnp.exp(s_self - m)
        denom = jnp.sum(p_c, axis=-1, keepdims=True) + p_self + jnp.exp(sink_col - m)
        o = (lax.dot_general(p_c.astype(BF16), cv_ref[b].astype(BF16), NT, preferred_element_type=F32)
             + p_self * v_new) / denom
        o_sw = pltpu.roll(o, 64, 1)
        for t in range(ATT_WIDTH // LANES):
            g = t // 2
            first = (o if g == 0 else o_sw)[2 * t:2 * t + 1, :]
            second = (o_sw if g == 0 else o)[2 * t + 1:2 * t + 2, :]
            oatt_ref[b, :, t * LANES:(t + 1) * LANES] = jnp.where(lo64_row, first, second)

        for new_row, src_ref, dst_ref in ((k_new, ck_ref, nk_ref), (v_new, cv_ref, nv_ref)):
            new_col = jnp.sum(jnp.where(diag, jnp.broadcast_to(new_row, (LANES, LANES)), 0.0),
                              axis=1, keepdims=True)
            shifted = pltpu.roll(src_ref[b], WINDOW - 1, 1)
            dst_ref[b] = jnp.where(last_lane, new_col, shifted)

    for b in range(nb):
        ret = ret_ref[b]
        for h in range(RET_HEADS):
            qh = ret[:, h * RET_DK:(h + 1) * RET_DK]
            q8 = jnp.broadcast_to(qh, (SUBLANES, RET_DK)).astype(BF16)
            oh = _dot(q8, ns_ref[b, h].astype(BF16))[0:1, :]
            ms = jnp.mean(oh * oh, axis=-1, keepdims=True)
            oret_ref[b, :, h * RET_DV:(h + 1) * RET_DV] = oh * lax.rsqrt(ms + RMS_EPS)


def _sample_out_kernel(x_ref, mod_ref, oatt_ref, ga_ref, oret_ref, rg_ref, mg_ref,
                       wpa_ref, wpr_ref, wout_ref, lng_ref, lnb_ref, y_ref):
    gate_c = mod_ref[:, 2 * D_MODEL:3 * D_MODEL]
    za = _dot((oatt_ref[:, 0, :] * ga_ref[...]).astype(BF16), wpa_ref[...])
    zr = _dot((oret_ref[:, 0, :] * rg_ref[...]).astype(BF16), wpr_ref[...])
    z = mg_ref[:, 0:D_MODEL] * za + mg_ref[:, D_MODEL:2 * D_MODEL] * zr
    u = _dot(z.astype(BF16), wout_ref[...])
    t = ALPHA * x_ref[:, 0, :] + gate_c * u
    mu = jnp.mean(t, axis=-1, keepdims=True)
    d = t - mu
    var = jnp.mean(d * d, axis=-1, keepdims=True)
    y_ref[:, 0, :] = d * lax.rsqrt(var + LN_EPS) * lng_ref[...] + lnb_ref[...]


def _sample_out(layer, x, mod, oatt, ga, oret, rg, mg, w_pa_b, w_pr_b, w_out_b, ln_g, ln_b):
    rows = x.shape[0]
    return pl.pallas_call(
        _sample_out_kernel,
        out_shape=jax.ShapeDtypeStruct(x.shape, F32),
        grid=(1,),
        in_specs=[
            _const_spec((rows, 1, D_MODEL)),
            _layer_spec((rows, 3 * D_MODEL), layer),
            _const_spec((rows, 1, ATT_WIDTH)), _const_spec((rows, ATT_WIDTH)),
            _const_spec((rows, 1, RET_WIDTH)), _const_spec((rows, RET_WIDTH)),
            _const_spec((rows, 2 * D_MODEL)),
            _const_spec((ATT_WIDTH, D_MODEL)),
            _const_spec((RET_WIDTH, D_MODEL)),
            _const_spec((D_MODEL, D_MODEL)),
            _layer_spec((1, D_MODEL), layer), _layer_spec((1, D_MODEL), layer),
        ],
        out_specs=pl.BlockSpec((rows, 1, D_MODEL), lambda i: (0, 0, 0)),
        compiler_params=pltpu.CompilerParams(
            dimension_semantics=("arbitrary",), vmem_limit_bytes=VMEM_LIMIT_BYTES),
        name="sample_out",
    )(x, mod, oatt, ga, oret, rg, mg, w_pa_b, w_pr_b, w_out_b, ln_g, ln_b)


def kernel(x_prompt, x_sample, c_prompt, c_sample, cache_k, cache_v, state_ret, w_in, attn_sinks,
           w_cond, b_cond, w_proj_attn, w_proj_ret, w_out, ln_g, ln_b):
    seq = x_prompt.shape[1]
    nbatch = x_sample.shape[0]
    win = cache_k.shape[2]
    assert seq % PROMPT_ROWS == 0 and nbatch % (seq // PROMPT_ROWS) == 0 and nbatch % SUBLANES == 0

    weights_f32 = (w_in, w_proj_attn, w_proj_ret, w_out)
    weights_b = tuple(w[0].astype(BF16) for w in weights_f32)
    ln_g3 = ln_g.reshape(DEPTH, 1, D_MODEL)
    ln_b3 = ln_b.reshape(DEPTH, 1, D_MODEL)

    c_all = jnp.concatenate([c_sample, c_prompt, jnp.zeros((SUBLANES - 1, D_MODEL), F32)], axis=0)
    mod = _cond_call(c_all, w_cond, b_cond)
    prompt_mod_block = nbatch // SUBLANES

    base, rtab = _prompt_rope_tables(seq, PROMPT_ROWS)
    stab = _sample_rope_table(float(PAST_LEN))
    ret_tabs = _retention_tables()

    def feature_major(c):
        return c.transpose(0, 1, 3, 4, 2).reshape(DEPTH, nbatch, ATT_KV_WIDTH, win)

    def window_major(c):
        return c.reshape(DEPTH, nbatch, ATT_KV_HEADS, ATT_HEAD_DIM, win).transpose(0, 1, 4, 2, 3)

    ck = feature_major(cache_k)
    cv = feature_major(cache_v)

    yp = x_prompt[0]
    ys = x_sample
    kp, vp, sp = [], [], []
    prev = None
    for l in range(DEPTH):
        w_in_b, w_pa_b, w_pr_b, w_out_b = weights_b
        qkv, ga, ret, rg, mg = _sample_proj(l, ys, mod, stab, w_in_b)
        outs = _fused_layer(
            l, yp, mod, prompt_mod_block, base, rtab, weights_b,
            weights_f32 if l + 1 < DEPTH else None,
            ln_g3, ln_b3, attn_sinks, ret_tabs, qkv, ret, ck, cv, state_ret, prev)
        yp, k_new, v_new, s_new, oatt, oret, nk, nv, ns = outs[:N_PROMPT_OUTPUTS]
        kp.append(k_new.reshape(1, WINDOW, ATT_KV_HEADS, ATT_HEAD_DIM))
        vp.append(v_new.reshape(1, WINDOW, ATT_KV_HEADS, ATT_HEAD_DIM))
        sp.append(s_new[None])
        prev = (nk, nv, ns)
        ys = _sample_out(l, ys, mod, oatt, ga, oret, rg, mg, w_pa_b, w_pr_b, w_out_b, ln_g3, ln_b3)
        weights_b = tuple(outs[N_PROMPT_OUTPUTS:])

    nk, nv, ns = prev
    return (yp[None], ys, jnp.stack(kp), jnp.stack(vp), jnp.stack(sp),
            window_major(nk), window_major(nv), ns)
```

```python
import functools

import jax
import jax.numpy as jnp
import numpy as np
from jax import lax
from jax.experimental import pallas as pl
from jax.experimental.pallas import tpu as pltpu

D_MODEL = 1024
DEPTH = 2
PAST_LEN = 16384
ATT_HEADS = 8
ATT_KV_HEADS = 2
ATT_HEAD_DIM = 64
ATT_GROUP = ATT_HEADS // ATT_KV_HEADS
ATT_WIDTH = ATT_HEADS * ATT_HEAD_DIM
ATT_KV_WIDTH = ATT_KV_HEADS * ATT_HEAD_DIM
WINDOW = 128
RET_HEADS = 4
RET_DK = 128
RET_DV = 256
RET_QK_WIDTH = RET_HEADS * RET_DK
RET_WIDTH = RET_HEADS * RET_DV
RET_CHUNK = 128
ROPE_THETA = 10000.0
ALPHA = (2.0 * DEPTH) ** 0.25
LN_EPS = 1e-5
RMS_EPS = 1e-6
ATT_SCALE = ATT_HEAD_DIM ** -0.5
RET_K_SCALE = RET_DK ** -0.5

C_AQ = 0
C_AK = C_AQ + ATT_WIDTH
C_AV = C_AK + ATT_KV_WIDTH
C_AG = C_AV + ATT_KV_WIDTH
C_RQ = C_AG + ATT_WIDTH
C_RK = C_RQ + RET_QK_WIDTH
C_RV = C_RK + RET_QK_WIDTH
C_RG = C_RV + RET_WIDTH
C_MA = C_RG + RET_WIDTH
C_MR = C_MA + D_MODEL
IN_COLS = C_MR + D_MODEL

LANES = 128
SUBLANES = 8
BF16_SUBLANES = 16
VMEM_LIMIT_BYTES = 56 * 1024 * 1024

PROMPT_ROWS = 256

BF16 = jnp.bfloat16
F32 = jnp.float32
NT = (((1,), (1,)), ((), ()))

_LOG_GAMMA = np.log(1.0 - 2.0 ** (-5.0 - np.arange(RET_HEADS, dtype=np.float64)))
CHUNK_DECAY = tuple(float(v) for v in np.exp(RET_CHUNK * _LOG_GAMMA))
TOKEN_DECAY = tuple(float(v) for v in np.exp(_LOG_GAMMA))


def _sigmoid(x):
    return 0.5 * jnp.tanh(0.5 * x) + 0.5


def _silu(x):
    return x * _sigmoid(x)


def _dot(a, b):
    return jnp.dot(a, b, preferred_element_type=F32)


def _rope_attn_tile(x, cos, sin_signed, first_half):
    rot = jnp.where(first_half, pltpu.roll(x, LANES - 32, 1), pltpu.roll(x, 32, 1))
    return x * cos + rot * sin_signed


def _rope_ret_tile(x, cos, sin_signed):
    return x * cos + pltpu.roll(x, 64, 1) * sin_signed


def _lane_iota(shape):
    return lax.broadcasted_iota(jnp.int32, shape, len(shape) - 1)


def _layer_spec(shape, layer):
    nd = len(shape)
    return pl.BlockSpec((None,) + tuple(shape), lambda i: (layer,) + (0,) * nd,
                        pipeline_mode=pl.Buffered(1))


def _const_spec(shape):
    nd = len(shape)
    return pl.BlockSpec(tuple(shape), lambda i: (0,) * nd, pipeline_mode=pl.Buffered(1))


def _rope_lane_patterns():
    lane = np.arange(LANES)
    f_att = ROPE_THETA ** (-(lane % 32) / 32.0)
    s_att = np.where(lane % 64 < 32, -1.0, 1.0)
    f_ret = ROPE_THETA ** (-(lane % 64) / 64.0)
    s_ret = np.where(lane < 64, -1.0, 1.0)
    return (f_att, s_att), (f_ret, s_ret)


def _prompt_rope_tables(seq, tm):
    starts = np.arange(seq // tm, dtype=np.float64)[:, None] * tm
    offs = np.arange(tm, dtype=np.float64)[:, None]
    base, within = [], []
    for freq, sign in _rope_lane_patterns():
        base += [np.cos(starts * freq), np.sin(starts * freq)]
        c, s = np.cos(offs * freq), np.sin(offs * freq)
        within += [c, s, sign * c, sign * s]
    return (jnp.asarray(np.stack(base, axis=1), F32),
            jnp.asarray(np.stack(within, axis=0), F32))


def _sample_rope_table(pos):
    rows = []
    for freq, sign in _rope_lane_patterns():
        rows += [np.cos(pos * freq), sign * np.sin(pos * freq)]
    return jnp.asarray(np.stack(rows, axis=0), F32)


def _retention_tables():
    c = RET_CHUNK
    idx = np.arange(c, dtype=np.float64)
    diff = idx[:, None] - idx[None, :]
    lg = _LOG_GAMMA[:, None, None]
    dmask = np.where(diff >= 0, np.exp(np.maximum(diff, 0.0)[None] * lg), 0.0)
    qdec = np.broadcast_to(np.exp((idx + 1.0)[None, :, None] * lg), (RET_HEADS, c, RET_DK))
    kdec = np.broadcast_to(np.exp((c - 1.0 - idx)[None, :, None] * lg), (RET_HEADS, c, RET_DK))
    return jnp.asarray(dmask, F32), jnp.asarray(qdec, F32), jnp.asarray(kdec, F32)


def _cond_kernel(c_ref, w_ref, b_ref, o_ref):
    a = _silu(c_ref[...]).astype(BF16)
    o_ref[...] = _dot(a, w_ref[...].astype(BF16)) + b_ref[...]


def _cond_call(c_all, w_cond, b_cond):
    rows = c_all.shape[0]
    tn = 3 * D_MODEL // 2
    return pl.pallas_call(
        _cond_kernel,
        out_shape=jax.ShapeDtypeStruct((DEPTH, rows, 3 * D_MODEL), F32),
        grid=(DEPTH, 3 * D_MODEL // tn),
        in_specs=[
            pl.BlockSpec((rows, D_MODEL), lambda l, j: (0, 0)),
            pl.BlockSpec((None, D_MODEL, tn), lambda l, j: (l, 0, j)),
            pl.BlockSpec((None, 1, tn), lambda l, j: (l, 0, j)),
        ],
        out_specs=pl.BlockSpec((None, rows, tn), lambda l, j: (l, 0, j)),
        compiler_params=pltpu.CompilerParams(
            dimension_semantics=("arbitrary", "arbitrary"), vmem_limit_bytes=VMEM_LIMIT_BYTES),
        name="cond_mod",
    )(c_all, w_cond, b_cond.reshape(DEPTH, 1, 3 * D_MODEL))


N_PROMPT_INPUTS = 19
N_PROMPT_OUTPUTS = 9
N_WEIGHTS = 4


def _prompt_kernel(*refs, layer, n_aliased, n_cast):
    (sinks_ref, x_ref, mod_ref, base_ref, rtab_ref,
     win_ref, wpa_ref, wpr_ref, wout_ref, lng_ref, lnb_ref,
     dmask_ref, qdec_ref, kdec_ref,
     sqkv_ref, sret_ref, ck_ref, cv_ref, st_ref) = refs[:N_PROMPT_INPUTS]
    cast_in = refs[N_PROMPT_INPUTS:N_PROMPT_INPUTS + n_cast]
    refs = refs[N_PROMPT_INPUTS + n_cast + n_aliased:]
    (y_ref, knew_ref, vnew_ref, snew_ref,
     oatt_ref, oret_ref, nk_ref, nv_ref, ns_ref) = refs[:N_PROMPT_OUTPUTS]
    cast_out = refs[N_PROMPT_OUTPUTS:N_PROMPT_OUTPUTS + n_cast]
    (tab_scr, hb_scr, q_scr, kvar_scr, vvar_scr, ga_scr, a_scr,
     rq_scr, rk_scr, rv_scr, rg_scr, r_scr, s_scr,
     ma_scr, mr_scr) = refs[N_PROMPT_OUTPUTS + n_cast:]
    step = pl.program_id(0)

    def cast_next_layer_weights():
        for src, dst in zip(cast_in, cast_out):
            dst[...] = src[...].astype(BF16)

    tm = x_ref.shape[0]
    nsub = tm // WINDOW

    def proj(c0, c1):
        return _dot(hb_scr[...], win_ref[:, c0:c1])

    lane = _lane_iota((tm, LANES))
    first_half32 = (lane & 32) == 0
    lo64 = lane < 64

    def rope_attn(t):
        return _rope_attn_tile(t, tab_scr[0], tab_scr[1], first_half32)

    def rope_ret(t):
        return _rope_ret_tile(t, tab_scr[2], tab_scr[3])

    def store_variants(scr, t, fill):
        swapped = pltpu.roll(t, 64, 1)
        other = jnp.full_like(t, fill)
        scr[0, WINDOW:WINDOW + tm, :] = jnp.where(lo64, t, other).astype(BF16)
        scr[1, WINDOW:WINDOW + tm, :] = jnp.where(lo64, other, swapped).astype(BF16)
        scr[2, WINDOW:WINDOW + tm, :] = jnp.where(lo64, swapped, other).astype(BF16)
        scr[3, WINDOW:WINDOW + tm, :] = jnp.where(lo64, other, t).astype(BF16)

    def block_prepare(src_x_ref, src_base_ref):
        shift = mod_ref[0:1, 0:D_MODEL]
        scale = mod_ref[0:1, D_MODEL:2 * D_MODEL]
        hb_scr[...] = (src_x_ref[...] * (1.0 + scale) + shift).astype(BF16)
        for fam in range(2):
            cb = src_base_ref[2 * fam:2 * fam + 1, :]
            sb = src_base_ref[2 * fam + 1:2 * fam + 2, :]
            tab_scr[2 * fam] = cb * rtab_ref[4 * fam] - sb * rtab_ref[4 * fam + 1]
            tab_scr[2 * fam + 1] = sb * rtab_ref[4 * fam + 2] + cb * rtab_ref[4 * fam + 3]

    def block_head():
        kv = proj(C_AK, C_AG)
        k_rot = rope_attn(kv[:, 0:LANES])
        v_raw = kv[:, LANES:2 * LANES]
        knew_ref[...] = k_rot[tm - WINDOW:, :]
        vnew_ref[...] = v_raw[tm - WINDOW:, :]
        store_variants(kvar_scr, k_rot, 0.0)
        store_variants(vvar_scr, v_raw, 1.0)
        qp = proj(C_AQ, C_AK)
        for t in range(ATT_WIDTH // LANES):
            qt = rope_attn(qp[:, t * LANES:(t + 1) * LANES])
            q_scr[:, t * LANES:(t + 1) * LANES] = (qt * ATT_SCALE).astype(BF16)
        ga_scr[...] = proj(C_AG, C_RQ)

    @pl.when(step == 0)
    def _():
        kvar_scr[:, 0:WINDOW, :] = jnp.zeros((4, WINDOW, LANES), BF16)
        vvar_scr[:, 0:WINDOW, :] = jnp.zeros((4, WINDOW, LANES), BF16)
        s_scr[...] = jnp.zeros(s_scr.shape, F32)

    block_prepare(x_ref, base_ref)
    block_head()

    def chunk_rq():
        p = proj(C_RQ, C_RK)
        for h in range(RET_HEADS):
            sl = slice(h * RET_DK, (h + 1) * RET_DK)
            rq_scr[:, sl] = rope_ret(p[:, sl])

    def chunk_rk():
        p = proj(C_RK, C_RV)
        for h in range(RET_HEADS):
            sl = slice(h * RET_DK, (h + 1) * RET_DK)
            rk_scr[:, sl] = rope_ret(p[:, sl]) * RET_K_SCALE

    def half_chunk(c0, dst, fn, half):
        w = dst.shape[1] // 2
        def run():
            dst[:, half * w:(half + 1) * w] = fn(proj(c0 + half * w, c0 + (half + 1) * w))
        return run

    to_bf16 = lambda v: v.astype(BF16)
    raw = lambda v: v
    att_companions = [
        chunk_rq, chunk_rk,
        half_chunk(C_RV, rv_scr, to_bf16, 0), half_chunk(C_RV, rv_scr, to_bf16, 1),
        half_chunk(C_RG, rg_scr, raw, 0), half_chunk(C_RG, rg_scr, raw, 1),
        half_chunk(C_MA, ma_scr, raw, 0), half_chunk(C_MA, ma_scr, raw, 1),
    ]

    row = lax.broadcasted_iota(jnp.int32, (WINDOW, 2 * WINDOW), 0)
    col = lax.broadcasted_iota(jnp.int32, (WINDOW, 2 * WINDOW), 1)
    in_window = col <= row + WINDOW
    mask_std = (col >= row) & in_window
    off = jnp.where(step > 0, 0, 4 * WINDOW)
    mask_first = ((col >= row + off) | (col >= WINDOW)) & in_window
    lo64_w = _lane_iota((WINDOW, LANES)) < 64

    def attention_logits(i, t):
        r0 = i * WINDOW
        g = t // 2
        qt = q_scr[r0:r0 + WINDOW, t * LANES:(t + 1) * LANES]
        return [lax.dot_general(qt, kvar_scr[2 * g + p, r0:r0 + 2 * WINDOW, :], NT,
                                preferred_element_type=F32) for p in range(2)]

    def attention_values(i, t, logits):
        r0 = i * WINDOW
        mask = mask_first if i == 0 else mask_std
        g = t // 2
        outs, sink_terms = [], []
        for p in range(2):
            head = 2 * t + p
            s = jnp.where(mask, logits[p], -jnp.inf)
            sink = sinks_ref[layer, head]
            m = jnp.maximum(jnp.max(s, axis=-1, keepdims=True), sink)
            pe = jnp.exp(s - m).astype(BF16)
            sink_terms.append(jnp.exp(sink - m))
            outs.append(_dot(pe, vvar_scr[2 * g + p, r0:r0 + 2 * WINDOW, :]))
        weighted = jnp.where(lo64_w, outs[0], outs[1])
        row_sums = pltpu.roll(jnp.where(lo64_w, outs[1], outs[0]), 64, 1)
        denom = row_sums + jnp.where(lo64_w, sink_terms[0], sink_terms[1])
        a_scr[r0:r0 + WINDOW, t * LANES:(t + 1) * LANES] = weighted * (1.0 / denom)

    n = 0
    for i in range(nsub):
        for t in range(ATT_WIDTH // LANES):
            logits = attention_logits(i, t)
            if n < len(att_companions):
                att_companions[n]()
            if n == 2:
                cast_next_layer_weights()
            n += 1
            attention_values(i, t, logits)
    for run in att_companions[n:]:
        run()

    kvar_scr[:, 0:WINDOW, :] = kvar_scr[:, tm:tm + WINDOW, :]
    vvar_scr[:, 0:WINDOW, :] = vvar_scr[:, tm:tm + WINDOW, :]

    def ret_slices(c, h):
        rows = slice(c * RET_CHUNK, (c + 1) * RET_CHUNK)
        return rows, slice(h * RET_DK, (h + 1) * RET_DK), slice(h * RET_DV, (h + 1) * RET_DV)

    def retention_scores_and_state(c):
        inners, s_olds = [], []
        for h in range(RET_HEADS):
            rows, sk, sv = ret_slices(c, h)
            qh = rq_scr[rows, sk]
            kh = rk_scr[rows, sk]
            vh = rv_scr[rows, sv]
            inners.append(lax.dot_general(qh.astype(BF16), kh.astype(BF16), NT, preferred_element_type=F32))
            s_old = s_scr[h]
            s_olds.append(s_old.astype(BF16))
            kd = (kh * kdec_ref[h]).astype(BF16)
            s_scr[h] = s_old * CHUNK_DECAY[h] + lax.dot_general(
                kd, vh, (((0,), (0,)), ((), ())), preferred_element_type=F32)
        return inners, s_olds

    def retention_outputs(c, inners, s_olds):
        for h in range(RET_HEADS):
            rows, sk, sv = ret_slices(c, h)
            qh = rq_scr[rows, sk]
            lhs = jnp.concatenate([(inners[h] * dmask_ref[h]).astype(BF16),
                                   (qh * qdec_ref[h]).astype(BF16)], axis=1)
            rhs = jnp.concatenate([rv_scr[rows, sv], s_olds[h]], axis=0)
            o = _dot(lhs, rhs)
            ms = jnp.mean(o * o, axis=-1, keepdims=True)
            on = o * lax.rsqrt(ms + RMS_EPS)
            r_scr[rows, sv] = on

    ret_companions = [half_chunk(C_MR, mr_scr, raw, 0), half_chunk(C_MR, mr_scr, raw, 1)]
    n = 0
    for c in range(nsub):
        inners, s_olds = retention_scores_and_state(c)
        if n < len(ret_companions):
            ret_companions[n]()
        n += 1
        retention_outputs(c, inners, s_olds)
    for run in ret_companions[n:]:
        run()
    sample_qmats, sample_logits = _sample_logits(sqkv_ref, ck_ref)
    for b in range(sqkv_ref.shape[0]):
        for h in range(RET_HEADS):
            _sample_state_update(sret_ref, st_ref, ns_ref, b, h)

    @pl.when(step == pl.num_programs(0) - 1)
    def _():
        snew_ref[...] = s_scr[...]

    gate_c = mod_ref[0:1, 2 * D_MODEL:3 * D_MODEL]
    lng = lng_ref[...]
    lnb = lnb_ref[...]
    windows = [slice(i * WINDOW, (i + 1) * WINDOW) for i in range(nsub)]
    za_all = _dot((a_scr[...] * _silu(ga_scr[...])).astype(BF16), wpa_ref[...])
    zr_all = _dot((r_scr[...] * _silu(rg_scr[...])).astype(BF16), wpr_ref[...])
    for rows in windows:
        z = _sigmoid(ma_scr[rows, :]) * za_all[rows, :] + _sigmoid(mr_scr[rows, :]) * zr_all[rows, :]
        u = _dot(z.astype(BF16), wout_ref[...])
        t = ALPHA * x_ref[rows, :] + gate_c * u
        mu = jnp.mean(t, axis=-1, keepdims=True)
        d = t - mu
        var = jnp.mean(d * d, axis=-1, keepdims=True)
        y_ref[rows, :] = d * lax.rsqrt(var + LN_EPS) * lng + lnb

    _sample_outputs(sinks_ref, layer, sqkv_ref, sret_ref, ck_ref, cv_ref, sample_qmats, sample_logits,
                    oatt_ref, oret_ref, nk_ref, nv_ref, ns_ref)


def _fused_layer(layer, x, mod, mod_row_block, base, rtab, weights_b, next_weights,
                 ln_g, ln_b, sinks, ret_tabs, sample_qkv, sample_ret, cache_k, cache_v, state, shared_out):
    w_in_b, w_pa_b, w_pr_b, w_out_b = weights_b
    seq = x.shape[0]
    tm = PROMPT_ROWS
    steps = seq // tm
    nbatch = sample_qkv.shape[0]
    nb = nbatch // steps
    win = cache_k.shape[3]
    assert win == WINDOW == LANES
    dmask, qdec, kdec = ret_tabs
    smem = pl.BlockSpec(memory_space=pltpu.SMEM)
    row_spec = lambda w: pl.BlockSpec((tm, w), lambda i: (i, 0))
    seq_spec = lambda w: pl.BlockSpec((nb, 1, w), lambda i: (i, 0, 0))
    cache_spec = pl.BlockSpec((None, nb, ATT_KV_WIDTH, win), lambda i: (layer, i, 0, 0))
    state_spec = pl.BlockSpec((None, nb, RET_HEADS, RET_DK, RET_DV), lambda i: (layer, i, 0, 0, 0))
    in_specs = [
        smem,
        row_spec(D_MODEL),
        pl.BlockSpec((None, SUBLANES, 3 * D_MODEL), lambda i: (layer, mod_row_block, 0),
                     pipeline_mode=pl.Buffered(1)),
        pl.BlockSpec((None, 4, LANES), lambda i: (i, 0, 0)),
        _const_spec((8, tm, LANES)),
        _const_spec((D_MODEL, IN_COLS)),
        _const_spec((ATT_WIDTH, D_MODEL)),
        _const_spec((RET_WIDTH, D_MODEL)),
        _const_spec((D_MODEL, D_MODEL)),
        _layer_spec((1, D_MODEL), layer), _layer_spec((1, D_MODEL), layer),
        _const_spec((RET_HEADS, RET_CHUNK, RET_CHUNK)),
        _const_spec((RET_HEADS, RET_CHUNK, RET_DK)),
        _const_spec((RET_HEADS, RET_CHUNK, RET_DK)),
        seq_spec(C_AG), seq_spec(2 * RET_QK_WIDTH + RET_WIDTH),
        cache_spec, cache_spec, state_spec,
    ]
    args = [sinks, x, mod, base, rtab, w_in_b, w_pa_b, w_pr_b, w_out_b, ln_g, ln_b, dmask, qdec, kdec,
            sample_qkv, sample_ret, cache_k, cache_v, state]
    assert len(args) == N_PROMPT_INPUTS
    cast_shapes, cast_specs = [], []
    if next_weights is not None:
        assert len(next_weights) == N_WEIGHTS
        for w in next_weights:
            _, rows, cols = w.shape
            slab = max(BF16_SUBLANES, rows // steps)
            nslab = rows // slab
            assert rows % slab == 0 and nslab <= steps
            args.append(w)
            in_specs.append(pl.BlockSpec((None, slab, cols),
                                         lambda i, n=nslab: (layer + 1, jnp.minimum(i, n - 1), 0)))
            cast_shapes.append(jax.ShapeDtypeStruct((rows, cols), BF16))
            cast_specs.append(pl.BlockSpec((slab, cols), lambda i, n=nslab: (jnp.minimum(i, n - 1), 0)))
    aliases = {}
    if shared_out is not None:
        for j, arr in enumerate(shared_out):
            aliases[len(args)] = N_PROMPT_OUTPUTS - len(shared_out) + j
            args.append(arr)
            in_specs.append(pl.BlockSpec(memory_space=pl.ANY))
    return pl.pallas_call(
        functools.partial(_prompt_kernel, layer=layer, n_aliased=len(aliases), n_cast=len(cast_shapes)),
        out_shape=(
            jax.ShapeDtypeStruct((seq, D_MODEL), F32),
            jax.ShapeDtypeStruct((WINDOW, ATT_KV_WIDTH), F32),
            jax.ShapeDtypeStruct((WINDOW, ATT_KV_WIDTH), F32),
            jax.ShapeDtypeStruct((RET_HEADS, RET_DK, RET_DV), F32),
            jax.ShapeDtypeStruct((nbatch, 1, ATT_WIDTH), F32),
            jax.ShapeDtypeStruct((nbatch, 1, RET_WIDTH), F32),
            jax.ShapeDtypeStruct(cache_k.shape, F32),
            jax.ShapeDtypeStruct(cache_v.shape, F32),
            jax.ShapeDtypeStruct(state.shape, F32),
        ) + tuple(cast_shapes),
        grid=(steps,),
        in_specs=in_specs,
        out_specs=(
            row_spec(D_MODEL),
            pl.BlockSpec((WINDOW, ATT_KV_WIDTH), lambda i: (0, 0)),
            pl.BlockSpec((WINDOW, ATT_KV_WIDTH), lambda i: (0, 0)),
            pl.BlockSpec((RET_HEADS, RET_DK, RET_DV), lambda i: (0, 0, 0)),
            seq_spec(ATT_WIDTH), seq_spec(RET_WIDTH),
            cache_spec, cache_spec, state_spec,
        ) + tuple(cast_specs),
        input_output_aliases=aliases,
        scratch_shapes=[
            pltpu.VMEM((4, tm, LANES), F32),
            pltpu.VMEM((tm, D_MODEL), BF16),
            pltpu.VMEM((tm, ATT_WIDTH), BF16),
            pltpu.VMEM((4, WINDOW + tm, LANES), BF16),
            pltpu.VMEM((4, WINDOW + tm, LANES), BF16),
            pltpu.VMEM((tm, ATT_WIDTH), F32),
            pltpu.VMEM((tm, ATT_WIDTH), F32),
            pltpu.VMEM((tm, RET_QK_WIDTH), F32),
            pltpu.VMEM((tm, RET_QK_WIDTH), F32),
            pltpu.VMEM((tm, RET_WIDTH), BF16),
            pltpu.VMEM((tm, RET_WIDTH), F32),
            pltpu.VMEM((tm, RET_WIDTH), F32),
            pltpu.VMEM((RET_HEADS, RET_DK, RET_DV), F32),
            pltpu.VMEM((tm, D_MODEL), F32),
            pltpu.VMEM((tm, D_MODEL), F32),
        ],
        compiler_params=pltpu.CompilerParams(
            dimension_semantics=("arbitrary",), vmem_limit_bytes=VMEM_LIMIT_BYTES),
        name="fused_layer",
    )(*args)


def _sample_proj_kernel(x_ref, mod_ref, tab_ref, win_ref,
                        qkv_ref, ga_ref, ret_ref, rg_ref, mg_ref):
    rows = x_ref.shape[0]
    x = x_ref[:, 0, :]
    shift = mod_ref[:, 0:D_MODEL]
    scale = mod_ref[:, D_MODEL:2 * D_MODEL]
    hb = (x * (1.0 + scale) + shift).astype(BF16)

    def proj(c0, c1):
        return _dot(hb, win_ref[:, c0:c1])

    lane = _lane_iota((rows, LANES))
    first_half32 = (lane & 32) == 0
    ca = tab_ref[0:1, :]
    sa = tab_ref[1:2, :]
    cr = tab_ref[2:3, :]
    sr = tab_ref[3:4, :]

    qkv = proj(C_AQ, C_AG)
    for t in range(ATT_WIDTH // LANES):
        qt = _rope_attn_tile(qkv[:, t * LANES:(t + 1) * LANES], ca, sa, first_half32)
        qkv_ref[:, 0, t * LANES:(t + 1) * LANES] = qt * ATT_SCALE
    qkv_ref[:, 0, C_AK:C_AV] = _rope_attn_tile(qkv[:, C_AK:C_AV], ca, sa, first_half32)
    qkv_ref[:, 0, C_AV:C_AG] = qkv[:, C_AV:C_AG]
    ga_ref[...] = _silu(proj(C_AG, C_RQ))

    rqk = proj(C_RQ, C_RV)
    for h in range(RET_HEADS):
        sl = slice(h * RET_DK, (h + 1) * RET_DK)
        ret_ref[:, 0, sl] = _rope_ret_tile(rqk[:, sl], cr, sr)
        sk = slice(RET_QK_WIDTH + h * RET_DK, RET_QK_WIDTH + (h + 1) * RET_DK)
        ret_ref[:, 0, sk] = _rope_ret_tile(rqk[:, sk], cr, sr) * RET_K_SCALE
    ret_ref[:, 0, 2 * RET_QK_WIDTH:] = proj(C_RV, C_RG)
    rg_ref[...] = _silu(proj(C_RG, C_MA))
    mg_ref[...] = _sigmoid(proj(C_MA, IN_COLS))


def _sample_proj(layer, x, mod, tab, w_in_b):
    rows = x.shape[0]
    shapes = ((rows, 1, C_AG), (rows, ATT_WIDTH), (rows, 1, 2 * RET_QK_WIDTH + RET_WIDTH),
              (rows, RET_WIDTH), (rows, 2 * D_MODEL))
    return pl.pallas_call(
        _sample_proj_kernel,
        out_shape=tuple(jax.ShapeDtypeStruct(s, F32) for s in shapes),
        grid=(1,),
        in_specs=[
            _const_spec((rows, 1, D_MODEL)),
            _layer_spec((rows, 3 * D_MODEL), layer),
            _const_spec((4, LANES)),
            _const_spec((D_MODEL, IN_COLS)),
        ],
        out_specs=tuple(pl.BlockSpec(s, lambda i, nd=len(s): (0,) * nd) for s in shapes),
        compiler_params=pltpu.CompilerParams(
            dimension_semantics=("arbitrary",), vmem_limit_bytes=VMEM_LIMIT_BYTES),
        name="sample_proj",
    )(x, mod, tab, w_in_b)


def _sample_logits(qkv_ref, ck_ref):
    nb = qkv_ref.shape[0]
    rowi = lax.broadcasted_iota(jnp.int32, (ATT_HEADS, LANES), 0)
    lanei = lax.broadcasted_iota(jnp.int32, (ATT_HEADS, LANES), 1)
    lane_group = lanei // ATT_HEAD_DIM
    qmats, logits = [], []
    for b in range(nb):
        qkv = qkv_ref[b]
        qmat = jnp.zeros((ATT_HEADS, LANES), F32)
        for h in range(ATT_HEADS):
            t, p, g = h // 2, h % 2, h // ATT_GROUP
            tile = qkv[:, t * LANES:(t + 1) * LANES]
            src = tile if p == g else pltpu.roll(tile, 64, 1)
            qmat = jnp.where((rowi == h) & (lane_group == g), jnp.broadcast_to(src, (ATT_HEADS, LANES)), qmat)
        qmats.append(qmat)
        logits.append(_dot(qmat.astype(BF16), ck_ref[b].astype(BF16)))
    return qmats, logits


def _sample_state_update(ret_ref, st_ref, ns_ref, b, h):
    dr = lax.broadcasted_iota(jnp.int32, (RET_DK, RET_DK), 0)
    dc = lax.broadcasted_iota(jnp.int32, (RET_DK, RET_DK), 1)
    ret = ret_ref[b]
    kh = ret[:, RET_QK_WIDTH + h * RET_DK:RET_QK_WIDTH + (h + 1) * RET_DK]
    vh = ret[:, 2 * RET_QK_WIDTH + h * RET_DV:2 * RET_QK_WIDTH + (h + 1) * RET_DV]
    kdiag = jnp.where(dr == dc, jnp.broadcast_to(kh, (RET_DK, RET_DK)), 0.0).astype(BF16)
    vfull = jnp.broadcast_to(vh, (RET_DK, RET_DV)).astype(BF16)
    ns_ref[b, h] = st_ref[b, h] * TOKEN_DECAY[h] + _dot(kdiag, vfull)


def _sample_outputs(sinks_ref, layer, qkv_ref, ret_ref, ck_ref, cv_ref, qmats, logits,
                    oatt_ref, oret_ref, nk_ref, nv_ref, ns_ref):
    nb = qkv_ref.shape[0]
    sink_col = jnp.zeros((ATT_HEADS, 1), F32)
    rowc = lax.broadcasted_iota(jnp.int32, (ATT_HEADS, 1), 0)
    for h in range(ATT_HEADS):
        sink_col = jnp.where(rowc == h, sinks_ref[layer, h], sink_col)
    lo64_row = _lane_iota((1, LANES)) < 64
    dr = lax.broadcasted_iota(jnp.int32, (LANES, LANES), 0)
    dc = lax.broadcasted_iota(jnp.int32, (LANES, LANES), 1)
    diag = dr == dc
    last_lane = dc == WINDOW - 1

    for b in range(nb):
        qkv = qkv_ref[b]
        k_new = qkv[:, C_AK:C_AV]
        v_new = qkv[:, C_AV:C_AG]
        s_c = logits[b]
        s_self = jnp.sum(qmats[b] * k_new, axis=-1, keepdims=True)
        m = jnp.maximum(jnp.maximum(jnp.max(s_c, axis=-1, keepdims=True), s_self), sink_col)
        p_c = jnp.exp(s_c - m)
        p_self = jnp.exp(s_self - m)
        denom = jnp.sum(p_c, axis=-1, keepdims=True) + p_self + jnp.exp(sink_col - m)
        o = (lax.dot_general(p_c.astype(BF16), cv_ref[b].astype(BF16), NT, preferred_element_type=F32)
             + p_self * v_new) / denom
        o_sw = pltpu.roll(o, 64, 1)
        for t in range(ATT_WIDTH // LANES):
            g = t // 2
            first = (o if g == 0 else o_sw)[2 * t:2 * t + 1, :]
            second = (o_sw if g == 0 else o)[2 * t + 1:2 * t + 2, :]
            oatt_ref[b, :, t * LANES:(t + 1) * LANES] = jnp.where(lo64_row, first, second)

        for new_row, src_ref, dst_ref in ((k_new, ck_ref, nk_ref), (v_new, cv_ref, nv_ref)):
            new_col = jnp.sum(jnp.where(diag, jnp.broadcast_to(new_row, (LANES, LANES)), 0.0),
                              axis=1, keepdims=True)
            shifted = pltpu.roll(src_ref[b], WINDOW - 1, 1)
            dst_ref[b] = jnp.where(last_lane, new_col, shifted)

    for b in range(nb):
        ret = ret_ref[b]
        for h in range(RET_HEADS):
            qh = ret[:, h * RET_DK:(h + 1) * RET_DK]
            q8 = jnp.broadcast_to(qh, (SUBLANES, RET_DK)).astype(BF16)
            oh = _dot(q8, ns_ref[b, h].astype(BF16))[0:1, :]
            ms = jnp.mean(oh * oh, axis=-1, keepdims=True)
            oret_ref[b, :, h * RET_DV:(h + 1) * RET_DV] = oh * lax.rsqrt(ms + RMS_EPS)


def _sample_out_kernel(x_ref, mod_ref, oatt_ref, ga_ref, oret_ref, rg_ref, mg_ref,
                       wpa_ref, wpr_ref, wout_ref, lng_ref, lnb_ref, y_ref):
    gate_c = mod_ref[:, 2 * D_MODEL:3 * D_MODEL]
    za = _dot((oatt_ref[:, 0, :] * ga_ref[...]).astype(BF16), wpa_ref[...])
    zr = _dot((oret_ref[:, 0, :] * rg_ref[...]).astype(BF16), wpr_ref[...])
    z = mg_ref[:, 0:D_MODEL] * za + mg_ref[:, D_MODEL:2 * D_MODEL] * zr
    u = _dot(z.astype(BF16), wout_ref[...])
    t = ALPHA * x_ref[:, 0, :] + gate_c * u
    mu = jnp.mean(t, axis=-1, keepdims=True)
    d = t - mu
    var = jnp.mean(d * d, axis=-1, keepdims=True)
    y_ref[:, 0, :] = d * lax.rsqrt(var + LN_EPS) * lng_ref[...] + lnb_ref[...]


def _sample_out(layer, x, mod, oatt, ga, oret, rg, mg, w_pa_b, w_pr_b, w_out_b, ln_g, ln_b):
    rows = x.shape[0]
    return pl.pallas_call(
        _sample_out_kernel,
        out_shape=jax.ShapeDtypeStruct(x.shape, F32),
        grid=(1,),
        in_specs=[
            _const_spec((rows, 1, D_MODEL)),
            _layer_spec((rows, 3 * D_MODEL), layer),
            _const_spec((rows, 1, ATT_WIDTH)), _const_spec((rows, ATT_WIDTH)),
            _const_spec((rows, 1, RET_WIDTH)), _const_spec((rows, RET_WIDTH)),
            _const_spec((rows, 2 * D_MODEL)),
            _const_spec((ATT_WIDTH, D_MODEL)),
            _const_spec((RET_WIDTH, D_MODEL)),
            _const_spec((D_MODEL, D_MODEL)),
            _layer_spec((1, D_MODEL), layer), _layer_spec((1, D_MODEL), layer),
        ],
        out_specs=pl.BlockSpec((rows, 1, D_MODEL), lambda i: (0, 0, 0)),
        compiler_params=pltpu.CompilerParams(
            dimension_semantics=("arbitrary",), vmem_limit_bytes=VMEM_LIMIT_BYTES),
        name="sample_out",
    )(x, mod, oatt, ga, oret, rg, mg, w_pa_b, w_pr_b, w_out_b, ln_g, ln_b)


def kernel(x_prompt, x_sample, c_prompt, c_sample, cache_k, cache_v, state_ret, w_in, attn_sinks,
           w_cond, b_cond, w_proj_attn, w_proj_ret, w_out, ln_g, ln_b):
    seq = x_prompt.shape[1]
    nbatch = x_sample.shape[0]
    win = cache_k.shape[2]
    assert seq % PROMPT_ROWS == 0 and nbatch % (seq // PROMPT_ROWS) == 0 and nbatch % SUBLANES == 0

    weights_f32 = (w_in, w_proj_attn, w_proj_ret, w_out)
    weights_b = tuple(w[0].astype(BF16) for w in weights_f32)
    ln_g3 = ln_g.reshape(DEPTH, 1, D_MODEL)
    ln_b3 = ln_b.reshape(DEPTH, 1, D_MODEL)

    c_all = jnp.concatenate([c_sample, c_prompt, jnp.zeros((SUBLANES - 1, D_MODEL), F32)], axis=0)
    mod = _cond_call(c_all, w_cond, b_cond)
    prompt_mod_block = nbatch // SUBLANES

    base, rtab = _prompt_rope_tables(seq, PROMPT_ROWS)
    stab = _sample_rope_table(float(PAST_LEN))
    ret_tabs = _retention_tables()

    def feature_major(c):
        return c.transpose(0, 1, 3, 4, 2).reshape(DEPTH, nbatch, ATT_KV_WIDTH, win)

    def window_major(c):
        return c.reshape(DEPTH, nbatch, ATT_KV_HEADS, ATT_HEAD_DIM, win).transpose(0, 1, 4, 2, 3)

    ck = feature_major(cache_k)
    cv = feature_major(cache_v)

    yp = x_prompt[0]
    ys = x_sample
    kp, vp, sp = [], [], []
    prev = None
    for l in range(DEPTH):
        w_in_b, w_pa_b, w_pr_b, w_out_b = weights_b
        qkv, ga, ret, rg, mg = _sample_proj(l, ys, mod, stab, w_in_b)
        outs = _fused_layer(
            l, yp, mod, prompt_mod_block, base, rtab, weights_b,
            weights_f32 if l + 1 < DEPTH else None,
            ln_g3, ln_b3, attn_sinks, ret_tabs, qkv, ret, ck, cv, state_ret, prev)
        yp, k_new, v_new, s_new, oatt, oret, nk, nv, ns = outs[:N_PROMPT_OUTPUTS]
        kp.append(k_new.reshape(1, WINDOW, ATT_KV_HEADS, ATT_HEAD_DIM))
        vp.append(v_new.reshape(1, WINDOW, ATT_KV_HEADS, ATT_HEAD_DIM))
        sp.append(s_new[None])
        prev = (nk, nv, ns)
        ys = _sample_out(l, ys, mod, oatt, ga, oret, rg, mg, w_pa_b, w_pr_b, w_out_b, ln_g3, ln_b3)
        weights_b = tuple(outs[N_PROMPT_OUTPUTS:])

    nk, nv, ns = prev
    return (yp[None], ys, jnp.stack(kp), jnp.stack(vp), jnp.stack(sp),
            window_major(nk), window_major(nv), ns)
```

```python
import functools

import jax
import jax.numpy as jnp
import numpy as np
from jax import lax
from jax.experimental import pallas as pl
from jax.experimental.pallas import tpu as pltpu

D_MODEL = 1024
DEPTH = 2
PAST_LEN = 16384
ATT_HEADS = 8
ATT_KV_HEADS = 2
ATT_HEAD_DIM = 64
ATT_GROUP = ATT_HEADS // ATT_KV_HEADS
ATT_WIDTH = ATT_HEADS * ATT_HEAD_DIM
ATT_KV_WIDTH = ATT_KV_HEADS * ATT_HEAD_DIM
WINDOW = 128
RET_HEADS = 4
RET_DK = 128
RET_DV = 256
RET_QK_WIDTH = RET_HEADS * RET_DK
RET_WIDTH = RET_HEADS * RET_DV
RET_CHUNK = 128
ROPE_THETA = 10000.0
ALPHA = (2.0 * DEPTH) ** 0.25
LN_EPS = 1e-5
RMS_EPS = 1e-6
ATT_SCALE = ATT_HEAD_DIM ** -0.5
RET_K_SCALE = RET_DK ** -0.5

C_AQ = 0
C_AK = C_AQ + ATT_WIDTH
C_AV = C_AK + ATT_KV_WIDTH
C_AG = C_AV + ATT_KV_WIDTH
C_RQ = C_AG + ATT_WIDTH
C_RK = C_RQ + RET_QK_WIDTH
C_RV = C_RK + RET_QK_WIDTH
C_RG = C_RV + RET_WIDTH
C_MA = C_RG + RET_WIDTH
C_MR = C_MA + D_MODEL
IN_COLS = C_MR + D_MODEL

LANES = 128
SUBLANES = 8
BF16_SUBLANES = 16
VMEM_LIMIT_BYTES = 56 * 1024 * 1024

PROMPT_ROWS = 256

BF16 = jnp.bfloat16
F32 = jnp.float32
NT = (((1,), (1,)), ((), ()))

_LOG_GAMMA = np.log(1.0 - 2.0 ** (-5.0 - np.arange(RET_HEADS, dtype=np.float64)))
CHUNK_DECAY = tuple(float(v) for v in np.exp(RET_CHUNK * _LOG_GAMMA))
TOKEN_DECAY = tuple(float(v) for v in np.exp(_LOG_GAMMA))


def _sigmoid(x):
    return 0.5 * jnp.tanh(0.5 * x) + 0.5


def _silu(x):
    return x * _sigmoid(x)


def _dot(a, b):
    return jnp.dot(a, b, preferred_element_type=F32)


def _rope_attn_tile(x, cos, sin_signed, first_half):
    rot = jnp.where(first_half, pltpu.roll(x, LANES - 32, 1), pltpu.roll(x, 32, 1))
    return x * cos + rot * sin_signed


def _rope_ret_tile(x, cos, sin_signed):
    return x * cos + pltpu.roll(x, 64, 1) * sin_signed


def _lane_iota(shape):
    return lax.broadcasted_iota(jnp.int32, shape, len(shape) - 1)


def _layer_spec(shape, layer):
    nd = len(shape)
    return pl.BlockSpec((None,) + tuple(shape), lambda i: (layer,) + (0,) * nd,
                        pipeline_mode=pl.Buffered(1))


def _const_spec(shape):
    nd = len(shape)
    return pl.BlockSpec(tuple(shape), lambda i: (0,) * nd, pipeline_mode=pl.Buffered(1))


def _rope_lane_patterns():
    lane = np.arange(LANES)
    f_att = ROPE_THETA ** (-(lane % 32) / 32.0)
    s_att = np.where(lane % 64 < 32, -1.0, 1.0)
    f_ret = ROPE_THETA ** (-(lane % 64) / 64.0)
    s_ret = np.where(lane < 64, -1.0, 1.0)
    return (f_att, s_att), (f_ret, s_ret)


def _prompt_rope_tables(seq, tm):
    starts = np.arange(seq // tm, dtype=np.float64)[:, None] * tm
    offs = np.arange(tm, dtype=np.float64)[:, None]
    base, within = [], []
    for freq, sign in _rope_lane_patterns():
        base += [np.cos(starts * freq), np.sin(starts * freq)]
        c, s = np.cos(offs * freq), np.sin(offs * freq)
        within += [c, s, sign * c, sign * s]
    return (jnp.asarray(np.stack(base, axis=1), F32),
            jnp.asarray(np.stack(within, axis=0), F32))


def _sample_rope_table(pos):
    rows = []
    for freq, sign in _rope_lane_patterns():
        rows += [np.cos(pos * freq), sign * np.sin(pos * freq)]
    return jnp.asarray(np.stack(rows, axis=0), F32)


def _retention_tables():
    c = RET_CHUNK
    idx = np.arange(c, dtype=np.float64)
    diff = idx[:, None] - idx[None, :]
    lg = _LOG_GAMMA[:, None, None]
    dmask = np.where(diff >= 0, np.exp(np.maximum(diff, 0.0)[None] * lg), 0.0)
    qdec = np.broadcast_to(np.exp((idx + 1.0)[None, :, None] * lg), (RET_HEADS, c, RET_DK))
    kdec = np.broadcast_to(np.exp((c - 1.0 - idx)[None, :, None] * lg), (RET_HEADS, c, RET_DK))
    return jnp.asarray(dmask, F32), jnp.asarray(qdec, F32), jnp.asarray(kdec, F32)


def _cond_kernel(c_ref, w_ref, b_ref, o_ref):
    a = _silu(c_ref[...]).astype(BF16)
    o_ref[...] = _dot(a, w_ref[...].astype(BF16)) + b_ref[...]


def _cond_call(c_all, w_cond, b_cond):
    rows = c_all.shape[0]
    tn = 3 * D_MODEL // 2
    return pl.pallas_call(
        _cond_kernel,
        out_shape=jax.ShapeDtypeStruct((DEPTH, rows, 3 * D_MODEL), F32),
        grid=(DEPTH, 3 * D_MODEL // tn),
        in_specs=[
            pl.BlockSpec((rows, D_MODEL), lambda l, j: (0, 0)),
            pl.BlockSpec((None, D_MODEL, tn), lambda l, j: (l, 0, j)),
            pl.BlockSpec((None, 1, tn), lambda l, j: (l, 0, j)),
        ],
        out_specs=pl.BlockSpec((None, rows, tn), lambda l, j: (l, 0, j)),
        compiler_params=pltpu.CompilerParams(
            dimension_semantics=("arbitrary", "arbitrary"), vmem_limit_bytes=VMEM_LIMIT_BYTES),
        name="cond_mod",
    )(c_all, w_cond, b_cond.reshape(DEPTH, 1, 3 * D_MODEL))


N_PROMPT_INPUTS = 19
N_PROMPT_OUTPUTS = 9
N_WEIGHTS = 4


def _prompt_kernel(*refs, layer, n_aliased, n_cast):
    (sinks_ref, x_ref, mod_ref, base_ref, rtab_ref,
     win_ref, wpa_ref, wpr_ref, wout_ref, lng_ref, lnb_ref,
     dmask_ref, qdec_ref, kdec_ref,
     sqkv_ref, sret_ref, ck_ref, cv_ref, st_ref) = refs[:N_PROMPT_INPUTS]
    cast_in = refs[N_PROMPT_INPUTS:N_PROMPT_INPUTS + n_cast]
    refs = refs[N_PROMPT_INPUTS + n_cast + n_aliased:]
    (y_ref, knew_ref, vnew_ref, snew_ref,
     oatt_ref, oret_ref, nk_ref, nv_ref, ns_ref) = refs[:N_PROMPT_OUTPUTS]
    cast_out = refs[N_PROMPT_OUTPUTS:N_PROMPT_OUTPUTS + n_cast]
    (tab_scr, hb_scr, q_scr, kvar_scr, vvar_scr, ga_scr, a_scr,
     rq_scr, rk_scr, rv_scr, rg_scr, r_scr, s_scr,
     ma_scr, mr_scr) = refs[N_PROMPT_OUTPUTS + n_cast:]
    step = pl.program_id(0)

    def cast_next_layer_weights():
        for src, dst in zip(cast_in, cast_out):
            dst[...] = src[...].astype(BF16)

    tm = x_ref.shape[0]
    nsub = tm // WINDOW

    def proj(c0, c1):
        return _dot(hb_scr[...], win_ref[:, c0:c1])

    lane = _lane_iota((tm, LANES))
    first_half32 = (lane & 32) == 0
    lo64 = lane < 64

    def rope_attn(t):
        return _rope_attn_tile(t, tab_scr[0], tab_scr[1], first_half32)

    def rope_ret(t):
        return _rope_ret_tile(t, tab_scr[2], tab_scr[3])

    def store_variants(scr, t, fill):
        swapped = pltpu.roll(t, 64, 1)
        other = jnp.full_like(t, fill)
        scr[0, WINDOW:WINDOW + tm, :] = jnp.where(lo64, t, other).astype(BF16)
        scr[1, WINDOW:WINDOW + tm, :] = jnp.where(lo64, other, swapped).astype(BF16)
        scr[2, WINDOW:WINDOW + tm, :] = jnp.where(lo64, swapped, other).astype(BF16)
        scr[3, WINDOW:WINDOW + tm, :] = jnp.where(lo64, other, t).astype(BF16)

    def block_prepare(src_x_ref, src_base_ref):
        shift = mod_ref[0:1, 0:D_MODEL]
        scale = mod_ref[0:1, D_MODEL:2 * D_MODEL]
        hb_scr[...] = (src_x_ref[...] * (1.0 + scale) + shift).astype(BF16)
        for fam in range(2):
            cb = src_base_ref[2 * fam:2 * fam + 1, :]
            sb = src_base_ref[2 * fam + 1:2 * fam + 2, :]
            tab_scr[2 * fam] = cb * rtab_ref[4 * fam] - sb * rtab_ref[4 * fam + 1]
            tab_scr[2 * fam + 1] = sb * rtab_ref[4 * fam + 2] + cb * rtab_ref[4 * fam + 3]

    def block_head():
        kv = proj(C_AK, C_AG)
        k_rot = rope_attn(kv[:, 0:LANES])
        v_raw = kv[:, LANES:2 * LANES]
        knew_ref[...] = k_rot[tm - WINDOW:, :]
        vnew_ref[...] = v_raw[tm - WINDOW:, :]
        store_variants(kvar_scr, k_rot, 0.0)
        store_variants(vvar_scr, v_raw, 1.0)
        qp = proj(C_AQ, C_AK)
        for t in range(ATT_WIDTH // LANES):
            qt = rope_attn(qp[:, t * LANES:(t + 1) * LANES])
            q_scr[:, t * LANES:(t + 1) * LANES] = (qt * ATT_SCALE).astype(BF16)
        ga_scr[...] = proj(C_AG, C_RQ)

    @pl.when(step == 0)
    def _():
        kvar_scr[:, 0:WINDOW, :] = jnp.zeros((4, WINDOW, LANES), BF16)
        vvar_scr[:, 0:WINDOW, :] = jnp.zeros((4, WINDOW, LANES), BF16)
        s_scr[...] = jnp.zeros(s_scr.shape, F32)

    block_prepare(x_ref, base_ref)
    block_head()

    def chunk_rq():
        p = proj(C_RQ, C_RK)
        for h in range(RET_HEADS):
            sl = slice(h * RET_DK, (h + 1) * RET_DK)
            rq_scr[:, sl] = rope_ret(p[:, sl])

    def chunk_rk():
        p = proj(C_RK, C_RV)
        for h in range(RET_HEADS):
            sl = slice(h * RET_DK, (h + 1) * RET_DK)
            rk_scr[:, sl] = rope_ret(p[:, sl]) * RET_K_SCALE

    def half_chunk(c0, dst, fn, half):
        w = dst.shape[1] // 2
        def run():
            dst[:, half * w:(half + 1) * w] = fn(proj(c0 + half * w, c0 + (half + 1) * w))
        return run

    to_bf16 = lambda v: v.astype(BF16)
    raw = lambda v: v
    att_companions = [
        chunk_rq, chunk_rk,
        half_chunk(C_RV, rv_scr, to_bf16, 0), half_chunk(C_RV, rv_scr, to_bf16, 1),
        half_chunk(C_RG, rg_scr, raw, 0), half_chunk(C_RG, rg_scr, raw, 1),
        half_chunk(C_MA, ma_scr, raw, 0), half_chunk(C_MA, ma_scr, raw, 1),
    ]

    row = lax.broadcasted_iota(jnp.int32, (WINDOW, 2 * WINDOW), 0)
    col = lax.broadcasted_iota(jnp.int32, (WINDOW, 2 * WINDOW), 1)
    in_window = col <= row + WINDOW
    mask_std = (col >= row) & in_window
    off = jnp.where(step > 0, 0, 4 * WINDOW)
    mask_first = ((col >= row + off) | (col >= WINDOW)) & in_window
    lo64_w = _lane_iota((WINDOW, LANES)) < 64

    def attention_logits(i, t):
        r0 = i * WINDOW
        g = t // 2
        qt = q_scr[r0:r0 + WINDOW, t * LANES:(t + 1) * LANES]
        return [lax.dot_general(qt, kvar_scr[2 * g + p, r0:r0 + 2 * WINDOW, :], NT,
                                preferred_element_type=F32) for p in range(2)]

    def attention_values(i, t, logits):
        r0 = i * WINDOW
        mask = mask_first if i == 0 else mask_std
        g = t // 2
        outs, sink_terms = [], []
        for p in range(2):
            head = 2 * t + p
            s = jnp.where(mask, logits[p], -jnp.inf)
            sink = sinks_ref[layer, head]
            m = jnp.maximum(jnp.max(s, axis=-1, keepdims=True), sink)
            pe = jnp.exp(s - m).astype(BF16)
            sink_terms.append(jnp.exp(sink - m))
            outs.append(_dot(pe, vvar_scr[2 * g + p, r0:r0 + 2 * WINDOW, :]))
        weighted = jnp.where(lo64_w, outs[0], outs[1])
        row_sums = pltpu.roll(jnp.where(lo64_w, outs[1], outs[0]), 64, 1)
        denom = row_sums + jnp.where(lo64_w, sink_terms[0], sink_terms[1])
        a_scr[r0:r0 + WINDOW, t * LANES:(t + 1) * LANES] = weighted * (1.0 / denom)

    n = 0
    for i in range(nsub):
        for t in range(ATT_WIDTH // LANES):
            logits = attention_logits(i, t)
            if n < len(att_companions):
                att_companions[n]()
            if n == 2:
                cast_next_layer_weights()
            n += 1
            attention_values(i, t, logits)
    for run in att_companions[n:]:
        run()

    kvar_scr[:, 0:WINDOW, :] = kvar_scr[:, tm:tm + WINDOW, :]
    vvar_scr[:, 0:WINDOW, :] = vvar_scr[:, tm:tm + WINDOW, :]

    def ret_slices(c, h):
        rows = slice(c * RET_CHUNK, (c + 1) * RET_CHUNK)
        return rows, slice(h * RET_DK, (h + 1) * RET_DK), slice(h * RET_DV, (h + 1) * RET_DV)

    def retention_scores_and_state(c):
        inners, s_olds = [], []
        for h in range(RET_HEADS):
            rows, sk, sv = ret_slices(c, h)
            qh = rq_scr[rows, sk]
            kh = rk_scr[rows, sk]
            vh = rv_scr[rows, sv]
            inners.append(lax.dot_general(qh.astype(BF16), kh.astype(BF16), NT, preferred_element_type=F32))
            s_old = s_scr[h]
            s_olds.append(s_old.astype(BF16))
            kd = (kh * kdec_ref[h]).astype(BF16)
            s_scr[h] = s_old * CHUNK_DECAY[h] + lax.dot_general(
                kd, vh, (((0,), (0,)), ((), ())), preferred_element_type=F32)
        return inners, s_olds

    def retention_outputs(c, inners, s_olds):
        for h in range(RET_HEADS):
            rows, sk, sv = ret_slices(c, h)
            qh = rq_scr[rows, sk]
            lhs = jnp.concatenate([(inners[h] * dmask_ref[h]).astype(BF16),
                                   (qh * qdec_ref[h]).astype(BF16)], axis=1)
            rhs = jnp.concatenate([rv_scr[rows, sv], s_olds[h]], axis=0)
            o = _dot(lhs, rhs)
            ms = jnp.mean(o * o, axis=-1, keepdims=True)
            on = o * lax.rsqrt(ms + RMS_EPS)
            r_scr[rows, sv] = on

    ret_companions = [half_chunk(C_MR, mr_scr, raw, 0), half_chunk(C_MR, mr_scr, raw, 1)]
    n = 0
    for c in range(nsub):
        inners, s_olds = retention_scores_and_state(c)
        if n < len(ret_companions):
            ret_companions[n]()
        n += 1
        retention_outputs(c, inners, s_olds)
    for run in ret_companions[n:]:
        run()
    sample_qmats, sample_logits = _sample_logits(sqkv_ref, ck_ref)
    for b in range(sqkv_ref.shape[0]):
        for h in range(RET_HEADS):
            _sample_state_update(sret_ref, st_ref, ns_ref, b, h)

    gate_c = mod_ref[0:1, 2 * D_MODEL:3 * D_MODEL]
    lng = lng_ref[...]
    lnb = lnb_ref[...]
    windows = [slice(i * WINDOW, (i + 1) * WINDOW) for i in range(nsub)]
    za_all = _dot((a_scr[...] * _silu(ga_scr[...])).astype(BF16), wpa_ref[...])
    zr_all = _dot((r_scr[...] * _silu(rg_scr[...])).astype(BF16), wpr_ref[...])
    for rows in windows:
        z = _sigmoid(ma_scr[rows, :]) * za_all[rows, :] + _sigmoid(mr_scr[rows, :]) * zr_all[rows, :]
        u = _dot(z.astype(BF16), wout_ref[...])
        t = ALPHA * x_ref[rows, :] + gate_c * u
        mu = jnp.mean(t, axis=-1, keepdims=True)
        d = t - mu
        var = jnp.mean(d * d, axis=-1, keepdims=True)
        y_ref[rows, :] = d * lax.rsqrt(var + LN_EPS) * lng + lnb

    _sample_outputs(sinks_ref, layer, sqkv_ref, sret_ref, ck_ref, cv_ref, sample_qmats, sample_logits,
                    oatt_ref, oret_ref, nk_ref, nv_ref, ns_ref)

    @pl.when(step == pl.num_programs(0) - 1)
    def _():
        snew_ref[...] = s_scr[...]


def _fused_layer(layer, x, mod, mod_row_block, base, rtab, weights_b, next_weights,
                 ln_g, ln_b, sinks, ret_tabs, sample_qkv, sample_ret, cache_k, cache_v, state, shared_out):
    w_in_b, w_pa_b, w_pr_b, w_out_b = weights_b
    seq = x.shape[0]
    tm = PROMPT_ROWS
    steps = seq // tm
    nbatch = sample_qkv.shape[0]
    nb = nbatch // steps
    win = cache_k.shape[3]
    assert win == WINDOW == LANES
    dmask, qdec, kdec = ret_tabs
    smem = pl.BlockSpec(memory_space=pltpu.SMEM)
    row_spec = lambda w: pl.BlockSpec((tm, w), lambda i: (i, 0))
    seq_spec = lambda w: pl.BlockSpec((nb, 1, w), lambda i: (i, 0, 0))
    cache_spec = pl.BlockSpec((None, nb, ATT_KV_WIDTH, win), lambda i: (layer, i, 0, 0))
    state_spec = pl.BlockSpec((None, nb, RET_HEADS, RET_DK, RET_DV), lambda i: (layer, i, 0, 0, 0))
    in_specs = [
        smem,
        row_spec(D_MODEL),
        pl.BlockSpec((None, SUBLANES, 3 * D_MODEL), lambda i: (layer, mod_row_block, 0),
                     pipeline_mode=pl.Buffered(1)),
        pl.BlockSpec((None, 4, LANES), lambda i: (i, 0, 0)),
        _const_spec((8, tm, LANES)),
        _const_spec((D_MODEL, IN_COLS)),
        _const_spec((ATT_WIDTH, D_MODEL)),
        _const_spec((RET_WIDTH, D_MODEL)),
        _const_spec((D_MODEL, D_MODEL)),
        _layer_spec((1, D_MODEL), layer), _layer_spec((1, D_MODEL), layer),
        _const_spec((RET_HEADS, RET_CHUNK, RET_CHUNK)),
        _const_spec((RET_HEADS, RET_CHUNK, RET_DK)),
        _const_spec((RET_HEADS, RET_CHUNK, RET_DK)),
        seq_spec(C_AG), seq_spec(2 * RET_QK_WIDTH + RET_WIDTH),
        cache_spec, cache_spec, state_spec,
    ]
    args = [sinks, x, mod, base, rtab, w_in_b, w_pa_b, w_pr_b, w_out_b, ln_g, ln_b, dmask, qdec, kdec,
            sample_qkv, sample_ret, cache_k, cache_v, state]
    assert len(args) == N_PROMPT_INPUTS
    cast_shapes, cast_specs = [], []
    if next_weights is not None:
        assert len(next_weights) == N_WEIGHTS
        for w in next_weights:
            _, rows, cols = w.shape
            slab = max(BF16_SUBLANES, rows // steps)
            nslab = rows // slab
            assert rows % slab == 0 and nslab <= steps
            args.append(w)
            in_specs.append(pl.BlockSpec((None, slab, cols),
                                         lambda i, n=nslab: (layer + 1, jnp.minimum(i, n - 1), 0)))
            cast_shapes.append(jax.ShapeDtypeStruct((rows, cols), BF16))
            cast_specs.append(pl.BlockSpec((slab, cols), lambda i, n=nslab: (jnp.minimum(i, n - 1), 0)))
    aliases = {}
    if shared_out is not None:
        for j, arr in enumerate(shared_out):
            aliases[len(args)] = N_PROMPT_OUTPUTS - len(shared_out) + j
            args.append(arr)
            in_specs.append(pl.BlockSpec(memory_space=pl.ANY))
    return pl.pallas_call(
        functools.partial(_prompt_kernel, layer=layer, n_aliased=len(aliases), n_cast=len(cast_shapes)),
        out_shape=(
            jax.ShapeDtypeStruct((seq, D_MODEL), F32),
            jax.ShapeDtypeStruct((WINDOW, ATT_KV_WIDTH), F32),
            jax.ShapeDtypeStruct((WINDOW, ATT_KV_WIDTH), F32),
            jax.ShapeDtypeStruct((RET_HEADS, RET_DK, RET_DV), F32),
            jax.ShapeDtypeStruct((nbatch, 1, ATT_WIDTH), F32),
            jax.ShapeDtypeStruct((nbatch, 1, RET_WIDTH), F32),
            jax.ShapeDtypeStruct(cache_k.shape, F32),
            jax.ShapeDtypeStruct(cache_v.shape, F32),
            jax.ShapeDtypeStruct(state.shape, F32),
        ) + tuple(cast_shapes),
        grid=(steps,),
        in_specs=in_specs,
        out_specs=(
            row_spec(D_MODEL),
            pl.BlockSpec((WINDOW, ATT_KV_WIDTH), lambda i: (0, 0)),
            pl.BlockSpec((WINDOW, ATT_KV_WIDTH), lambda i: (0, 0)),
            pl.BlockSpec((RET_HEADS, RET_DK, RET_DV), lambda i: (0, 0, 0)),
            seq_spec(ATT_WIDTH), seq_spec(RET_WIDTH),
            cache_spec, cache_spec, state_spec,
        ) + tuple(cast_specs),
        input_output_aliases=aliases,
        scratch_shapes=[
            pltpu.VMEM((4, tm, LANES), F32),
            pltpu.VMEM((tm, D_MODEL), BF16),
            pltpu.VMEM((tm, ATT_WIDTH), BF16),
            pltpu.VMEM((4, WINDOW + tm, LANES), BF16),
            pltpu.VMEM((4, WINDOW + tm, LANES), BF16),
            pltpu.VMEM((tm, ATT_WIDTH), F32),
            pltpu.VMEM((tm, ATT_WIDTH), F32),
            pltpu.VMEM((tm, RET_QK_WIDTH), F32),
            pltpu.VMEM((tm, RET_QK_WIDTH), F32),
            pltpu.VMEM((tm, RET_WIDTH), BF16),
            pltpu.VMEM((tm, RET_WIDTH), F32),
            pltpu.VMEM((tm, RET_WIDTH), F32),
            pltpu.VMEM((RET_HEADS, RET_DK, RET_DV), F32),
            pltpu.VMEM((tm, D_MODEL), F32),
            pltpu.VMEM((tm, D_MODEL), F32),
        ],
        compiler_params=pltpu.CompilerParams(
            dimension_semantics=("arbitrary",), vmem_limit_bytes=VMEM_LIMIT_BYTES),
        name="fused_layer",
    )(*args)


def _sample_proj_kernel(x_ref, mod_ref, tab_ref, win_ref,
                        qkv_ref, ga_ref, ret_ref, rg_ref, mg_ref):
    rows = x_ref.shape[0]
    x = x_ref[:, 0, :]
    shift = mod_ref[:, 0:D_MODEL]
    scale = mod_ref[:, D_MODEL:2 * D_MODEL]
    hb = (x * (1.0 + scale) + shift).astype(BF16)

    def proj(c0, c1):
        return _dot(hb, win_ref[:, c0:c1])

    lane = _lane_iota((rows, LANES))
    first_half32 = (lane & 32) == 0
    ca = tab_ref[0:1, :]
    sa = tab_ref[1:2, :]
    cr = tab_ref[2:3, :]
    sr = tab_ref[3:4, :]

    qkv = proj(C_AQ, C_AG)
    for t in range(ATT_WIDTH // LANES):
        qt = _rope_attn_tile(qkv[:, t * LANES:(t + 1) * LANES], ca, sa, first_half32)
        qkv_ref[:, 0, t * LANES:(t + 1) * LANES] = qt * ATT_SCALE
    qkv_ref[:, 0, C_AK:C_AV] = _rope_attn_tile(qkv[:, C_AK:C_AV], ca, sa, first_half32)
    qkv_ref[:, 0, C_AV:C_AG] = qkv[:, C_AV:C_AG]
    ga_ref[...] = _silu(proj(C_AG, C_RQ))

    rqk = proj(C_RQ, C_RV)
    for h in range(RET_HEADS):
        sl = slice(h * RET_DK, (h + 1) * RET_DK)
        ret_ref[:, 0, sl] = _rope_ret_tile(rqk[:, sl], cr, sr)
        sk = slice(RET_QK_WIDTH + h * RET_DK, RET_QK_WIDTH + (h + 1) * RET_DK)
        ret_ref[:, 0, sk] = _rope_ret_tile(rqk[:, sk], cr, sr) * RET_K_SCALE
    ret_ref[:, 0, 2 * RET_QK_WIDTH:] = proj(C_RV, C_RG)
    rg_ref[...] = _silu(proj(C_RG, C_MA))
    mg_ref[...] = _sigmoid(proj(C_MA, IN_COLS))


def _sample_proj(layer, x, mod, tab, w_in_b):
    rows = x.shape[0]
    shapes = ((rows, 1, C_AG), (rows, ATT_WIDTH), (rows, 1, 2 * RET_QK_WIDTH + RET_WIDTH),
              (rows, RET_WIDTH), (rows, 2 * D_MODEL))
    return pl.pallas_call(
        _sample_proj_kernel,
        out_shape=tuple(jax.ShapeDtypeStruct(s, F32) for s in shapes),
        grid=(1,),
        in_specs=[
            _const_spec((rows, 1, D_MODEL)),
            _layer_spec((rows, 3 * D_MODEL), layer),
            _const_spec((4, LANES)),
            _const_spec((D_MODEL, IN_COLS)),
        ],
        out_specs=tuple(pl.BlockSpec(s, lambda i, nd=len(s): (0,) * nd) for s in shapes),
        compiler_params=pltpu.CompilerParams(
            dimension_semantics=("arbitrary",), vmem_limit_bytes=VMEM_LIMIT_BYTES),
        name="sample_proj",
    )(x, mod, tab, w_in_b)


def _sample_logits(qkv_ref, ck_ref):
    nb = qkv_ref.shape[0]
    rowi = lax.broadcasted_iota(jnp.int32, (ATT_HEADS, LANES), 0)
    lanei = lax.broadcasted_iota(jnp.int32, (ATT_HEADS, LANES), 1)
    lane_group = lanei // ATT_HEAD_DIM
    qmats, logits = [], []
    for b in range(nb):
        qkv = qkv_ref[b]
        qmat = jnp.zeros((ATT_HEADS, LANES), F32)
        for h in range(ATT_HEADS):
            t, p, g = h // 2, h % 2, h // ATT_GROUP
            tile = qkv[:, t * LANES:(t + 1) * LANES]
            src = tile if p == g else pltpu.roll(tile, 64, 1)
            qmat = jnp.where((rowi == h) & (lane_group == g), jnp.broadcast_to(src, (ATT_HEADS, LANES)), qmat)
        qmats.append(qmat)
        logits.append(_dot(qmat.astype(BF16), ck_ref[b].astype(BF16)))
    return qmats, logits


def _sample_state_update(ret_ref, st_ref, ns_ref, b, h):
    dr = lax.broadcasted_iota(jnp.int32, (RET_DK, RET_DK), 0)
    dc = lax.broadcasted_iota(jnp.int32, (RET_DK, RET_DK), 1)
    ret = ret_ref[b]
    kh = ret[:, RET_QK_WIDTH + h * RET_DK:RET_QK_WIDTH + (h + 1) * RET_DK]
    vh = ret[:, 2 * RET_QK_WIDTH + h * RET_DV:2 * RET_QK_WIDTH + (h + 1) * RET_DV]
    kdiag = jnp.where(dr == dc, jnp.broadcast_to(kh, (RET_DK, RET_DK)), 0.0).astype(BF16)
    vfull = jnp.broadcast_to(vh, (RET_DK, RET_DV)).astype(BF16)
    ns_ref[b, h] = st_ref[b, h] * TOKEN_DECAY[h] + _dot(kdiag, vfull)


def _sample_outputs(sinks_ref, layer, qkv_ref, ret_ref, ck_ref, cv_ref, qmats, logits,
                    oatt_ref, oret_ref, nk_ref, nv_ref, ns_ref):
    nb = qkv_ref.shape[0]
    sink_col = jnp.zeros((ATT_HEADS, 1), F32)
    rowc = lax.broadcasted_iota(jnp.int32, (ATT_HEADS, 1), 0)
    for h in range(ATT_HEADS):
        sink_col = jnp.where(rowc == h, sinks_ref[layer, h], sink_col)
    lo64_row = _lane_iota((1, LANES)) < 64
    dr = lax.broadcasted_iota(jnp.int32, (LANES, LANES), 0)
    dc = lax.broadcasted_iota(jnp.int32, (LANES, LANES), 1)
    diag = dr == dc
    last_lane = dc == WINDOW - 1

    for b in range(nb):
        qkv = qkv_ref[b]
        k_new = qkv[:, C_AK:C_AV]
        v_new = qkv[:, C_AV:C_AG]
        s_c = logits[b]
        s_self = jnp.sum(qmats[b] * k_new, axis=-1, keepdims=True)
        m = jnp.maximum(jnp.maximum(jnp.max(s_c, axis=-1, keepdims=True), s_self), sink_col)
        p_c = jnp.exp(s_c - m)
        p_self = jnp.exp(s_self - m)
        denom = jnp.sum(p_c, axis=-1, keepdims=True) + p_self + jnp.exp(sink_col - m)
        o = (lax.dot_general(p_c.astype(BF16), cv_ref[b].astype(BF16), NT, preferred_element_type=F32)
             + p_self * v_new) / denom
        o_sw = pltpu.roll(o, 64, 1)
        for t in range(ATT_WIDTH // LANES):
            g = t // 2
            first = (o if g == 0 else o_sw)[2 * t:2 * t + 1, :]
            second = (o_sw if g == 0 else o)[2 * t + 1:2 * t + 2, :]
            oatt_ref[b, :, t * LANES:(t + 1) * LANES] = jnp.where(lo64_row, first, second)

        for new_row, src_ref, dst_ref in ((k_new, ck_ref, nk_ref), (v_new, cv_ref, nv_ref)):
            new_col = jnp.sum(jnp.where(diag, jnp.broadcast_to(new_row, (LANES, LANES)), 0.0),
                              axis=1, keepdims=True)
            shifted = pltpu.roll(src_ref[b], WINDOW - 1, 1)
            dst_ref[b] = jnp.where(last_lane, new_col, shifted)

    for b in range(nb):
        ret = ret_ref[b]
        for h in range(RET_HEADS):
            qh = ret[:, h * RET_DK:(h + 1) * RET_DK]
            q8 = jnp.broadcast_to(qh, (SUBLANES, RET_DK)).astype(BF16)
            oh = _dot(q8, ns_ref[b, h].astype(BF16))[0:1, :]
            ms = jnp.mean(oh * oh, axis=-1, keepdims=True)
            oret_ref[b, :, h * RET_DV:(h + 1) * RET_DV] = oh * lax.rsqrt(ms + RMS_EPS)


def _sample_out_kernel(x_ref, mod_ref, oatt_ref, ga_ref, oret_ref, rg_ref, mg_ref,
                       wpa_ref, wpr_ref, wout_ref, lng_ref, lnb_ref, y_ref):
    gate_c = mod_ref[:, 2 * D_MODEL:3 * D_MODEL]
    za = _dot((oatt_ref[:, 0, :] * ga_ref[...]).astype(BF16), wpa_ref[...])
    zr = _dot((oret_ref[:, 0, :] * rg_ref[...]).astype(BF16), wpr_ref[...])
    z = mg_ref[:, 0:D_MODEL] * za + mg_ref[:, D_MODEL:2 * D_MODEL] * zr
    u = _dot(z.astype(BF16), wout_ref[...])
    t = ALPHA * x_ref[:, 0, :] + gate_c * u
    mu = jnp.mean(t, axis=-1, keepdims=True)
    d = t - mu
    var = jnp.mean(d * d, axis=-1, keepdims=True)
    y_ref[:, 0, :] = d * lax.rsqrt(var + LN_EPS) * lng_ref[...] + lnb_ref[...]


def _sample_out(layer, x, mod, oatt, ga, oret, rg, mg, w_pa_b, w_pr_b, w_out_b, ln_g, ln_b):
    rows = x.shape[0]
    return pl.pallas_call(
        _sample_out_kernel,
        out_shape=jax.ShapeDtypeStruct(x.shape, F32),
        grid=(1,),
        in_specs=[
            _const_spec((rows, 1, D_MODEL)),
            _layer_spec((rows, 3 * D_MODEL), layer),
            _const_spec((rows, 1, ATT_WIDTH)), _const_spec((rows, ATT_WIDTH)),
            _const_spec((rows, 1, RET_WIDTH)), _const_spec((rows, RET_WIDTH)),
            _const_spec((rows, 2 * D_MODEL)),
            _const_spec((ATT_WIDTH, D_MODEL)),
            _const_spec((RET_WIDTH, D_MODEL)),
            _const_spec((D_MODEL, D_MODEL)),
            _layer_spec((1, D_MODEL), layer), _layer_spec((1, D_MODEL), layer),
        ],
        out_specs=pl.BlockSpec((rows, 1, D_MODEL), lambda i: (0, 0, 0)),
        compiler_params=pltpu.CompilerParams(
            dimension_semantics=("arbitrary",), vmem_limit_bytes=VMEM_LIMIT_BYTES),
        name="sample_out",
    )(x, mod, oatt, ga, oret, rg, mg, w_pa_b, w_pr_b, w_out_b, ln_g, ln_b)


def kernel(x_prompt, x_sample, c_prompt, c_sample, cache_k, cache_v, state_ret, w_in, attn_sinks,
           w_cond, b_cond, w_proj_attn, w_proj_ret, w_out, ln_g, ln_b):
    seq = x_prompt.shape[1]
    nbatch = x_sample.shape[0]
    win = cache_k.shape[2]
    assert seq % PROMPT_ROWS == 0 and nbatch % (seq // PROMPT_ROWS) == 0 and nbatch % SUBLANES == 0

    weights_f32 = (w_in, w_proj_attn, w_proj_ret, w_out)
    weights_b = tuple(w[0].astype(BF16) for w in weights_f32)
    ln_g3 = ln_g.reshape(DEPTH, 1, D_MODEL)
    ln_b3 = ln_b.reshape(DEPTH, 1, D_MODEL)

    c_all = jnp.concatenate([c_sample, c_prompt, jnp.zeros((SUBLANES - 1, D_MODEL), F32)], axis=0)
    mod = _cond_call(c_all, w_cond, b_cond)
    prompt_mod_block = nbatch // SUBLANES

    base, rtab = _prompt_rope_tables(seq, PROMPT_ROWS)
    stab = _sample_rope_table(float(PAST_LEN))
    ret_tabs = _retention_tables()

    def feature_major(c):
        return c.transpose(0, 1, 3, 4, 2).reshape(DEPTH, nbatch, ATT_KV_WIDTH, win)

    def window_major(c):
        return c.reshape(DEPTH, nbatch, ATT_KV_HEADS, ATT_HEAD_DIM, win).transpose(0, 1, 4, 2, 3)

    ck = feature_major(cache_k)
    cv = feature_major(cache_v)

    yp = x_prompt[0]
    ys = x_sample
    kp, vp, sp = [], [], []
    prev = None
    for l in range(DEPTH):
        w_in_b, w_pa_b, w_pr_b, w_out_b = weights_b
        qkv, ga, ret, rg, mg = _sample_proj(l, ys, mod, stab, w_in_b)
        outs = _fused_layer(
            l, yp, mod, prompt_mod_block, base, rtab, weights_b,
            weights_f32 if l + 1 < DEPTH else None,
            ln_g3, ln_b3, attn_sinks, ret_tabs, qkv, ret, ck, cv, state_ret, prev)
        yp, k_new, v_new, s_new, oatt, oret, nk, nv, ns = outs[:N_PROMPT_OUTPUTS]
        kp.append(k_new.reshape(1, WINDOW, ATT_KV_HEADS, ATT_HEAD_DIM))
        vp.append(v_new.reshape(1, WINDOW, ATT_KV_HEADS, ATT_HEAD_DIM))
        sp.append(s_new[None])
        prev = (nk, nv, ns)
        ys = _sample_out(l, ys, mod, oatt, ga, oret, rg, mg, w_pa_b, w_pr_b, w_out_b, ln_g3, ln_b3)
        weights_b = tuple(outs[N_PROMPT_OUTPUTS:])

    nk, nv, ns = prev
    return (yp[None], ys, jnp.stack(kp), jnp.stack(vp), jnp.stack(sp),
            window_major(nk), window_major(nv), ns)
```

```python
import functools

import jax
import jax.numpy as jnp
import numpy as np
from jax import lax
from jax.experimental import pallas as pl
from jax.experimental.pallas import tpu as pltpu

D_MODEL = 1024
DEPTH = 2
PAST_LEN = 16384
ATT_HEADS = 8
ATT_KV_HEADS = 2
ATT_HEAD_DIM = 64
ATT_GROUP = ATT_HEADS // ATT_KV_HEADS
ATT_WIDTH = ATT_HEADS * ATT_HEAD_DIM
ATT_KV_WIDTH = ATT_KV_HEADS * ATT_HEAD_DIM
WINDOW = 128
RET_HEADS = 4
RET_DK = 128
RET_DV = 256
RET_QK_WIDTH = RET_HEADS * RET_DK
RET_WIDTH = RET_HEADS * RET_DV
RET_CHUNK = 128
ROPE_THETA = 10000.0
ALPHA = (2.0 * DEPTH) ** 0.25
LN_EPS = 1e-5
RMS_EPS = 1e-6
ATT_SCALE = ATT_HEAD_DIM ** -0.5
RET_K_SCALE = RET_DK ** -0.5

C_AQ = 0
C_AK = C_AQ + ATT_WIDTH
C_AV = C_AK + ATT_KV_WIDTH
C_AG = C_AV + ATT_KV_WIDTH
C_RQ = C_AG + ATT_WIDTH
C_RK = C_RQ + RET_QK_WIDTH
C_RV = C_RK + RET_QK_WIDTH
C_RG = C_RV + RET_WIDTH
C_MA = C_RG + RET_WIDTH
C_MR = C_MA + D_MODEL
IN_COLS = C_MR + D_MODEL

LANES = 128
SUBLANES = 8
BF16_SUBLANES = 16
VMEM_LIMIT_BYTES = 56 * 1024 * 1024

PROMPT_ROWS = 256
SAMPLE_PROJ_SLAB = 256
COND_SLAB = 256

BF16 = jnp.bfloat16
F32 = jnp.float32
NT = (((1,), (1,)), ((), ()))

_LOG_GAMMA = np.log(1.0 - 2.0 ** (-5.0 - np.arange(RET_HEADS, dtype=np.float64)))
CHUNK_DECAY = tuple(float(v) for v in np.exp(RET_CHUNK * _LOG_GAMMA))
TOKEN_DECAY = tuple(float(v) for v in np.exp(_LOG_GAMMA))


def _sigmoid(x):
    return 0.5 * jnp.tanh(0.5 * x) + 0.5


def _silu(x):
    return x * _sigmoid(x)


def _dot(a, b):
    return jnp.dot(a, b, preferred_element_type=F32)


def _rope_attn_tile(x, cos, sin_signed, first_half):
    rot = jnp.where(first_half, pltpu.roll(x, LANES - 32, 1), pltpu.roll(x, 32, 1))
    return x * cos + rot * sin_signed


def _rope_ret_tile(x, cos, sin_signed):
    return x * cos + pltpu.roll(x, 64, 1) * sin_signed


def _lane_iota(shape):
    return lax.broadcasted_iota(jnp.int32, shape, len(shape) - 1)


def _layer_spec(shape, layer):
    nd = len(shape)
    return pl.BlockSpec((None,) + tuple(shape), lambda i: (layer,) + (0,) * nd,
                        pipeline_mode=pl.Buffered(1))


def _const_spec(shape):
    nd = len(shape)
    return pl.BlockSpec(tuple(shape), lambda i: (0,) * nd, pipeline_mode=pl.Buffered(1))


def _rope_lane_patterns():
    lane = np.arange(LANES)
    f_att = ROPE_THETA ** (-(lane % 32) / 32.0)
    s_att = np.where(lane % 64 < 32, -1.0, 1.0)
    f_ret = ROPE_THETA ** (-(lane % 64) / 64.0)
    s_ret = np.where(lane < 64, -1.0, 1.0)
    return (f_att, s_att), (f_ret, s_ret)


def _prompt_rope_tables(seq, tm):
    starts = np.arange(seq // tm, dtype=np.float64)[:, None] * tm
    offs = np.arange(tm, dtype=np.float64)[:, None]
    base, within = [], []
    for freq, sign in _rope_lane_patterns():
        base += [np.cos(starts * freq), np.sin(starts * freq)]
        c, s = np.cos(offs * freq), np.sin(offs * freq)
        within += [c, s, sign * c, sign * s]
    return (jnp.asarray(np.stack(base, axis=1), F32),
            jnp.asarray(np.stack(within, axis=0), F32))


def _sample_rope_table(pos):
    rows = []
    for freq, sign in _rope_lane_patterns():
        rows += [np.cos(pos * freq), sign * np.sin(pos * freq)]
    return jnp.asarray(np.stack(rows, axis=0), F32)


def _retention_tables():
    c = RET_CHUNK
    idx = np.arange(c, dtype=np.float64)
    diff = idx[:, None] - idx[None, :]
    lg = _LOG_GAMMA[:, None, None]
    dmask = np.where(diff >= 0, np.exp(np.maximum(diff, 0.0)[None] * lg), 0.0)
    qdec = np.broadcast_to(np.exp((idx + 1.0)[None, :, None] * lg), (RET_HEADS, c, RET_DK))
    kdec = np.broadcast_to(np.exp((c - 1.0 - idx)[None, :, None] * lg), (RET_HEADS, c, RET_DK))
    return jnp.asarray(dmask, F32), jnp.asarray(qdec, F32), jnp.asarray(kdec, F32)


def _cond_kernel(c_ref, w_ref, b_ref, o_ref):
    @pl.when(pl.program_id(1) == 0)
    def _():
        o_ref[...] = jnp.broadcast_to(b_ref[...], o_ref.shape)

    a = _silu(c_ref[...]).astype(BF16)
    o_ref[...] += _dot(a, w_ref[...].astype(BF16))


def _cond_call(c_all, w_cond, b_cond):
    rows = c_all.shape[0]
    tk = COND_SLAB
    return pl.pallas_call(
        _cond_kernel,
        out_shape=jax.ShapeDtypeStruct((DEPTH, rows, 3 * D_MODEL), F32),
        grid=(DEPTH, D_MODEL // tk),
        in_specs=[
            pl.BlockSpec((rows, tk), lambda l, k: (0, k)),
            pl.BlockSpec((None, tk, 3 * D_MODEL), lambda l, k: (l, k, 0)),
            pl.BlockSpec((None, 1, 3 * D_MODEL), lambda l, k: (l, 0, 0)),
        ],
        out_specs=pl.BlockSpec((None, rows, 3 * D_MODEL), lambda l, k: (l, 0, 0)),
        compiler_params=pltpu.CompilerParams(
            dimension_semantics=("arbitrary", "arbitrary"), vmem_limit_bytes=VMEM_LIMIT_BYTES),
        name="cond_mod",
    )(c_all, w_cond, b_cond.reshape(DEPTH, 1, 3 * D_MODEL))


N_PROMPT_INPUTS = 19
N_PROMPT_OUTPUTS = 9
N_WEIGHTS = 4


def _prompt_kernel(*refs, layer, n_aliased, n_cast):
    (sinks_ref, x_ref, mod_ref, base_ref, rtab_ref,
     win_ref, wpa_ref, wpr_ref, wout_ref, lng_ref, lnb_ref,
     dmask_ref, qdec_ref, kdec_ref,
     sqkv_ref, sret_ref, ck_ref, cv_ref, st_ref) = refs[:N_PROMPT_INPUTS]
    cast_in = refs[N_PROMPT_INPUTS:N_PROMPT_INPUTS + n_cast]
    refs = refs[N_PROMPT_INPUTS + n_cast + n_aliased:]
    (y_ref, knew_ref, vnew_ref, snew_ref,
     oatt_ref, oret_ref, nk_ref, nv_ref, ns_ref) = refs[:N_PROMPT_OUTPUTS]
    cast_out = refs[N_PROMPT_OUTPUTS:N_PROMPT_OUTPUTS + n_cast]
    (tab_scr, hb_scr, q_scr, kvar_scr, vvar_scr, ga_scr, a_scr,
     rq_scr, rk_scr, rv_scr, rg_scr, r_scr, s_scr,
     ma_scr, mr_scr) = refs[N_PROMPT_OUTPUTS + n_cast:]
    step = pl.program_id(0)

    def cast_next_layer_weights():
        for src, dst in zip(cast_in, cast_out):
            dst[...] = src[...].astype(BF16)

    tm = x_ref.shape[0]
    nsub = tm // WINDOW

    def proj(c0, c1):
        return _dot(hb_scr[...], win_ref[:, c0:c1])

    lane = _lane_iota((tm, LANES))
    first_half32 = (lane & 32) == 0
    lo64 = lane < 64

    def rope_attn(t):
        return _rope_attn_tile(t, tab_scr[0], tab_scr[1], first_half32)

    def rope_ret(t):
        return _rope_ret_tile(t, tab_scr[2], tab_scr[3])

    def store_variants(scr, t, fill):
        swapped = pltpu.roll(t, 64, 1)
        other = jnp.full_like(t, fill)
        scr[0, WINDOW:WINDOW + tm, :] = jnp.where(lo64, t, other).astype(BF16)
        scr[1, WINDOW:WINDOW + tm, :] = jnp.where(lo64, other, swapped).astype(BF16)
        scr[2, WINDOW:WINDOW + tm, :] = jnp.where(lo64, swapped, other).astype(BF16)
        scr[3, WINDOW:WINDOW + tm, :] = jnp.where(lo64, other, t).astype(BF16)

    def block_prepare(src_x_ref, src_base_ref):
        shift = mod_ref[0:1, 0:D_MODEL]
        scale = mod_ref[0:1, D_MODEL:2 * D_MODEL]
        hb_scr[...] = (src_x_ref[...] * (1.0 + scale) + shift).astype(BF16)
        for fam in range(2):
            cb = src_base_ref[2 * fam:2 * fam + 1, :]
            sb = src_base_ref[2 * fam + 1:2 * fam + 2, :]
            tab_scr[2 * fam] = cb * rtab_ref[4 * fam] - sb * rtab_ref[4 * fam + 1]
            tab_scr[2 * fam + 1] = sb * rtab_ref[4 * fam + 2] + cb * rtab_ref[4 * fam + 3]

    def block_head():
        kv = proj(C_AK, C_AG)
        k_rot = rope_attn(kv[:, 0:LANES])
        v_raw = kv[:, LANES:2 * LANES]
        knew_ref[...] = k_rot[tm - WINDOW:, :]
        vnew_ref[...] = v_raw[tm - WINDOW:, :]
        store_variants(kvar_scr, k_rot, 0.0)
        store_variants(vvar_scr, v_raw, 1.0)
        qp = proj(C_AQ, C_AK)
        for t in range(ATT_WIDTH // LANES):
            qt = rope_attn(qp[:, t * LANES:(t + 1) * LANES])
            q_scr[:, t * LANES:(t + 1) * LANES] = (qt * ATT_SCALE).astype(BF16)
        ga_scr[...] = proj(C_AG, C_RQ)

    @pl.when(step == 0)
    def _():
        kvar_scr[:, 0:WINDOW, :] = jnp.zeros((4, WINDOW, LANES), BF16)
        vvar_scr[:, 0:WINDOW, :] = jnp.zeros((4, WINDOW, LANES), BF16)
        s_scr[...] = jnp.zeros(s_scr.shape, F32)

    block_prepare(x_ref, base_ref)
    block_head()

    def chunk_rq():
        p = proj(C_RQ, C_RK)
        for h in range(RET_HEADS):
            sl = slice(h * RET_DK, (h + 1) * RET_DK)
            rq_scr[:, sl] = rope_ret(p[:, sl])

    def chunk_rk():
        p = proj(C_RK, C_RV)
        for h in range(RET_HEADS):
            sl = slice(h * RET_DK, (h + 1) * RET_DK)
            rk_scr[:, sl] = rope_ret(p[:, sl]) * RET_K_SCALE

    def half_chunk(c0, dst, fn, half):
        w = dst.shape[1] // 2
        def run():
            dst[:, half * w:(half + 1) * w] = fn(proj(c0 + half * w, c0 + (half + 1) * w))
        return run

    to_bf16 = lambda v: v.astype(BF16)
    raw = lambda v: v
    att_companions = [
        chunk_rq, chunk_rk,
        half_chunk(C_RV, rv_scr, to_bf16, 0), half_chunk(C_RV, rv_scr, to_bf16, 1),
        half_chunk(C_RG, rg_scr, raw, 0), half_chunk(C_RG, rg_scr, raw, 1),
        half_chunk(C_MA, ma_scr, raw, 0), half_chunk(C_MA, ma_scr, raw, 1),
    ]

    row = lax.broadcasted_iota(jnp.int32, (WINDOW, 2 * WINDOW), 0)
    col = lax.broadcasted_iota(jnp.int32, (WINDOW, 2 * WINDOW), 1)
    in_window = col <= row + WINDOW
    mask_std = (col >= row) & in_window
    off = jnp.where(step > 0, 0, 4 * WINDOW)
    mask_first = ((col >= row + off) | (col >= WINDOW)) & in_window
    lo64_w = _lane_iota((WINDOW, LANES)) < 64

    def attention_logits(i, t):
        r0 = i * WINDOW
        g = t // 2
        qt = q_scr[r0:r0 + WINDOW, t * LANES:(t + 1) * LANES]
        return [lax.dot_general(qt, kvar_scr[2 * g + p, r0:r0 + 2 * WINDOW, :], NT,
                                preferred_element_type=F32) for p in range(2)]

    def attention_values(i, t, logits):
        r0 = i * WINDOW
        mask = mask_first if i == 0 else mask_std
        g = t // 2
        outs, sink_terms = [], []
        for p in range(2):
            head = 2 * t + p
            s = jnp.where(mask, logits[p], -jnp.inf)
            sink = sinks_ref[layer, head]
            m = jnp.maximum(jnp.max(s, axis=-1, keepdims=True), sink)
            pe = jnp.exp(s - m).astype(BF16)
            sink_terms.append(jnp.exp(sink - m))
            outs.append(_dot(pe, vvar_scr[2 * g + p, r0:r0 + 2 * WINDOW, :]))
        weighted = jnp.where(lo64_w, outs[0], outs[1])
        row_sums = pltpu.roll(jnp.where(lo64_w, outs[1], outs[0]), 64, 1)
        denom = row_sums + jnp.where(lo64_w, sink_terms[0], sink_terms[1])
        a_scr[r0:r0 + WINDOW, t * LANES:(t + 1) * LANES] = weighted * (1.0 / denom)

    n = 0
    for i in range(nsub):
        for t in range(ATT_WIDTH // LANES):
            logits = attention_logits(i, t)
            if n < len(att_companions):
                att_companions[n]()
            if n == 2:
                cast_next_layer_weights()
            n += 1
            attention_values(i, t, logits)
    for run in att_companions[n:]:
        run()

    kvar_scr[:, 0:WINDOW, :] = kvar_scr[:, tm:tm + WINDOW, :]
    vvar_scr[:, 0:WINDOW, :] = vvar_scr[:, tm:tm + WINDOW, :]

    def ret_slices(c, h):
        rows = slice(c * RET_CHUNK, (c + 1) * RET_CHUNK)
        return rows, slice(h * RET_DK, (h + 1) * RET_DK), slice(h * RET_DV, (h + 1) * RET_DV)

    def retention_scores_and_state(c):
        inners, s_olds = [], []
        for h in range(RET_HEADS):
            rows, sk, sv = ret_slices(c, h)
            qh = rq_scr[rows, sk]
            kh = rk_scr[rows, sk]
            vh = rv_scr[rows, sv]
            inners.append(lax.dot_general(qh.astype(BF16), kh.astype(BF16), NT, preferred_element_type=F32))
            s_old = s_scr[h]
            s_olds.append(s_old.astype(BF16))
            kd = (kh * kdec_ref[h]).astype(BF16)
            s_scr[h] = s_old * CHUNK_DECAY[h] + lax.dot_general(
                kd, vh, (((0,), (0,)), ((), ())), preferred_element_type=F32)
        return inners, s_olds

    def retention_outputs(c, inners, s_olds):
        for h in range(RET_HEADS):
            rows, sk, sv = ret_slices(c, h)
            qh = rq_scr[rows, sk]
            lhs = jnp.concatenate([(inners[h] * dmask_ref[h]).astype(BF16),
                                   (qh * qdec_ref[h]).astype(BF16)], axis=1)
            rhs = jnp.concatenate([rv_scr[rows, sv], s_olds[h]], axis=0)
            o = _dot(lhs, rhs)
            ms = jnp.mean(o * o, axis=-1, keepdims=True)
            on = o * lax.rsqrt(ms + RMS_EPS)
            r_scr[rows, sv] = on

    ret_companions = [half_chunk(C_MR, mr_scr, raw, 0), half_chunk(C_MR, mr_scr, raw, 1)]
    n = 0
    for c in range(nsub):
        inners, s_olds = retention_scores_and_state(c)
        if n < len(ret_companions):
            ret_companions[n]()
        n += 1
        retention_outputs(c, inners, s_olds)
    for run in ret_companions[n:]:
        run()
    sample_qmats, sample_logits = _sample_logits(sqkv_ref, ck_ref)
    for b in range(sqkv_ref.shape[0]):
        for h in range(RET_HEADS):
            _sample_state_update(sret_ref, st_ref, ns_ref, b, h)

    gate_c = mod_ref[0:1, 2 * D_MODEL:3 * D_MODEL]
    lng = lng_ref[...]
    lnb = lnb_ref[...]
    windows = [slice(i * WINDOW, (i + 1) * WINDOW) for i in range(nsub)]
    za_all = _dot((a_scr[...] * _silu(ga_scr[...])).astype(BF16), wpa_ref[...])
    zr_all = _dot((r_scr[...] * _silu(rg_scr[...])).astype(BF16), wpr_ref[...])
    for rows in windows:
        z = _sigmoid(ma_scr[rows, :]) * za_all[rows, :] + _sigmoid(mr_scr[rows, :]) * zr_all[rows, :]
        u = _dot(z.astype(BF16), wout_ref[...])
        t = ALPHA * x_ref[rows, :] + gate_c * u
        mu = jnp.mean(t, axis=-1, keepdims=True)
        d = t - mu
        var = jnp.mean(d * d, axis=-1, keepdims=True)
        y_ref[rows, :] = d * lax.rsqrt(var + LN_EPS) * lng + lnb

    _sample_outputs(sinks_ref, layer, sqkv_ref, sret_ref, ck_ref, cv_ref, sample_qmats, sample_logits,
                    oatt_ref, oret_ref, nk_ref, nv_ref, ns_ref)

    @pl.when(step == pl.num_programs(0) - 1)
    def _():
        snew_ref[...] = s_scr[...]


def _fused_layer(layer, x, mod, mod_row_block, base, rtab, weights_b, next_weights,
                 ln_g, ln_b, sinks, ret_tabs, sample_qkv, sample_ret, cache_k, cache_v, state, shared_out):
    w_in_b, w_pa_b, w_pr_b, w_out_b = weights_b
    seq = x.shape[0]
    tm = PROMPT_ROWS
    steps = seq // tm
    nbatch = sample_qkv.shape[0]
    nb = nbatch // steps
    win = cache_k.shape[3]
    assert win == WINDOW == LANES
    dmask, qdec, kdec = ret_tabs
    smem = pl.BlockSpec(memory_space=pltpu.SMEM)
    row_spec = lambda w: pl.BlockSpec((tm, w), lambda i: (i, 0))
    seq_spec = lambda w: pl.BlockSpec((nb, 1, w), lambda i: (i, 0, 0))
    cache_spec = pl.BlockSpec((None, nb, ATT_KV_WIDTH, win), lambda i: (layer, i, 0, 0))
    state_spec = pl.BlockSpec((None, nb, RET_HEADS, RET_DK, RET_DV), lambda i: (layer, i, 0, 0, 0))
    in_specs = [
        smem,
        row_spec(D_MODEL),
        pl.BlockSpec((None, SUBLANES, 3 * D_MODEL), lambda i: (layer, mod_row_block, 0),
                     pipeline_mode=pl.Buffered(1)),
        pl.BlockSpec((None, 4, LANES), lambda i: (i, 0, 0)),
        _const_spec((8, tm, LANES)),
        _const_spec((D_MODEL, IN_COLS)),
        _const_spec((ATT_WIDTH, D_MODEL)),
        _const_spec((RET_WIDTH, D_MODEL)),
        _const_spec((D_MODEL, D_MODEL)),
        _layer_spec((1, D_MODEL), layer), _layer_spec((1, D_MODEL), layer),
        _const_spec((RET_HEADS, RET_CHUNK, RET_CHUNK)),
        _const_spec((RET_HEADS, RET_CHUNK, RET_DK)),
        _const_spec((RET_HEADS, RET_CHUNK, RET_DK)),
        seq_spec(C_AG), seq_spec(2 * RET_QK_WIDTH + RET_WIDTH),
        cache_spec, cache_spec, state_spec,
    ]
    args = [sinks, x, mod, base, rtab, w_in_b, w_pa_b, w_pr_b, w_out_b, ln_g, ln_b, dmask, qdec, kdec,
            sample_qkv, sample_ret, cache_k, cache_v, state]
    assert len(args) == N_PROMPT_INPUTS
    cast_shapes, cast_specs = [], []
    if next_weights is not None:
        assert len(next_weights) == N_WEIGHTS
        for w in next_weights:
            _, rows, cols = w.shape
            slab = max(BF16_SUBLANES, rows // steps)
            nslab = rows // slab
            assert rows % slab == 0 and nslab <= steps
            args.append(w)
            in_specs.append(pl.BlockSpec((None, slab, cols),
                                         lambda i, n=nslab: (layer + 1, jnp.minimum(i, n - 1), 0)))
            cast_shapes.append(jax.ShapeDtypeStruct((rows, cols), BF16))
            cast_specs.append(pl.BlockSpec((slab, cols), lambda i, n=nslab: (jnp.minimum(i, n - 1), 0)))
    aliases = {}
    if shared_out is not None:
        for j, arr in enumerate(shared_out):
            aliases[len(args)] = N_PROMPT_OUTPUTS - len(shared_out) + j
            args.append(arr)
            in_specs.append(pl.BlockSpec(memory_space=pl.ANY))
    return pl.pallas_call(
        functools.partial(_prompt_kernel, layer=layer, n_aliased=len(aliases), n_cast=len(cast_shapes)),
        out_shape=(
            jax.ShapeDtypeStruct((seq, D_MODEL), F32),
            jax.ShapeDtypeStruct((WINDOW, ATT_KV_WIDTH), F32),
            jax.ShapeDtypeStruct((WINDOW, ATT_KV_WIDTH), F32),
            jax.ShapeDtypeStruct((RET_HEADS, RET_DK, RET_DV), F32),
            jax.ShapeDtypeStruct((nbatch, 1, ATT_WIDTH), F32),
            jax.ShapeDtypeStruct((nbatch, 1, RET_WIDTH), F32),
            jax.ShapeDtypeStruct(cache_k.shape, F32),
            jax.ShapeDtypeStruct(cache_v.shape, F32),
            jax.ShapeDtypeStruct(state.shape, F32),
        ) + tuple(cast_shapes),
        grid=(steps,),
        in_specs=in_specs,
        out_specs=(
            row_spec(D_MODEL),
            pl.BlockSpec((WINDOW, ATT_KV_WIDTH), lambda i: (0, 0)),
            pl.BlockSpec((WINDOW, ATT_KV_WIDTH), lambda i: (0, 0)),
            pl.BlockSpec((RET_HEADS, RET_DK, RET_DV), lambda i: (0, 0, 0)),
            seq_spec(ATT_WIDTH), seq_spec(RET_WIDTH),
            cache_spec, cache_spec, state_spec,
        ) + tuple(cast_specs),
        input_output_aliases=aliases,
        scratch_shapes=[
            pltpu.VMEM((4, tm, LANES), F32),
            pltpu.VMEM((tm, D_MODEL), BF16),
            pltpu.VMEM((tm, ATT_WIDTH), BF16),
            pltpu.VMEM((4, WINDOW + tm, LANES), BF16),
            pltpu.VMEM((4, WINDOW + tm, LANES), BF16),
            pltpu.VMEM((tm, ATT_WIDTH), F32),
            pltpu.VMEM((tm, ATT_WIDTH), F32),
            pltpu.VMEM((tm, RET_QK_WIDTH), F32),
            pltpu.VMEM((tm, RET_QK_WIDTH), F32),
            pltpu.VMEM((tm, RET_WIDTH), BF16),
            pltpu.VMEM((tm, RET_WIDTH), F32),
            pltpu.VMEM((tm, RET_WIDTH), F32),
            pltpu.VMEM((RET_HEADS, RET_DK, RET_DV), F32),
            pltpu.VMEM((tm, D_MODEL), F32),
            pltpu.VMEM((tm, D_MODEL), F32),
        ],
        compiler_params=pltpu.CompilerParams(
            dimension_semantics=("arbitrary",), vmem_limit_bytes=VMEM_LIMIT_BYTES),
        name="fused_layer",
    )(*args)


def _sample_proj_kernel(x_ref, shift_ref, scale_ref, tab_ref, win_ref,
                        qkv_ref, ga_ref, ret_ref, rg_ref, mg_ref, acc_scr):
    k = pl.program_id(0)
    rows = x_ref.shape[0]
    hb = (x_ref[:, 0, :] * (1.0 + scale_ref[...]) + shift_ref[...]).astype(BF16)

    @pl.when(k == 0)
    def _():
        acc_scr[...] = jnp.zeros(acc_scr.shape, F32)

    acc_scr[...] += _dot(hb, win_ref[...])

    @pl.when(k == pl.num_programs(0) - 1)
    def _():
        _sample_proj_finish(rows, tab_ref, acc_scr, qkv_ref, ga_ref, ret_ref, rg_ref, mg_ref)


def _sample_proj_finish(rows, tab_ref, acc_scr, qkv_ref, ga_ref, ret_ref, rg_ref, mg_ref):
    def proj(c0, c1):
        return acc_scr[:, c0:c1]

    lane = _lane_iota((rows, LANES))
    first_half32 = (lane & 32) == 0
    ca = tab_ref[0:1, :]
    sa = tab_ref[1:2, :]
    cr = tab_ref[2:3, :]
    sr = tab_ref[3:4, :]

    qkv = proj(C_AQ, C_AG)
    for t in range(ATT_WIDTH // LANES):
        qt = _rope_attn_tile(qkv[:, t * LANES:(t + 1) * LANES], ca, sa, first_half32)
        qkv_ref[:, 0, t * LANES:(t + 1) * LANES] = qt * ATT_SCALE
    qkv_ref[:, 0, C_AK:C_AV] = _rope_attn_tile(qkv[:, C_AK:C_AV], ca, sa, first_half32)
    qkv_ref[:, 0, C_AV:C_AG] = qkv[:, C_AV:C_AG]
    ga_ref[...] = _silu(proj(C_AG, C_RQ))

    rqk = proj(C_RQ, C_RV)
    for h in range(RET_HEADS):
        sl = slice(h * RET_DK, (h + 1) * RET_DK)
        ret_ref[:, 0, sl] = _rope_ret_tile(rqk[:, sl], cr, sr)
        sk = slice(RET_QK_WIDTH + h * RET_DK, RET_QK_WIDTH + (h + 1) * RET_DK)
        ret_ref[:, 0, sk] = _rope_ret_tile(rqk[:, sk], cr, sr) * RET_K_SCALE
    ret_ref[:, 0, 2 * RET_QK_WIDTH:] = proj(C_RV, C_RG)
    rg_ref[...] = _silu(proj(C_RG, C_MA))
    mg_ref[...] = _sigmoid(proj(C_MA, IN_COLS))


def _sample_proj(layer, x, mod, tab, w_in_b):
    rows = x.shape[0]
    shapes = ((rows, 1, C_AG), (rows, ATT_WIDTH), (rows, 1, 2 * RET_QK_WIDTH + RET_WIDTH),
              (rows, RET_WIDTH), (rows, 2 * D_MODEL))
    slab = SAMPLE_PROJ_SLAB
    return pl.pallas_call(
        _sample_proj_kernel,
        out_shape=tuple(jax.ShapeDtypeStruct(s, F32) for s in shapes),
        grid=(D_MODEL // slab,),
        in_specs=[
            pl.BlockSpec((rows, 1, slab), lambda k: (0, 0, k)),
            pl.BlockSpec((None, rows, slab), lambda k: (layer, 0, k)),
            pl.BlockSpec((None, rows, slab), lambda k: (layer, 0, D_MODEL // slab + k)),
            _const_spec((4, LANES)),
            pl.BlockSpec((slab, IN_COLS), lambda k: (k, 0)),
        ],
        out_specs=tuple(pl.BlockSpec(s, lambda k, nd=len(s): (0,) * nd) for s in shapes),
        scratch_shapes=[pltpu.VMEM((rows, IN_COLS), F32)],
        compiler_params=pltpu.CompilerParams(
            dimension_semantics=("arbitrary",), vmem_limit_bytes=VMEM_LIMIT_BYTES),
        name="sample_proj",
    )(x, mod, mod, tab, w_in_b)


def _sample_logits(qkv_ref, ck_ref):
    nb = qkv_ref.shape[0]
    rowi = lax.broadcasted_iota(jnp.int32, (ATT_HEADS, LANES), 0)
    lanei = lax.broadcasted_iota(jnp.int32, (ATT_HEADS, LANES), 1)
    lane_group = lanei // ATT_HEAD_DIM
    qmats, logits = [], []
    for b in range(nb):
        qkv = qkv_ref[b]
        qmat = jnp.zeros((ATT_HEADS, LANES), F32)
        for h in range(ATT_HEADS):
            t, p, g = h // 2, h % 2, h // ATT_GROUP
            tile = qkv[:, t * LANES:(t + 1) * LANES]
            src = tile if p == g else pltpu.roll(tile, 64, 1)
            qmat = jnp.where((rowi == h) & (lane_group == g), jnp.broadcast_to(src, (ATT_HEADS, LANES)), qmat)
        qmats.append(qmat)
        logits.append(_dot(qmat.astype(BF16), ck_ref[b].astype(BF16)))
    return qmats, logits


def _sample_state_update(ret_ref, st_ref, ns_ref, b, h):
    dr = lax.broadcasted_iota(jnp.int32, (RET_DK, RET_DK), 0)
    dc = lax.broadcasted_iota(jnp.int32, (RET_DK, RET_DK), 1)
    ret = ret_ref[b]
    kh = ret[:, RET_QK_WIDTH + h * RET_DK:RET_QK_WIDTH + (h + 1) * RET_DK]
    vh = ret[:, 2 * RET_QK_WIDTH + h * RET_DV:2 * RET_QK_WIDTH + (h + 1) * RET_DV]
    kdiag = jnp.where(dr == dc, jnp.broadcast_to(kh, (RET_DK, RET_DK)), 0.0).astype(BF16)
    vfull = jnp.broadcast_to(vh, (RET_DK, RET_DV)).astype(BF16)
    ns_ref[b, h] = st_ref[b, h] * TOKEN_DECAY[h] + _dot(kdiag, vfull)


def _sample_outputs(sinks_ref, layer, qkv_ref, ret_ref, ck_ref, cv_ref, qmats, logits,
                    oatt_ref, oret_ref, nk_ref, nv_ref, ns_ref):
    nb = qkv_ref.shape[0]
    sink_col = jnp.zeros((ATT_HEADS, 1), F32)
    rowc = lax.broadcasted_iota(jnp.int32, (ATT_HEADS, 1), 0)
    for h in range(ATT_HEADS):
        sink_col = jnp.where(rowc == h, sinks_ref[layer, h], sink_col)
    lo64_row = _lane_iota((1, LANES)) < 64
    dr = lax.broadcasted_iota(jnp.int32, (LANES, LANES), 0)
    dc = lax.broadcasted_iota(jnp.int32, (LANES, LANES), 1)
    diag = dr == dc
    last_lane = dc == WINDOW - 1

    for b in range(nb):
        qkv = qkv_ref[b]
        k_new = qkv[:, C_AK:C_AV]
        v_new = qkv[:, C_AV:C_AG]
        s_c = logits[b]
        s_self = jnp.sum(qmats[b] * k_new, axis=-1, keepdims=True)
        m = jnp.maximum(jnp.maximum(jnp.max(s_c, axis=-1, keepdims=True), s_self), sink_col)
        p_c = jnp.exp(s_c - m)
        p_self = jnp.exp(s_self - m)
        denom = jnp.sum(p_c, axis=-1, keepdims=True) + p_self + jnp.exp(sink_col - m)
        o = (lax.dot_general(p_c.astype(BF16), cv_ref[b].astype(BF16), NT, preferred_element_type=F32)
             + p_self * v_new) / denom
        o_sw = pltpu.roll(o, 64, 1)
        for t in range(ATT_WIDTH // LANES):
            g = t // 2
            first = (o if g == 0 else o_sw)[2 * t:2 * t + 1, :]
            second = (o_sw if g == 0 else o)[2 * t + 1:2 * t + 2, :]
            oatt_ref[b, :, t * LANES:(t + 1) * LANES] = jnp.where(lo64_row, first, second)

        for new_row, src_ref, dst_ref in ((k_new, ck_ref, nk_ref), (v_new, cv_ref, nv_ref)):
            new_col = jnp.sum(jnp.where(diag, jnp.broadcast_to(new_row, (LANES, LANES)), 0.0),
                              axis=1, keepdims=True)
            shifted = pltpu.roll(src_ref[b], WINDOW - 1, 1)
            dst_ref[b] = jnp.where(last_lane, new_col, shifted)

    for b in range(nb):
        ret = ret_ref[b]
        for h in range(RET_HEADS):
            qh = ret[:, h * RET_DK:(h + 1) * RET_DK]
            q8 = jnp.broadcast_to(qh, (SUBLANES, RET_DK)).astype(BF16)
            oh = _dot(q8, ns_ref[b, h].astype(BF16))[0:1, :]
            ms = jnp.mean(oh * oh, axis=-1, keepdims=True)
            oret_ref[b, :, h * RET_DV:(h + 1) * RET_DV] = oh * lax.rsqrt(ms + RMS_EPS)


def _sample_out_kernel(x_ref, mod_ref, oatt_ref, ga_ref, oret_ref, rg_ref, mg_ref,
                       wpa_ref, wpr_ref, wout_ref, lng_ref, lnb_ref, y_ref):
    gate_c = mod_ref[:, 2 * D_MODEL:3 * D_MODEL]
    za = _dot((oatt_ref[:, 0, :] * ga_ref[...]).astype(BF16), wpa_ref[...])
    zr = _dot((oret_ref[:, 0, :] * rg_ref[...]).astype(BF16), wpr_ref[...])
    z = mg_ref[:, 0:D_MODEL] * za + mg_ref[:, D_MODEL:2 * D_MODEL] * zr
    u = _dot(z.astype(BF16), wout_ref[...])
    t = ALPHA * x_ref[:, 0, :] + gate_c * u
    mu = jnp.mean(t, axis=-1, keepdims=True)
    d = t - mu
    var = jnp.mean(d * d, axis=-1, keepdims=True)
    y_ref[:, 0, :] = d * lax.rsqrt(var + LN_EPS) * lng_ref[...] + lnb_ref[...]


def _sample_out(layer, x, mod, oatt, ga, oret, rg, mg, w_pa_b, w_pr_b, w_out_b, ln_g, ln_b):
    rows = x.shape[0]
    return pl.pallas_call(
        _sample_out_kernel,
        out_shape=jax.ShapeDtypeStruct(x.shape, F32),
        grid=(1,),
        in_specs=[
            _const_spec((rows, 1, D_MODEL)),
            _layer_spec((rows, 3 * D_MODEL), layer),
            _const_spec((rows, 1, ATT_WIDTH)), _const_spec((rows, ATT_WIDTH)),
            _const_spec((rows, 1, RET_WIDTH)), _const_spec((rows, RET_WIDTH)),
            _const_spec((rows, 2 * D_MODEL)),
            _const_spec((ATT_WIDTH, D_MODEL)),
            _const_spec((RET_WIDTH, D_MODEL)),
            _const_spec((D_MODEL, D_MODEL)),
            _layer_spec((1, D_MODEL), layer), _layer_spec((1, D_MODEL), layer),
        ],
        out_specs=pl.BlockSpec((rows, 1, D_MODEL), lambda i: (0, 0, 0)),
        compiler_params=pltpu.CompilerParams(
            dimension_semantics=("arbitrary",), vmem_limit_bytes=VMEM_LIMIT_BYTES),
        name="sample_out",
    )(x, mod, oatt, ga, oret, rg, mg, w_pa_b, w_pr_b, w_out_b, ln_g, ln_b)


def kernel(x_prompt, x_sample, c_prompt, c_sample, cache_k, cache_v, state_ret, w_in, attn_sinks,
           w_cond, b_cond, w_proj_attn, w_proj_ret, w_out, ln_g, ln_b):
    seq = x_prompt.shape[1]
    nbatch = x_sample.shape[0]
    win = cache_k.shape[2]
    assert seq % PROMPT_ROWS == 0 and nbatch % (seq // PROMPT_ROWS) == 0 and nbatch % SUBLANES == 0

    weights_f32 = (w_in, w_proj_attn, w_proj_ret, w_out)
    weights_b = tuple(w[0].astype(BF16) for w in weights_f32)
    ln_g3 = ln_g.reshape(DEPTH, 1, D_MODEL)
    ln_b3 = ln_b.reshape(DEPTH, 1, D_MODEL)

    c_all = jnp.concatenate([c_sample, c_prompt, jnp.zeros((SUBLANES - 1, D_MODEL), F32)], axis=0)
    mod = _cond_call(c_all, w_cond, b_cond)
    prompt_mod_block = nbatch // SUBLANES

    base, rtab = _prompt_rope_tables(seq, PROMPT_ROWS)
    stab = _sample_rope_table(float(PAST_LEN))
    ret_tabs = _retention_tables()

    def feature_major(c):
        return c.transpose(0, 1, 3, 4, 2).reshape(DEPTH, nbatch, ATT_KV_WIDTH, win)

    def window_major(c):
        return c.reshape(DEPTH, nbatch, ATT_KV_HEADS, ATT_HEAD_DIM, win).transpose(0, 1, 4, 2, 3)

    ck = feature_major(cache_k)
    cv = feature_major(cache_v)

    yp = x_prompt[0]
    ys = x_sample
    kp, vp, sp = [], [], []
    prev = None
    for l in range(DEPTH):
        w_in_b, w_pa_b, w_pr_b, w_out_b = weights_b
        qkv, ga, ret, rg, mg = _sample_proj(l, ys, mod, stab, w_in_b)
        outs = _fused_layer(
            l, yp, mod, prompt_mod_block, base, rtab, weights_b,
            weights_f32 if l + 1 < DEPTH else None,
            ln_g3, ln_b3, attn_sinks, ret_tabs, qkv, ret, ck, cv, state_ret, prev)
        yp, k_new, v_new, s_new, oatt, oret, nk, nv, ns = outs[:N_PROMPT_OUTPUTS]
        kp.append(k_new.reshape(1, WINDOW, ATT_KV_HEADS, ATT_HEAD_DIM))
        vp.append(v_new.reshape(1, WINDOW, ATT_KV_HEADS, ATT_HEAD_DIM))
        sp.append(s_new[None])
        prev = (nk, nv, ns)
        ys = _sample_out(l, ys, mod, oatt, ga, oret, rg, mg, w_pa_b, w_pr_b, w_out_b, ln_g3, ln_b3)
        weights_b = tuple(outs[N_PROMPT_OUTPUTS:])

    nk, nv, ns = prev
    return (yp[None], ys, jnp.stack(kp), jnp.stack(vp), jnp.stack(sp),
            window_major(nk), window_major(nv), ns)
```

```python
import functools

import jax
import jax.numpy as jnp
import numpy as np
from jax import lax
from jax.experimental import pallas as pl
from jax.experimental.pallas import tpu as pltpu

D_MODEL = 1024
DEPTH = 2
PAST_LEN = 16384
ATT_HEADS = 8
ATT_KV_HEADS = 2
ATT_HEAD_DIM = 64
ATT_GROUP = ATT_HEADS // ATT_KV_HEADS
ATT_WIDTH = ATT_HEADS * ATT_HEAD_DIM
ATT_KV_WIDTH = ATT_KV_HEADS * ATT_HEAD_DIM
WINDOW = 128
RET_HEADS = 4
RET_DK = 128
RET_DV = 256
RET_QK_WIDTH = RET_HEADS * RET_DK
RET_WIDTH = RET_HEADS * RET_DV
RET_CHUNK = 128
ROPE_THETA = 10000.0
ALPHA = (2.0 * DEPTH) ** 0.25
LN_EPS = 1e-5
RMS_EPS = 1e-6
ATT_SCALE = ATT_HEAD_DIM ** -0.5
RET_K_SCALE = RET_DK ** -0.5

C_AQ = 0
C_AK = C_AQ + ATT_WIDTH
C_AV = C_AK + ATT_KV_WIDTH
C_AG = C_AV + ATT_KV_WIDTH
C_RQ = C_AG + ATT_WIDTH
C_RK = C_RQ + RET_QK_WIDTH
C_RV = C_RK + RET_QK_WIDTH
C_RG = C_RV + RET_WIDTH
C_MA = C_RG + RET_WIDTH
C_MR = C_MA + D_MODEL
IN_COLS = C_MR + D_MODEL

LANES = 128
SUBLANES = 8
BF16_SUBLANES = 16
VMEM_LIMIT_BYTES = 56 * 1024 * 1024

PROMPT_ROWS = 256

BF16 = jnp.bfloat16
F32 = jnp.float32
NT = (((1,), (1,)), ((), ()))

_LOG_GAMMA = np.log(1.0 - 2.0 ** (-5.0 - np.arange(RET_HEADS, dtype=np.float64)))
CHUNK_DECAY = tuple(float(v) for v in np.exp(RET_CHUNK * _LOG_GAMMA))
TOKEN_DECAY = tuple(float(v) for v in np.exp(_LOG_GAMMA))


def _sigmoid(x):
    return 0.5 * jnp.tanh(0.5 * x) + 0.5


def _silu(x):
    return x * _sigmoid(x)


def _dot(a, b):
    return jnp.dot(a, b, preferred_element_type=F32)


def _rope_attn_tile(x, cos, sin_signed, first_half):
    rot = jnp.where(first_half, pltpu.roll(x, LANES - 32, 1), pltpu.roll(x, 32, 1))
    return x * cos + rot * sin_signed


def _rope_ret_tile(x, cos, sin_signed):
    return x * cos + pltpu.roll(x, 64, 1) * sin_signed


def _lane_iota(shape):
    return lax.broadcasted_iota(jnp.int32, shape, len(shape) - 1)


def _layer_spec(shape, layer):
    nd = len(shape)
    return pl.BlockSpec((None,) + tuple(shape), lambda i: (layer,) + (0,) * nd,
                        pipeline_mode=pl.Buffered(1))


def _const_spec(shape):
    nd = len(shape)
    return pl.BlockSpec(tuple(shape), lambda i: (0,) * nd, pipeline_mode=pl.Buffered(1))


def _rope_lane_patterns():
    lane = np.arange(LANES)
    f_att = ROPE_THETA ** (-(lane % 32) / 32.0)
    s_att = np.where(lane % 64 < 32, -1.0, 1.0)
    f_ret = ROPE_THETA ** (-(lane % 64) / 64.0)
    s_ret = np.where(lane < 64, -1.0, 1.0)
    return (f_att, s_att), (f_ret, s_ret)


def _prompt_rope_tables(seq, tm):
    starts = np.arange(seq // tm, dtype=np.float64)[:, None] * tm
    offs = np.arange(tm, dtype=np.float64)[:, None]
    base, within = [], []
    for freq, sign in _rope_lane_patterns():
        base += [np.cos(starts * freq), np.sin(starts * freq)]
        c, s = np.cos(offs * freq), np.sin(offs * freq)
        within += [c, s, sign * c, sign * s]
    return (jnp.asarray(np.stack(base, axis=1), F32),
            jnp.asarray(np.stack(within, axis=0), F32))


def _sample_rope_table(pos):
    rows = []
    for freq, sign in _rope_lane_patterns():
        rows += [np.cos(pos * freq), sign * np.sin(pos * freq)]
    return jnp.asarray(np.stack(rows, axis=0), F32)


def _retention_tables():
    c = RET_CHUNK
    idx = np.arange(c, dtype=np.float64)
    diff = idx[:, None] - idx[None, :]
    lg = _LOG_GAMMA[:, None, None]
    dmask = np.where(diff >= 0, np.exp(np.maximum(diff, 0.0)[None] * lg), 0.0)
    qdec = np.broadcast_to(np.exp((idx + 1.0)[None, :, None] * lg), (RET_HEADS, c, RET_DK))
    kdec = np.broadcast_to(np.exp((c - 1.0 - idx)[None, :, None] * lg), (RET_HEADS, c, RET_DK))
    return jnp.asarray(dmask, F32), jnp.asarray(qdec, F32), jnp.asarray(kdec, F32)


def _cond_kernel(c_ref, w_ref, b_ref, o_ref):
    a = _silu(c_ref[...]).astype(BF16)
    o_ref[...] = _dot(a, w_ref[...].astype(BF16)) + b_ref[...]


def _cond_call(c_all, w_cond, b_cond):
    rows = c_all.shape[0]
    tn = 3 * D_MODEL // 2
    return pl.pallas_call(
        _cond_kernel,
        out_shape=jax.ShapeDtypeStruct((DEPTH, rows, 3 * D_MODEL), F32),
        grid=(DEPTH, 3 * D_MODEL // tn),
        in_specs=[
            pl.BlockSpec((rows, D_MODEL), lambda l, j: (0, 0)),
            pl.BlockSpec((None, D_MODEL, tn), lambda l, j: (l, 0, j)),
            pl.BlockSpec((None, 1, tn), lambda l, j: (l, 0, j)),
        ],
        out_specs=pl.BlockSpec((None, rows, tn), lambda l, j: (l, 0, j)),
        compiler_params=pltpu.CompilerParams(
            dimension_semantics=("arbitrary", "arbitrary"), vmem_limit_bytes=VMEM_LIMIT_BYTES),
        name="cond_mod",
    )(c_all, w_cond, b_cond.reshape(DEPTH, 1, 3 * D_MODEL))


N_PROMPT_INPUTS = 19
N_PROMPT_OUTPUTS = 9
N_WEIGHTS = 4


def _prompt_kernel(*refs, layer, n_aliased, n_cast):
    (sinks_ref, x_ref, mod_ref, base_ref, rtab_ref,
     win_ref, wpa_ref, wpr_ref, wout_ref, lng_ref, lnb_ref,
     dmask_ref, qdec_ref, kdec_ref,
     sqkv_ref, sret_ref, ck_ref, cv_ref, st_ref) = refs[:N_PROMPT_INPUTS]
    cast_in = refs[N_PROMPT_INPUTS:N_PROMPT_INPUTS + n_cast]
    refs = refs[N_PROMPT_INPUTS + n_cast + n_aliased:]
    (y_ref, knew_ref, vnew_ref, snew_ref,
     oatt_ref, oret_ref, nk_ref, nv_ref, ns_ref) = refs[:N_PROMPT_OUTPUTS]
    cast_out = refs[N_PROMPT_OUTPUTS:N_PROMPT_OUTPUTS + n_cast]
    (tab_scr, hb_scr, q_scr, kvar_scr, vvar_scr, ga_scr, a_scr,
     rq_scr, rk_scr, rv_scr, rg_scr, r_scr, s_scr,
     ma_scr, mr_scr) = refs[N_PROMPT_OUTPUTS + n_cast:]
    step = pl.program_id(0)

    def cast_next_layer_weights():
        for src, dst in zip(cast_in, cast_out):
            dst[...] = src[...].astype(BF16)

    tm = x_ref.shape[0]
    nsub = tm // WINDOW

    def proj(c0, c1):
        return _dot(hb_scr[...], win_ref[:, c0:c1])

    lane = _lane_iota((tm, LANES))
    first_half32 = (lane & 32) == 0
    lo64 = lane < 64

    def rope_attn(t):
        return _rope_attn_tile(t, tab_scr[0], tab_scr[1], first_half32)

    def rope_ret(t):
        return _rope_ret_tile(t, tab_scr[2], tab_scr[3])

    def store_variants(scr, t, fill):
        swapped = pltpu.roll(t, 64, 1)
        other = jnp.full_like(t, fill)
        scr[0, WINDOW:WINDOW + tm, :] = jnp.where(lo64, t, other).astype(BF16)
        scr[1, WINDOW:WINDOW + tm, :] = jnp.where(lo64, other, swapped).astype(BF16)
        scr[2, WINDOW:WINDOW + tm, :] = jnp.where(lo64, swapped, other).astype(BF16)
        scr[3, WINDOW:WINDOW + tm, :] = jnp.where(lo64, other, t).astype(BF16)

    def block_prepare(src_x_ref, src_base_ref):
        shift = mod_ref[0:1, 0:D_MODEL]
        scale = mod_ref[0:1, D_MODEL:2 * D_MODEL]
        hb_scr[...] = (src_x_ref[...] * (1.0 + scale) + shift).astype(BF16)
        for fam in range(2):
            cb = src_base_ref[2 * fam:2 * fam + 1, :]
            sb = src_base_ref[2 * fam + 1:2 * fam + 2, :]
            tab_scr[2 * fam] = cb * rtab_ref[4 * fam] - sb * rtab_ref[4 * fam + 1]
            tab_scr[2 * fam + 1] = sb * rtab_ref[4 * fam + 2] + cb * rtab_ref[4 * fam + 3]

    def block_head():
        kv = proj(C_AK, C_AG)
        k_rot = rope_attn(kv[:, 0:LANES])
        v_raw = kv[:, LANES:2 * LANES]
        knew_ref[...] = k_rot[tm - WINDOW:, :]
        vnew_ref[...] = v_raw[tm - WINDOW:, :]
        store_variants(kvar_scr, k_rot, 0.0)
        store_variants(vvar_scr, v_raw, 1.0)
        qp = proj(C_AQ, C_AK)
        for t in range(ATT_WIDTH // LANES):
            qt = rope_attn(qp[:, t * LANES:(t + 1) * LANES])
            q_scr[:, t * LANES:(t + 1) * LANES] = (qt * ATT_SCALE).astype(BF16)
        ga_scr[...] = proj(C_AG, C_RQ)

    @pl.when(step == 0)
    def _():
        kvar_scr[:, 0:WINDOW, :] = jnp.zeros((4, WINDOW, LANES), BF16)
        vvar_scr[:, 0:WINDOW, :] = jnp.zeros((4, WINDOW, LANES), BF16)
        s_scr[...] = jnp.zeros(s_scr.shape, F32)

    block_prepare(x_ref, base_ref)
    block_head()

    def chunk_rq():
        p = proj(C_RQ, C_RK)
        for h in range(RET_HEADS):
            sl = slice(h * RET_DK, (h + 1) * RET_DK)
            rq_scr[:, sl] = rope_ret(p[:, sl])

    def chunk_rk():
        p = proj(C_RK, C_RV)
        for h in range(RET_HEADS):
            sl = slice(h * RET_DK, (h + 1) * RET_DK)
            rk_scr[:, sl] = rope_ret(p[:, sl]) * RET_K_SCALE

    def half_chunk(c0, dst, fn, half):
        w = dst.shape[1] // 2
        def run():
            dst[:, half * w:(half + 1) * w] = fn(proj(c0 + half * w, c0 + (half + 1) * w))
        return run

    to_bf16 = lambda v: v.astype(BF16)
    raw = lambda v: v
    att_companions = [
        chunk_rq, chunk_rk,
        half_chunk(C_RV, rv_scr, to_bf16, 0), half_chunk(C_RV, rv_scr, to_bf16, 1),
        half_chunk(C_RG, rg_scr, raw, 0), half_chunk(C_RG, rg_scr, raw, 1),
        half_chunk(C_MA, ma_scr, raw, 0), half_chunk(C_MA, ma_scr, raw, 1),
    ]

    row = lax.broadcasted_iota(jnp.int32, (WINDOW, 2 * WINDOW), 0)
    col = lax.broadcasted_iota(jnp.int32, (WINDOW, 2 * WINDOW), 1)
    in_window = col <= row + WINDOW
    mask_std = (col >= row) & in_window
    off = jnp.where(step > 0, 0, 4 * WINDOW)
    mask_first = ((col >= row + off) | (col >= WINDOW)) & in_window
    lo64_w = _lane_iota((WINDOW, LANES)) < 64

    def attention_logits(i, t):
        r0 = i * WINDOW
        g = t // 2
        qt = q_scr[r0:r0 + WINDOW, t * LANES:(t + 1) * LANES]
        return [lax.dot_general(qt, kvar_scr[2 * g + p, r0:r0 + 2 * WINDOW, :], NT,
                                preferred_element_type=F32) for p in range(2)]

    def attention_values(i, t, logits):
        r0 = i * WINDOW
        mask = mask_first if i == 0 else mask_std
        g = t // 2
        outs, sink_terms = [], []
        for p in range(2):
            head = 2 * t + p
            s = jnp.where(mask, logits[p], -jnp.inf)
            sink = sinks_ref[layer, head]
            m = jnp.maximum(jnp.max(s, axis=-1, keepdims=True), sink)
            pe = jnp.exp(s - m).astype(BF16)
            sink_terms.append(jnp.exp(sink - m))
            outs.append(_dot(pe, vvar_scr[2 * g + p, r0:r0 + 2 * WINDOW, :]))
        weighted = jnp.where(lo64_w, outs[0], outs[1])
        row_sums = pltpu.roll(jnp.where(lo64_w, outs[1], outs[0]), 64, 1)
        denom = row_sums + jnp.where(lo64_w, sink_terms[0], sink_terms[1])
        a_scr[r0:r0 + WINDOW, t * LANES:(t + 1) * LANES] = weighted * (1.0 / denom)

    n = 0
    for i in range(nsub):
        for t in range(ATT_WIDTH // LANES):
            logits = attention_logits(i, t)
            if n < len(att_companions):
                att_companions[n]()
            if n == 2:
                cast_next_layer_weights()
            n += 1
            attention_values(i, t, logits)
    for run in att_companions[n:]:
        run()

    kvar_scr[:, 0:WINDOW, :] = kvar_scr[:, tm:tm + WINDOW, :]
    vvar_scr[:, 0:WINDOW, :] = vvar_scr[:, tm:tm + WINDOW, :]

    def ret_slices(c, h):
        rows = slice(c * RET_CHUNK, (c + 1) * RET_CHUNK)
        return rows, slice(h * RET_DK, (h + 1) * RET_DK), slice(h * RET_DV, (h + 1) * RET_DV)

    def retention_scores_and_state(c):
        inners, s_olds = [], []
        for h in range(RET_HEADS):
            rows, sk, sv = ret_slices(c, h)
            qh = rq_scr[rows, sk]
            kh = rk_scr[rows, sk]
            vh = rv_scr[rows, sv]
            inners.append(lax.dot_general(qh.astype(BF16), kh.astype(BF16), NT, preferred_element_type=F32))
            s_old = s_scr[h]
            s_olds.append(s_old.astype(BF16))
            kd = (kh * kdec_ref[h]).astype(BF16)
            s_scr[h] = s_old * CHUNK_DECAY[h] + lax.dot_general(
                kd, vh, (((0,), (0,)), ((), ())), preferred_element_type=F32)
        return inners, s_olds

    def retention_outputs(c, inners, s_olds):
        for h in range(RET_HEADS):
            rows, sk, sv = ret_slices(c, h)
            qh = rq_scr[rows, sk]
            lhs = jnp.concatenate([(inners[h] * dmask_ref[h]).astype(BF16),
                                   (qh * qdec_ref[h]).astype(BF16)], axis=1)
            rhs = jnp.concatenate([rv_scr[rows, sv], s_olds[h]], axis=0)
            o = _dot(lhs, rhs)
            ms = jnp.mean(o * o, axis=-1, keepdims=True)
            on = o * lax.rsqrt(ms + RMS_EPS)
            r_scr[rows, sv] = on

    ret_companions = [half_chunk(C_MR, mr_scr, raw, 0), half_chunk(C_MR, mr_scr, raw, 1)]
    n = 0
    for c in range(nsub):
        inners, s_olds = retention_scores_and_state(c)
        if n < len(ret_companions):
            ret_companions[n]()
        n += 1
        retention_outputs(c, inners, s_olds)
    for run in ret_companions[n:]:
        run()
    sample_qmats, sample_logits = _sample_logits(sqkv_ref, ck_ref)
    for b in range(sqkv_ref.shape[0]):
        for h in range(RET_HEADS):
            _sample_state_update(sret_ref, st_ref, ns_ref, b, h)

    gate_c = mod_ref[0:1, 2 * D_MODEL:3 * D_MODEL]
    lng = lng_ref[...]
    lnb = lnb_ref[...]
    windows = [slice(i * WINDOW, (i + 1) * WINDOW) for i in range(nsub)]
    za_all = _dot((a_scr[...] * _silu(ga_scr[...])).astype(BF16), wpa_ref[...])
    zr_all = _dot((r_scr[...] * _silu(rg_scr[...])).astype(BF16), wpr_ref[...])
    for rows in windows:
        z = _sigmoid(ma_scr[rows, :]) * za_all[rows, :] + _sigmoid(mr_scr[rows, :]) * zr_all[rows, :]
        u = _dot(z.astype(BF16), wout_ref[...])
        t = ALPHA * x_ref[rows, :] + gate_c * u
        mu = jnp.mean(t, axis=-1, keepdims=True)
        d = t - mu
        var = jnp.mean(d * d, axis=-1, keepdims=True)
        y_ref[rows, :] = d * lax.rsqrt(var + LN_EPS) * lng + lnb

    _sample_outputs(sinks_ref, layer, sqkv_ref, sret_ref, ck_ref, cv_ref, sample_qmats, sample_logits,
                    oatt_ref, oret_ref, nk_ref, nv_ref, ns_ref)

    @pl.when(step == pl.num_programs(0) - 1)
    def _():
        snew_ref[...] = s_scr[...]


def _fused_layer(layer, x, mod, mod_row_block, base, rtab, weights_b, next_weights,
                 ln_g, ln_b, sinks, ret_tabs, sample_qkv, sample_ret, cache_k, cache_v, state, shared_out):
    w_in_b, w_pa_b, w_pr_b, w_out_b = weights_b
    seq = x.shape[0]
    tm = PROMPT_ROWS
    steps = seq // tm
    nbatch = sample_qkv.shape[0]
    nb = nbatch // steps
    win = cache_k.shape[3]
    assert win == WINDOW == LANES
    dmask, qdec, kdec = ret_tabs
    smem = pl.BlockSpec(memory_space=pltpu.SMEM)
    row_spec = lambda w: pl.BlockSpec((tm, w), lambda i: (i, 0))
    seq_spec = lambda w: pl.BlockSpec((nb, 1, w), lambda i: (i, 0, 0))
    cache_spec = pl.BlockSpec((None, nb, ATT_KV_WIDTH, win), lambda i: (layer, i, 0, 0))
    state_spec = pl.BlockSpec((None, nb, RET_HEADS, RET_DK, RET_DV), lambda i: (layer, i, 0, 0, 0))
    in_specs = [
        smem,
        row_spec(D_MODEL),
        pl.BlockSpec((None, SUBLANES, 3 * D_MODEL), lambda i: (layer, mod_row_block, 0),
                     pipeline_mode=pl.Buffered(1)),
        pl.BlockSpec((None, 4, LANES), lambda i: (i, 0, 0)),
        _const_spec((8, tm, LANES)),
        _const_spec((D_MODEL, IN_COLS)),
        _const_spec((ATT_WIDTH, D_MODEL)),
        _const_spec((RET_WIDTH, D_MODEL)),
        _const_spec((D_MODEL, D_MODEL)),
        _layer_spec((1, D_MODEL), layer), _layer_spec((1, D_MODEL), layer),
        _const_spec((RET_HEADS, RET_CHUNK, RET_CHUNK)),
        _const_spec((RET_HEADS, RET_CHUNK, RET_DK)),
        _const_spec((RET_HEADS, RET_CHUNK, RET_DK)),
        seq_spec(C_AG), seq_spec(2 * RET_QK_WIDTH + RET_WIDTH),
        cache_spec, cache_spec, state_spec,
    ]
    args = [sinks, x, mod, base, rtab, w_in_b, w_pa_b, w_pr_b, w_out_b, ln_g, ln_b, dmask, qdec, kdec,
            sample_qkv, sample_ret, cache_k, cache_v, state]
    assert len(args) == N_PROMPT_INPUTS
    cast_shapes, cast_specs = [], []
    if next_weights is not None:
        assert len(next_weights) == N_WEIGHTS
        for w in next_weights:
            _, rows, cols = w.shape
            slab = max(BF16_SUBLANES, rows // steps)
            nslab = rows // slab
            assert rows % slab == 0 and nslab <= steps
            args.append(w)
            in_specs.append(pl.BlockSpec((None, slab, cols),
                                         lambda i, n=nslab: (layer + 1, jnp.minimum(i, n - 1), 0)))
            cast_shapes.append(jax.ShapeDtypeStruct((rows, cols), BF16))
            cast_specs.append(pl.BlockSpec((slab, cols), lambda i, n=nslab: (jnp.minimum(i, n - 1), 0)))
    aliases = {}
    if shared_out is not None:
        for j, arr in enumerate(shared_out):
            aliases[len(args)] = N_PROMPT_OUTPUTS - len(shared_out) + j
            args.append(arr)
            in_specs.append(pl.BlockSpec(memory_space=pl.ANY))
    return pl.pallas_call(
        functools.partial(_prompt_kernel, layer=layer, n_aliased=len(aliases), n_cast=len(cast_shapes)),
        out_shape=(
            jax.ShapeDtypeStruct((seq, D_MODEL), F32),
            jax.ShapeDtypeStruct((WINDOW, ATT_KV_WIDTH), F32),
            jax.ShapeDtypeStruct((WINDOW, ATT_KV_WIDTH), F32),
            jax.ShapeDtypeStruct((RET_HEADS, RET_DK, RET_DV), F32),
            jax.ShapeDtypeStruct((nbatch, 1, ATT_WIDTH), F32),
            jax.ShapeDtypeStruct((nbatch, 1, RET_WIDTH), F32),
            jax.ShapeDtypeStruct(cache_k.shape, F32),
            jax.ShapeDtypeStruct(cache_v.shape, F32),
            jax.ShapeDtypeStruct(state.shape, F32),
        ) + tuple(cast_shapes),
        grid=(steps,),
        in_specs=in_specs,
        out_specs=(
            row_spec(D_MODEL),
            pl.BlockSpec((WINDOW, ATT_KV_WIDTH), lambda i: (0, 0)),
            pl.BlockSpec((WINDOW, ATT_KV_WIDTH), lambda i: (0, 0)),
            pl.BlockSpec((RET_HEADS, RET_DK, RET_DV), lambda i: (0, 0, 0)),
            seq_spec(ATT_WIDTH), seq_spec(RET_WIDTH),
            cache_spec, cache_spec, state_spec,
        ) + tuple(cast_specs),
        input_output_aliases=aliases,
        scratch_shapes=[
            pltpu.VMEM((4, tm, LANES), F32),
            pltpu.VMEM((tm, D_MODEL), BF16),
            pltpu.VMEM((tm, ATT_WIDTH), BF16),
            pltpu.VMEM((4, WINDOW + tm, LANES), BF16),
            pltpu.VMEM((4, WINDOW + tm, LANES), BF16),
            pltpu.VMEM((tm, ATT_WIDTH), F32),
            pltpu.VMEM((tm, ATT_WIDTH), F32),
            pltpu.VMEM((tm, RET_QK_WIDTH), F32),
            pltpu.VMEM((tm, RET_QK_WIDTH), F32),
            pltpu.VMEM((tm, RET_WIDTH), BF16),
            pltpu.VMEM((tm, RET_WIDTH), F32),
            pltpu.VMEM((tm, RET_WIDTH), F32),
            pltpu.VMEM((RET_HEADS, RET_DK, RET_DV), F32),
            pltpu.VMEM((tm, D_MODEL), F32),
            pltpu.VMEM((tm, D_MODEL), F32),
        ],
        compiler_params=pltpu.CompilerParams(
            dimension_semantics=("arbitrary",), vmem_limit_bytes=VMEM_LIMIT_BYTES),
        name="fused_layer",
    )(*args)


def _sample_proj_kernel(x_ref, mod_ref, tab_ref, win_ref,
                        qkv_ref, ga_ref, ret_ref, rg_ref, mg_ref):
    rows = x_ref.shape[0]
    x = x_ref[:, 0, :]
    shift = mod_ref[:, 0:D_MODEL]
    scale = mod_ref[:, D_MODEL:2 * D_MODEL]
    hb = (x * (1.0 + scale) + shift).astype(BF16)

    def proj(c0, c1):
        return _dot(hb, win_ref[:, c0:c1])

    lane = _lane_iota((rows, LANES))
    first_half32 = (lane & 32) == 0
    ca = tab_ref[0:1, :]
    sa = tab_ref[1:2, :]
    cr = tab_ref[2:3, :]
    sr = tab_ref[3:4, :]

    qkv = proj(C_AQ, C_AG)
    for t in range(ATT_WIDTH // LANES):
        qt = _rope_attn_tile(qkv[:, t * LANES:(t + 1) * LANES], ca, sa, first_half32)
        qkv_ref[:, 0, t * LANES:(t + 1) * LANES] = qt * ATT_SCALE
    qkv_ref[:, 0, C_AK:C_AV] = _rope_attn_tile(qkv[:, C_AK:C_AV], ca, sa, first_half32)
    qkv_ref[:, 0, C_AV:C_AG] = qkv[:, C_AV:C_AG]
    ga_ref[...] = _silu(proj(C_AG, C_RQ))

    rqk = proj(C_RQ, C_RV)
    for h in range(RET_HEADS):
        sl = slice(h * RET_DK, (h + 1) * RET_DK)
        ret_ref[:, 0, sl] = _rope_ret_tile(rqk[:, sl], cr, sr)
        sk = slice(RET_QK_WIDTH + h * RET_DK, RET_QK_WIDTH + (h + 1) * RET_DK)
        ret_ref[:, 0, sk] = _rope_ret_tile(rqk[:, sk], cr, sr) * RET_K_SCALE
    ret_ref[:, 0, 2 * RET_QK_WIDTH:] = proj(C_RV, C_RG)
    rg_ref[...] = _silu(proj(C_RG, C_MA))
    mg_ref[...] = _sigmoid(proj(C_MA, IN_COLS))


def _sample_proj(layer, x, mod, tab, w_in_b):
    rows = x.shape[0]
    shapes = ((rows, 1, C_AG), (rows, ATT_WIDTH), (rows, 1, 2 * RET_QK_WIDTH + RET_WIDTH),
              (rows, RET_WIDTH), (rows, 2 * D_MODEL))
    return pl.pallas_call(
        _sample_proj_kernel,
        out_shape=tuple(jax.ShapeDtypeStruct(s, F32) for s in shapes),
        grid=(1,),
        in_specs=[
            _const_spec((rows, 1, D_MODEL)),
            _layer_spec((rows, 3 * D_MODEL), layer),
            _const_spec((4, LANES)),
            _const_spec((D_MODEL, IN_COLS)),
        ],
        out_specs=tuple(pl.BlockSpec(s, lambda i, nd=len(s): (0,) * nd) for s in shapes),
        compiler_params=pltpu.CompilerParams(
            dimension_semantics=("arbitrary",), vmem_limit_bytes=VMEM_LIMIT_BYTES),
        name="sample_proj",
    )(x, mod, tab, w_in_b)


def _sample_logits(qkv_ref, ck_ref):
    nb = qkv_ref.shape[0]
    rowi = lax.broadcasted_iota(jnp.int32, (ATT_HEADS, LANES), 0)
    lanei = lax.broadcasted_iota(jnp.int32, (ATT_HEADS, LANES), 1)
    lane_group = lanei // ATT_HEAD_DIM
    qmats, logits = [], []
    for b in range(nb):
        qkv = qkv_ref[b]
        qmat = jnp.zeros((ATT_HEADS, LANES), F32)
        for h in range(ATT_HEADS):
            t, p, g = h // 2, h % 2, h // ATT_GROUP
            tile = qkv[:, t * LANES:(t + 1) * LANES]
            src = tile if p == g else pltpu.roll(tile, 64, 1)
            qmat = jnp.where((rowi == h) & (lane_group == g), jnp.broadcast_to(src, (ATT_HEADS, LANES)), qmat)
        qmats.append(qmat)
        logits.append(_dot(qmat.astype(BF16), ck_ref[b].astype(BF16)))
    return qmats, logits


def _sample_state_update(ret_ref, st_ref, ns_ref, b, h):
    dr = lax.broadcasted_iota(jnp.int32, (RET_DK, RET_DK), 0)
    dc = lax.broadcasted_iota(jnp.int32, (RET_DK, RET_DK), 1)
    ret = ret_ref[b]
    kh = ret[:, RET_QK_WIDTH + h * RET_DK:RET_QK_WIDTH + (h + 1) * RET_DK]
    vh = ret[:, 2 * RET_QK_WIDTH + h * RET_DV:2 * RET_QK_WIDTH + (h + 1) * RET_DV]
    k_col = jnp.sum(jnp.where(dr == dc, jnp.broadcast_to(kh, (RET_DK, RET_DK)), 0.0),
                    axis=1, keepdims=True)
    ns_ref[b, h] = st_ref[b, h] * TOKEN_DECAY[h] + k_col * vh


def _sample_outputs(sinks_ref, layer, qkv_ref, ret_ref, ck_ref, cv_ref, qmats, logits,
                    oatt_ref, oret_ref, nk_ref, nv_ref, ns_ref):
    nb = qkv_ref.shape[0]
    sink_col = jnp.zeros((ATT_HEADS, 1), F32)
    rowc = lax.broadcasted_iota(jnp.int32, (ATT_HEADS, 1), 0)
    for h in range(ATT_HEADS):
        sink_col = jnp.where(rowc == h, sinks_ref[layer, h], sink_col)
    lo64_row = _lane_iota((1, LANES)) < 64
    dr = lax.broadcasted_iota(jnp.int32, (LANES, LANES), 0)
    dc = lax.broadcasted_iota(jnp.int32, (LANES, LANES), 1)
    diag = dr == dc
    last_lane = dc == WINDOW - 1

    for b in range(nb):
        qkv = qkv_ref[b]
        k_new = qkv[:, C_AK:C_AV]
        v_new = qkv[:, C_AV:C_AG]
        s_c = logits[b]
        s_self = jnp.sum(qmats[b] * k_new, axis=-1, keepdims=True)
        m = jnp.maximum(jnp.maximum(jnp.max(s_c, axis=-1, keepdims=True), s_self), sink_col)
        p_c = jnp.exp(s_c - m)
        p_self = jnp.exp(s_self - m)
        denom = jnp.sum(p_c, axis=-1, keepdims=True) + p_self + jnp.exp(sink_col - m)
        o = (lax.dot_general(p_c.astype(BF16), cv_ref[b].astype(BF16), NT, preferred_element_type=F32)
             + p_self * v_new) / denom
        o_sw = pltpu.roll(o, 64, 1)
        for t in range(ATT_WIDTH // LANES):
            g = t // 2
            first = (o if g == 0 else o_sw)[2 * t:2 * t + 1, :]
            second = (o_sw if g == 0 else o)[2 * t + 1:2 * t + 2, :]
            oatt_ref[b, :, t * LANES:(t + 1) * LANES] = jnp.where(lo64_row, first, second)

        for new_row, src_ref, dst_ref in ((k_new, ck_ref, nk_ref), (v_new, cv_ref, nv_ref)):
            new_col = jnp.sum(jnp.where(diag, jnp.broadcast_to(new_row, (LANES, LANES)), 0.0),
                              axis=1, keepdims=True)
            shifted = pltpu.roll(src_ref[b], WINDOW - 1, 1)
            dst_ref[b] = jnp.where(last_lane, new_col, shifted)

    for b in range(nb):
        ret = ret_ref[b]
        for h in range(RET_HEADS):
            qh = ret[:, h * RET_DK:(h + 1) * RET_DK]
            q8 = jnp.broadcast_to(qh, (SUBLANES, RET_DK)).astype(BF16)
            oh = _dot(q8, ns_ref[b, h].astype(BF16))[0:1, :]
            ms = jnp.mean(oh * oh, axis=-1, keepdims=True)
            oret_ref[b, :, h * RET_DV:(h + 1) * RET_DV] = oh * lax.rsqrt(ms + RMS_EPS)


def _sample_out_kernel(x_ref, mod_ref, oatt_ref, ga_ref, oret_ref, rg_ref, mg_ref,
                       wpa_ref, wpr_ref, wout_ref, lng_ref, lnb_ref, y_ref):
    gate_c = mod_ref[:, 2 * D_MODEL:3 * D_MODEL]
    za = _dot((oatt_ref[:, 0, :] * ga_ref[...]).astype(BF16), wpa_ref[...])
    zr = _dot((oret_ref[:, 0, :] * rg_ref[...]).astype(BF16), wpr_ref[...])
    z = mg_ref[:, 0:D_MODEL] * za + mg_ref[:, D_MODEL:2 * D_MODEL] * zr
    u = _dot(z.astype(BF16), wout_ref[...])
    t = ALPHA * x_ref[:, 0, :] + gate_c * u
    mu = jnp.mean(t, axis=-1, keepdims=True)
    d = t - mu
    var = jnp.mean(d * d, axis=-1, keepdims=True)
    y_ref[:, 0, :] = d * lax.rsqrt(var + LN_EPS) * lng_ref[...] + lnb_ref[...]


def _sample_out(layer, x, mod, oatt, ga, oret, rg, mg, w_pa_b, w_pr_b, w_out_b, ln_g, ln_b):
    rows = x.shape[0]
    return pl.pallas_call(
        _sample_out_kernel,
        out_shape=jax.ShapeDtypeStruct(x.shape, F32),
        grid=(1,),
        in_specs=[
            _const_spec((rows, 1, D_MODEL)),
            _layer_spec((rows, 3 * D_MODEL), layer),
            _const_spec((rows, 1, ATT_WIDTH)), _const_spec((rows, ATT_WIDTH)),
            _const_spec((rows, 1, RET_WIDTH)), _const_spec((rows, RET_WIDTH)),
            _const_spec((rows, 2 * D_MODEL)),
            _const_spec((ATT_WIDTH, D_MODEL)),
            _const_spec((RET_WIDTH, D_MODEL)),
            _const_spec((D_MODEL, D_MODEL)),
            _layer_spec((1, D_MODEL), layer), _layer_spec((1, D_MODEL), layer),
        ],
        out_specs=pl.BlockSpec((rows, 1, D_MODEL), lambda i: (0, 0, 0)),
        compiler_params=pltpu.CompilerParams(
            dimension_semantics=("arbitrary",), vmem_limit_bytes=VMEM_LIMIT_BYTES),
        name="sample_out",
    )(x, mod, oatt, ga, oret, rg, mg, w_pa_b, w_pr_b, w_out_b, ln_g, ln_b)


def kernel(x_prompt, x_sample, c_prompt, c_sample, cache_k, cache_v, state_ret, w_in, attn_sinks,
           w_cond, b_cond, w_proj_attn, w_proj_ret, w_out, ln_g, ln_b):
    seq = x_prompt.shape[1]
    nbatch = x_sample.shape[0]
    win = cache_k.shape[2]
    assert seq % PROMPT_ROWS == 0 and nbatch % (seq // PROMPT_ROWS) == 0 and nbatch % SUBLANES == 0

    weights_f32 = (w_in, w_proj_attn, w_proj_ret, w_out)
    weights_b = tuple(w[0].astype(BF16) for w in weights_f32)
    ln_g3 = ln_g.reshape(DEPTH, 1, D_MODEL)
    ln_b3 = ln_b.reshape(DEPTH, 1, D_MODEL)

    c_all = jnp.concatenate([c_sample, c_prompt, jnp.zeros((SUBLANES - 1, D_MODEL), F32)], axis=0)
    mod = _cond_call(c_all, w_cond, b_cond)
    prompt_mod_block = nbatch // SUBLANES

    base, rtab = _prompt_rope_tables(seq, PROMPT_ROWS)
    stab = _sample_rope_table(float(PAST_LEN))
    ret_tabs = _retention_tables()

    def feature_major(c):
        return c.transpose(0, 1, 3, 4, 2).reshape(DEPTH, nbatch, ATT_KV_WIDTH, win)

    def window_major(c):
        return c.reshape(DEPTH, nbatch, ATT_KV_HEADS, ATT_HEAD_DIM, win).transpose(0, 1, 4, 2, 3)

    ck = feature_major(cache_k)
    cv = feature_major(cache_v)

    yp = x_prompt[0]
    ys = x_sample
    kp, vp, sp = [], [], []
    prev = None
    for l in range(DEPTH):
        w_in_b, w_pa_b, w_pr_b, w_out_b = weights_b
        qkv, ga, ret, rg, mg = _sample_proj(l, ys, mod, stab, w_in_b)
        outs = _fused_layer(
            l, yp, mod, prompt_mod_block, base, rtab, weights_b,
            weights_f32 if l + 1 < DEPTH else None,
            ln_g3, ln_b3, attn_sinks, ret_tabs, qkv, ret, ck, cv, state_ret, prev)
        yp, k_new, v_new, s_new, oatt, oret, nk, nv, ns = outs[:N_PROMPT_OUTPUTS]
        kp.append(k_new.reshape(1, WINDOW, ATT_KV_HEADS, ATT_HEAD_DIM))
        vp.append(v_new.reshape(1, WINDOW, ATT_KV_HEADS, ATT_HEAD_DIM))
        sp.append(s_new[None])
        prev = (nk, nv, ns)
        ys = _sample_out(l, ys, mod, oatt, ga, oret, rg, mg, w_pa_b, w_pr_b, w_out_b, ln_g3, ln_b3)
        weights_b = tuple(outs[N_PROMPT_OUTPUTS:])

    nk, nv, ns = prev
    return (yp[None], ys, jnp.stack(kp), jnp.stack(vp), jnp.stack(sp),
            window_major(nk), window_major(nv), ns)
```

```python
import functools

import jax
import jax.numpy as jnp
import numpy as np
from jax import lax
from jax.experimental import pallas as pl
from jax.experimental.pallas import tpu as pltpu

D_MODEL = 1024
DEPTH = 2
PAST_LEN = 16384
ATT_HEADS = 8
ATT_KV_HEADS = 2
ATT_HEAD_DIM = 64
ATT_GROUP = ATT_HEADS // ATT_KV_HEADS
ATT_WIDTH = ATT_HEADS * ATT_HEAD_DIM
ATT_KV_WIDTH = ATT_KV_HEADS * ATT_HEAD_DIM
WINDOW = 128
RET_HEADS = 4
RET_DK = 128
RET_DV = 256
RET_QK_WIDTH = RET_HEADS * RET_DK
RET_WIDTH = RET_HEADS * RET_DV
RET_CHUNK = 128
ROPE_THETA = 10000.0
ALPHA = (2.0 * DEPTH) ** 0.25
LN_EPS = 1e-5
RMS_EPS = 1e-6
ATT_SCALE = ATT_HEAD_DIM ** -0.5
RET_K_SCALE = RET_DK ** -0.5

C_AQ = 0
C_AK = C_AQ + ATT_WIDTH
C_AV = C_AK + ATT_KV_WIDTH
C_AG = C_AV + ATT_KV_WIDTH
C_RQ = C_AG + ATT_WIDTH
C_RK = C_RQ + RET_QK_WIDTH
C_RV = C_RK + RET_QK_WIDTH
C_RG = C_RV + RET_WIDTH
C_MA = C_RG + RET_WIDTH
C_MR = C_MA + D_MODEL
IN_COLS = C_MR + D_MODEL

LANES = 128
SUBLANES = 8
BF16_SUBLANES = 16
VMEM_LIMIT_BYTES = 56 * 1024 * 1024

PROMPT_ROWS = 256

BF16 = jnp.bfloat16
F32 = jnp.float32
NT = (((1,), (1,)), ((), ()))

_LOG_GAMMA = np.log(1.0 - 2.0 ** (-5.0 - np.arange(RET_HEADS, dtype=np.float64)))
CHUNK_DECAY = tuple(float(v) for v in np.exp(RET_CHUNK * _LOG_GAMMA))
TOKEN_DECAY = tuple(float(v) for v in np.exp(_LOG_GAMMA))


def _sigmoid(x):
    return 0.5 * jnp.tanh(0.5 * x) + 0.5


GATE_PRESCALE = 0.5


RESIDUAL_GATE_SCALE = 0.5 / ALPHA
POST_NORM_EPS = LN_EPS / ALPHA ** 2


def _silu_of_half(xh):
    return xh * (jnp.tanh(xh) + 1.0)


def _twice_sigmoid_of_half(xh):
    return jnp.tanh(xh) + 1.0


def _gate_column_scale():
    scale = np.ones((1, IN_COLS), np.float32)
    for c0, c1 in ((C_AG, C_RQ), (C_RG, IN_COLS)):
        scale[:, c0:c1] = GATE_PRESCALE
    return jnp.asarray(scale)


def _silu(x):
    return x * _sigmoid(x)


def _dot(a, b):
    return jnp.dot(a, b, preferred_element_type=F32)


def _rope_attn_tile(x, cos, sin_signed, first_half):
    rot = jnp.where(first_half, pltpu.roll(x, LANES - 32, 1), pltpu.roll(x, 32, 1))
    return x * cos + rot * sin_signed


def _rope_ret_tile(x, cos, sin_signed):
    return x * cos + pltpu.roll(x, 64, 1) * sin_signed


def _lane_iota(shape):
    return lax.broadcasted_iota(jnp.int32, shape, len(shape) - 1)


def _layer_spec(shape, layer):
    nd = len(shape)
    return pl.BlockSpec((None,) + tuple(shape), lambda i: (layer,) + (0,) * nd,
                        pipeline_mode=pl.Buffered(1))


def _const_spec(shape):
    nd = len(shape)
    return pl.BlockSpec(tuple(shape), lambda i: (0,) * nd, pipeline_mode=pl.Buffered(1))


def _rope_lane_patterns():
    lane = np.arange(LANES)
    f_att = ROPE_THETA ** (-(lane % 32) / 32.0)
    s_att = np.where(lane % 64 < 32, -1.0, 1.0)
    f_ret = ROPE_THETA ** (-(lane % 64) / 64.0)
    s_ret = np.where(lane < 64, -1.0, 1.0)
    return (f_att, s_att), (f_ret, s_ret)


def _prompt_rope_tables(seq, tm):
    starts = np.arange(seq // tm, dtype=np.float64)[:, None] * tm
    offs = np.arange(tm, dtype=np.float64)[:, None]
    base, within = [], []
    for freq, sign in _rope_lane_patterns():
        base += [np.cos(starts * freq), np.sin(starts * freq)]
        c, s = np.cos(offs * freq), np.sin(offs * freq)
        within += [c, s, sign * c, sign * s]
    return (jnp.asarray(np.stack(base, axis=1), F32),
            jnp.asarray(np.stack(within, axis=0), F32))


def _sample_rope_table(pos):
    rows = []
    for freq, sign in _rope_lane_patterns():
        rows += [np.cos(pos * freq), sign * np.sin(pos * freq)]
    return jnp.asarray(np.stack(rows, axis=0), F32)


def _retention_tables():
    c = RET_CHUNK
    idx = np.arange(c, dtype=np.float64)
    diff = idx[:, None] - idx[None, :]
    lg = _LOG_GAMMA[:, None, None]
    dmask = np.where(diff >= 0, np.exp(np.maximum(diff, 0.0)[None] * lg), 0.0)
    qdec = np.broadcast_to(np.exp((idx + 1.0)[None, :, None] * lg), (RET_HEADS, c, RET_DK))
    kdec = np.broadcast_to(np.exp((c - 1.0 - idx)[None, :, None] * lg), (RET_HEADS, c, RET_DK))
    return jnp.asarray(dmask, F32), jnp.asarray(qdec, F32), jnp.asarray(kdec, F32)


def _cond_kernel(c_ref, w_ref, b_ref, o_ref):
    a = _silu(c_ref[...]).astype(BF16)
    o_ref[...] = _dot(a, w_ref[...].astype(BF16)) + b_ref[...]


def _cond_call(c_all, w_cond, b_cond):
    rows = c_all.shape[0]
    tn = 3 * D_MODEL // 2
    return pl.pallas_call(
        _cond_kernel,
        out_shape=jax.ShapeDtypeStruct((DEPTH, rows, 3 * D_MODEL), F32),
        grid=(DEPTH, 3 * D_MODEL // tn),
        in_specs=[
            pl.BlockSpec((rows, D_MODEL), lambda l, j: (0, 0)),
            pl.BlockSpec((None, D_MODEL, tn), lambda l, j: (l, 0, j)),
            pl.BlockSpec((None, 1, tn), lambda l, j: (l, 0, j)),
        ],
        out_specs=pl.BlockSpec((None, rows, tn), lambda l, j: (l, 0, j)),
        compiler_params=pltpu.CompilerParams(
            dimension_semantics=("arbitrary", "arbitrary"), vmem_limit_bytes=VMEM_LIMIT_BYTES),
        name="cond_mod",
    )(c_all, w_cond, b_cond.reshape(DEPTH, 1, 3 * D_MODEL))


N_PROMPT_INPUTS = 20
N_PROMPT_OUTPUTS = 9
N_WEIGHTS = 4


def _prompt_kernel(*refs, layer, n_aliased, n_cast):
    (sinks_ref, x_ref, mod_ref, base_ref, rtab_ref,
     win_ref, wpa_ref, wpr_ref, wout_ref, lng_ref, lnb_ref,
     dmask_ref, qdec_ref, kdec_ref,
     sqkv_ref, sret_ref, ck_ref, cv_ref, st_ref, wscale_ref) = refs[:N_PROMPT_INPUTS]
    cast_in = refs[N_PROMPT_INPUTS:N_PROMPT_INPUTS + n_cast]
    refs = refs[N_PROMPT_INPUTS + n_cast + n_aliased:]
    (y_ref, knew_ref, vnew_ref, snew_ref,
     oatt_ref, oret_ref, nk_ref, nv_ref, ns_ref) = refs[:N_PROMPT_OUTPUTS]
    cast_out = refs[N_PROMPT_OUTPUTS:N_PROMPT_OUTPUTS + n_cast]
    (tab_scr, hb_scr, q_scr, kvar_scr, vvar_scr, ga_scr, a_scr,
     rq_scr, rk_scr, rv_scr, rg_scr, r_scr, s_scr,
     ma_scr, mr_scr) = refs[N_PROMPT_OUTPUTS + n_cast:]
    step = pl.program_id(0)

    def cast_next_layer_weights():
        for j, (src, dst) in enumerate(zip(cast_in, cast_out)):
            w = src[...]
            if j == 0:
                w = w * wscale_ref[...]
            dst[...] = w.astype(BF16)

    tm = x_ref.shape[0]
    nsub = tm // WINDOW

    def proj(c0, c1):
        return _dot(hb_scr[...], win_ref[:, c0:c1])

    lane = _lane_iota((tm, LANES))
    first_half32 = (lane & 32) == 0
    lo64 = lane < 64

    def rope_attn(t):
        return _rope_attn_tile(t, tab_scr[0], tab_scr[1], first_half32)

    def rope_ret(t):
        return _rope_ret_tile(t, tab_scr[2], tab_scr[3])

    def store_variants(scr, t, fill):
        swapped = pltpu.roll(t, 64, 1)
        other = jnp.full_like(t, fill)
        scr[0, WINDOW:WINDOW + tm, :] = jnp.where(lo64, t, other).astype(BF16)
        scr[1, WINDOW:WINDOW + tm, :] = jnp.where(lo64, other, swapped).astype(BF16)
        scr[2, WINDOW:WINDOW + tm, :] = jnp.where(lo64, swapped, other).astype(BF16)
        scr[3, WINDOW:WINDOW + tm, :] = jnp.where(lo64, other, t).astype(BF16)

    def block_prepare(src_x_ref, src_base_ref):
        shift = mod_ref[0:1, 0:D_MODEL]
        scale = mod_ref[0:1, D_MODEL:2 * D_MODEL]
        hb_scr[...] = (src_x_ref[...] * (1.0 + scale) + shift).astype(BF16)
        for fam in range(2):
            cb = src_base_ref[2 * fam:2 * fam + 1, :]
            sb = src_base_ref[2 * fam + 1:2 * fam + 2, :]
            tab_scr[2 * fam] = cb * rtab_ref[4 * fam] - sb * rtab_ref[4 * fam + 1]
            tab_scr[2 * fam + 1] = sb * rtab_ref[4 * fam + 2] + cb * rtab_ref[4 * fam + 3]

    def block_head():
        kv = proj(C_AK, C_AG)
        k_rot = rope_attn(kv[:, 0:LANES])
        v_raw = kv[:, LANES:2 * LANES]
        knew_ref[...] = k_rot[tm - WINDOW:, :]
        vnew_ref[...] = v_raw[tm - WINDOW:, :]
        store_variants(kvar_scr, k_rot, 0.0)
        store_variants(vvar_scr, v_raw, 1.0)
        qp = proj(C_AQ, C_AK)
        for t in range(ATT_WIDTH // LANES):
            qt = rope_attn(qp[:, t * LANES:(t + 1) * LANES])
            q_scr[:, t * LANES:(t + 1) * LANES] = (qt * ATT_SCALE).astype(BF16)
        ga_scr[...] = proj(C_AG, C_RQ)

    @pl.when(step == 0)
    def _():
        kvar_scr[:, 0:WINDOW, :] = jnp.zeros((4, WINDOW, LANES), BF16)
        vvar_scr[:, 0:WINDOW, :] = jnp.zeros((4, WINDOW, LANES), BF16)
        s_scr[...] = jnp.zeros(s_scr.shape, F32)

    block_prepare(x_ref, base_ref)
    block_head()

    def chunk_rq():
        p = proj(C_RQ, C_RK)
        for h in range(RET_HEADS):
            sl = slice(h * RET_DK, (h + 1) * RET_DK)
            rq_scr[:, sl] = rope_ret(p[:, sl])

    def chunk_rk():
        p = proj(C_RK, C_RV)
        for h in range(RET_HEADS):
            sl = slice(h * RET_DK, (h + 1) * RET_DK)
            rk_scr[:, sl] = rope_ret(p[:, sl]) * RET_K_SCALE

    def half_chunk(c0, dst, fn, half):
        w = dst.shape[1] // 2
        def run():
            dst[:, half * w:(half + 1) * w] = fn(proj(c0 + half * w, c0 + (half + 1) * w))
        return run

    to_bf16 = lambda v: v.astype(BF16)
    raw = lambda v: v
    att_companions = [
        chunk_rq, chunk_rk,
        half_chunk(C_RV, rv_scr, to_bf16, 0), half_chunk(C_RV, rv_scr, to_bf16, 1),
        half_chunk(C_RG, rg_scr, raw, 0), half_chunk(C_RG, rg_scr, raw, 1),
        half_chunk(C_MA, ma_scr, raw, 0), half_chunk(C_MA, ma_scr, raw, 1),
    ]

    row = lax.broadcasted_iota(jnp.int32, (WINDOW, 2 * WINDOW), 0)
    col = lax.broadcasted_iota(jnp.int32, (WINDOW, 2 * WINDOW), 1)
    in_window = col <= row + WINDOW
    mask_std = (col >= row) & in_window
    off = jnp.where(step > 0, 0, 4 * WINDOW)
    mask_first = ((col >= row + off) | (col >= WINDOW)) & in_window
    lo64_w = _lane_iota((WINDOW, LANES)) < 64

    def attention_logits(i, t):
        r0 = i * WINDOW
        g = t // 2
        qt = q_scr[r0:r0 + WINDOW, t * LANES:(t + 1) * LANES]
        return [lax.dot_general(qt, kvar_scr[2 * g + p, r0:r0 + 2 * WINDOW, :], NT,
                                preferred_element_type=F32) for p in range(2)]

    def attention_values(i, t, logits):
        r0 = i * WINDOW
        mask = mask_first if i == 0 else mask_std
        g = t // 2
        outs, sink_terms = [], []
        for p in range(2):
            head = 2 * t + p
            s = jnp.where(mask, logits[p], -jnp.inf)
            sink = sinks_ref[layer, head]
            m = jnp.maximum(jnp.max(s, axis=-1, keepdims=True), sink)
            pe = jnp.exp(s - m).astype(BF16)
            sink_terms.append(jnp.exp(sink - m))
            outs.append(_dot(pe, vvar_scr[2 * g + p, r0:r0 + 2 * WINDOW, :]))
        weighted = jnp.where(lo64_w, outs[0], outs[1])
        row_sums = pltpu.roll(jnp.where(lo64_w, outs[1], outs[0]), 64, 1)
        denom = row_sums + jnp.where(lo64_w, sink_terms[0], sink_terms[1])
        a_scr[r0:r0 + WINDOW, t * LANES:(t + 1) * LANES] = weighted * (1.0 / denom)

    n = 0
    for i in range(nsub):
        for t in range(ATT_WIDTH // LANES):
            logits = attention_logits(i, t)
            if n < len(att_companions):
                att_companions[n]()
            if n == 2:
                cast_next_layer_weights()
            n += 1
            attention_values(i, t, logits)
    for run in att_companions[n:]:
        run()

    kvar_scr[:, 0:WINDOW, :] = kvar_scr[:, tm:tm + WINDOW, :]
    vvar_scr[:, 0:WINDOW, :] = vvar_scr[:, tm:tm + WINDOW, :]

    def ret_slices(c, h):
        rows = slice(c * RET_CHUNK, (c + 1) * RET_CHUNK)
        return rows, slice(h * RET_DK, (h + 1) * RET_DK), slice(h * RET_DV, (h + 1) * RET_DV)

    def retention_scores_and_state(c):
        inners, s_olds = [], []
        for h in range(RET_HEADS):
            rows, sk, sv = ret_slices(c, h)
            qh = rq_scr[rows, sk]
            kh = rk_scr[rows, sk]
            vh = rv_scr[rows, sv]
            inners.append(lax.dot_general(qh.astype(BF16), kh.astype(BF16), NT, preferred_element_type=F32))
            s_old = s_scr[h]
            s_olds.append(s_old.astype(BF16))
            kd = (kh * kdec_ref[h]).astype(BF16)
            s_scr[h] = s_old * CHUNK_DECAY[h] + lax.dot_general(
                kd, vh, (((0,), (0,)), ((), ())), preferred_element_type=F32)
        return inners, s_olds

    def retention_outputs(c, inners, s_olds):
        for h in range(RET_HEADS):
            rows, sk, sv = ret_slices(c, h)
            qh = rq_scr[rows, sk]
            lhs = jnp.concatenate([(inners[h] * dmask_ref[h]).astype(BF16),
                                   (qh * qdec_ref[h]).astype(BF16)], axis=1)
            rhs = jnp.concatenate([rv_scr[rows, sv], s_olds[h]], axis=0)
            o = _dot(lhs, rhs)
            ms = jnp.mean(o * o, axis=-1, keepdims=True)
            on = o * lax.rsqrt(ms + RMS_EPS)
            r_scr[rows, sv] = on

    ret_companions = [half_chunk(C_MR, mr_scr, raw, 0), half_chunk(C_MR, mr_scr, raw, 1)]
    n = 0
    for c in range(nsub):
        inners, s_olds = retention_scores_and_state(c)
        if n < len(ret_companions):
            ret_companions[n]()
        n += 1
        retention_outputs(c, inners, s_olds)
    for run in ret_companions[n:]:
        run()
    sample_qmats, sample_logits = _sample_logits(sqkv_ref, ck_ref)
    for b in range(sqkv_ref.shape[0]):
        for h in range(RET_HEADS):
            _sample_state_update(sret_ref, st_ref, ns_ref, b, h)

    lng = lng_ref[...]
    lnb = lnb_ref[...]
    res_gate = mod_ref[0:1, 2 * D_MODEL:3 * D_MODEL] * RESIDUAL_GATE_SCALE
    windows = [slice(i * WINDOW, (i + 1) * WINDOW) for i in range(nsub)]
    za_all = _dot((a_scr[...] * _silu_of_half(ga_scr[...])).astype(BF16), wpa_ref[...])
    zr_all = _dot((r_scr[...] * _silu_of_half(rg_scr[...])).astype(BF16), wpr_ref[...])
    for rows in windows:
        z2 = (_twice_sigmoid_of_half(ma_scr[rows, :]) * za_all[rows, :]
              + _twice_sigmoid_of_half(mr_scr[rows, :]) * zr_all[rows, :])
        u2 = _dot(z2.astype(BF16), wout_ref[...])
        t = x_ref[rows, :] + res_gate * u2
        mu = jnp.mean(t, axis=-1, keepdims=True)
        d = t - mu
        var = jnp.mean(d * d, axis=-1, keepdims=True)
        y_ref[rows, :] = d * lax.rsqrt(var + POST_NORM_EPS) * lng + lnb

    _sample_outputs(sinks_ref, layer, sqkv_ref, sret_ref, ck_ref, cv_ref, sample_qmats, sample_logits,
                    oatt_ref, oret_ref, nk_ref, nv_ref, ns_ref)

    @pl.when(step == pl.num_programs(0) - 1)
    def _():
        snew_ref[...] = s_scr[...]


def _fused_layer(layer, x, mod, mod_row_block, base, rtab, weights_b, next_weights,
                 ln_g, ln_b, sinks, ret_tabs, sample_qkv, sample_ret, cache_k, cache_v, state, shared_out):
    w_in_b, w_pa_b, w_pr_b, w_out_b = weights_b
    seq = x.shape[0]
    tm = PROMPT_ROWS
    steps = seq // tm
    nbatch = sample_qkv.shape[0]
    nb = nbatch // steps
    win = cache_k.shape[3]
    assert win == WINDOW == LANES
    dmask, qdec, kdec = ret_tabs
    smem = pl.BlockSpec(memory_space=pltpu.SMEM)
    row_spec = lambda w: pl.BlockSpec((tm, w), lambda i: (i, 0))
    seq_spec = lambda w: pl.BlockSpec((nb, 1, w), lambda i: (i, 0, 0))
    cache_spec = pl.BlockSpec((None, nb, ATT_KV_WIDTH, win), lambda i: (layer, i, 0, 0))
    state_spec = pl.BlockSpec((None, nb, RET_HEADS, RET_DK, RET_DV), lambda i: (layer, i, 0, 0, 0))
    in_specs = [
        smem,
        row_spec(D_MODEL),
        pl.BlockSpec((None, SUBLANES, 3 * D_MODEL), lambda i: (layer, mod_row_block, 0),
                     pipeline_mode=pl.Buffered(1)),
        pl.BlockSpec((None, 4, LANES), lambda i: (i, 0, 0)),
        _const_spec((8, tm, LANES)),
        _const_spec((D_MODEL, IN_COLS)),
        _const_spec((ATT_WIDTH, D_MODEL)),
        _const_spec((RET_WIDTH, D_MODEL)),
        _const_spec((D_MODEL, D_MODEL)),
        _layer_spec((1, D_MODEL), layer), _layer_spec((1, D_MODEL), layer),
        _const_spec((RET_HEADS, RET_CHUNK, RET_CHUNK)),
        _const_spec((RET_HEADS, RET_CHUNK, RET_DK)),
        _const_spec((RET_HEADS, RET_CHUNK, RET_DK)),
        seq_spec(C_AG), seq_spec(2 * RET_QK_WIDTH + RET_WIDTH),
        cache_spec, cache_spec, state_spec,
        _const_spec((1, IN_COLS)),
    ]
    args = [sinks, x, mod, base, rtab, w_in_b, w_pa_b, w_pr_b, w_out_b, ln_g, ln_b, dmask, qdec, kdec,
            sample_qkv, sample_ret, cache_k, cache_v, state, _gate_column_scale()]
    assert len(args) == N_PROMPT_INPUTS
    cast_shapes, cast_specs = [], []
    if next_weights is not None:
        assert len(next_weights) == N_WEIGHTS
        for w in next_weights:
            _, rows, cols = w.shape
            slab = max(BF16_SUBLANES, rows // steps)
            nslab = rows // slab
            assert rows % slab == 0 and nslab <= steps
            args.append(w)
            in_specs.append(pl.BlockSpec((None, slab, cols),
                                         lambda i, n=nslab: (layer + 1, jnp.minimum(i, n - 1), 0)))
            cast_shapes.append(jax.ShapeDtypeStruct((rows, cols), BF16))
            cast_specs.append(pl.BlockSpec((slab, cols), lambda i, n=nslab: (jnp.minimum(i, n - 1), 0)))
    aliases = {}
    if shared_out is not None:
        for j, arr in enumerate(shared_out):
            aliases[len(args)] = N_PROMPT_OUTPUTS - len(shared_out) + j
            args.append(arr)
            in_specs.append(pl.BlockSpec(memory_space=pl.ANY))
    return pl.pallas_call(
        functools.partial(_prompt_kernel, layer=layer, n_aliased=len(aliases), n_cast=len(cast_shapes)),
        out_shape=(
            jax.ShapeDtypeStruct((seq, D_MODEL), F32),
            jax.ShapeDtypeStruct((WINDOW, ATT_KV_WIDTH), F32),
            jax.ShapeDtypeStruct((WINDOW, ATT_KV_WIDTH), F32),
            jax.ShapeDtypeStruct((RET_HEADS, RET_DK, RET_DV), F32),
            jax.ShapeDtypeStruct((nbatch, 1, ATT_WIDTH), F32),
            jax.ShapeDtypeStruct((nbatch, 1, RET_WIDTH), F32),
            jax.ShapeDtypeStruct(cache_k.shape, F32),
            jax.ShapeDtypeStruct(cache_v.shape, F32),
            jax.ShapeDtypeStruct(state.shape, F32),
        ) + tuple(cast_shapes),
        grid=(steps,),
        in_specs=in_specs,
        out_specs=(
            row_spec(D_MODEL),
            pl.BlockSpec((WINDOW, ATT_KV_WIDTH), lambda i: (0, 0)),
            pl.BlockSpec((WINDOW, ATT_KV_WIDTH), lambda i: (0, 0)),
            pl.BlockSpec((RET_HEADS, RET_DK, RET_DV), lambda i: (0, 0, 0)),
            seq_spec(ATT_WIDTH), seq_spec(RET_WIDTH),
            cache_spec, cache_spec, state_spec,
        ) + tuple(cast_specs),
        input_output_aliases=aliases,
        scratch_shapes=[
            pltpu.VMEM((4, tm, LANES), F32),
            pltpu.VMEM((tm, D_MODEL), BF16),
            pltpu.VMEM((tm, ATT_WIDTH), BF16),
            pltpu.VMEM((4, WINDOW + tm, LANES), BF16),
            pltpu.VMEM((4, WINDOW + tm, LANES), BF16),
            pltpu.VMEM((tm, ATT_WIDTH), F32),
            pltpu.VMEM((tm, ATT_WIDTH), F32),
            pltpu.VMEM((tm, RET_QK_WIDTH), F32),
            pltpu.VMEM((tm, RET_QK_WIDTH), F32),
            pltpu.VMEM((tm, RET_WIDTH), BF16),
            pltpu.VMEM((tm, RET_WIDTH), F32),
            pltpu.VMEM((tm, RET_WIDTH), F32),
            pltpu.VMEM((RET_HEADS, RET_DK, RET_DV), F32),
            pltpu.VMEM((tm, D_MODEL), F32),
            pltpu.VMEM((tm, D_MODEL), F32),
        ],
        compiler_params=pltpu.CompilerParams(
            dimension_semantics=("arbitrary",), vmem_limit_bytes=VMEM_LIMIT_BYTES),
        name="fused_layer",
    )(*args)


def _sample_proj_kernel(x_ref, mod_ref, tab_ref, win_ref,
                        qkv_ref, ga_ref, ret_ref, rg_ref, mg_ref):
    rows = x_ref.shape[0]
    x = x_ref[:, 0, :]
    shift = mod_ref[:, 0:D_MODEL]
    scale = mod_ref[:, D_MODEL:2 * D_MODEL]
    hb = (x * (1.0 + scale) + shift).astype(BF16)

    def proj(c0, c1):
        return _dot(hb, win_ref[:, c0:c1])

    lane = _lane_iota((rows, LANES))
    first_half32 = (lane & 32) == 0
    ca = tab_ref[0:1, :]
    sa = tab_ref[1:2, :]
    cr = tab_ref[2:3, :]
    sr = tab_ref[3:4, :]

    qkv = proj(C_AQ, C_AG)
    for t in range(ATT_WIDTH // LANES):
        qt = _rope_attn_tile(qkv[:, t * LANES:(t + 1) * LANES], ca, sa, first_half32)
        qkv_ref[:, 0, t * LANES:(t + 1) * LANES] = qt * ATT_SCALE
    qkv_ref[:, 0, C_AK:C_AV] = _rope_attn_tile(qkv[:, C_AK:C_AV], ca, sa, first_half32)
    qkv_ref[:, 0, C_AV:C_AG] = qkv[:, C_AV:C_AG]
    ga_ref[...] = _silu_of_half(proj(C_AG, C_RQ))

    rqk = proj(C_RQ, C_RV)
    for h in range(RET_HEADS):
        sl = slice(h * RET_DK, (h + 1) * RET_DK)
        ret_ref[:, 0, sl] = _rope_ret_tile(rqk[:, sl], cr, sr)
        sk = slice(RET_QK_WIDTH + h * RET_DK, RET_QK_WIDTH + (h + 1) * RET_DK)
        ret_ref[:, 0, sk] = _rope_ret_tile(rqk[:, sk], cr, sr) * RET_K_SCALE
    ret_ref[:, 0, 2 * RET_QK_WIDTH:] = proj(C_RV, C_RG)
    rg_ref[...] = _silu_of_half(proj(C_RG, C_MA))
    mg_ref[...] = _twice_sigmoid_of_half(proj(C_MA, IN_COLS))


def _sample_proj(layer, x, mod, tab, w_in_b):
    rows = x.shape[0]
    shapes = ((rows, 1, C_AG), (rows, ATT_WIDTH), (rows, 1, 2 * RET_QK_WIDTH + RET_WIDTH),
              (rows, RET_WIDTH), (rows, 2 * D_MODEL))
    return pl.pallas_call(
        _sample_proj_kernel,
        out_shape=tuple(jax.ShapeDtypeStruct(s, F32) for s in shapes),
        grid=(1,),
        in_specs=[
            _const_spec((rows, 1, D_MODEL)),
            _layer_spec((rows, 3 * D_MODEL), layer),
            _const_spec((4, LANES)),
            _const_spec((D_MODEL, IN_COLS)),
        ],
        out_specs=tuple(pl.BlockSpec(s, lambda i, nd=len(s): (0,) * nd) for s in shapes),
        compiler_params=pltpu.CompilerParams(
            dimension_semantics=("arbitrary",), vmem_limit_bytes=VMEM_LIMIT_BYTES),
        name="sample_proj",
    )(x, mod, tab, w_in_b)


def _sample_logits(qkv_ref, ck_ref):
    nb = qkv_ref.shape[0]
    rowi = lax.broadcasted_iota(jnp.int32, (ATT_HEADS, LANES), 0)
    lanei = lax.broadcasted_iota(jnp.int32, (ATT_HEADS, LANES), 1)
    lane_group = lanei // ATT_HEAD_DIM
    qmats, logits = [], []
    for b in range(nb):
        qkv = qkv_ref[b]
        qmat = jnp.zeros((ATT_HEADS, LANES), F32)
        for h in range(ATT_HEADS):
            t, p, g = h // 2, h % 2, h // ATT_GROUP
            tile = qkv[:, t * LANES:(t + 1) * LANES]
            src = tile if p == g else pltpu.roll(tile, 64, 1)
            qmat = jnp.where((rowi == h) & (lane_group == g), jnp.broadcast_to(src, (ATT_HEADS, LANES)), qmat)
        qmats.append(qmat)
        logits.append(_dot(qmat.astype(BF16), ck_ref[b].astype(BF16)))
    return qmats, logits


def _sample_state_update(ret_ref, st_ref, ns_ref, b, h):
    dr = lax.broadcasted_iota(jnp.int32, (RET_DK, RET_DK), 0)
    dc = lax.broadcasted_iota(jnp.int32, (RET_DK, RET_DK), 1)
    ret = ret_ref[b]
    kh = ret[:, RET_QK_WIDTH + h * RET_DK:RET_QK_WIDTH + (h + 1) * RET_DK]
    vh = ret[:, 2 * RET_QK_WIDTH + h * RET_DV:2 * RET_QK_WIDTH + (h + 1) * RET_DV]
    k_col = jnp.sum(jnp.where(dr == dc, jnp.broadcast_to(kh, (RET_DK, RET_DK)), 0.0),
                    axis=1, keepdims=True)
    ns_ref[b, h] = st_ref[b, h] * TOKEN_DECAY[h] + k_col * vh


def _sample_outputs(sinks_ref, layer, qkv_ref, ret_ref, ck_ref, cv_ref, qmats, logits,
                    oatt_ref, oret_ref, nk_ref, nv_ref, ns_ref):
    nb = qkv_ref.shape[0]
    sink_col = jnp.zeros((ATT_HEADS, 1), F32)
    rowc = lax.broadcasted_iota(jnp.int32, (ATT_HEADS, 1), 0)
    for h in range(ATT_HEADS):
        sink_col = jnp.where(rowc == h, sinks_ref[layer, h], sink_col)
    lo64_row = _lane_iota((1, LANES)) < 64
    dr = lax.broadcasted_iota(jnp.int32, (LANES, LANES), 0)
    dc = lax.broadcasted_iota(jnp.int32, (LANES, LANES), 1)
    diag = dr == dc
    last_lane = dc == WINDOW - 1

    for b in range(nb):
        qkv = qkv_ref[b]
        k_new = qkv[:, C_AK:C_AV]
        v_new = qkv[:, C_AV:C_AG]
        s_c = logits[b]
        s_self = jnp.sum(qmats[b] * k_new, axis=-1, keepdims=True)
        m = jnp.maximum(jnp.maximum(jnp.max(s_c, axis=-1, keepdims=True), s_self), sink_col)
        p_c = jnp.exp(s_c - m)
        p_self = jnp.exp(s_self - m)
        denom = jnp.sum(p_c, axis=-1, keepdims=True) + p_self + jnp.exp(sink_col - m)
        o = (lax.dot_general(p_c.astype(BF16), cv_ref[b].astype(BF16), NT, preferred_element_type=F32)
             + p_self * v_new) / denom
        o_sw = pltpu.roll(o, 64, 1)
        for t in range(ATT_WIDTH // LANES):
            g = t // 2
            first = (o if g == 0 else o_sw)[2 * t:2 * t + 1, :]
            second = (o_sw if g == 0 else o)[2 * t + 1:2 * t + 2, :]
            oatt_ref[b, :, t * LANES:(t + 1) * LANES] = jnp.where(lo64_row, first, second)

        for new_row, src_ref, dst_ref in ((k_new, ck_ref, nk_ref), (v_new, cv_ref, nv_ref)):
            new_col = jnp.sum(jnp.where(diag, jnp.broadcast_to(new_row, (LANES, LANES)), 0.0),
                              axis=1, keepdims=True)
            shifted = pltpu.roll(src_ref[b], WINDOW - 1, 1)
            dst_ref[b] = jnp.where(last_lane, new_col, shifted)

    for b in range(nb):
        ret = ret_ref[b]
        for h in range(RET_HEADS):
            qh = ret[:, h * RET_DK:(h + 1) * RET_DK]
            q8 = jnp.broadcast_to(qh, (SUBLANES, RET_DK)).astype(BF16)
            oh = _dot(q8, ns_ref[b, h].astype(BF16))[0:1, :]
            ms = jnp.mean(oh * oh, axis=-1, keepdims=True)
            oret_ref[b, :, h * RET_DV:(h + 1) * RET_DV] = oh * lax.rsqrt(ms + RMS_EPS)


def _sample_out_kernel(x_ref, mod_ref, oatt_ref, ga_ref, oret_ref, rg_ref, mg_ref,
                       wpa_ref, wpr_ref, wout_ref, lng_ref, lnb_ref, y_ref):
    res_gate = mod_ref[:, 2 * D_MODEL:3 * D_MODEL] * RESIDUAL_GATE_SCALE
    za = _dot((oatt_ref[:, 0, :] * ga_ref[...]).astype(BF16), wpa_ref[...])
    zr = _dot((oret_ref[:, 0, :] * rg_ref[...]).astype(BF16), wpr_ref[...])
    z2 = mg_ref[:, 0:D_MODEL] * za + mg_ref[:, D_MODEL:2 * D_MODEL] * zr
    u2 = _dot(z2.astype(BF16), wout_ref[...])
    t = x_ref[:, 0, :] + res_gate * u2
    mu = jnp.mean(t, axis=-1, keepdims=True)
    d = t - mu
    var = jnp.mean(d * d, axis=-1, keepdims=True)
    y_ref[:, 0, :] = d * lax.rsqrt(var + POST_NORM_EPS) * lng_ref[...] + lnb_ref[...]


def _sample_out(layer, x, mod, oatt, ga, oret, rg, mg, w_pa_b, w_pr_b, w_out_b, ln_g, ln_b):
    rows = x.shape[0]
    return pl.pallas_call(
        _sample_out_kernel,
        out_shape=jax.ShapeDtypeStruct(x.shape, F32),
        grid=(1,),
        in_specs=[
            _const_spec((rows, 1, D_MODEL)),
            _layer_spec((rows, 3 * D_MODEL), layer),
            _const_spec((rows, 1, ATT_WIDTH)), _const_spec((rows, ATT_WIDTH)),
            _const_spec((rows, 1, RET_WIDTH)), _const_spec((rows, RET_WIDTH)),
            _const_spec((rows, 2 * D_MODEL)),
            _const_spec((ATT_WIDTH, D_MODEL)),
            _const_spec((RET_WIDTH, D_MODEL)),
            _const_spec((D_MODEL, D_MODEL)),
            _layer_spec((1, D_MODEL), layer), _layer_spec((1, D_MODEL), layer),
        ],
        out_specs=pl.BlockSpec((rows, 1, D_MODEL), lambda i: (0, 0, 0)),
        compiler_params=pltpu.CompilerParams(
            dimension_semantics=("arbitrary",), vmem_limit_bytes=VMEM_LIMIT_BYTES),
        name="sample_out",
    )(x, mod, oatt, ga, oret, rg, mg, w_pa_b, w_pr_b, w_out_b, ln_g, ln_b)


def kernel(x_prompt, x_sample, c_prompt, c_sample, cache_k, cache_v, state_ret, w_in, attn_sinks,
           w_cond, b_cond, w_proj_attn, w_proj_ret, w_out, ln_g, ln_b):
    seq = x_prompt.shape[1]
    nbatch = x_sample.shape[0]
    win = cache_k.shape[2]
    assert seq % PROMPT_ROWS == 0 and nbatch % (seq // PROMPT_ROWS) == 0 and nbatch % SUBLANES == 0

    weights_f32 = (w_in, w_proj_attn, w_proj_ret, w_out)
    weights_b = ((w_in[0] * _gate_column_scale()).astype(BF16),) + tuple(
        w[0].astype(BF16) for w in weights_f32[1:])
    ln_g3 = ln_g.reshape(DEPTH, 1, D_MODEL)
    ln_b3 = ln_b.reshape(DEPTH, 1, D_MODEL)

    c_all = jnp.concatenate([c_sample, c_prompt, jnp.zeros((SUBLANES - 1, D_MODEL), F32)], axis=0)
    mod = _cond_call(c_all, w_cond, b_cond)
    prompt_mod_block = nbatch // SUBLANES

    base, rtab = _prompt_rope_tables(seq, PROMPT_ROWS)
    stab = _sample_rope_table(float(PAST_LEN))
    ret_tabs = _retention_tables()

    def feature_major(c):
        return c.transpose(0, 1, 3, 4, 2).reshape(DEPTH, nbatch, ATT_KV_WIDTH, win)

    def window_major(c):
        return c.reshape(DEPTH, nbatch, ATT_KV_HEADS, ATT_HEAD_DIM, win).transpose(0, 1, 4, 2, 3)

    ck = feature_major(cache_k)
    cv = feature_major(cache_v)

    yp = x_prompt[0]
    ys = x_sample
    kp, vp, sp = [], [], []
    prev = None
    for l in range(DEPTH):
        w_in_b, w_pa_b, w_pr_b, w_out_b = weights_b
        qkv, ga, ret, rg, mg = _sample_proj(l, ys, mod, stab, w_in_b)
        outs = _fused_layer(
            l, yp, mod, prompt_mod_block, base, rtab, weights_b,
            weights_f32 if l + 1 < DEPTH else None,
            ln_g3, ln_b3, attn_sinks, ret_tabs, qkv, ret, ck, cv, state_ret, prev)
        yp, k_new, v_new, s_new, oatt, oret, nk, nv, ns = outs[:N_PROMPT_OUTPUTS]
        kp.append(k_new.reshape(1, WINDOW, ATT_KV_HEADS, ATT_HEAD_DIM))
        vp.append(v_new.reshape(1, WINDOW, ATT_KV_HEADS, ATT_HEAD_DIM))
        sp.append(s_new[None])
        prev = (nk, nv, ns)
        ys = _sample_out(l, ys, mod, oatt, ga, oret, rg, mg, w_pa_b, w_pr_b, w_out_b, ln_g3, ln_b3)
        weights_b = tuple(outs[N_PROMPT_OUTPUTS:])

    nk, nv, ns = prev
    return (yp[None], ys, jnp.stack(kp), jnp.stack(vp), jnp.stack(sp),
            window_major(nk), window_major(nv), ns)
```

```python
import functools

import jax
import jax.numpy as jnp
import numpy as np
from jax import lax
from jax.experimental import pallas as pl
from jax.experimental.pallas import tpu as pltpu

D_MODEL = 1024
DEPTH = 2
PAST_LEN = 16384
ATT_HEADS = 8
ATT_KV_HEADS = 2
ATT_HEAD_DIM = 64
ATT_GROUP = ATT_HEADS // ATT_KV_HEADS
ATT_WIDTH = ATT_HEADS * ATT_HEAD_DIM
ATT_KV_WIDTH = ATT_KV_HEADS * ATT_HEAD_DIM
WINDOW = 128
RET_HEADS = 4
RET_DK = 128
RET_DV = 256
RET_QK_WIDTH = RET_HEADS * RET_DK
RET_WIDTH = RET_HEADS * RET_DV
RET_CHUNK = 128
ROPE_THETA = 10000.0
ALPHA = (2.0 * DEPTH) ** 0.25
LN_EPS = 1e-5
RMS_EPS = 1e-6
ATT_SCALE = ATT_HEAD_DIM ** -0.5
RET_K_SCALE = RET_DK ** -0.5

C_AQ = 0
C_AK = C_AQ + ATT_WIDTH
C_AV = C_AK + ATT_KV_WIDTH
C_AG = C_AV + ATT_KV_WIDTH
C_RQ = C_AG + ATT_WIDTH
C_RK = C_RQ + RET_QK_WIDTH
C_RV = C_RK + RET_QK_WIDTH
C_RG = C_RV + RET_WIDTH
C_MA = C_RG + RET_WIDTH
C_MR = C_MA + D_MODEL
IN_COLS = C_MR + D_MODEL

LANES = 128
SUBLANES = 8
BF16_SUBLANES = 16
VMEM_LIMIT_BYTES = 56 * 1024 * 1024

PROMPT_ROWS = 256

BF16 = jnp.bfloat16
F32 = jnp.float32
NT = (((1,), (1,)), ((), ()))

_LOG_GAMMA = np.log(1.0 - 2.0 ** (-5.0 - np.arange(RET_HEADS, dtype=np.float64)))
CHUNK_DECAY = tuple(float(v) for v in np.exp(RET_CHUNK * _LOG_GAMMA))
TOKEN_DECAY = tuple(float(v) for v in np.exp(_LOG_GAMMA))


def _sigmoid(x):
    return 0.5 * jnp.tanh(0.5 * x) + 0.5


def _silu(x):
    return x * _sigmoid(x)


def _dot(a, b):
    return jnp.dot(a, b, preferred_element_type=F32)


def _rope_attn_tile(x, cos, sin_signed, first_half):
    rot = jnp.where(first_half, pltpu.roll(x, LANES - 32, 1), pltpu.roll(x, 32, 1))
    return x * cos + rot * sin_signed


def _rope_ret_tile(x, cos, sin_signed):
    return x * cos + pltpu.roll(x, 64, 1) * sin_signed


def _lane_iota(shape):
    return lax.broadcasted_iota(jnp.int32, shape, len(shape) - 1)


def _layer_spec(shape, layer):
    nd = len(shape)
    return pl.BlockSpec((None,) + tuple(shape), lambda i: (layer,) + (0,) * nd,
                        pipeline_mode=pl.Buffered(1))


def _const_spec(shape):
    nd = len(shape)
    return pl.BlockSpec(tuple(shape), lambda i: (0,) * nd, pipeline_mode=pl.Buffered(1))


def _rope_lane_patterns():
    lane = np.arange(LANES)
    f_att = ROPE_THETA ** (-(lane % 32) / 32.0)
    s_att = np.where(lane % 64 < 32, -1.0, 1.0)
    f_ret = ROPE_THETA ** (-(lane % 64) / 64.0)
    s_ret = np.where(lane < 64, -1.0, 1.0)
    return (f_att, s_att), (f_ret, s_ret)


def _prompt_rope_tables(seq, tm):
    starts = np.arange(seq // tm, dtype=np.float64)[:, None] * tm
    offs = np.arange(tm, dtype=np.float64)[:, None]
    base, within = [], []
    for freq, sign in _rope_lane_patterns():
        base += [np.cos(starts * freq), np.sin(starts * freq)]
        c, s = np.cos(offs * freq), np.sin(offs * freq)
        within += [c, s, sign * c, sign * s]
    return (jnp.asarray(np.stack(base, axis=1), F32),
            jnp.asarray(np.stack(within, axis=0), F32))


def _sample_rope_table(pos):
    rows = []
    for freq, sign in _rope_lane_patterns():
        rows += [np.cos(pos * freq), sign * np.sin(pos * freq)]
    return jnp.asarray(np.stack(rows, axis=0), F32)


def _retention_tables():
    c = RET_CHUNK
    idx = np.arange(c, dtype=np.float64)
    diff = idx[:, None] - idx[None, :]
    lg = _LOG_GAMMA[:, None, None]
    dmask = np.where(diff >= 0, np.exp(np.maximum(diff, 0.0)[None] * lg), 0.0)
    qdec = np.broadcast_to(np.exp((idx + 1.0)[None, :, None] * lg), (RET_HEADS, c, RET_DK))
    kdec = np.broadcast_to(np.exp((c - 1.0 - idx)[None, :, None] * lg), (RET_HEADS, c, RET_DK))
    return jnp.asarray(dmask, F32), jnp.asarray(qdec, F32), jnp.asarray(kdec, F32)


def _cond_kernel(c_ref, w_ref, b_ref, o_ref):
    a = _silu(c_ref[...]).astype(BF16)
    o_ref[...] = _dot(a, w_ref[...].astype(BF16)) + b_ref[...]


def _cond_call(c_all, w_cond, b_cond):
    rows = c_all.shape[0]
    tn = 3 * D_MODEL // 2
    return pl.pallas_call(
        _cond_kernel,
        out_shape=jax.ShapeDtypeStruct((DEPTH, rows, 3 * D_MODEL), F32),
        grid=(DEPTH, 3 * D_MODEL // tn),
        in_specs=[
            pl.BlockSpec((rows, D_MODEL), lambda l, j: (0, 0)),
            pl.BlockSpec((None, D_MODEL, tn), lambda l, j: (l, 0, j)),
            pl.BlockSpec((None, 1, tn), lambda l, j: (l, 0, j)),
        ],
        out_specs=pl.BlockSpec((None, rows, tn), lambda l, j: (l, 0, j)),
        compiler_params=pltpu.CompilerParams(
            dimension_semantics=("arbitrary", "arbitrary"), vmem_limit_bytes=VMEM_LIMIT_BYTES),
        name="cond_mod",
    )(c_all, w_cond, b_cond.reshape(DEPTH, 1, 3 * D_MODEL))


N_PROMPT_INPUTS = 19
N_PROMPT_OUTPUTS = 9
N_WEIGHTS = 4


def _prompt_kernel(*refs, layer, n_aliased, n_cast):
    (sinks_ref, x_ref, mod_ref, base_ref, rtab_ref,
     win_ref, wpa_ref, wpr_ref, wout_ref, lng_ref, lnb_ref,
     dmask_ref, qdec_ref, kdec_ref,
     sqkv_ref, sret_ref, ck_ref, cv_ref, st_ref) = refs[:N_PROMPT_INPUTS]
    cast_in = refs[N_PROMPT_INPUTS:N_PROMPT_INPUTS + n_cast]
    refs = refs[N_PROMPT_INPUTS + n_cast + n_aliased:]
    (y_ref, knew_ref, vnew_ref, snew_ref,
     oatt_ref, oret_ref, nk_ref, nv_ref, ns_ref) = refs[:N_PROMPT_OUTPUTS]
    cast_out = refs[N_PROMPT_OUTPUTS:N_PROMPT_OUTPUTS + n_cast]
    (tab_scr, hb_scr, q_scr, kvar_scr, vvar_scr, ga_scr, a_scr,
     rq_scr, rk_scr, rv_scr, rg_scr, r_scr, s_scr,
     ma_scr, mr_scr) = refs[N_PROMPT_OUTPUTS + n_cast:]
    step = pl.program_id(0)

    def cast_next_layer_weights():
        for src, dst in zip(cast_in, cast_out):
            dst[...] = src[...].astype(BF16)

    tm = x_ref.shape[0]
    nsub = tm // WINDOW

    def proj(c0, c1):
        return _dot(hb_scr[...], win_ref[:, c0:c1])

    lane = _lane_iota((tm, LANES))
    first_half32 = (lane & 32) == 0
    lo64 = lane < 64

    def rope_attn(t):
        return _rope_attn_tile(t, tab_scr[0], tab_scr[1], first_half32)

    def rope_ret(t):
        return _rope_ret_tile(t, tab_scr[2], tab_scr[3])

    def store_variants(scr, t, fill):
        swapped = pltpu.roll(t, 64, 1)
        other = jnp.full_like(t, fill)
        scr[0, WINDOW:WINDOW + tm, :] = jnp.where(lo64, t, other).astype(BF16)
        scr[1, WINDOW:WINDOW + tm, :] = jnp.where(lo64, other, swapped).astype(BF16)
        scr[2, WINDOW:WINDOW + tm, :] = jnp.where(lo64, swapped, other).astype(BF16)
        scr[3, WINDOW:WINDOW + tm, :] = jnp.where(lo64, other, t).astype(BF16)

    def block_prepare(src_x_ref, src_base_ref):
        shift = mod_ref[0:1, 0:D_MODEL]
        scale = mod_ref[0:1, D_MODEL:2 * D_MODEL]
        hb_scr[...] = (src_x_ref[...] * (1.0 + scale) + shift).astype(BF16)
        for fam in range(2):
            cb = src_base_ref[2 * fam:2 * fam + 1, :]
            sb = src_base_ref[2 * fam + 1:2 * fam + 2, :]
            tab_scr[2 * fam] = cb * rtab_ref[4 * fam] - sb * rtab_ref[4 * fam + 1]
            tab_scr[2 * fam + 1] = sb * rtab_ref[4 * fam + 2] + cb * rtab_ref[4 * fam + 3]

    def block_head():
        kv = proj(C_AK, C_AG)
        k_rot = rope_attn(kv[:, 0:LANES])
        v_raw = kv[:, LANES:2 * LANES]
        knew_ref[...] = k_rot[tm - WINDOW:, :]
        vnew_ref[...] = v_raw[tm - WINDOW:, :]
        store_variants(kvar_scr, k_rot, 0.0)
        store_variants(vvar_scr, v_raw, 1.0)
        qp = proj(C_AQ, C_AK)
        for t in range(ATT_WIDTH // LANES):
            qt = rope_attn(qp[:, t * LANES:(t + 1) * LANES])
            q_scr[:, t * LANES:(t + 1) * LANES] = (qt * ATT_SCALE).astype(BF16)
        ga_scr[...] = proj(C_AG, C_RQ)

    @pl.when(step == 0)
    def _():
        kvar_scr[:, 0:WINDOW, :] = jnp.zeros((4, WINDOW, LANES), BF16)
        vvar_scr[:, 0:WINDOW, :] = jnp.zeros((4, WINDOW, LANES), BF16)
        s_scr[...] = jnp.zeros(s_scr.shape, F32)

    block_prepare(x_ref, base_ref)
    block_head()

    def chunk_rq():
        p = proj(C_RQ, C_RK)
        for h in range(RET_HEADS):
            sl = slice(h * RET_DK, (h + 1) * RET_DK)
            rq_scr[:, sl] = rope_ret(p[:, sl])

    def chunk_rk():
        p = proj(C_RK, C_RV)
        for h in range(RET_HEADS):
            sl = slice(h * RET_DK, (h + 1) * RET_DK)
            rk_scr[:, sl] = rope_ret(p[:, sl]) * RET_K_SCALE

    def half_chunk(c0, dst, fn, half):
        w = dst.shape[1] // 2
        def run():
            dst[:, half * w:(half + 1) * w] = fn(proj(c0 + half * w, c0 + (half + 1) * w))
        return run

    to_bf16 = lambda v: v.astype(BF16)
    raw = lambda v: v
    att_companions = [
        chunk_rq, chunk_rk,
        half_chunk(C_RV, rv_scr, to_bf16, 0), half_chunk(C_RV, rv_scr, to_bf16, 1),
        half_chunk(C_RG, rg_scr, raw, 0), half_chunk(C_RG, rg_scr, raw, 1),
        half_chunk(C_MA, ma_scr, raw, 0), half_chunk(C_MA, ma_scr, raw, 1),
    ]

    row = lax.broadcasted_iota(jnp.int32, (WINDOW, 2 * WINDOW), 0)
    col = lax.broadcasted_iota(jnp.int32, (WINDOW, 2 * WINDOW), 1)
    in_window = col <= row + WINDOW
    mask_std = (col >= row) & in_window
    off = jnp.where(step > 0, 0, 4 * WINDOW)
    mask_first = ((col >= row + off) | (col >= WINDOW)) & in_window
    lo64_w = _lane_iota((WINDOW, LANES)) < 64

    def attention_logits(i, t):
        r0 = i * WINDOW
        g = t // 2
        qt = q_scr[r0:r0 + WINDOW, t * LANES:(t + 1) * LANES]
        return [lax.dot_general(qt, kvar_scr[2 * g + p, r0:r0 + 2 * WINDOW, :], NT,
                                preferred_element_type=F32) for p in range(2)]

    def attention_values(i, t, logits):
        r0 = i * WINDOW
        mask = mask_first if i == 0 else mask_std
        g = t // 2
        outs, sink_terms = [], []
        for p in range(2):
            head = 2 * t + p
            s = jnp.where(mask, logits[p], -jnp.inf)
            sink = sinks_ref[layer, head]
            m = jnp.maximum(jnp.max(s, axis=-1, keepdims=True), sink)
            pe = jnp.exp(s - m).astype(BF16)
            sink_terms.append(jnp.exp(sink - m))
            outs.append(_dot(pe, vvar_scr[2 * g + p, r0:r0 + 2 * WINDOW, :]))
        weighted = jnp.where(lo64_w, outs[0], outs[1])
        row_sums = pltpu.roll(jnp.where(lo64_w, outs[1], outs[0]), 64, 1)
        denom = row_sums + jnp.where(lo64_w, sink_terms[0], sink_terms[1])
        a_scr[r0:r0 + WINDOW, t * LANES:(t + 1) * LANES] = weighted * (1.0 / denom)

    n_tiles = ATT_WIDTH // LANES
    per_window = -(-len(att_companions) // nsub)
    for i in range(nsub):
        logits = [attention_logits(i, t) for t in range(n_tiles)]
        for run in att_companions[i * per_window:(i + 1) * per_window]:
            run()
        if i == 0:
            cast_next_layer_weights()
        for t in range(n_tiles):
            attention_values(i, t, logits[t])

    kvar_scr[:, 0:WINDOW, :] = kvar_scr[:, tm:tm + WINDOW, :]
    vvar_scr[:, 0:WINDOW, :] = vvar_scr[:, tm:tm + WINDOW, :]

    def ret_slices(c, h):
        rows = slice(c * RET_CHUNK, (c + 1) * RET_CHUNK)
        return rows, slice(h * RET_DK, (h + 1) * RET_DK), slice(h * RET_DV, (h + 1) * RET_DV)

    def retention_scores_and_state(c):
        inners, s_olds = [], []
        for h in range(RET_HEADS):
            rows, sk, sv = ret_slices(c, h)
            qh = rq_scr[rows, sk]
            kh = rk_scr[rows, sk]
            vh = rv_scr[rows, sv]
            inners.append(lax.dot_general(qh.astype(BF16), kh.astype(BF16), NT, preferred_element_type=F32))
            s_old = s_scr[h]
            s_olds.append(s_old.astype(BF16))
            kd = (kh * kdec_ref[h]).astype(BF16)
            s_scr[h] = s_old * CHUNK_DECAY[h] + lax.dot_general(
                kd, vh, (((0,), (0,)), ((), ())), preferred_element_type=F32)
        return inners, s_olds

    def retention_outputs(c, inners, s_olds):
        for h in range(RET_HEADS):
            rows, sk, sv = ret_slices(c, h)
            qh = rq_scr[rows, sk]
            lhs = jnp.concatenate([(inners[h] * dmask_ref[h]).astype(BF16),
                                   (qh * qdec_ref[h]).astype(BF16)], axis=1)
            rhs = jnp.concatenate([rv_scr[rows, sv], s_olds[h]], axis=0)
            o = _dot(lhs, rhs)
            ms = jnp.mean(o * o, axis=-1, keepdims=True)
            on = o * lax.rsqrt(ms + RMS_EPS)
            r_scr[rows, sv] = on

    ret_companions = [half_chunk(C_MR, mr_scr, raw, 0), half_chunk(C_MR, mr_scr, raw, 1)]
    n = 0
    for c in range(nsub):
        inners, s_olds = retention_scores_and_state(c)
        if n < len(ret_companions):
            ret_companions[n]()
        n += 1
        retention_outputs(c, inners, s_olds)
    for run in ret_companions[n:]:
        run()
    sample_qmats, sample_logits = _sample_logits(sqkv_ref, ck_ref)
    for b in range(sqkv_ref.shape[0]):
        for h in range(RET_HEADS):
            _sample_state_update(sret_ref, st_ref, ns_ref, b, h)

    gate_c = mod_ref[0:1, 2 * D_MODEL:3 * D_MODEL]
    lng = lng_ref[...]
    lnb = lnb_ref[...]
    windows = [slice(i * WINDOW, (i + 1) * WINDOW) for i in range(nsub)]
    za_all = _dot((a_scr[...] * _silu(ga_scr[...])).astype(BF16), wpa_ref[...])
    zr_all = _dot((r_scr[...] * _silu(rg_scr[...])).astype(BF16), wpr_ref[...])
    for rows in windows:
        z = _sigmoid(ma_scr[rows, :]) * za_all[rows, :] + _sigmoid(mr_scr[rows, :]) * zr_all[rows, :]
        u = _dot(z.astype(BF16), wout_ref[...])
        t = ALPHA * x_ref[rows, :] + gate_c * u
        mu = jnp.mean(t, axis=-1, keepdims=True)
        d = t - mu
        var = jnp.mean(d * d, axis=-1, keepdims=True)
        y_ref[rows, :] = d * lax.rsqrt(var + LN_EPS) * lng + lnb

    _sample_outputs(sinks_ref, layer, sqkv_ref, sret_ref, ck_ref, cv_ref, sample_qmats, sample_logits,
                    oatt_ref, oret_ref, nk_ref, nv_ref, ns_ref)

    @pl.when(step == pl.num_programs(0) - 1)
    def _():
        snew_ref[...] = s_scr[...]


def _fused_layer(layer, x, mod, mod_row_block, base, rtab, weights_b, next_weights,
                 ln_g, ln_b, sinks, ret_tabs, sample_qkv, sample_ret, cache_k, cache_v, state, shared_out):
    w_in_b, w_pa_b, w_pr_b, w_out_b = weights_b
    seq = x.shape[0]
    tm = PROMPT_ROWS
    steps = seq // tm
    nbatch = sample_qkv.shape[0]
    nb = nbatch // steps
    win = cache_k.shape[3]
    assert win == WINDOW == LANES
    dmask, qdec, kdec = ret_tabs
    smem = pl.BlockSpec(memory_space=pltpu.SMEM)
    row_spec = lambda w: pl.BlockSpec((tm, w), lambda i: (i, 0))
    seq_spec = lambda w: pl.BlockSpec((nb, 1, w), lambda i: (i, 0, 0))
    cache_spec = pl.BlockSpec((None, nb, ATT_KV_WIDTH, win), lambda i: (layer, i, 0, 0))
    state_spec = pl.BlockSpec((None, nb, RET_HEADS, RET_DK, RET_DV), lambda i: (layer, i, 0, 0, 0))
    in_specs = [
        smem,
        row_spec(D_MODEL),
        pl.BlockSpec((None, SUBLANES, 3 * D_MODEL), lambda i: (layer, mod_row_block, 0),
                     pipeline_mode=pl.Buffered(1)),
        pl.BlockSpec((None, 4, LANES), lambda i: (i, 0, 0)),
        _const_spec((8, tm, LANES)),
        _const_spec((D_MODEL, IN_COLS)),
        _const_spec((ATT_WIDTH, D_MODEL)),
        _const_spec((RET_WIDTH, D_MODEL)),
        _const_spec((D_MODEL, D_MODEL)),
        _layer_spec((1, D_MODEL), layer), _layer_spec((1, D_MODEL), layer),
        _const_spec((RET_HEADS, RET_CHUNK, RET_CHUNK)),
        _const_spec((RET_HEADS, RET_CHUNK, RET_DK)),
        _const_spec((RET_HEADS, RET_CHUNK, RET_DK)),
        seq_spec(C_AG), seq_spec(2 * RET_QK_WIDTH + RET_WIDTH),
        cache_spec, cache_spec, state_spec,
    ]
    args = [sinks, x, mod, base, rtab, w_in_b, w_pa_b, w_pr_b, w_out_b, ln_g, ln_b, dmask, qdec, kdec,
            sample_qkv, sample_ret, cache_k, cache_v, state]
    assert len(args) == N_PROMPT_INPUTS
    cast_shapes, cast_specs = [], []
    if next_weights is not None:
        assert len(next_weights) == N_WEIGHTS
        for w in next_weights:
            _, rows, cols = w.shape
            slab = max(BF16_SUBLANES, rows // steps)
            nslab = rows // slab
            assert rows % slab == 0 and nslab <= steps
            args.append(w)
            in_specs.append(pl.BlockSpec((None, slab, cols),
                                         lambda i, n=nslab: (layer + 1, jnp.minimum(i, n - 1), 0)))
            cast_shapes.append(jax.ShapeDtypeStruct((rows, cols), BF16))
            cast_specs.append(pl.BlockSpec((slab, cols), lambda i, n=nslab: (jnp.minimum(i, n - 1), 0)))
    aliases = {}
    if shared_out is not None:
        for j, arr in enumerate(shared_out):
            aliases[len(args)] = N_PROMPT_OUTPUTS - len(shared_out) + j
            args.append(arr)
            in_specs.append(pl.BlockSpec(memory_space=pl.ANY))
    return pl.pallas_call(
        functools.partial(_prompt_kernel, layer=layer, n_aliased=len(aliases), n_cast=len(cast_shapes)),
        out_shape=(
            jax.ShapeDtypeStruct((seq, D_MODEL), F32),
            jax.ShapeDtypeStruct((WINDOW, ATT_KV_WIDTH), F32),
            jax.ShapeDtypeStruct((WINDOW, ATT_KV_WIDTH), F32),
            jax.ShapeDtypeStruct((RET_HEADS, RET_DK, RET_DV), F32),
            jax.ShapeDtypeStruct((nbatch, 1, ATT_WIDTH), F32),
            jax.ShapeDtypeStruct((nbatch, 1, RET_WIDTH), F32),
            jax.ShapeDtypeStruct(cache_k.shape, F32),
            jax.ShapeDtypeStruct(cache_v.shape, F32),
            jax.ShapeDtypeStruct(state.shape, F32),
        ) + tuple(cast_shapes),
        grid=(steps,),
        in_specs=in_specs,
        out_specs=(
            row_spec(D_MODEL),
            pl.BlockSpec((WINDOW, ATT_KV_WIDTH), lambda i: (0, 0)),
            pl.BlockSpec((WINDOW, ATT_KV_WIDTH), lambda i: (0, 0)),
            pl.BlockSpec((RET_HEADS, RET_DK, RET_DV), lambda i: (0, 0, 0)),
            seq_spec(ATT_WIDTH), seq_spec(RET_WIDTH),
            cache_spec, cache_spec, state_spec,
        ) + tuple(cast_specs),
        input_output_aliases=aliases,
        scratch_shapes=[
            pltpu.VMEM((4, tm, LANES), F32),
            pltpu.VMEM((tm, D_MODEL), BF16),
            pltpu.VMEM((tm, ATT_WIDTH), BF16),
            pltpu.VMEM((4, WINDOW + tm, LANES), BF16),
            pltpu.VMEM((4, WINDOW + tm, LANES), BF16),
            pltpu.VMEM((tm, ATT_WIDTH), F32),
            pltpu.VMEM((tm, ATT_WIDTH), F32),
            pltpu.VMEM((tm, RET_QK_WIDTH), F32),
            pltpu.VMEM((tm, RET_QK_WIDTH), F32),
            pltpu.VMEM((tm, RET_WIDTH), BF16),
            pltpu.VMEM((tm, RET_WIDTH), F32),
            pltpu.VMEM((tm, RET_WIDTH), F32),
            pltpu.VMEM((RET_HEADS, RET_DK, RET_DV), F32),
            pltpu.VMEM((tm, D_MODEL), F32),
            pltpu.VMEM((tm, D_MODEL), F32),
        ],
        compiler_params=pltpu.CompilerParams(
            dimension_semantics=("arbitrary",), vmem_limit_bytes=VMEM_LIMIT_BYTES),
        name="fused_layer",
    )(*args)


def _sample_proj_kernel(x_ref, mod_ref, tab_ref, win_ref,
                        qkv_ref, ga_ref, ret_ref, rg_ref, mg_ref):
    rows = x_ref.shape[0]
    x = x_ref[:, 0, :]
    shift = mod_ref[:, 0:D_MODEL]
    scale = mod_ref[:, D_MODEL:2 * D_MODEL]
    hb = (x * (1.0 + scale) + shift).astype(BF16)

    def proj(c0, c1):
        return _dot(hb, win_ref[:, c0:c1])

    lane = _lane_iota((rows, LANES))
    first_half32 = (lane & 32) == 0
    ca = tab_ref[0:1, :]
    sa = tab_ref[1:2, :]
    cr = tab_ref[2:3, :]
    sr = tab_ref[3:4, :]

    qkv = proj(C_AQ, C_AG)
    for t in range(ATT_WIDTH // LANES):
        qt = _rope_attn_tile(qkv[:, t * LANES:(t + 1) * LANES], ca, sa, first_half32)
        qkv_ref[:, 0, t * LANES:(t + 1) * LANES] = qt * ATT_SCALE
    qkv_ref[:, 0, C_AK:C_AV] = _rope_attn_tile(qkv[:, C_AK:C_AV], ca, sa, first_half32)
    qkv_ref[:, 0, C_AV:C_AG] = qkv[:, C_AV:C_AG]
    ga_ref[...] = _silu(proj(C_AG, C_RQ))

    rqk = proj(C_RQ, C_RV)
    for h in range(RET_HEADS):
        sl = slice(h * RET_DK, (h + 1) * RET_DK)
        ret_ref[:, 0, sl] = _rope_ret_tile(rqk[:, sl], cr, sr)
        sk = slice(RET_QK_WIDTH + h * RET_DK, RET_QK_WIDTH + (h + 1) * RET_DK)
        ret_ref[:, 0, sk] = _rope_ret_tile(rqk[:, sk], cr, sr) * RET_K_SCALE
    ret_ref[:, 0, 2 * RET_QK_WIDTH:] = proj(C_RV, C_RG)
    rg_ref[...] = _silu(proj(C_RG, C_MA))
    mg_ref[...] = _sigmoid(proj(C_MA, IN_COLS))


def _sample_proj(layer, x, mod, tab, w_in_b):
    rows = x.shape[0]
    shapes = ((rows, 1, C_AG), (rows, ATT_WIDTH), (rows, 1, 2 * RET_QK_WIDTH + RET_WIDTH),
              (rows, RET_WIDTH), (rows, 2 * D_MODEL))
    return pl.pallas_call(
        _sample_proj_kernel,
        out_shape=tuple(jax.ShapeDtypeStruct(s, F32) for s in shapes),
        grid=(1,),
        in_specs=[
            _const_spec((rows, 1, D_MODEL)),
            _layer_spec((rows, 3 * D_MODEL), layer),
            _const_spec((4, LANES)),
            _const_spec((D_MODEL, IN_COLS)),
        ],
        out_specs=tuple(pl.BlockSpec(s, lambda i, nd=len(s): (0,) * nd) for s in shapes),
        compiler_params=pltpu.CompilerParams(
            dimension_semantics=("arbitrary",), vmem_limit_bytes=VMEM_LIMIT_BYTES),
        name="sample_proj",
    )(x, mod, tab, w_in_b)


def _sample_logits(qkv_ref, ck_ref):
    nb = qkv_ref.shape[0]
    rowi = lax.broadcasted_iota(jnp.int32, (ATT_HEADS, LANES), 0)
    lanei = lax.broadcasted_iota(jnp.int32, (ATT_HEADS, LANES), 1)
    lane_group = lanei // ATT_HEAD_DIM
    qmats, logits = [], []
    for b in range(nb):
        qkv = qkv_ref[b]
        qmat = jnp.zeros((ATT_HEADS, LANES), F32)
        for h in range(ATT_HEADS):
            t, p, g = h // 2, h % 2, h // ATT_GROUP
            tile = qkv[:, t * LANES:(t + 1) * LANES]
            src = tile if p == g else pltpu.roll(tile, 64, 1)
            qmat = jnp.where((rowi == h) & (lane_group == g), jnp.broadcast_to(src, (ATT_HEADS, LANES)), qmat)
        qmats.append(qmat)
        logits.append(_dot(qmat.astype(BF16), ck_ref[b].astype(BF16)))
    return qmats, logits


def _sample_state_update(ret_ref, st_ref, ns_ref, b, h):
    dr = lax.broadcasted_iota(jnp.int32, (RET_DK, RET_DK), 0)
    dc = lax.broadcasted_iota(jnp.int32, (RET_DK, RET_DK), 1)
    ret = ret_ref[b]
    kh = ret[:, RET_QK_WIDTH + h * RET_DK:RET_QK_WIDTH + (h + 1) * RET_DK]
    vh = ret[:, 2 * RET_QK_WIDTH + h * RET_DV:2 * RET_QK_WIDTH + (h + 1) * RET_DV]
    k_col = jnp.sum(jnp.where(dr == dc, jnp.broadcast_to(kh, (RET_DK, RET_DK)), 0.0),
                    axis=1, keepdims=True)
    ns_ref[b, h] = st_ref[b, h] * TOKEN_DECAY[h] + k_col * vh


def _sample_outputs(sinks_ref, layer, qkv_ref, ret_ref, ck_ref, cv_ref, qmats, logits,
                    oatt_ref, oret_ref, nk_ref, nv_ref, ns_ref):
    nb = qkv_ref.shape[0]
    sink_col = jnp.zeros((ATT_HEADS, 1), F32)
    rowc = lax.broadcasted_iota(jnp.int32, (ATT_HEADS, 1), 0)
    for h in range(ATT_HEADS):
        sink_col = jnp.where(rowc == h, sinks_ref[layer, h], sink_col)
    lo64_row = _lane_iota((1, LANES)) < 64
    dr = lax.broadcasted_iota(jnp.int32, (LANES, LANES), 0)
    dc = lax.broadcasted_iota(jnp.int32, (LANES, LANES), 1)
    diag = dr == dc
    last_lane = dc == WINDOW - 1

    for b in range(nb):
        qkv = qkv_ref[b]
        k_new = qkv[:, C_AK:C_AV]
        v_new = qkv[:, C_AV:C_AG]
        s_c = logits[b]
        s_self = jnp.sum(qmats[b] * k_new, axis=-1, keepdims=True)
        m = jnp.maximum(jnp.maximum(jnp.max(s_c, axis=-1, keepdims=True), s_self), sink_col)
        p_c = jnp.exp(s_c - m)
        p_self = jnp.exp(s_self - m)
        denom = jnp.sum(p_c, axis=-1, keepdims=True) + p_self + jnp.exp(sink_col - m)
        o = (lax.dot_general(p_c.astype(BF16), cv_ref[b].astype(BF16), NT, preferred_element_type=F32)
             + p_self * v_new) / denom
        o_sw = pltpu.roll(o, 64, 1)
        for t in range(ATT_WIDTH // LANES):
            g = t // 2
            first = (o if g == 0 else o_sw)[2 * t:2 * t + 1, :]
            second = (o_sw if g == 0 else o)[2 * t + 1:2 * t + 2, :]
            oatt_ref[b, :, t * LANES:(t + 1) * LANES] = jnp.where(lo64_row, first, second)

        for new_row, src_ref, dst_ref in ((k_new, ck_ref, nk_ref), (v_new, cv_ref, nv_ref)):
            new_col = jnp.sum(jnp.where(diag, jnp.broadcast_to(new_row, (LANES, LANES)), 0.0),
                              axis=1, keepdims=True)
            shifted = pltpu.roll(src_ref[b], WINDOW - 1, 1)
            dst_ref[b] = jnp.where(last_lane, new_col, shifted)

    for b in range(nb):
        ret = ret_ref[b]
        for h in range(RET_HEADS):
            qh = ret[:, h * RET_DK:(h + 1) * RET_DK]
            q8 = jnp.broadcast_to(qh, (SUBLANES, RET_DK)).astype(BF16)
            oh = _dot(q8, ns_ref[b, h].astype(BF16))[0:1, :]
            ms = jnp.mean(oh * oh, axis=-1, keepdims=True)
            oret_ref[b, :, h * RET_DV:(h + 1) * RET_DV] = oh * lax.rsqrt(ms + RMS_EPS)


def _sample_out_kernel(x_ref, mod_ref, oatt_ref, ga_ref, oret_ref, rg_ref, mg_ref,
                       wpa_ref, wpr_ref, wout_ref, lng_ref, lnb_ref, y_ref):
    gate_c = mod_ref[:, 2 * D_MODEL:3 * D_MODEL]
    za = _dot((oatt_ref[:, 0, :] * ga_ref[...]).astype(BF16), wpa_ref[...])
    zr = _dot((oret_ref[:, 0, :] * rg_ref[...]).astype(BF16), wpr_ref[...])
    z = mg_ref[:, 0:D_MODEL] * za + mg_ref[:, D_MODEL:2 * D_MODEL] * zr
    u = _dot(z.astype(BF16), wout_ref[...])
    t = ALPHA * x_ref[:, 0, :] + gate_c * u
    mu = jnp.mean(t, axis=-1, keepdims=True)
    d = t - mu
    var = jnp.mean(d * d, axis=-1, keepdims=True)
    y_ref[:, 0, :] = d * lax.rsqrt(var + LN_EPS) * lng_ref[...] + lnb_ref[...]


def _sample_out(layer, x, mod, oatt, ga, oret, rg, mg, w_pa_b, w_pr_b, w_out_b, ln_g, ln_b):
    rows = x.shape[0]
    return pl.pallas_call(
        _sample_out_kernel,
        out_shape=jax.ShapeDtypeStruct(x.shape, F32),
        grid=(1,),
        in_specs=[
            _const_spec((rows, 1, D_MODEL)),
            _layer_spec((rows, 3 * D_MODEL), layer),
            _const_spec((rows, 1, ATT_WIDTH)), _const_spec((rows, ATT_WIDTH)),
            _const_spec((rows, 1, RET_WIDTH)), _const_spec((rows, RET_WIDTH)),
            _const_spec((rows, 2 * D_MODEL)),
            _const_spec((ATT_WIDTH, D_MODEL)),
            _const_spec((RET_WIDTH, D_MODEL)),
            _const_spec((D_MODEL, D_MODEL)),
            _layer_spec((1, D_MODEL), layer), _layer_spec((1, D_MODEL), layer),
        ],
        out_specs=pl.BlockSpec((rows, 1, D_MODEL), lambda i: (0, 0, 0)),
        compiler_params=pltpu.CompilerParams(
            dimension_semantics=("arbitrary",), vmem_limit_bytes=VMEM_LIMIT_BYTES),
        name="sample_out",
    )(x, mod, oatt, ga, oret, rg, mg, w_pa_b, w_pr_b, w_out_b, ln_g, ln_b)


def kernel(x_prompt, x_sample, c_prompt, c_sample, cache_k, cache_v, state_ret, w_in, attn_sinks,
           w_cond, b_cond, w_proj_attn, w_proj_ret, w_out, ln_g, ln_b):
    seq = x_prompt.shape[1]
    nbatch = x_sample.shape[0]
    win = cache_k.shape[2]
    assert seq % PROMPT_ROWS == 0 and nbatch % (seq // PROMPT_ROWS) == 0 and nbatch % SUBLANES == 0

    weights_f32 = (w_in, w_proj_attn, w_proj_ret, w_out)
    weights_b = tuple(w[0].astype(BF16) for w in weights_f32)
    ln_g3 = ln_g.reshape(DEPTH, 1, D_MODEL)
    ln_b3 = ln_b.reshape(DEPTH, 1, D_MODEL)

    c_all = jnp.concatenate([c_sample, c_prompt, jnp.zeros((SUBLANES - 1, D_MODEL), F32)], axis=0)
    mod = _cond_call(c_all, w_cond, b_cond)
    prompt_mod_block = nbatch // SUBLANES

    base, rtab = _prompt_rope_tables(seq, PROMPT_ROWS)
    stab = _sample_rope_table(float(PAST_LEN))
    ret_tabs = _retention_tables()

    def feature_major(c):
        return c.transpose(0, 1, 3, 4, 2).reshape(DEPTH, nbatch, ATT_KV_WIDTH, win)

    def window_major(c):
        return c.reshape(DEPTH, nbatch, ATT_KV_HEADS, ATT_HEAD_DIM, win).transpose(0, 1, 4, 2, 3)

    ck = feature_major(cache_k)
    cv = feature_major(cache_v)

    yp = x_prompt[0]
    ys = x_sample
    kp, vp, sp = [], [], []
    prev = None
    for l in range(DEPTH):
        w_in_b, w_pa_b, w_pr_b, w_out_b = weights_b
        qkv, ga, ret, rg, mg = _sample_proj(l, ys, mod, stab, w_in_b)
        outs = _fused_layer(
            l, yp, mod, prompt_mod_block, base, rtab, weights_b,
            weights_f32 if l + 1 < DEPTH else None,
            ln_g3, ln_b3, attn_sinks, ret_tabs, qkv, ret, ck, cv, state_ret, prev)
        yp, k_new, v_new, s_new, oatt, oret, nk, nv, ns = outs[:N_PROMPT_OUTPUTS]
        kp.append(k_new.reshape(1, WINDOW, ATT_KV_HEADS, ATT_HEAD_DIM))
        vp.append(v_new.reshape(1, WINDOW, ATT_KV_HEADS, ATT_HEAD_DIM))
        sp.append(s_new[None])
        prev = (nk, nv, ns)
        ys = _sample_out(l, ys, mod, oatt, ga, oret, rg, mg, w_pa_b, w_pr_b, w_out_b, ln_g3, ln_b3)
        weights_b = tuple(outs[N_PROMPT_OUTPUTS:])

    nk, nv, ns = prev
    return (yp[None], ys, jnp.stack(kp), jnp.stack(vp), jnp.stack(sp),
            window_major(nk), window_major(nv), ns)
```

```python
import functools

import jax
import jax.numpy as jnp
import numpy as np
from jax import lax
from jax.experimental import pallas as pl
from jax.experimental.pallas import tpu as pltpu

D_MODEL = 1024
DEPTH = 2
PAST_LEN = 16384
ATT_HEADS = 8
ATT_KV_HEADS = 2
ATT_HEAD_DIM = 64
ATT_GROUP = ATT_HEADS // ATT_KV_HEADS
ATT_WIDTH = ATT_HEADS * ATT_HEAD_DIM
ATT_KV_WIDTH = ATT_KV_HEADS * ATT_HEAD_DIM
WINDOW = 128
RET_HEADS = 4
RET_DK = 128
RET_DV = 256
RET_QK_WIDTH = RET_HEADS * RET_DK
RET_WIDTH = RET_HEADS * RET_DV
RET_CHUNK = 128
ROPE_THETA = 10000.0
ALPHA = (2.0 * DEPTH) ** 0.25
LN_EPS = 1e-5
RMS_EPS = 1e-6
ATT_SCALE = ATT_HEAD_DIM ** -0.5
RET_K_SCALE = RET_DK ** -0.5

C_AQ = 0
C_AK = C_AQ + ATT_WIDTH
C_AV = C_AK + ATT_KV_WIDTH
C_AG = C_AV + ATT_KV_WIDTH
C_RQ = C_AG + ATT_WIDTH
C_RK = C_RQ + RET_QK_WIDTH
C_RV = C_RK + RET_QK_WIDTH
C_RG = C_RV + RET_WIDTH
C_MA = C_RG + RET_WIDTH
C_MR = C_MA + D_MODEL
IN_COLS = C_MR + D_MODEL

LANES = 128
SUBLANES = 8
BF16_SUBLANES = 16
VMEM_LIMIT_BYTES = 56 * 1024 * 1024

PROMPT_ROWS = 256

BF16 = jnp.bfloat16
F32 = jnp.float32
NT = (((1,), (1,)), ((), ()))

_LOG_GAMMA = np.log(1.0 - 2.0 ** (-5.0 - np.arange(RET_HEADS, dtype=np.float64)))
CHUNK_DECAY = tuple(float(v) for v in np.exp(RET_CHUNK * _LOG_GAMMA))
TOKEN_DECAY = tuple(float(v) for v in np.exp(_LOG_GAMMA))


def _sigmoid(x):
    return 0.5 * jnp.tanh(0.5 * x) + 0.5


def _silu(x):
    return x * _sigmoid(x)


def _dot(a, b):
    return jnp.dot(a, b, preferred_element_type=F32)


def _rope_attn_tile(x, cos, sin_signed, first_half):
    rot = jnp.where(first_half, pltpu.roll(x, LANES - 32, 1), pltpu.roll(x, 32, 1))
    return x * cos + rot * sin_signed


def _rope_ret_tile(x, cos, sin_signed):
    return x * cos + pltpu.roll(x, 64, 1) * sin_signed


def _lane_iota(shape):
    return lax.broadcasted_iota(jnp.int32, shape, len(shape) - 1)


def _layer_spec(shape, layer):
    nd = len(shape)
    return pl.BlockSpec((None,) + tuple(shape), lambda i: (layer,) + (0,) * nd,
                        pipeline_mode=pl.Buffered(1))


def _const_spec(shape):
    nd = len(shape)
    return pl.BlockSpec(tuple(shape), lambda i: (0,) * nd, pipeline_mode=pl.Buffered(1))


def _rope_lane_patterns():
    lane = np.arange(LANES)
    f_att = ROPE_THETA ** (-(lane % 32) / 32.0)
    s_att = np.where(lane % 64 < 32, -1.0, 1.0)
    f_ret = ROPE_THETA ** (-(lane % 64) / 64.0)
    s_ret = np.where(lane < 64, -1.0, 1.0)
    return (f_att, s_att), (f_ret, s_ret)


def _prompt_rope_tables(seq, tm):
    starts = np.arange(seq // tm, dtype=np.float64)[:, None] * tm
    offs = np.arange(tm, dtype=np.float64)[:, None]
    base, within = [], []
    for freq, sign in _rope_lane_patterns():
        base += [np.cos(starts * freq), np.sin(starts * freq)]
        c, s = np.cos(offs * freq), np.sin(offs * freq)
        within += [c, s, sign * c, sign * s]
    return (jnp.asarray(np.stack(base, axis=1), F32),
            jnp.asarray(np.stack(within, axis=0), F32))


def _sample_rope_table(pos):
    rows = []
    for freq, sign in _rope_lane_patterns():
        rows += [np.cos(pos * freq), sign * np.sin(pos * freq)]
    return jnp.asarray(np.stack(rows, axis=0), F32)


def _retention_tables():
    c = RET_CHUNK
    idx = np.arange(c, dtype=np.float64)
    diff = idx[:, None] - idx[None, :]
    lg = _LOG_GAMMA[:, None, None]
    dmask = np.where(diff >= 0, np.exp(np.maximum(diff, 0.0)[None] * lg), 0.0)
    qdec = np.broadcast_to(np.exp((idx + 1.0)[None, :, None] * lg), (RET_HEADS, c, RET_DK))
    kdec = np.broadcast_to(np.exp((c - 1.0 - idx)[None, :, None] * lg), (RET_HEADS, c, RET_DK))
    return jnp.asarray(dmask, F32), jnp.asarray(qdec, F32), jnp.asarray(kdec, F32)


def _cond_kernel(c_ref, w_ref, b_ref, o_ref):
    a = _silu(c_ref[...]).astype(BF16)
    o_ref[...] = _dot(a, w_ref[...].astype(BF16)) + b_ref[...]


def _cond_call(c_all, w_cond, b_cond):
    rows = c_all.shape[0]
    tn = 3 * D_MODEL // 2
    return pl.pallas_call(
        _cond_kernel,
        out_shape=jax.ShapeDtypeStruct((DEPTH, rows, 3 * D_MODEL), F32),
        grid=(DEPTH, 3 * D_MODEL // tn),
        in_specs=[
            pl.BlockSpec((rows, D_MODEL), lambda l, j: (0, 0)),
            pl.BlockSpec((None, D_MODEL, tn), lambda l, j: (l, 0, j)),
            pl.BlockSpec((None, 1, tn), lambda l, j: (l, 0, j)),
        ],
        out_specs=pl.BlockSpec((None, rows, tn), lambda l, j: (l, 0, j)),
        compiler_params=pltpu.CompilerParams(
            dimension_semantics=("arbitrary", "arbitrary"), vmem_limit_bytes=VMEM_LIMIT_BYTES),
        name="cond_mod",
    )(c_all, w_cond, b_cond.reshape(DEPTH, 1, 3 * D_MODEL))


N_PROMPT_INPUTS = 19
N_PROMPT_OUTPUTS = 9
N_WEIGHTS = 4


def _prompt_kernel(*refs, layer, n_aliased, n_cast):
    (sinks_ref, x_ref, mod_ref, base_ref, rtab_ref,
     win_ref, wpa_ref, wpr_ref, wout_ref, lng_ref, lnb_ref,
     dmask_ref, qdec_ref, kdec_ref,
     sqkv_ref, sret_ref, ck_ref, cv_ref, st_ref) = refs[:N_PROMPT_INPUTS]
    cast_in = refs[N_PROMPT_INPUTS:N_PROMPT_INPUTS + n_cast]
    refs = refs[N_PROMPT_INPUTS + n_cast + n_aliased:]
    (y_ref, knew_ref, vnew_ref, snew_ref,
     oatt_ref, oret_ref, nk_ref, nv_ref, ns_ref) = refs[:N_PROMPT_OUTPUTS]
    cast_out = refs[N_PROMPT_OUTPUTS:N_PROMPT_OUTPUTS + n_cast]
    (tab_scr, hb_scr, q_scr, kvar_scr, vvar_scr, ga_scr, a_scr,
     rq_scr, rk_scr, rv_scr, rg_scr, r_scr, s_scr,
     ma_scr, mr_scr) = refs[N_PROMPT_OUTPUTS + n_cast:]
    step = pl.program_id(0)

    def cast_next_layer_weights():
        for src, dst in zip(cast_in, cast_out):
            dst[...] = src[...].astype(BF16)

    tm = x_ref.shape[0]
    nsub = tm // WINDOW

    def proj(c0, c1):
        return _dot(hb_scr[...], win_ref[:, c0:c1])

    lane = _lane_iota((tm, LANES))
    first_half32 = (lane & 32) == 0
    lo64 = lane < 64

    def rope_attn(t):
        return _rope_attn_tile(t, tab_scr[0], tab_scr[1], first_half32)

    def rope_ret(t):
        return _rope_ret_tile(t, tab_scr[2], tab_scr[3])

    def store_variants(scr, t, fill):
        swapped = pltpu.roll(t, 64, 1)
        other = jnp.full_like(t, fill)
        scr[0, WINDOW:WINDOW + tm, :] = jnp.where(lo64, t, other).astype(BF16)
        scr[1, WINDOW:WINDOW + tm, :] = jnp.where(lo64, other, swapped).astype(BF16)
        scr[2, WINDOW:WINDOW + tm, :] = jnp.where(lo64, swapped, other).astype(BF16)
        scr[3, WINDOW:WINDOW + tm, :] = jnp.where(lo64, other, t).astype(BF16)

    def block_prepare(src_x_ref, src_base_ref):
        shift = mod_ref[0:1, 0:D_MODEL]
        scale = mod_ref[0:1, D_MODEL:2 * D_MODEL]
        hb_scr[...] = (src_x_ref[...] * (1.0 + scale) + shift).astype(BF16)
        for fam in range(2):
            cb = src_base_ref[2 * fam:2 * fam + 1, :]
            sb = src_base_ref[2 * fam + 1:2 * fam + 2, :]
            tab_scr[2 * fam] = cb * rtab_ref[4 * fam] - sb * rtab_ref[4 * fam + 1]
            tab_scr[2 * fam + 1] = sb * rtab_ref[4 * fam + 2] + cb * rtab_ref[4 * fam + 3]

    def block_head():
        kv = proj(C_AK, C_AG)
        k_rot = rope_attn(kv[:, 0:LANES])
        v_raw = kv[:, LANES:2 * LANES]
        knew_ref[...] = k_rot[tm - WINDOW:, :]
        vnew_ref[...] = v_raw[tm - WINDOW:, :]
        store_variants(kvar_scr, k_rot, 0.0)
        store_variants(vvar_scr, v_raw, 1.0)
        qp = proj(C_AQ, C_AK)
        for t in range(ATT_WIDTH // LANES):
            qt = rope_attn(qp[:, t * LANES:(t + 1) * LANES])
            q_scr[:, t * LANES:(t + 1) * LANES] = (qt * ATT_SCALE).astype(BF16)
        ga_scr[...] = proj(C_AG, C_RQ)

    @pl.when(step == 0)
    def _():
        kvar_scr[:, 0:WINDOW, :] = jnp.zeros((4, WINDOW, LANES), BF16)
        vvar_scr[:, 0:WINDOW, :] = jnp.zeros((4, WINDOW, LANES), BF16)
        s_scr[...] = jnp.zeros(s_scr.shape, F32)

    block_prepare(x_ref, base_ref)
    block_head()

    def chunk_rq():
        p = proj(C_RQ, C_RK)
        for h in range(RET_HEADS):
            sl = slice(h * RET_DK, (h + 1) * RET_DK)
            rq_scr[:, sl] = rope_ret(p[:, sl])

    def chunk_rk():
        p = proj(C_RK, C_RV)
        for h in range(RET_HEADS):
            sl = slice(h * RET_DK, (h + 1) * RET_DK)
            rk_scr[:, sl] = rope_ret(p[:, sl]) * RET_K_SCALE

    def half_chunk(c0, dst, fn, half):
        w = dst.shape[1] // 2
        def run():
            dst[:, half * w:(half + 1) * w] = fn(proj(c0 + half * w, c0 + (half + 1) * w))
        return run

    to_bf16 = lambda v: v.astype(BF16)
    raw = lambda v: v
    att_companions = [
        chunk_rq, chunk_rk,
        half_chunk(C_RV, rv_scr, to_bf16, 0), half_chunk(C_RV, rv_scr, to_bf16, 1),
        half_chunk(C_RG, rg_scr, raw, 0), half_chunk(C_RG, rg_scr, raw, 1),
        half_chunk(C_MA, ma_scr, raw, 0), half_chunk(C_MA, ma_scr, raw, 1),
    ]

    row = lax.broadcasted_iota(jnp.int32, (WINDOW, 2 * WINDOW), 0)
    col = lax.broadcasted_iota(jnp.int32, (WINDOW, 2 * WINDOW), 1)
    in_window = col <= row + WINDOW
    mask_std = (col >= row) & in_window
    off = jnp.where(step > 0, 0, 4 * WINDOW)
    mask_first = ((col >= row + off) | (col >= WINDOW)) & in_window
    lo64_w = _lane_iota((WINDOW, LANES)) < 64

    def attention_logits(i, t):
        r0 = i * WINDOW
        g = t // 2
        qt = q_scr[r0:r0 + WINDOW, t * LANES:(t + 1) * LANES]
        return [lax.dot_general(qt, kvar_scr[2 * g + p, r0:r0 + 2 * WINDOW, :], NT,
                                preferred_element_type=F32) for p in range(2)]

    def attention_values(i, t, logits):
        r0 = i * WINDOW
        mask = mask_first if i == 0 else mask_std
        g = t // 2
        outs, sink_terms = [], []
        for p in range(2):
            head = 2 * t + p
            s = jnp.where(mask, logits[p], -jnp.inf)
            sink = sinks_ref[layer, head]
            m = jnp.maximum(jnp.max(s, axis=-1, keepdims=True), sink)
            pe = jnp.exp(s - m).astype(BF16)
            sink_terms.append(jnp.exp(sink - m))
            outs.append(_dot(pe, vvar_scr[2 * g + p, r0:r0 + 2 * WINDOW, :]))
        weighted = jnp.where(lo64_w, outs[0], outs[1])
        row_sums = pltpu.roll(jnp.where(lo64_w, outs[1], outs[0]), 64, 1)
        denom = row_sums + jnp.where(lo64_w, sink_terms[0], sink_terms[1])
        a_scr[r0:r0 + WINDOW, t * LANES:(t + 1) * LANES] = weighted * (1.0 / denom)

    n_tiles = ATT_WIDTH // LANES
    per_window = -(-len(att_companions) // nsub)
    for i in range(nsub):
        logits = [attention_logits(i, t) for t in range(n_tiles)]
        for run in att_companions[i * per_window:(i + 1) * per_window]:
            run()
        if i == 0:
            cast_next_layer_weights()
        for t in range(n_tiles):
            attention_values(i, t, logits[t])

    kvar_scr[:, 0:WINDOW, :] = kvar_scr[:, tm:tm + WINDOW, :]
    vvar_scr[:, 0:WINDOW, :] = vvar_scr[:, tm:tm + WINDOW, :]

    def ret_slices(c, h):
        rows = slice(c * RET_CHUNK, (c + 1) * RET_CHUNK)
        return rows, slice(h * RET_DK, (h + 1) * RET_DK), slice(h * RET_DV, (h + 1) * RET_DV)

    def retention_scores_and_state(c):
        inners, s_olds = [], []
        for h in range(RET_HEADS):
            rows, sk, sv = ret_slices(c, h)
            qh = rq_scr[rows, sk]
            kh = rk_scr[rows, sk]
            vh = rv_scr[rows, sv]
            inners.append(lax.dot_general(qh.astype(BF16), kh.astype(BF16), NT, preferred_element_type=F32))
            s_old = s_scr[h]
            s_olds.append(s_old.astype(BF16))
            kd = (kh * kdec_ref[h]).astype(BF16)
            s_scr[h] = s_old * CHUNK_DECAY[h] + lax.dot_general(
                kd, vh, (((0,), (0,)), ((), ())), preferred_element_type=F32)
        return inners, s_olds

    def retention_outputs(c, inners, s_olds):
        for h in range(RET_HEADS):
            rows, sk, sv = ret_slices(c, h)
            qh = rq_scr[rows, sk]
            lhs = jnp.concatenate([(inners[h] * dmask_ref[h]).astype(BF16),
                                   (qh * qdec_ref[h]).astype(BF16)], axis=1)
            rhs = jnp.concatenate([rv_scr[rows, sv], s_olds[h]], axis=0)
            o = _dot(lhs, rhs)
            ms = jnp.mean(o * o, axis=-1, keepdims=True)
            on = o * lax.rsqrt(ms + RMS_EPS)
            r_scr[rows, sv] = on

    ret_companions = [half_chunk(C_MR, mr_scr, raw, 0), half_chunk(C_MR, mr_scr, raw, 1)]
    staged = [retention_scores_and_state(c) for c in range(nsub)]
    for run in ret_companions:
        run()
    for c in range(nsub):
        retention_outputs(c, *staged[c])
    sample_qmats, sample_logits = _sample_logits(sqkv_ref, ck_ref)
    for b in range(sqkv_ref.shape[0]):
        for h in range(RET_HEADS):
            _sample_state_update(sret_ref, st_ref, ns_ref, b, h)

    gate_c = mod_ref[0:1, 2 * D_MODEL:3 * D_MODEL]
    lng = lng_ref[...]
    lnb = lnb_ref[...]
    windows = [slice(i * WINDOW, (i + 1) * WINDOW) for i in range(nsub)]
    za_all = _dot((a_scr[...] * _silu(ga_scr[...])).astype(BF16), wpa_ref[...])
    zr_all = _dot((r_scr[...] * _silu(rg_scr[...])).astype(BF16), wpr_ref[...])
    for rows in windows:
        z = _sigmoid(ma_scr[rows, :]) * za_all[rows, :] + _sigmoid(mr_scr[rows, :]) * zr_all[rows, :]
        u = _dot(z.astype(BF16), wout_ref[...])
        t = ALPHA * x_ref[rows, :] + gate_c * u
        mu = jnp.mean(t, axis=-1, keepdims=True)
        d = t - mu
        var = jnp.mean(d * d, axis=-1, keepdims=True)
        y_ref[rows, :] = d * lax.rsqrt(var + LN_EPS) * lng + lnb

    _sample_outputs(sinks_ref, layer, sqkv_ref, sret_ref, ck_ref, cv_ref, sample_qmats, sample_logits,
                    oatt_ref, oret_ref, nk_ref, nv_ref, ns_ref)

    @pl.when(step == pl.num_programs(0) - 1)
    def _():
        snew_ref[...] = s_scr[...]


def _fused_layer(layer, x, mod, mod_row_block, base, rtab, weights_b, next_weights,
                 ln_g, ln_b, sinks, ret_tabs, sample_qkv, sample_ret, cache_k, cache_v, state, shared_out):
    w_in_b, w_pa_b, w_pr_b, w_out_b = weights_b
    seq = x.shape[0]
    tm = PROMPT_ROWS
    steps = seq // tm
    nbatch = sample_qkv.shape[0]
    nb = nbatch // steps
    win = cache_k.shape[3]
    assert win == WINDOW == LANES
    dmask, qdec, kdec = ret_tabs
    smem = pl.BlockSpec(memory_space=pltpu.SMEM)
    row_spec = lambda w: pl.BlockSpec((tm, w), lambda i: (i, 0))
    seq_spec = lambda w: pl.BlockSpec((nb, 1, w), lambda i: (i, 0, 0))
    cache_spec = pl.BlockSpec((None, nb, ATT_KV_WIDTH, win), lambda i: (layer, i, 0, 0))
    state_spec = pl.BlockSpec((None, nb, RET_HEADS, RET_DK, RET_DV), lambda i: (layer, i, 0, 0, 0))
    in_specs = [
        smem,
        row_spec(D_MODEL),
        pl.BlockSpec((None, SUBLANES, 3 * D_MODEL), lambda i: (layer, mod_row_block, 0),
                     pipeline_mode=pl.Buffered(1)),
        pl.BlockSpec((None, 4, LANES), lambda i: (i, 0, 0)),
        _const_spec((8, tm, LANES)),
        _const_spec((D_MODEL, IN_COLS)),
        _const_spec((ATT_WIDTH, D_MODEL)),
        _const_spec((RET_WIDTH, D_MODEL)),
        _const_spec((D_MODEL, D_MODEL)),
        _layer_spec((1, D_MODEL), layer), _layer_spec((1, D_MODEL), layer),
        _const_spec((RET_HEADS, RET_CHUNK, RET_CHUNK)),
        _const_spec((RET_HEADS, RET_CHUNK, RET_DK)),
        _const_spec((RET_HEADS, RET_CHUNK, RET_DK)),
        seq_spec(C_AG), seq_spec(2 * RET_QK_WIDTH + RET_WIDTH),
        cache_spec, cache_spec, state_spec,
    ]
    args = [sinks, x, mod, base, rtab, w_in_b, w_pa_b, w_pr_b, w_out_b, ln_g, ln_b, dmask, qdec, kdec,
            sample_qkv, sample_ret, cache_k, cache_v, state]
    assert len(args) == N_PROMPT_INPUTS
    cast_shapes, cast_specs = [], []
    if next_weights is not None:
        assert len(next_weights) == N_WEIGHTS
        for w in next_weights:
            _, rows, cols = w.shape
            slab = max(BF16_SUBLANES, rows // steps)
            nslab = rows // slab
            assert rows % slab == 0 and nslab <= steps
            args.append(w)
            in_specs.append(pl.BlockSpec((None, slab, cols),
                                         lambda i, n=nslab: (layer + 1, jnp.minimum(i, n - 1), 0)))
            cast_shapes.append(jax.ShapeDtypeStruct((rows, cols), BF16))
            cast_specs.append(pl.BlockSpec((slab, cols), lambda i, n=nslab: (jnp.minimum(i, n - 1), 0)))
    aliases = {}
    if shared_out is not None:
        for j, arr in enumerate(shared_out):
            aliases[len(args)] = N_PROMPT_OUTPUTS - len(shared_out) + j
            args.append(arr)
            in_specs.append(pl.BlockSpec(memory_space=pl.ANY))
    return pl.pallas_call(
        functools.partial(_prompt_kernel, layer=layer, n_aliased=len(aliases), n_cast=len(cast_shapes)),
        out_shape=(
            jax.ShapeDtypeStruct((seq, D_MODEL), F32),
            jax.ShapeDtypeStruct((WINDOW, ATT_KV_WIDTH), F32),
            jax.ShapeDtypeStruct((WINDOW, ATT_KV_WIDTH), F32),
            jax.ShapeDtypeStruct((RET_HEADS, RET_DK, RET_DV), F32),
            jax.ShapeDtypeStruct((nbatch, 1, ATT_WIDTH), F32),
            jax.ShapeDtypeStruct((nbatch, 1, RET_WIDTH), F32),
            jax.ShapeDtypeStruct(cache_k.shape, F32),
            jax.ShapeDtypeStruct(cache_v.shape, F32),
            jax.ShapeDtypeStruct(state.shape, F32),
        ) + tuple(cast_shapes),
        grid=(steps,),
        in_specs=in_specs,
        out_specs=(
            row_spec(D_MODEL),
            pl.BlockSpec((WINDOW, ATT_KV_WIDTH), lambda i: (0, 0)),
            pl.BlockSpec((WINDOW, ATT_KV_WIDTH), lambda i: (0, 0)),
            pl.BlockSpec((RET_HEADS, RET_DK, RET_DV), lambda i: (0, 0, 0)),
            seq_spec(ATT_WIDTH), seq_spec(RET_WIDTH),
            cache_spec, cache_spec, state_spec,
        ) + tuple(cast_specs),
        input_output_aliases=aliases,
        scratch_shapes=[
            pltpu.VMEM((4, tm, LANES), F32),
            pltpu.VMEM((tm, D_MODEL), BF16),
            pltpu.VMEM((tm, ATT_WIDTH), BF16),
            pltpu.VMEM((4, WINDOW + tm, LANES), BF16),
            pltpu.VMEM((4, WINDOW + tm, LANES), BF16),
            pltpu.VMEM((tm, ATT_WIDTH), F32),
            pltpu.VMEM((tm, ATT_WIDTH), F32),
            pltpu.VMEM((tm, RET_QK_WIDTH), F32),
            pltpu.VMEM((tm, RET_QK_WIDTH), F32),
            pltpu.VMEM((tm, RET_WIDTH), BF16),
            pltpu.VMEM((tm, RET_WIDTH), F32),
            pltpu.VMEM((tm, RET_WIDTH), F32),
            pltpu.VMEM((RET_HEADS, RET_DK, RET_DV), F32),
            pltpu.VMEM((tm, D_MODEL), F32),
            pltpu.VMEM((tm, D_MODEL), F32),
        ],
        compiler_params=pltpu.CompilerParams(
            dimension_semantics=("arbitrary",), vmem_limit_bytes=VMEM_LIMIT_BYTES),
        name="fused_layer",
    )(*args)


def _sample_proj_kernel(x_ref, mod_ref, tab_ref, win_ref,
                        qkv_ref, ga_ref, ret_ref, rg_ref, mg_ref):
    rows = x_ref.shape[0]
    x = x_ref[:, 0, :]
    shift = mod_ref[:, 0:D_MODEL]
    scale = mod_ref[:, D_MODEL:2 * D_MODEL]
    hb = (x * (1.0 + scale) + shift).astype(BF16)

    def proj(c0, c1):
        return _dot(hb, win_ref[:, c0:c1])

    lane = _lane_iota((rows, LANES))
    first_half32 = (lane & 32) == 0
    ca = tab_ref[0:1, :]
    sa = tab_ref[1:2, :]
    cr = tab_ref[2:3, :]
    sr = tab_ref[3:4, :]

    qkv = proj(C_AQ, C_AG)
    for t in range(ATT_WIDTH // LANES):
        qt = _rope_attn_tile(qkv[:, t * LANES:(t + 1) * LANES], ca, sa, first_half32)
        qkv_ref[:, 0, t * LANES:(t + 1) * LANES] = qt * ATT_SCALE
    qkv_ref[:, 0, C_AK:C_AV] = _rope_attn_tile(qkv[:, C_AK:C_AV], ca, sa, first_half32)
    qkv_ref[:, 0, C_AV:C_AG] = qkv[:, C_AV:C_AG]
    ga_ref[...] = _silu(proj(C_AG, C_RQ))

    rqk = proj(C_RQ, C_RV)
    for h in range(RET_HEADS):
        sl = slice(h * RET_DK, (h + 1) * RET_DK)
        ret_ref[:, 0, sl] = _rope_ret_tile(rqk[:, sl], cr, sr)
        sk = slice(RET_QK_WIDTH + h * RET_DK, RET_QK_WIDTH + (h + 1) * RET_DK)
        ret_ref[:, 0, sk] = _rope_ret_tile(rqk[:, sk], cr, sr) * RET_K_SCALE
    ret_ref[:, 0, 2 * RET_QK_WIDTH:] = proj(C_RV, C_RG)
    rg_ref[...] = _silu(proj(C_RG, C_MA))
    mg_ref[...] = _sigmoid(proj(C_MA, IN_COLS))


def _sample_proj(layer, x, mod, tab, w_in_b):
    rows = x.shape[0]
    shapes = ((rows, 1, C_AG), (rows, ATT_WIDTH), (rows, 1, 2 * RET_QK_WIDTH + RET_WIDTH),
              (rows, RET_WIDTH), (rows, 2 * D_MODEL))
    return pl.pallas_call(
        _sample_proj_kernel,
        out_shape=tuple(jax.ShapeDtypeStruct(s, F32) for s in shapes),
        grid=(1,),
        in_specs=[
            _const_spec((rows, 1, D_MODEL)),
            _layer_spec((rows, 3 * D_MODEL), layer),
            _const_spec((4, LANES)),
            _const_spec((D_MODEL, IN_COLS)),
        ],
        out_specs=tuple(pl.BlockSpec(s, lambda i, nd=len(s): (0,) * nd) for s in shapes),
        compiler_params=pltpu.CompilerParams(
            dimension_semantics=("arbitrary",), vmem_limit_bytes=VMEM_LIMIT_BYTES),
        name="sample_proj",
    )(x, mod, tab, w_in_b)


def _sample_logits(qkv_ref, ck_ref):
    nb = qkv_ref.shape[0]
    rowi = lax.broadcasted_iota(jnp.int32, (ATT_HEADS, LANES), 0)
    lanei = lax.broadcasted_iota(jnp.int32, (ATT_HEADS, LANES), 1)
    lane_group = lanei // ATT_HEAD_DIM
    qmats, logits = [], []
    for b in range(nb):
        qkv = qkv_ref[b]
        qmat = jnp.zeros((ATT_HEADS, LANES), F32)
        for h in range(ATT_HEADS):
            t, p, g = h // 2, h % 2, h // ATT_GROUP
            tile = qkv[:, t * LANES:(t + 1) * LANES]
            src = tile if p == g else pltpu.roll(tile, 64, 1)
            qmat = jnp.where((rowi == h) & (lane_group == g), jnp.broadcast_to(src, (ATT_HEADS, LANES)), qmat)
        qmats.append(qmat)
        logits.append(_dot(qmat.astype(BF16), ck_ref[b].astype(BF16)))
    return qmats, logits


def _sample_state_update(ret_ref, st_ref, ns_ref, b, h):
    dr = lax.broadcasted_iota(jnp.int32, (RET_DK, RET_DK), 0)
    dc = lax.broadcasted_iota(jnp.int32, (RET_DK, RET_DK), 1)
    ret = ret_ref[b]
    kh = ret[:, RET_QK_WIDTH + h * RET_DK:RET_QK_WIDTH + (h + 1) * RET_DK]
    vh = ret[:, 2 * RET_QK_WIDTH + h * RET_DV:2 * RET_QK_WIDTH + (h + 1) * RET_DV]
    k_col = jnp.sum(jnp.where(dr == dc, jnp.broadcast_to(kh, (RET_DK, RET_DK)), 0.0),
                    axis=1, keepdims=True)
    ns_ref[b, h] = st_ref[b, h] * TOKEN_DECAY[h] + k_col * vh


def _sample_outputs(sinks_ref, layer, qkv_ref, ret_ref, ck_ref, cv_ref, qmats, logits,
                    oatt_ref, oret_ref, nk_ref, nv_ref, ns_ref):
    nb = qkv_ref.shape[0]
    sink_col = jnp.zeros((ATT_HEADS, 1), F32)
    rowc = lax.broadcasted_iota(jnp.int32, (ATT_HEADS, 1), 0)
    for h in range(ATT_HEADS):
        sink_col = jnp.where(rowc == h, sinks_ref[layer, h], sink_col)
    lo64_row = _lane_iota((1, LANES)) < 64
    dr = lax.broadcasted_iota(jnp.int32, (LANES, LANES), 0)
    dc = lax.broadcasted_iota(jnp.int32, (LANES, LANES), 1)
    diag = dr == dc
    last_lane = dc == WINDOW - 1

    for b in range(nb):
        qkv = qkv_ref[b]
        k_new = qkv[:, C_AK:C_AV]
        v_new = qkv[:, C_AV:C_AG]
        s_c = logits[b]
        s_self = jnp.sum(qmats[b] * k_new, axis=-1, keepdims=True)
        m = jnp.maximum(jnp.maximum(jnp.max(s_c, axis=-1, keepdims=True), s_self), sink_col)
        p_c = jnp.exp(s_c - m)
        p_self = jnp.exp(s_self - m)
        denom = jnp.sum(p_c, axis=-1, keepdims=True) + p_self + jnp.exp(sink_col - m)
        o = (lax.dot_general(p_c.astype(BF16), cv_ref[b].astype(BF16), NT, preferred_element_type=F32)
             + p_self * v_new) / denom
        o_sw = pltpu.roll(o, 64, 1)
        for t in range(ATT_WIDTH // LANES):
            g = t // 2
            first = (o if g == 0 else o_sw)[2 * t:2 * t + 1, :]
            second = (o_sw if g == 0 else o)[2 * t + 1:2 * t + 2, :]
            oatt_ref[b, :, t * LANES:(t + 1) * LANES] = jnp.where(lo64_row, first, second)

        for new_row, src_ref, dst_ref in ((k_new, ck_ref, nk_ref), (v_new, cv_ref, nv_ref)):
            new_col = jnp.sum(jnp.where(diag, jnp.broadcast_to(new_row, (LANES, LANES)), 0.0),
                              axis=1, keepdims=True)
            shifted = pltpu.roll(src_ref[b], WINDOW - 1, 1)
            dst_ref[b] = jnp.where(last_lane, new_col, shifted)

    for b in range(nb):
        ret = ret_ref[b]
        for h in range(RET_HEADS):
            qh = ret[:, h * RET_DK:(h + 1) * RET_DK]
            q8 = jnp.broadcast_to(qh, (SUBLANES, RET_DK)).astype(BF16)
            oh = _dot(q8, ns_ref[b, h].astype(BF16))[0:1, :]
            ms = jnp.mean(oh * oh, axis=-1, keepdims=True)
            oret_ref[b, :, h * RET_DV:(h + 1) * RET_DV] = oh * lax.rsqrt(ms + RMS_EPS)


def _sample_out_kernel(x_ref, mod_ref, oatt_ref, ga_ref, oret_ref, rg_ref, mg_ref,
                       wpa_ref, wpr_ref, wout_ref, lng_ref, lnb_ref, y_ref):
    gate_c = mod_ref[:, 2 * D_MODEL:3 * D_MODEL]
    za = _dot((oatt_ref[:, 0, :] * ga_ref[...]).astype(BF16), wpa_ref[...])
    zr = _dot((oret_ref[:, 0, :] * rg_ref[...]).astype(BF16), wpr_ref[...])
    z = mg_ref[:, 0:D_MODEL] * za + mg_ref[:, D_MODEL:2 * D_MODEL] * zr
    u = _dot(z.astype(BF16), wout_ref[...])
    t = ALPHA * x_ref[:, 0, :] + gate_c * u
    mu = jnp.mean(t, axis=-1, keepdims=True)
    d = t - mu
    var = jnp.mean(d * d, axis=-1, keepdims=True)
    y_ref[:, 0, :] = d * lax.rsqrt(var + LN_EPS) * lng_ref[...] + lnb_ref[...]


def _sample_out(layer, x, mod, oatt, ga, oret, rg, mg, w_pa_b, w_pr_b, w_out_b, ln_g, ln_b):
    rows = x.shape[0]
    return pl.pallas_call(
        _sample_out_kernel,
        out_shape=jax.ShapeDtypeStruct(x.shape, F32),
        grid=(1,),
        in_specs=[
            _const_spec((rows, 1, D_MODEL)),
            _layer_spec((rows, 3 * D_MODEL), layer),
            _const_spec((rows, 1, ATT_WIDTH)), _const_spec((rows, ATT_WIDTH)),
            _const_spec((rows, 1, RET_WIDTH)), _const_spec((rows, RET_WIDTH)),
            _const_spec((rows, 2 * D_MODEL)),
            _const_spec((ATT_WIDTH, D_MODEL)),
            _const_spec((RET_WIDTH, D_MODEL)),
            _const_spec((D_MODEL, D_MODEL)),
            _layer_spec((1, D_MODEL), layer), _layer_spec((1, D_MODEL), layer),
        ],
        out_specs=pl.BlockSpec((rows, 1, D_MODEL), lambda i: (0, 0, 0)),
        compiler_params=pltpu.CompilerParams(
            dimension_semantics=("arbitrary",), vmem_limit_bytes=VMEM_LIMIT_BYTES),
        name="sample_out",
    )(x, mod, oatt, ga, oret, rg, mg, w_pa_b, w_pr_b, w_out_b, ln_g, ln_b)


def kernel(x_prompt, x_sample, c_prompt, c_sample, cache_k, cache_v, state_ret, w_in, attn_sinks,
           w_cond, b_cond, w_proj_attn, w_proj_ret, w_out, ln_g, ln_b):
    seq = x_prompt.shape[1]
    nbatch = x_sample.shape[0]
    win = cache_k.shape[2]
    assert seq % PROMPT_ROWS == 0 and nbatch % (seq // PROMPT_ROWS) == 0 and nbatch % SUBLANES == 0

    weights_f32 = (w_in, w_proj_attn, w_proj_ret, w_out)
    weights_b = tuple(w[0].astype(BF16) for w in weights_f32)
    ln_g3 = ln_g.reshape(DEPTH, 1, D_MODEL)
    ln_b3 = ln_b.reshape(DEPTH, 1, D_MODEL)

    c_all = jnp.concatenate([c_sample, c_prompt, jnp.zeros((SUBLANES - 1, D_MODEL), F32)], axis=0)
    mod = _cond_call(c_all, w_cond, b_cond)
    prompt_mod_block = nbatch // SUBLANES

    base, rtab = _prompt_rope_tables(seq, PROMPT_ROWS)
    stab = _sample_rope_table(float(PAST_LEN))
    ret_tabs = _retention_tables()

    def feature_major(c):
        return c.transpose(0, 1, 3, 4, 2).reshape(DEPTH, nbatch, ATT_KV_WIDTH, win)

    def window_major(c):
        return c.reshape(DEPTH, nbatch, ATT_KV_HEADS, ATT_HEAD_DIM, win).transpose(0, 1, 4, 2, 3)

    ck = feature_major(cache_k)
    cv = feature_major(cache_v)

    yp = x_prompt[0]
    ys = x_sample
    kp, vp, sp = [], [], []
    prev = None
    for l in range(DEPTH):
        w_in_b, w_pa_b, w_pr_b, w_out_b = weights_b
        qkv, ga, ret, rg, mg = _sample_proj(l, ys, mod, stab, w_in_b)
        outs = _fused_layer(
            l, yp, mod, prompt_mod_block, base, rtab, weights_b,
            weights_f32 if l + 1 < DEPTH else None,
            ln_g3, ln_b3, attn_sinks, ret_tabs, qkv, ret, ck, cv, state_ret, prev)
        yp, k_new, v_new, s_new, oatt, oret, nk, nv, ns = outs[:N_PROMPT_OUTPUTS]
        kp.append(k_new.reshape(1, WINDOW, ATT_KV_HEADS, ATT_HEAD_DIM))
        vp.append(v_new.reshape(1, WINDOW, ATT_KV_HEADS, ATT_HEAD_DIM))
        sp.append(s_new[None])
        prev = (nk, nv, ns)
        ys = _sample_out(l, ys, mod, oatt, ga, oret, rg, mg, w_pa_b, w_pr_b, w_out_b, ln_g3, ln_b3)
        weights_b = tuple(outs[N_PROMPT_OUTPUTS:])

    nk, nv, ns = prev
    return (yp[None], ys, jnp.stack(kp), jnp.stack(vp), jnp.stack(sp),
            window_major(nk), window_major(nv), ns)
```

```python
import functools

import jax
import jax.numpy as jnp
import numpy as np
from jax import lax
from jax.experimental import pallas as pl
from jax.experimental.pallas import tpu as pltpu

D_MODEL = 1024
DEPTH = 2
PAST_LEN = 16384
ATT_HEADS = 8
ATT_KV_HEADS = 2
ATT_HEAD_DIM = 64
ATT_GROUP = ATT_HEADS // ATT_KV_HEADS
ATT_WIDTH = ATT_HEADS * ATT_HEAD_DIM
ATT_KV_WIDTH = ATT_KV_HEADS * ATT_HEAD_DIM
WINDOW = 128
RET_HEADS = 4
RET_DK = 128
RET_DV = 256
RET_QK_WIDTH = RET_HEADS * RET_DK
RET_WIDTH = RET_HEADS * RET_DV
RET_CHUNK = 128
ROPE_THETA = 10000.0
ALPHA = (2.0 * DEPTH) ** 0.25
LN_EPS = 1e-5
RMS_EPS = 1e-6
ATT_SCALE = ATT_HEAD_DIM ** -0.5
RET_K_SCALE = RET_DK ** -0.5

C_AQ = 0
C_AK = C_AQ + ATT_WIDTH
C_AV = C_AK + ATT_KV_WIDTH
C_AG = C_AV + ATT_KV_WIDTH
C_RQ = C_AG + ATT_WIDTH
C_RK = C_RQ + RET_QK_WIDTH
C_RV = C_RK + RET_QK_WIDTH
C_RG = C_RV + RET_WIDTH
C_MA = C_RG + RET_WIDTH
C_MR = C_MA + D_MODEL
IN_COLS = C_MR + D_MODEL

LANES = 128
SUBLANES = 8
BF16_SUBLANES = 16
VMEM_LIMIT_BYTES = 56 * 1024 * 1024

PROMPT_ROWS = 256

BF16 = jnp.bfloat16
F32 = jnp.float32
NT = (((1,), (1,)), ((), ()))

_LOG_GAMMA = np.log(1.0 - 2.0 ** (-5.0 - np.arange(RET_HEADS, dtype=np.float64)))
CHUNK_DECAY = tuple(float(v) for v in np.exp(RET_CHUNK * _LOG_GAMMA))
TOKEN_DECAY = tuple(float(v) for v in np.exp(_LOG_GAMMA))


def _sigmoid(x):
    return 0.5 * jnp.tanh(0.5 * x) + 0.5


def _silu(x):
    return x * _sigmoid(x)


def _dot(a, b):
    return jnp.dot(a, b, preferred_element_type=F32)


def _rope_attn_tile(x, cos, sin_signed, first_half):
    rot = jnp.where(first_half, pltpu.roll(x, LANES - 32, 1), pltpu.roll(x, 32, 1))
    return x * cos + rot * sin_signed


def _rope_ret_tile(x, cos, sin_signed):
    return x * cos + pltpu.roll(x, 64, 1) * sin_signed


def _lane_iota(shape):
    return lax.broadcasted_iota(jnp.int32, shape, len(shape) - 1)


def _layer_spec(shape, layer):
    nd = len(shape)
    return pl.BlockSpec((None,) + tuple(shape), lambda i: (layer,) + (0,) * nd,
                        pipeline_mode=pl.Buffered(1))


def _const_spec(shape):
    nd = len(shape)
    return pl.BlockSpec(tuple(shape), lambda i: (0,) * nd, pipeline_mode=pl.Buffered(1))


def _rope_lane_patterns():
    lane = np.arange(LANES)
    f_att = ROPE_THETA ** (-(lane % 32) / 32.0)
    s_att = np.where(lane % 64 < 32, -1.0, 1.0)
    f_ret = ROPE_THETA ** (-(lane % 64) / 64.0)
    s_ret = np.where(lane < 64, -1.0, 1.0)
    return (f_att, s_att), (f_ret, s_ret)


def _prompt_rope_tables(seq, tm):
    starts = np.arange(seq // tm, dtype=np.float64)[:, None] * tm
    offs = np.arange(tm, dtype=np.float64)[:, None]
    base, within = [], []
    for freq, sign in _rope_lane_patterns():
        base += [np.cos(starts * freq), np.sin(starts * freq)]
        c, s = np.cos(offs * freq), np.sin(offs * freq)
        within += [c, s, sign * c, sign * s]
    return (jnp.asarray(np.stack(base, axis=1), F32),
            jnp.asarray(np.stack(within, axis=0), F32))


def _sample_rope_table(pos):
    rows = []
    for freq, sign in _rope_lane_patterns():
        rows += [np.cos(pos * freq), sign * np.sin(pos * freq)]
    return jnp.asarray(np.stack(rows, axis=0), F32)


def _retention_tables():
    c = RET_CHUNK
    idx = np.arange(c, dtype=np.float64)
    diff = idx[:, None] - idx[None, :]
    lg = _LOG_GAMMA[:, None, None]
    dmask = np.where(diff >= 0, np.exp(np.maximum(diff, 0.0)[None] * lg), 0.0)
    qdec = np.broadcast_to(np.exp((idx + 1.0)[None, :, None] * lg), (RET_HEADS, c, RET_DK))
    kdec = np.broadcast_to(np.exp((c - 1.0 - idx)[None, :, None] * lg), (RET_HEADS, c, RET_DK))
    return jnp.asarray(dmask, F32), jnp.asarray(qdec, F32), jnp.asarray(kdec, F32)


def _cond_kernel(c_ref, w_ref, b_ref, o_ref):
    a = _silu(c_ref[...]).astype(BF16)
    o_ref[...] = _dot(a, w_ref[...].astype(BF16)) + b_ref[...]


def _cond_call(c_all, w_cond, b_cond):
    rows = c_all.shape[0]
    tn = 3 * D_MODEL // 2
    return pl.pallas_call(
        _cond_kernel,
        out_shape=jax.ShapeDtypeStruct((DEPTH, rows, 3 * D_MODEL), F32),
        grid=(DEPTH, 3 * D_MODEL // tn),
        in_specs=[
            pl.BlockSpec((rows, D_MODEL), lambda l, j: (0, 0)),
            pl.BlockSpec((None, D_MODEL, tn), lambda l, j: (l, 0, j)),
            pl.BlockSpec((None, 1, tn), lambda l, j: (l, 0, j)),
        ],
        out_specs=pl.BlockSpec((None, rows, tn), lambda l, j: (l, 0, j)),
        compiler_params=pltpu.CompilerParams(
            dimension_semantics=("arbitrary", "arbitrary"), vmem_limit_bytes=VMEM_LIMIT_BYTES),
        name="cond_mod",
    )(c_all, w_cond, b_cond.reshape(DEPTH, 1, 3 * D_MODEL))


N_PROMPT_INPUTS = 19
N_PROMPT_OUTPUTS = 9
N_WEIGHTS = 4


def _prompt_kernel(*refs, layer, n_aliased, n_cast):
    (sinks_ref, x_ref, mod_ref, base_ref, rtab_ref,
     win_ref, wpa_ref, wpr_ref, wout_ref, lng_ref, lnb_ref,
     dmask_ref, qdec_ref, kdec_ref,
     sqkv_ref, sret_ref, ck_ref, cv_ref, st_ref) = refs[:N_PROMPT_INPUTS]
    cast_in = refs[N_PROMPT_INPUTS:N_PROMPT_INPUTS + n_cast]
    refs = refs[N_PROMPT_INPUTS + n_cast + n_aliased:]
    (y_ref, knew_ref, vnew_ref, snew_ref,
     oatt_ref, oret_ref, nk_ref, nv_ref, ns_ref) = refs[:N_PROMPT_OUTPUTS]
    cast_out = refs[N_PROMPT_OUTPUTS:N_PROMPT_OUTPUTS + n_cast]
    (tab_scr, hb_scr, q_scr, kvar_scr, vvar_scr, ga_scr, a_scr,
     rq_scr, rk_scr, rv_scr, rg_scr, r_scr, s_scr,
     ma_scr, mr_scr) = refs[N_PROMPT_OUTPUTS + n_cast:]
    step = pl.program_id(0)

    def cast_next_layer_weights():
        for src, dst in zip(cast_in, cast_out):
            dst[...] = src[...].astype(BF16)

    tm = x_ref.shape[0]
    nsub = tm // WINDOW

    def proj(c0, c1):
        return _dot(hb_scr[...], win_ref[:, c0:c1])

    lane = _lane_iota((tm, LANES))
    first_half32 = (lane & 32) == 0
    lo64 = lane < 64

    def rope_attn(t):
        return _rope_attn_tile(t, tab_scr[0], tab_scr[1], first_half32)

    def rope_ret(t):
        return _rope_ret_tile(t, tab_scr[2], tab_scr[3])

    def store_variants(scr, t, fill):
        swapped = pltpu.roll(t, 64, 1)
        other = jnp.full_like(t, fill)
        scr[0, WINDOW:WINDOW + tm, :] = jnp.where(lo64, t, other).astype(BF16)
        scr[1, WINDOW:WINDOW + tm, :] = jnp.where(lo64, other, swapped).astype(BF16)
        scr[2, WINDOW:WINDOW + tm, :] = jnp.where(lo64, swapped, other).astype(BF16)
        scr[3, WINDOW:WINDOW + tm, :] = jnp.where(lo64, other, t).astype(BF16)

    def block_prepare(src_x_ref, src_base_ref):
        shift = mod_ref[0:1, 0:D_MODEL]
        scale = mod_ref[0:1, D_MODEL:2 * D_MODEL]
        hb_scr[...] = (src_x_ref[...] * (1.0 + scale) + shift).astype(BF16)
        for fam in range(2):
            cb = src_base_ref[2 * fam:2 * fam + 1, :]
            sb = src_base_ref[2 * fam + 1:2 * fam + 2, :]
            tab_scr[2 * fam] = cb * rtab_ref[4 * fam] - sb * rtab_ref[4 * fam + 1]
            tab_scr[2 * fam + 1] = sb * rtab_ref[4 * fam + 2] + cb * rtab_ref[4 * fam + 3]

    def block_head():
        kv = proj(C_AK, C_AG)
        k_rot = rope_attn(kv[:, 0:LANES])
        v_raw = kv[:, LANES:2 * LANES]
        knew_ref[...] = k_rot[tm - WINDOW:, :]
        vnew_ref[...] = v_raw[tm - WINDOW:, :]
        store_variants(kvar_scr, k_rot, 0.0)
        store_variants(vvar_scr, v_raw, 1.0)
        qp = proj(C_AQ, C_AK)
        for t in range(ATT_WIDTH // LANES):
            qt = rope_attn(qp[:, t * LANES:(t + 1) * LANES])
            q_scr[:, t * LANES:(t + 1) * LANES] = (qt * ATT_SCALE).astype(BF16)
        ga_scr[...] = proj(C_AG, C_RQ)

    @pl.when(step == 0)
    def _():
        kvar_scr[:, 0:WINDOW, :] = jnp.zeros((4, WINDOW, LANES), BF16)
        vvar_scr[:, 0:WINDOW, :] = jnp.zeros((4, WINDOW, LANES), BF16)
        s_scr[...] = jnp.zeros(s_scr.shape, F32)

    block_prepare(x_ref, base_ref)
    block_head()

    def chunk_rq():
        p = proj(C_RQ, C_RK)
        for h in range(RET_HEADS):
            sl = slice(h * RET_DK, (h + 1) * RET_DK)
            rq_scr[:, sl] = rope_ret(p[:, sl])

    def chunk_rk():
        p = proj(C_RK, C_RV)
        for h in range(RET_HEADS):
            sl = slice(h * RET_DK, (h + 1) * RET_DK)
            rk_scr[:, sl] = rope_ret(p[:, sl]) * RET_K_SCALE

    def half_chunk(c0, dst, fn, half):
        w = dst.shape[1] // 2
        def run():
            dst[:, half * w:(half + 1) * w] = fn(proj(c0 + half * w, c0 + (half + 1) * w))
        return run

    to_bf16 = lambda v: v.astype(BF16)
    raw = lambda v: v
    att_companions = [
        chunk_rq, chunk_rk,
        half_chunk(C_RV, rv_scr, to_bf16, 0), half_chunk(C_RV, rv_scr, to_bf16, 1),
        half_chunk(C_RG, rg_scr, raw, 0), half_chunk(C_RG, rg_scr, raw, 1),
        half_chunk(C_MA, ma_scr, raw, 0), half_chunk(C_MA, ma_scr, raw, 1),
    ]

    row = lax.broadcasted_iota(jnp.int32, (WINDOW, 2 * WINDOW), 0)
    col = lax.broadcasted_iota(jnp.int32, (WINDOW, 2 * WINDOW), 1)
    in_window = col <= row + WINDOW
    mask_std = (col >= row) & in_window
    off = jnp.where(step > 0, 0, 4 * WINDOW)
    mask_first = ((col >= row + off) | (col >= WINDOW)) & in_window
    lo64_w = _lane_iota((WINDOW, LANES)) < 64

    def attention_logits(i, t):
        r0 = i * WINDOW
        g = t // 2
        qt = q_scr[r0:r0 + WINDOW, t * LANES:(t + 1) * LANES]
        return [lax.dot_general(qt, kvar_scr[2 * g + p, r0:r0 + 2 * WINDOW, :], NT,
                                preferred_element_type=F32) for p in range(2)]

    def attention_values(i, t, logits):
        r0 = i * WINDOW
        mask = mask_first if i == 0 else mask_std
        g = t // 2
        outs, sink_terms = [], []
        for p in range(2):
            head = 2 * t + p
            s = jnp.where(mask, logits[p], -jnp.inf)
            sink = sinks_ref[layer, head]
            m = jnp.maximum(jnp.max(s, axis=-1, keepdims=True), sink)
            pe = jnp.exp(s - m).astype(BF16)
            sink_terms.append(jnp.exp(sink - m))
            outs.append(_dot(pe, vvar_scr[2 * g + p, r0:r0 + 2 * WINDOW, :]))
        weighted = jnp.where(lo64_w, outs[0], outs[1])
        row_sums = pltpu.roll(jnp.where(lo64_w, outs[1], outs[0]), 64, 1)
        denom = row_sums + jnp.where(lo64_w, sink_terms[0], sink_terms[1])
        a_scr[r0:r0 + WINDOW, t * LANES:(t + 1) * LANES] = weighted * (1.0 / denom)

    n_tiles = ATT_WIDTH // LANES
    per_window = -(-len(att_companions) // nsub)
    for i in range(nsub):
        logits = [attention_logits(i, t) for t in range(n_tiles)]
        for run in att_companions[i * per_window:(i + 1) * per_window]:
            run()
        if i == 0:
            cast_next_layer_weights()
        for t in range(n_tiles):
            attention_values(i, t, logits[t])

    kvar_scr[:, 0:WINDOW, :] = kvar_scr[:, tm:tm + WINDOW, :]
    vvar_scr[:, 0:WINDOW, :] = vvar_scr[:, tm:tm + WINDOW, :]

    def ret_slices(c, h):
        rows = slice(c * RET_CHUNK, (c + 1) * RET_CHUNK)
        return rows, slice(h * RET_DK, (h + 1) * RET_DK), slice(h * RET_DV, (h + 1) * RET_DV)

    def retention_scores_and_state(c):
        inners, s_olds = [], []
        for h in range(RET_HEADS):
            rows, sk, sv = ret_slices(c, h)
            qh = rq_scr[rows, sk]
            kh = rk_scr[rows, sk]
            vh = rv_scr[rows, sv]
            inners.append(lax.dot_general(qh.astype(BF16), kh.astype(BF16), NT, preferred_element_type=F32))
            s_old = s_scr[h]
            s_olds.append(s_old.astype(BF16))
            kd = (kh * kdec_ref[h]).astype(BF16)
            s_scr[h] = s_old * CHUNK_DECAY[h] + lax.dot_general(
                kd, vh, (((0,), (0,)), ((), ())), preferred_element_type=F32)
        return inners, s_olds

    def retention_outputs(c, inners, s_olds):
        for h in range(RET_HEADS):
            rows, sk, sv = ret_slices(c, h)
            qh = rq_scr[rows, sk]
            lhs = jnp.concatenate([(inners[h] * dmask_ref[h]).astype(BF16),
                                   (qh * qdec_ref[h]).astype(BF16)], axis=1)
            rhs = jnp.concatenate([rv_scr[rows, sv], s_olds[h]], axis=0)
            o = _dot(lhs, rhs)
            ms = jnp.mean(o * o, axis=-1, keepdims=True)
            on = o * lax.rsqrt(ms + RMS_EPS)
            r_scr[rows, sv] = on

    ret_companions = [half_chunk(C_MR, mr_scr, raw, 0), half_chunk(C_MR, mr_scr, raw, 1)]
    n = 0
    for c in range(nsub):
        inners, s_olds = retention_scores_and_state(c)
        if n < len(ret_companions):
            ret_companions[n]()
        n += 1
        retention_outputs(c, inners, s_olds)
    for run in ret_companions[n:]:
        run()
    sample_qmats, sample_logits = _sample_logits(sqkv_ref, ck_ref)
    for b in range(sqkv_ref.shape[0]):
        for h in range(RET_HEADS):
            _sample_state_update(sret_ref, st_ref, ns_ref, b, h)

    gate_c = mod_ref[0:1, 2 * D_MODEL:3 * D_MODEL]
    lng = lng_ref[...]
    lnb = lnb_ref[...]
    windows = [slice(i * WINDOW, (i + 1) * WINDOW) for i in range(nsub)]
    za_all = _dot((a_scr[...] * _silu(ga_scr[...])).astype(BF16), wpa_ref[...])
    zr_all = _dot((r_scr[...] * _silu(rg_scr[...])).astype(BF16), wpr_ref[...])
    z_all = _sigmoid(ma_scr[...]) * za_all + _sigmoid(mr_scr[...]) * zr_all
    u_all = _dot(z_all.astype(BF16), wout_ref[...])
    for rows in windows:
        t = ALPHA * x_ref[rows, :] + gate_c * u_all[rows, :]
        mu = jnp.mean(t, axis=-1, keepdims=True)
        d = t - mu
        var = jnp.mean(d * d, axis=-1, keepdims=True)
        y_ref[rows, :] = d * lax.rsqrt(var + LN_EPS) * lng + lnb

    _sample_outputs(sinks_ref, layer, sqkv_ref, sret_ref, ck_ref, cv_ref, sample_qmats, sample_logits,
                    oatt_ref, oret_ref, nk_ref, nv_ref, ns_ref)

    @pl.when(step == pl.num_programs(0) - 1)
    def _():
        snew_ref[...] = s_scr[...]


def _fused_layer(layer, x, mod, mod_row_block, base, rtab, weights_b, next_weights,
                 ln_g, ln_b, sinks, ret_tabs, sample_qkv, sample_ret, cache_k, cache_v, state, shared_out):
    w_in_b, w_pa_b, w_pr_b, w_out_b = weights_b
    seq = x.shape[0]
    tm = PROMPT_ROWS
    steps = seq // tm
    nbatch = sample_qkv.shape[0]
    nb = nbatch // steps
    win = cache_k.shape[3]
    assert win == WINDOW == LANES
    dmask, qdec, kdec = ret_tabs
    smem = pl.BlockSpec(memory_space=pltpu.SMEM)
    row_spec = lambda w: pl.BlockSpec((tm, w), lambda i: (i, 0))
    seq_spec = lambda w: pl.BlockSpec((nb, 1, w), lambda i: (i, 0, 0))
    cache_spec = pl.BlockSpec((None, nb, ATT_KV_WIDTH, win), lambda i: (layer, i, 0, 0))
    state_spec = pl.BlockSpec((None, nb, RET_HEADS, RET_DK, RET_DV), lambda i: (layer, i, 0, 0, 0))
    in_specs = [
        smem,
        row_spec(D_MODEL),
        pl.BlockSpec((None, SUBLANES, 3 * D_MODEL), lambda i: (layer, mod_row_block, 0),
                     pipeline_mode=pl.Buffered(1)),
        pl.BlockSpec((None, 4, LANES), lambda i: (i, 0, 0)),
        _const_spec((8, tm, LANES)),
        _const_spec((D_MODEL, IN_COLS)),
        _const_spec((ATT_WIDTH, D_MODEL)),
        _const_spec((RET_WIDTH, D_MODEL)),
        _const_spec((D_MODEL, D_MODEL)),
        _layer_spec((1, D_MODEL), layer), _layer_spec((1, D_MODEL), layer),
        _const_spec((RET_HEADS, RET_CHUNK, RET_CHUNK)),
        _const_spec((RET_HEADS, RET_CHUNK, RET_DK)),
        _const_spec((RET_HEADS, RET_CHUNK, RET_DK)),
        seq_spec(C_AG), seq_spec(2 * RET_QK_WIDTH + RET_WIDTH),
        cache_spec, cache_spec, state_spec,
    ]
    args = [sinks, x, mod, base, rtab, w_in_b, w_pa_b, w_pr_b, w_out_b, ln_g, ln_b, dmask, qdec, kdec,
            sample_qkv, sample_ret, cache_k, cache_v, state]
    assert len(args) == N_PROMPT_INPUTS
    cast_shapes, cast_specs = [], []
    if next_weights is not None:
        assert len(next_weights) == N_WEIGHTS
        for w in next_weights:
            _, rows, cols = w.shape
            slab = max(BF16_SUBLANES, rows // steps)
            nslab = rows // slab
            assert rows % slab == 0 and nslab <= steps
            args.append(w)
            in_specs.append(pl.BlockSpec((None, slab, cols),
                                         lambda i, n=nslab: (layer + 1, jnp.minimum(i, n - 1), 0)))
            cast_shapes.append(jax.ShapeDtypeStruct((rows, cols), BF16))
            cast_specs.append(pl.BlockSpec((slab, cols), lambda i, n=nslab: (jnp.minimum(i, n - 1), 0)))
    aliases = {}
    if shared_out is not None:
        for j, arr in enumerate(shared_out):
            aliases[len(args)] = N_PROMPT_OUTPUTS - len(shared_out) + j
            args.append(arr)
            in_specs.append(pl.BlockSpec(memory_space=pl.ANY))
    return pl.pallas_call(
        functools.partial(_prompt_kernel, layer=layer, n_aliased=len(aliases), n_cast=len(cast_shapes)),
        out_shape=(
            jax.ShapeDtypeStruct((seq, D_MODEL), F32),
            jax.ShapeDtypeStruct((WINDOW, ATT_KV_WIDTH), F32),
            jax.ShapeDtypeStruct((WINDOW, ATT_KV_WIDTH), F32),
            jax.ShapeDtypeStruct((RET_HEADS, RET_DK, RET_DV), F32),
            jax.ShapeDtypeStruct((nbatch, 1, ATT_WIDTH), F32),
            jax.ShapeDtypeStruct((nbatch, 1, RET_WIDTH), F32),
            jax.ShapeDtypeStruct(cache_k.shape, F32),
            jax.ShapeDtypeStruct(cache_v.shape, F32),
            jax.ShapeDtypeStruct(state.shape, F32),
        ) + tuple(cast_shapes),
        grid=(steps,),
        in_specs=in_specs,
        out_specs=(
            row_spec(D_MODEL),
            pl.BlockSpec((WINDOW, ATT_KV_WIDTH), lambda i: (0, 0)),
            pl.BlockSpec((WINDOW, ATT_KV_WIDTH), lambda i: (0, 0)),
            pl.BlockSpec((RET_HEADS, RET_DK, RET_DV), lambda i: (0, 0, 0)),
            seq_spec(ATT_WIDTH), seq_spec(RET_WIDTH),
            cache_spec, cache_spec, state_spec,
        ) + tuple(cast_specs),
        input_output_aliases=aliases,
        scratch_shapes=[
            pltpu.VMEM((4, tm, LANES), F32),
            pltpu.VMEM((tm, D_MODEL), BF16),
            pltpu.VMEM((tm, ATT_WIDTH), BF16),
            pltpu.VMEM((4, WINDOW + tm, LANES), BF16),
            pltpu.VMEM((4, WINDOW + tm, LANES), BF16),
            pltpu.VMEM((tm, ATT_WIDTH), F32),
            pltpu.VMEM((tm, ATT_WIDTH), F32),
            pltpu.VMEM((tm, RET_QK_WIDTH), F32),
            pltpu.VMEM((tm, RET_QK_WIDTH), F32),
            pltpu.VMEM((tm, RET_WIDTH), BF16),
            pltpu.VMEM((tm, RET_WIDTH), F32),
            pltpu.VMEM((tm, RET_WIDTH), F32),
            pltpu.VMEM((RET_HEADS, RET_DK, RET_DV), F32),
            pltpu.VMEM((tm, D_MODEL), F32),
            pltpu.VMEM((tm, D_MODEL), F32),
        ],
        compiler_params=pltpu.CompilerParams(
            dimension_semantics=("arbitrary",), vmem_limit_bytes=VMEM_LIMIT_BYTES),
        name="fused_layer",
    )(*args)


def _sample_proj_kernel(x_ref, mod_ref, tab_ref, win_ref,
                        qkv_ref, ga_ref, ret_ref, rg_ref, mg_ref):
    rows = x_ref.shape[0]
    x = x_ref[:, 0, :]
    shift = mod_ref[:, 0:D_MODEL]
    scale = mod_ref[:, D_MODEL:2 * D_MODEL]
    hb = (x * (1.0 + scale) + shift).astype(BF16)

    def proj(c0, c1):
        return _dot(hb, win_ref[:, c0:c1])

    lane = _lane_iota((rows, LANES))
    first_half32 = (lane & 32) == 0
    ca = tab_ref[0:1, :]
    sa = tab_ref[1:2, :]
    cr = tab_ref[2:3, :]
    sr = tab_ref[3:4, :]

    qkv = proj(C_AQ, C_AG)
    for t in range(ATT_WIDTH // LANES):
        qt = _rope_attn_tile(qkv[:, t * LANES:(t + 1) * LANES], ca, sa, first_half32)
        qkv_ref[:, 0, t * LANES:(t + 1) * LANES] = qt * ATT_SCALE
    qkv_ref[:, 0, C_AK:C_AV] = _rope_attn_tile(qkv[:, C_AK:C_AV], ca, sa, first_half32)
    qkv_ref[:, 0, C_AV:C_AG] = qkv[:, C_AV:C_AG]
    ga_ref[...] = _silu(proj(C_AG, C_RQ))

    rqk = proj(C_RQ, C_RV)
    for h in range(RET_HEADS):
        sl = slice(h * RET_DK, (h + 1) * RET_DK)
        ret_ref[:, 0, sl] = _rope_ret_tile(rqk[:, sl], cr, sr)
        sk = slice(RET_QK_WIDTH + h * RET_DK, RET_QK_WIDTH + (h + 1) * RET_DK)
        ret_ref[:, 0, sk] = _rope_ret_tile(rqk[:, sk], cr, sr) * RET_K_SCALE
    ret_ref[:, 0, 2 * RET_QK_WIDTH:] = proj(C_RV, C_RG)
    rg_ref[...] = _silu(proj(C_RG, C_MA))
    mg_ref[...] = _sigmoid(proj(C_MA, IN_COLS))


def _sample_proj(layer, x, mod, tab, w_in_b):
    rows = x.shape[0]
    shapes = ((rows, 1, C_AG), (rows, ATT_WIDTH), (rows, 1, 2 * RET_QK_WIDTH + RET_WIDTH),
              (rows, RET_WIDTH), (rows, 2 * D_MODEL))
    return pl.pallas_call(
        _sample_proj_kernel,
        out_shape=tuple(jax.ShapeDtypeStruct(s, F32) for s in shapes),
        grid=(1,),
        in_specs=[
            _const_spec((rows, 1, D_MODEL)),
            _layer_spec((rows, 3 * D_MODEL), layer),
            _const_spec((4, LANES)),
            _const_spec((D_MODEL, IN_COLS)),
        ],
        out_specs=tuple(pl.BlockSpec(s, lambda i, nd=len(s): (0,) * nd) for s in shapes),
        compiler_params=pltpu.CompilerParams(
            dimension_semantics=("arbitrary",), vmem_limit_bytes=VMEM_LIMIT_BYTES),
        name="sample_proj",
    )(x, mod, tab, w_in_b)


def _sample_logits(qkv_ref, ck_ref):
    nb = qkv_ref.shape[0]
    rowi = lax.broadcasted_iota(jnp.int32, (ATT_HEADS, LANES), 0)
    lanei = lax.broadcasted_iota(jnp.int32, (ATT_HEADS, LANES), 1)
    lane_group = lanei // ATT_HEAD_DIM
    qmats, logits = [], []
    for b in range(nb):
        qkv = qkv_ref[b]
        qmat = jnp.zeros((ATT_HEADS, LANES), F32)
        for h in range(ATT_HEADS):
            t, p, g = h // 2, h % 2, h // ATT_GROUP
            tile = qkv[:, t * LANES:(t + 1) * LANES]
            src = tile if p == g else pltpu.roll(tile, 64, 1)
            qmat = jnp.where((rowi == h) & (lane_group == g), jnp.broadcast_to(src, (ATT_HEADS, LANES)), qmat)
        qmats.append(qmat)
        logits.append(_dot(qmat.astype(BF16), ck_ref[b].astype(BF16)))
    return qmats, logits


def _sample_state_update(ret_ref, st_ref, ns_ref, b, h):
    dr = lax.broadcasted_iota(jnp.int32, (RET_DK, RET_DK), 0)
    dc = lax.broadcasted_iota(jnp.int32, (RET_DK, RET_DK), 1)
    ret = ret_ref[b]
    kh = ret[:, RET_QK_WIDTH + h * RET_DK:RET_QK_WIDTH + (h + 1) * RET_DK]
    vh = ret[:, 2 * RET_QK_WIDTH + h * RET_DV:2 * RET_QK_WIDTH + (h + 1) * RET_DV]
    k_col = jnp.sum(jnp.where(dr == dc, jnp.broadcast_to(kh, (RET_DK, RET_DK)), 0.0),
                    axis=1, keepdims=True)
    ns_ref[b, h] = st_ref[b, h] * TOKEN_DECAY[h] + k_col * vh


def _sample_outputs(sinks_ref, layer, qkv_ref, ret_ref, ck_ref, cv_ref, qmats, logits,
                    oatt_ref, oret_ref, nk_ref, nv_ref, ns_ref):
    nb = qkv_ref.shape[0]
    sink_col = jnp.zeros((ATT_HEADS, 1), F32)
    rowc = lax.broadcasted_iota(jnp.int32, (ATT_HEADS, 1), 0)
    for h in range(ATT_HEADS):
        sink_col = jnp.where(rowc == h, sinks_ref[layer, h], sink_col)
    lo64_row = _lane_iota((1, LANES)) < 64
    dr = lax.broadcasted_iota(jnp.int32, (LANES, LANES), 0)
    dc = lax.broadcasted_iota(jnp.int32, (LANES, LANES), 1)
    diag = dr == dc
    last_lane = dc == WINDOW - 1

    for b in range(nb):
        qkv = qkv_ref[b]
        k_new = qkv[:, C_AK:C_AV]
        v_new = qkv[:, C_AV:C_AG]
        s_c = logits[b]
        s_self = jnp.sum(qmats[b] * k_new, axis=-1, keepdims=True)
        m = jnp.maximum(jnp.maximum(jnp.max(s_c, axis=-1, keepdims=True), s_self), sink_col)
        p_c = jnp.exp(s_c - m)
        p_self = jnp.exp(s_self - m)
        denom = jnp.sum(p_c, axis=-1, keepdims=True) + p_self + jnp.exp(sink_col - m)
        o = (lax.dot_general(p_c.astype(BF16), cv_ref[b].astype(BF16), NT, preferred_element_type=F32)
             + p_self * v_new) / denom
        o_sw = pltpu.roll(o, 64, 1)
        for t in range(ATT_WIDTH // LANES):
            g = t // 2
            first = (o if g == 0 else o_sw)[2 * t:2 * t + 1, :]
            second = (o_sw if g == 0 else o)[2 * t + 1:2 * t + 2, :]
            oatt_ref[b, :, t * LANES:(t + 1) * LANES] = jnp.where(lo64_row, first, second)

        for new_row, src_ref, dst_ref in ((k_new, ck_ref, nk_ref), (v_new, cv_ref, nv_ref)):
            new_col = jnp.sum(jnp.where(diag, jnp.broadcast_to(new_row, (LANES, LANES)), 0.0),
                              axis=1, keepdims=True)
            shifted = pltpu.roll(src_ref[b], WINDOW - 1, 1)
            dst_ref[b] = jnp.where(last_lane, new_col, shifted)

    for b in range(nb):
        ret = ret_ref[b]
        for h in range(RET_HEADS):
            qh = ret[:, h * RET_DK:(h + 1) * RET_DK]
            q8 = jnp.broadcast_to(qh, (SUBLANES, RET_DK)).astype(BF16)
            oh = _dot(q8, ns_ref[b, h].astype(BF16))[0:1, :]
            ms = jnp.mean(oh * oh, axis=-1, keepdims=True)
            oret_ref[b, :, h * RET_DV:(h + 1) * RET_DV] = oh * lax.rsqrt(ms + RMS_EPS)


def _sample_out_kernel(x_ref, mod_ref, oatt_ref, ga_ref, oret_ref, rg_ref, mg_ref,
                       wpa_ref, wpr_ref, wout_ref, lng_ref, lnb_ref, y_ref):
    gate_c = mod_ref[:, 2 * D_MODEL:3 * D_MODEL]
    za = _dot((oatt_ref[:, 0, :] * ga_ref[...]).astype(BF16), wpa_ref[...])
    zr = _dot((oret_ref[:, 0, :] * rg_ref[...]).astype(BF16), wpr_ref[...])
    z = mg_ref[:, 0:D_MODEL] * za + mg_ref[:, D_MODEL:2 * D_MODEL] * zr
    u = _dot(z.astype(BF16), wout_ref[...])
    t = ALPHA * x_ref[:, 0, :] + gate_c * u
    mu = jnp.mean(t, axis=-1, keepdims=True)
    d = t - mu
    var = jnp.mean(d * d, axis=-1, keepdims=True)
    y_ref[:, 0, :] = d * lax.rsqrt(var + LN_EPS) * lng_ref[...] + lnb_ref[...]


def _sample_out(layer, x, mod, oatt, ga, oret, rg, mg, w_pa_b, w_pr_b, w_out_b, ln_g, ln_b):
    rows = x.shape[0]
    return pl.pallas_call(
        _sample_out_kernel,
        out_shape=jax.ShapeDtypeStruct(x.shape, F32),
        grid=(1,),
        in_specs=[
            _const_spec((rows, 1, D_MODEL)),
            _layer_spec((rows, 3 * D_MODEL), layer),
            _const_spec((rows, 1, ATT_WIDTH)), _const_spec((rows, ATT_WIDTH)),
            _const_spec((rows, 1, RET_WIDTH)), _const_spec((rows, RET_WIDTH)),
            _const_spec((rows, 2 * D_MODEL)),
            _const_spec((ATT_WIDTH, D_MODEL)),
            _const_spec((RET_WIDTH, D_MODEL)),
            _const_spec((D_MODEL, D_MODEL)),
            _layer_spec((1, D_MODEL), layer), _layer_spec((1, D_MODEL), layer),
        ],
        out_specs=pl.BlockSpec((rows, 1, D_MODEL), lambda i: (0, 0, 0)),
        compiler_params=pltpu.CompilerParams(
            dimension_semantics=("arbitrary",), vmem_limit_bytes=VMEM_LIMIT_BYTES),
        name="sample_out",
    )(x, mod, oatt, ga, oret, rg, mg, w_pa_b, w_pr_b, w_out_b, ln_g, ln_b)


def kernel(x_prompt, x_sample, c_prompt, c_sample, cache_k, cache_v, state_ret, w_in, attn_sinks,
           w_cond, b_cond, w_proj_attn, w_proj_ret, w_out, ln_g, ln_b):
    seq = x_prompt.shape[1]
    nbatch = x_sample.shape[0]
    win = cache_k.shape[2]
    assert seq % PROMPT_ROWS == 0 and nbatch % (seq // PROMPT_ROWS) == 0 and nbatch % SUBLANES == 0

    weights_f32 = (w_in, w_proj_attn, w_proj_ret, w_out)
    weights_b = tuple(w[0].astype(BF16) for w in weights_f32)
    ln_g3 = ln_g.reshape(DEPTH, 1, D_MODEL)
    ln_b3 = ln_b.reshape(DEPTH, 1, D_MODEL)

    c_all = jnp.concatenate([c_sample, c_prompt, jnp.zeros((SUBLANES - 1, D_MODEL), F32)], axis=0)
    mod = _cond_call(c_all, w_cond, b_cond)
    prompt_mod_block = nbatch // SUBLANES

    base, rtab = _prompt_rope_tables(seq, PROMPT_ROWS)
    stab = _sample_rope_table(float(PAST_LEN))
    ret_tabs = _retention_tables()

    def feature_major(c):
        return c.transpose(0, 1, 3, 4, 2).reshape(DEPTH, nbatch, ATT_KV_WIDTH, win)

    def window_major(c):
        return c.reshape(DEPTH, nbatch, ATT_KV_HEADS, ATT_HEAD_DIM, win).transpose(0, 1, 4, 2, 3)

    ck = feature_major(cache_k)
    cv = feature_major(cache_v)

    yp = x_prompt[0]
    ys = x_sample
    kp, vp, sp = [], [], []
    prev = None
    for l in range(DEPTH):
        w_in_b, w_pa_b, w_pr_b, w_out_b = weights_b
        qkv, ga, ret, rg, mg = _sample_proj(l, ys, mod, stab, w_in_b)
        outs = _fused_layer(
            l, yp, mod, prompt_mod_block, base, rtab, weights_b,
            weights_f32 if l + 1 < DEPTH else None,
            ln_g3, ln_b3, attn_sinks, ret_tabs, qkv, ret, ck, cv, state_ret, prev)
        yp, k_new, v_new, s_new, oatt, oret, nk, nv, ns = outs[:N_PROMPT_OUTPUTS]
        kp.append(k_new.reshape(1, WINDOW, ATT_KV_HEADS, ATT_HEAD_DIM))
        vp.append(v_new.reshape(1, WINDOW, ATT_KV_HEADS, ATT_HEAD_DIM))
        sp.append(s_new[None])
        prev = (nk, nv, ns)
        ys = _sample_out(l, ys, mod, oatt, ga, oret, rg, mg, w_pa_b, w_pr_b, w_out_b, ln_g3, ln_b3)
        weights_b = tuple(outs[N_PROMPT_OUTPUTS:])

    nk, nv, ns = prev
    return (yp[None], ys, jnp.stack(kp), jnp.stack(vp), jnp.stack(sp),
            window_major(nk), window_major(nv), ns)
```

```python
import functools

import jax
import jax.numpy as jnp
import numpy as np
from jax import lax
from jax.experimental import pallas as pl
from jax.experimental.pallas import tpu as pltpu

D_MODEL = 1024
DEPTH = 2
PAST_LEN = 16384
ATT_HEADS = 8
ATT_KV_HEADS = 2
ATT_HEAD_DIM = 64
ATT_GROUP = ATT_HEADS // ATT_KV_HEADS
ATT_WIDTH = ATT_HEADS * ATT_HEAD_DIM
ATT_KV_WIDTH = ATT_KV_HEADS * ATT_HEAD_DIM
WINDOW = 128
RET_HEADS = 4
RET_DK = 128
RET_DV = 256
RET_QK_WIDTH = RET_HEADS * RET_DK
RET_WIDTH = RET_HEADS * RET_DV
RET_CHUNK = 128
ROPE_THETA = 10000.0
ALPHA = (2.0 * DEPTH) ** 0.25
LN_EPS = 1e-5
RMS_EPS = 1e-6
ATT_SCALE = ATT_HEAD_DIM ** -0.5
RET_K_SCALE = RET_DK ** -0.5

C_AQ = 0
C_AK = C_AQ + ATT_WIDTH
C_AV = C_AK + ATT_KV_WIDTH
C_AG = C_AV + ATT_KV_WIDTH
C_RQ = C_AG + ATT_WIDTH
C_RK = C_RQ + RET_QK_WIDTH
C_RV = C_RK + RET_QK_WIDTH
C_RG = C_RV + RET_WIDTH
C_MA = C_RG + RET_WIDTH
C_MR = C_MA + D_MODEL
IN_COLS = C_MR + D_MODEL

LANES = 128
SUBLANES = 8
BF16_SUBLANES = 16
VMEM_LIMIT_BYTES = 56 * 1024 * 1024

PROMPT_ROWS = 128

BF16 = jnp.bfloat16
F32 = jnp.float32
NT = (((1,), (1,)), ((), ()))

_LOG_GAMMA = np.log(1.0 - 2.0 ** (-5.0 - np.arange(RET_HEADS, dtype=np.float64)))
CHUNK_DECAY = tuple(float(v) for v in np.exp(RET_CHUNK * _LOG_GAMMA))
TOKEN_DECAY = tuple(float(v) for v in np.exp(_LOG_GAMMA))


def _sigmoid(x):
    return 0.5 * jnp.tanh(0.5 * x) + 0.5


def _silu(x):
    return x * _sigmoid(x)


def _dot(a, b):
    return jnp.dot(a, b, preferred_element_type=F32)


def _rope_attn_tile(x, cos, sin_signed, first_half):
    rot = jnp.where(first_half, pltpu.roll(x, LANES - 32, 1), pltpu.roll(x, 32, 1))
    return x * cos + rot * sin_signed


def _rope_ret_tile(x, cos, sin_signed):
    return x * cos + pltpu.roll(x, 64, 1) * sin_signed


def _lane_iota(shape):
    return lax.broadcasted_iota(jnp.int32, shape, len(shape) - 1)


def _layer_spec(shape, layer):
    nd = len(shape)
    return pl.BlockSpec((None,) + tuple(shape), lambda i: (layer,) + (0,) * nd,
                        pipeline_mode=pl.Buffered(1))


def _const_spec(shape):
    nd = len(shape)
    return pl.BlockSpec(tuple(shape), lambda i: (0,) * nd, pipeline_mode=pl.Buffered(1))


def _rope_lane_patterns():
    lane = np.arange(LANES)
    f_att = ROPE_THETA ** (-(lane % 32) / 32.0)
    s_att = np.where(lane % 64 < 32, -1.0, 1.0)
    f_ret = ROPE_THETA ** (-(lane % 64) / 64.0)
    s_ret = np.where(lane < 64, -1.0, 1.0)
    return (f_att, s_att), (f_ret, s_ret)


def _prompt_rope_tables(seq, tm):
    starts = np.arange(seq // tm, dtype=np.float64)[:, None] * tm
    offs = np.arange(tm, dtype=np.float64)[:, None]
    base, within = [], []
    for freq, sign in _rope_lane_patterns():
        base += [np.cos(starts * freq), np.sin(starts * freq)]
        c, s = np.cos(offs * freq), np.sin(offs * freq)
        within += [c, s, sign * c, sign * s]
    return (jnp.asarray(np.stack(base, axis=1), F32),
            jnp.asarray(np.stack(within, axis=0), F32))


def _sample_rope_table(pos):
    rows = []
    for freq, sign in _rope_lane_patterns():
        rows += [np.cos(pos * freq), sign * np.sin(pos * freq)]
    return jnp.asarray(np.stack(rows, axis=0), F32)


def _retention_tables():
    c = RET_CHUNK
    idx = np.arange(c, dtype=np.float64)
    diff = idx[:, None] - idx[None, :]
    lg = _LOG_GAMMA[:, None, None]
    dmask = np.where(diff >= 0, np.exp(np.maximum(diff, 0.0)[None] * lg), 0.0)
    qdec = np.broadcast_to(np.exp((idx + 1.0)[None, :, None] * lg), (RET_HEADS, c, RET_DK))
    kdec = np.broadcast_to(np.exp((c - 1.0 - idx)[None, :, None] * lg), (RET_HEADS, c, RET_DK))
    return jnp.asarray(dmask, F32), jnp.asarray(qdec, F32), jnp.asarray(kdec, F32)


def _cond_kernel(c_ref, w_ref, b_ref, o_ref):
    a = _silu(c_ref[...]).astype(BF16)
    o_ref[...] = _dot(a, w_ref[...].astype(BF16)) + b_ref[...]


def _cond_call(c_all, w_cond, b_cond):
    rows = c_all.shape[0]
    tn = 3 * D_MODEL // 2
    return pl.pallas_call(
        _cond_kernel,
        out_shape=jax.ShapeDtypeStruct((DEPTH, rows, 3 * D_MODEL), F32),
        grid=(DEPTH, 3 * D_MODEL // tn),
        in_specs=[
            pl.BlockSpec((rows, D_MODEL), lambda l, j: (0, 0)),
            pl.BlockSpec((None, D_MODEL, tn), lambda l, j: (l, 0, j)),
            pl.BlockSpec((None, 1, tn), lambda l, j: (l, 0, j)),
        ],
        out_specs=pl.BlockSpec((None, rows, tn), lambda l, j: (l, 0, j)),
        compiler_params=pltpu.CompilerParams(
            dimension_semantics=("arbitrary", "arbitrary"), vmem_limit_bytes=VMEM_LIMIT_BYTES),
        name="cond_mod",
    )(c_all, w_cond, b_cond.reshape(DEPTH, 1, 3 * D_MODEL))


N_PROMPT_INPUTS = 19
N_PROMPT_OUTPUTS = 9
N_WEIGHTS = 4


def _prompt_kernel(*refs, layer, n_aliased, n_cast):
    (sinks_ref, x_ref, mod_ref, base_ref, rtab_ref,
     win_ref, wpa_ref, wpr_ref, wout_ref, lng_ref, lnb_ref,
     dmask_ref, qdec_ref, kdec_ref,
     sqkv_ref, sret_ref, ck_ref, cv_ref, st_ref) = refs[:N_PROMPT_INPUTS]
    cast_in = refs[N_PROMPT_INPUTS:N_PROMPT_INPUTS + n_cast]
    refs = refs[N_PROMPT_INPUTS + n_cast + n_aliased:]
    (y_ref, knew_ref, vnew_ref, snew_ref,
     oatt_ref, oret_ref, nk_ref, nv_ref, ns_ref) = refs[:N_PROMPT_OUTPUTS]
    cast_out = refs[N_PROMPT_OUTPUTS:N_PROMPT_OUTPUTS + n_cast]
    (tab_scr, hb_scr, q_scr, kvar_scr, vvar_scr, ga_scr, a_scr,
     rq_scr, rk_scr, rv_scr, rg_scr, r_scr, s_scr,
     ma_scr, mr_scr) = refs[N_PROMPT_OUTPUTS + n_cast:]
    step = pl.program_id(0)

    def cast_next_layer_weights():
        for src, dst in zip(cast_in, cast_out):
            dst[...] = src[...].astype(BF16)

    tm = x_ref.shape[0]
    nsub = tm // WINDOW

    def proj(c0, c1):
        return _dot(hb_scr[...], win_ref[:, c0:c1])

    lane = _lane_iota((tm, LANES))
    first_half32 = (lane & 32) == 0
    lo64 = lane < 64

    def rope_attn(t):
        return _rope_attn_tile(t, tab_scr[0], tab_scr[1], first_half32)

    def rope_ret(t):
        return _rope_ret_tile(t, tab_scr[2], tab_scr[3])

    def store_variants(scr, t, fill):
        swapped = pltpu.roll(t, 64, 1)
        other = jnp.full_like(t, fill)
        scr[0, WINDOW:WINDOW + tm, :] = jnp.where(lo64, t, other).astype(BF16)
        scr[1, WINDOW:WINDOW + tm, :] = jnp.where(lo64, other, swapped).astype(BF16)
        scr[2, WINDOW:WINDOW + tm, :] = jnp.where(lo64, swapped, other).astype(BF16)
        scr[3, WINDOW:WINDOW + tm, :] = jnp.where(lo64, other, t).astype(BF16)

    def block_prepare(src_x_ref, src_base_ref):
        shift = mod_ref[0:1, 0:D_MODEL]
        scale = mod_ref[0:1, D_MODEL:2 * D_MODEL]
        hb_scr[...] = (src_x_ref[...] * (1.0 + scale) + shift).astype(BF16)
        for fam in range(2):
            cb = src_base_ref[2 * fam:2 * fam + 1, :]
            sb = src_base_ref[2 * fam + 1:2 * fam + 2, :]
            tab_scr[2 * fam] = cb * rtab_ref[4 * fam] - sb * rtab_ref[4 * fam + 1]
            tab_scr[2 * fam + 1] = sb * rtab_ref[4 * fam + 2] + cb * rtab_ref[4 * fam + 3]

    def block_head():
        kv = proj(C_AK, C_AG)
        k_rot = rope_attn(kv[:, 0:LANES])
        v_raw = kv[:, LANES:2 * LANES]
        knew_ref[...] = k_rot[tm - WINDOW:, :]
        vnew_ref[...] = v_raw[tm - WINDOW:, :]
        store_variants(kvar_scr, k_rot, 0.0)
        store_variants(vvar_scr, v_raw, 1.0)
        qp = proj(C_AQ, C_AK)
        for t in range(ATT_WIDTH // LANES):
            qt = rope_attn(qp[:, t * LANES:(t + 1) * LANES])
            q_scr[:, t * LANES:(t + 1) * LANES] = (qt * ATT_SCALE).astype(BF16)
        ga_scr[...] = proj(C_AG, C_RQ)

    @pl.when(step == 0)
    def _():
        kvar_scr[:, 0:WINDOW, :] = jnp.zeros((4, WINDOW, LANES), BF16)
        vvar_scr[:, 0:WINDOW, :] = jnp.zeros((4, WINDOW, LANES), BF16)
        s_scr[...] = jnp.zeros(s_scr.shape, F32)

    block_prepare(x_ref, base_ref)
    block_head()

    def chunk_rq():
        p = proj(C_RQ, C_RK)
        for h in range(RET_HEADS):
            sl = slice(h * RET_DK, (h + 1) * RET_DK)
            rq_scr[:, sl] = rope_ret(p[:, sl])

    def chunk_rk():
        p = proj(C_RK, C_RV)
        for h in range(RET_HEADS):
            sl = slice(h * RET_DK, (h + 1) * RET_DK)
            rk_scr[:, sl] = rope_ret(p[:, sl]) * RET_K_SCALE

    def half_chunk(c0, dst, fn, half):
        w = dst.shape[1] // 2
        def run():
            dst[:, half * w:(half + 1) * w] = fn(proj(c0 + half * w, c0 + (half + 1) * w))
        return run

    to_bf16 = lambda v: v.astype(BF16)
    raw = lambda v: v
    att_companions = [
        chunk_rq, chunk_rk,
        half_chunk(C_RV, rv_scr, to_bf16, 0), half_chunk(C_RV, rv_scr, to_bf16, 1),
        half_chunk(C_RG, rg_scr, raw, 0), half_chunk(C_RG, rg_scr, raw, 1),
        half_chunk(C_MA, ma_scr, raw, 0), half_chunk(C_MA, ma_scr, raw, 1),
    ]

    row = lax.broadcasted_iota(jnp.int32, (WINDOW, 2 * WINDOW), 0)
    col = lax.broadcasted_iota(jnp.int32, (WINDOW, 2 * WINDOW), 1)
    in_window = col <= row + WINDOW
    mask_std = (col >= row) & in_window
    off = jnp.where(step > 0, 0, 4 * WINDOW)
    mask_first = ((col >= row + off) | (col >= WINDOW)) & in_window
    lo64_w = _lane_iota((WINDOW, LANES)) < 64

    def attention_logits(i, t):
        r0 = i * WINDOW
        g = t // 2
        qt = q_scr[r0:r0 + WINDOW, t * LANES:(t + 1) * LANES]
        return [lax.dot_general(qt, kvar_scr[2 * g + p, r0:r0 + 2 * WINDOW, :], NT,
                                preferred_element_type=F32) for p in range(2)]

    def attention_values(i, t, logits):
        r0 = i * WINDOW
        mask = mask_first if i == 0 else mask_std
        g = t // 2
        outs, sink_terms = [], []
        for p in range(2):
            head = 2 * t + p
            s = jnp.where(mask, logits[p], -jnp.inf)
            sink = sinks_ref[layer, head]
            m = jnp.maximum(jnp.max(s, axis=-1, keepdims=True), sink)
            pe = jnp.exp(s - m).astype(BF16)
            sink_terms.append(jnp.exp(sink - m))
            outs.append(_dot(pe, vvar_scr[2 * g + p, r0:r0 + 2 * WINDOW, :]))
        weighted = jnp.where(lo64_w, outs[0], outs[1])
        row_sums = pltpu.roll(jnp.where(lo64_w, outs[1], outs[0]), 64, 1)
        denom = row_sums + jnp.where(lo64_w, sink_terms[0], sink_terms[1])
        a_scr[r0:r0 + WINDOW, t * LANES:(t + 1) * LANES] = weighted * (1.0 / denom)

    n_tiles = ATT_WIDTH // LANES
    per_window = -(-len(att_companions) // nsub)
    for i in range(nsub):
        logits = [attention_logits(i, t) for t in range(n_tiles)]
        for run in att_companions[i * per_window:(i + 1) * per_window]:
            run()
        if i == 0:
            cast_next_layer_weights()
        for t in range(n_tiles):
            attention_values(i, t, logits[t])

    kvar_scr[:, 0:WINDOW, :] = kvar_scr[:, tm:tm + WINDOW, :]
    vvar_scr[:, 0:WINDOW, :] = vvar_scr[:, tm:tm + WINDOW, :]

    def ret_slices(c, h):
        rows = slice(c * RET_CHUNK, (c + 1) * RET_CHUNK)
        return rows, slice(h * RET_DK, (h + 1) * RET_DK), slice(h * RET_DV, (h + 1) * RET_DV)

    def retention_scores_and_state(c):
        inners, s_olds = [], []
        for h in range(RET_HEADS):
            rows, sk, sv = ret_slices(c, h)
            qh = rq_scr[rows, sk]
            kh = rk_scr[rows, sk]
            vh = rv_scr[rows, sv]
            inners.append(lax.dot_general(qh.astype(BF16), kh.astype(BF16), NT, preferred_element_type=F32))
            s_old = s_scr[h]
            s_olds.append(s_old.astype(BF16))
            kd = (kh * kdec_ref[h]).astype(BF16)
            s_scr[h] = s_old * CHUNK_DECAY[h] + lax.dot_general(
                kd, vh, (((0,), (0,)), ((), ())), preferred_element_type=F32)
        return inners, s_olds

    def retention_outputs(c, inners, s_olds):
        for h in range(RET_HEADS):
            rows, sk, sv = ret_slices(c, h)
            qh = rq_scr[rows, sk]
            lhs = jnp.concatenate([(inners[h] * dmask_ref[h]).astype(BF16),
                                   (qh * qdec_ref[h]).astype(BF16)], axis=1)
            rhs = jnp.concatenate([rv_scr[rows, sv], s_olds[h]], axis=0)
            o = _dot(lhs, rhs)
            ms = jnp.mean(o * o, axis=-1, keepdims=True)
            on = o * lax.rsqrt(ms + RMS_EPS)
            r_scr[rows, sv] = on

    ret_companions = [half_chunk(C_MR, mr_scr, raw, 0), half_chunk(C_MR, mr_scr, raw, 1)]
    n = 0
    for c in range(nsub):
        inners, s_olds = retention_scores_and_state(c)
        if n < len(ret_companions):
            ret_companions[n]()
        n += 1
        retention_outputs(c, inners, s_olds)
    for run in ret_companions[n:]:
        run()
    sample_qmats, sample_logits = _sample_logits(sqkv_ref, ck_ref)
    for b in range(sqkv_ref.shape[0]):
        for h in range(RET_HEADS):
            _sample_state_update(sret_ref, st_ref, ns_ref, b, h)

    gate_c = mod_ref[0:1, 2 * D_MODEL:3 * D_MODEL]
    lng = lng_ref[...]
    lnb = lnb_ref[...]
    windows = [slice(i * WINDOW, (i + 1) * WINDOW) for i in range(nsub)]
    za_all = _dot((a_scr[...] * _silu(ga_scr[...])).astype(BF16), wpa_ref[...])
    zr_all = _dot((r_scr[...] * _silu(rg_scr[...])).astype(BF16), wpr_ref[...])
    z_all = _sigmoid(ma_scr[...]) * za_all + _sigmoid(mr_scr[...]) * zr_all
    u_all = _dot(z_all.astype(BF16), wout_ref[...])
    for rows in windows:
        t = ALPHA * x_ref[rows, :] + gate_c * u_all[rows, :]
        mu = jnp.mean(t, axis=-1, keepdims=True)
        d = t - mu
        var = jnp.mean(d * d, axis=-1, keepdims=True)
        y_ref[rows, :] = d * lax.rsqrt(var + LN_EPS) * lng + lnb

    _sample_outputs(sinks_ref, layer, sqkv_ref, sret_ref, ck_ref, cv_ref, sample_qmats, sample_logits,
                    oatt_ref, oret_ref, nk_ref, nv_ref, ns_ref)

    @pl.when(step == pl.num_programs(0) - 1)
    def _():
        snew_ref[...] = s_scr[...]


def _fused_layer(layer, x, mod, mod_row_block, base, rtab, weights_b, next_weights,
                 ln_g, ln_b, sinks, ret_tabs, sample_qkv, sample_ret, cache_k, cache_v, state, shared_out):
    w_in_b, w_pa_b, w_pr_b, w_out_b = weights_b
    seq = x.shape[0]
    tm = PROMPT_ROWS
    steps = seq // tm
    nbatch = sample_qkv.shape[0]
    nb = nbatch // steps
    win = cache_k.shape[3]
    assert win == WINDOW == LANES
    dmask, qdec, kdec = ret_tabs
    smem = pl.BlockSpec(memory_space=pltpu.SMEM)
    row_spec = lambda w: pl.BlockSpec((tm, w), lambda i: (i, 0))
    seq_spec = lambda w: pl.BlockSpec((nb, 1, w), lambda i: (i, 0, 0))
    cache_spec = pl.BlockSpec((None, nb, ATT_KV_WIDTH, win), lambda i: (layer, i, 0, 0))
    state_spec = pl.BlockSpec((None, nb, RET_HEADS, RET_DK, RET_DV), lambda i: (layer, i, 0, 0, 0))
    in_specs = [
        smem,
        row_spec(D_MODEL),
        pl.BlockSpec((None, SUBLANES, 3 * D_MODEL), lambda i: (layer, mod_row_block, 0),
                     pipeline_mode=pl.Buffered(1)),
        pl.BlockSpec((None, 4, LANES), lambda i: (i, 0, 0)),
        _const_spec((8, tm, LANES)),
        _const_spec((D_MODEL, IN_COLS)),
        _const_spec((ATT_WIDTH, D_MODEL)),
        _const_spec((RET_WIDTH, D_MODEL)),
        _const_spec((D_MODEL, D_MODEL)),
        _layer_spec((1, D_MODEL), layer), _layer_spec((1, D_MODEL), layer),
        _const_spec((RET_HEADS, RET_CHUNK, RET_CHUNK)),
        _const_spec((RET_HEADS, RET_CHUNK, RET_DK)),
        _const_spec((RET_HEADS, RET_CHUNK, RET_DK)),
        seq_spec(C_AG), seq_spec(2 * RET_QK_WIDTH + RET_WIDTH),
        cache_spec, cache_spec, state_spec,
    ]
    args = [sinks, x, mod, base, rtab, w_in_b, w_pa_b, w_pr_b, w_out_b, ln_g, ln_b, dmask, qdec, kdec,
            sample_qkv, sample_ret, cache_k, cache_v, state]
    assert len(args) == N_PROMPT_INPUTS
    cast_shapes, cast_specs = [], []
    if next_weights is not None:
        assert len(next_weights) == N_WEIGHTS
        for w in next_weights:
            _, rows, cols = w.shape
            slab = max(BF16_SUBLANES, rows // steps)
            nslab = rows // slab
            assert rows % slab == 0 and nslab <= steps
            args.append(w)
            in_specs.append(pl.BlockSpec((None, slab, cols),
                                         lambda i, n=nslab: (layer + 1, jnp.minimum(i, n - 1), 0)))
            cast_shapes.append(jax.ShapeDtypeStruct((rows, cols), BF16))
            cast_specs.append(pl.BlockSpec((slab, cols), lambda i, n=nslab: (jnp.minimum(i, n - 1), 0)))
    aliases = {}
    if shared_out is not None:
        for j, arr in enumerate(shared_out):
            aliases[len(args)] = N_PROMPT_OUTPUTS - len(shared_out) + j
            args.append(arr)
            in_specs.append(pl.BlockSpec(memory_space=pl.ANY))
    return pl.pallas_call(
        functools.partial(_prompt_kernel, layer=layer, n_aliased=len(aliases), n_cast=len(cast_shapes)),
        out_shape=(
            jax.ShapeDtypeStruct((seq, D_MODEL), F32),
            jax.ShapeDtypeStruct((WINDOW, ATT_KV_WIDTH), F32),
            jax.ShapeDtypeStruct((WINDOW, ATT_KV_WIDTH), F32),
            jax.ShapeDtypeStruct((RET_HEADS, RET_DK, RET_DV), F32),
            jax.ShapeDtypeStruct((nbatch, 1, ATT_WIDTH), F32),
            jax.ShapeDtypeStruct((nbatch, 1, RET_WIDTH), F32),
            jax.ShapeDtypeStruct(cache_k.shape, F32),
            jax.ShapeDtypeStruct(cache_v.shape, F32),
            jax.ShapeDtypeStruct(state.shape, F32),
        ) + tuple(cast_shapes),
        grid=(steps,),
        in_specs=in_specs,
        out_specs=(
            row_spec(D_MODEL),
            pl.BlockSpec((WINDOW, ATT_KV_WIDTH), lambda i: (0, 0)),
            pl.BlockSpec((WINDOW, ATT_KV_WIDTH), lambda i: (0, 0)),
            pl.BlockSpec((RET_HEADS, RET_DK, RET_DV), lambda i: (0, 0, 0)),
            seq_spec(ATT_WIDTH), seq_spec(RET_WIDTH),
            cache_spec, cache_spec, state_spec,
        ) + tuple(cast_specs),
        input_output_aliases=aliases,
        scratch_shapes=[
            pltpu.VMEM((4, tm, LANES), F32),
            pltpu.VMEM((tm, D_MODEL), BF16),
            pltpu.VMEM((tm, ATT_WIDTH), BF16),
            pltpu.VMEM((4, WINDOW + tm, LANES), BF16),
            pltpu.VMEM((4, WINDOW + tm, LANES), BF16),
            pltpu.VMEM((tm, ATT_WIDTH), F32),
            pltpu.VMEM((tm, ATT_WIDTH), F32),
            pltpu.VMEM((tm, RET_QK_WIDTH), F32),
            pltpu.VMEM((tm, RET_QK_WIDTH), F32),
            pltpu.VMEM((tm, RET_WIDTH), BF16),
            pltpu.VMEM((tm, RET_WIDTH), F32),
            pltpu.VMEM((tm, RET_WIDTH), F32),
            pltpu.VMEM((RET_HEADS, RET_DK, RET_DV), F32),
            pltpu.VMEM((tm, D_MODEL), F32),
            pltpu.VMEM((tm, D_MODEL), F32),
        ],
        compiler_params=pltpu.CompilerParams(
            dimension_semantics=("arbitrary",), vmem_limit_bytes=VMEM_LIMIT_BYTES),
        name="fused_layer",
    )(*args)


def _sample_proj_kernel(x_ref, mod_ref, tab_ref, win_ref,
                        qkv_ref, ga_ref, ret_ref, rg_ref, mg_ref):
    rows = x_ref.shape[0]
    x = x_ref[:, 0, :]
    shift = mod_ref[:, 0:D_MODEL]
    scale = mod_ref[:, D_MODEL:2 * D_MODEL]
    hb = (x * (1.0 + scale) + shift).astype(BF16)

    def proj(c0, c1):
        return _dot(hb, win_ref[:, c0:c1])

    lane = _lane_iota((rows, LANES))
    first_half32 = (lane & 32) == 0
    ca = tab_ref[0:1, :]
    sa = tab_ref[1:2, :]
    cr = tab_ref[2:3, :]
    sr = tab_ref[3:4, :]

    qkv = proj(C_AQ, C_AG)
    for t in range(ATT_WIDTH // LANES):
        qt = _rope_attn_tile(qkv[:, t * LANES:(t + 1) * LANES], ca, sa, first_half32)
        qkv_ref[:, 0, t * LANES:(t + 1) * LANES] = qt * ATT_SCALE
    qkv_ref[:, 0, C_AK:C_AV] = _rope_attn_tile(qkv[:, C_AK:C_AV], ca, sa, first_half32)
    qkv_ref[:, 0, C_AV:C_AG] = qkv[:, C_AV:C_AG]
    ga_ref[...] = _silu(proj(C_AG, C_RQ))

    rqk = proj(C_RQ, C_RV)
    for h in range(RET_HEADS):
        sl = slice(h * RET_DK, (h + 1) * RET_DK)
        ret_ref[:, 0, sl] = _rope_ret_tile(rqk[:, sl], cr, sr)
        sk = slice(RET_QK_WIDTH + h * RET_DK, RET_QK_WIDTH + (h + 1) * RET_DK)
        ret_ref[:, 0, sk] = _rope_ret_tile(rqk[:, sk], cr, sr) * RET_K_SCALE
    ret_ref[:, 0, 2 * RET_QK_WIDTH:] = proj(C_RV, C_RG)
    rg_ref[...] = _silu(proj(C_RG, C_MA))
    mg_ref[...] = _sigmoid(proj(C_MA, IN_COLS))


def _sample_proj(layer, x, mod, tab, w_in_b):
    rows = x.shape[0]
    shapes = ((rows, 1, C_AG), (rows, ATT_WIDTH), (rows, 1, 2 * RET_QK_WIDTH + RET_WIDTH),
              (rows, RET_WIDTH), (rows, 2 * D_MODEL))
    return pl.pallas_call(
        _sample_proj_kernel,
        out_shape=tuple(jax.ShapeDtypeStruct(s, F32) for s in shapes),
        grid=(1,),
        in_specs=[
            _const_spec((rows, 1, D_MODEL)),
            _layer_spec((rows, 3 * D_MODEL), layer),
            _const_spec((4, LANES)),
            _const_spec((D_MODEL, IN_COLS)),
        ],
        out_specs=tuple(pl.BlockSpec(s, lambda i, nd=len(s): (0,) * nd) for s in shapes),
        compiler_params=pltpu.CompilerParams(
            dimension_semantics=("arbitrary",), vmem_limit_bytes=VMEM_LIMIT_BYTES),
        name="sample_proj",
    )(x, mod, tab, w_in_b)


def _sample_logits(qkv_ref, ck_ref):
    nb = qkv_ref.shape[0]
    rowi = lax.broadcasted_iota(jnp.int32, (ATT_HEADS, LANES), 0)
    lanei = lax.broadcasted_iota(jnp.int32, (ATT_HEADS, LANES), 1)
    lane_group = lanei // ATT_HEAD_DIM
    qmats, logits = [], []
    for b in range(nb):
        qkv = qkv_ref[b]
        qmat = jnp.zeros((ATT_HEADS, LANES), F32)
        for h in range(ATT_HEADS):
            t, p, g = h // 2, h % 2, h // ATT_GROUP
            tile = qkv[:, t * LANES:(t + 1) * LANES]
            src = tile if p == g else pltpu.roll(tile, 64, 1)
            qmat = jnp.where((rowi == h) & (lane_group == g), jnp.broadcast_to(src, (ATT_HEADS, LANES)), qmat)
        qmats.append(qmat)
        logits.append(_dot(qmat.astype(BF16), ck_ref[b].astype(BF16)))
    return qmats, logits


def _sample_state_update(ret_ref, st_ref, ns_ref, b, h):
    dr = lax.broadcasted_iota(jnp.int32, (RET_DK, RET_DK), 0)
    dc = lax.broadcasted_iota(jnp.int32, (RET_DK, RET_DK), 1)
    ret = ret_ref[b]
    kh = ret[:, RET_QK_WIDTH + h * RET_DK:RET_QK_WIDTH + (h + 1) * RET_DK]
    vh = ret[:, 2 * RET_QK_WIDTH + h * RET_DV:2 * RET_QK_WIDTH + (h + 1) * RET_DV]
    k_col = jnp.sum(jnp.where(dr == dc, jnp.broadcast_to(kh, (RET_DK, RET_DK)), 0.0),
                    axis=1, keepdims=True)
    ns_ref[b, h] = st_ref[b, h] * TOKEN_DECAY[h] + k_col * vh


def _sample_outputs(sinks_ref, layer, qkv_ref, ret_ref, ck_ref, cv_ref, qmats, logits,
                    oatt_ref, oret_ref, nk_ref, nv_ref, ns_ref):
    nb = qkv_ref.shape[0]
    sink_col = jnp.zeros((ATT_HEADS, 1), F32)
    rowc = lax.broadcasted_iota(jnp.int32, (ATT_HEADS, 1), 0)
    for h in range(ATT_HEADS):
        sink_col = jnp.where(rowc == h, sinks_ref[layer, h], sink_col)
    lo64_row = _lane_iota((1, LANES)) < 64
    dr = lax.broadcasted_iota(jnp.int32, (LANES, LANES), 0)
    dc = lax.broadcasted_iota(jnp.int32, (LANES, LANES), 1)
    diag = dr == dc
    last_lane = dc == WINDOW - 1

    for b in range(nb):
        qkv = qkv_ref[b]
        k_new = qkv[:, C_AK:C_AV]
        v_new = qkv[:, C_AV:C_AG]
        s_c = logits[b]
        s_self = jnp.sum(qmats[b] * k_new, axis=-1, keepdims=True)
        m = jnp.maximum(jnp.maximum(jnp.max(s_c, axis=-1, keepdims=True), s_self), sink_col)
        p_c = jnp.exp(s_c - m)
        p_self = jnp.exp(s_self - m)
        denom = jnp.sum(p_c, axis=-1, keepdims=True) + p_self + jnp.exp(sink_col - m)
        o = (lax.dot_general(p_c.astype(BF16), cv_ref[b].astype(BF16), NT, preferred_element_type=F32)
             + p_self * v_new) / denom
        o_sw = pltpu.roll(o, 64, 1)
        for t in range(ATT_WIDTH // LANES):
            g = t // 2
            first = (o if g == 0 else o_sw)[2 * t:2 * t + 1, :]
            second = (o_sw if g == 0 else o)[2 * t + 1:2 * t + 2, :]
            oatt_ref[b, :, t * LANES:(t + 1) * LANES] = jnp.where(lo64_row, first, second)

        for new_row, src_ref, dst_ref in ((k_new, ck_ref, nk_ref), (v_new, cv_ref, nv_ref)):
            new_col = jnp.sum(jnp.where(diag, jnp.broadcast_to(new_row, (LANES, LANES)), 0.0),
                              axis=1, keepdims=True)
            shifted = pltpu.roll(src_ref[b], WINDOW - 1, 1)
            dst_ref[b] = jnp.where(last_lane, new_col, shifted)

    for b in range(nb):
        ret = ret_ref[b]
        for h in range(RET_HEADS):
            qh = ret[:, h * RET_DK:(h + 1) * RET_DK]
            q8 = jnp.broadcast_to(qh, (SUBLANES, RET_DK)).astype(BF16)
            oh = _dot(q8, ns_ref[b, h].astype(BF16))[0:1, :]
            ms = jnp.mean(oh * oh, axis=-1, keepdims=True)
            oret_ref[b, :, h * RET_DV:(h + 1) * RET_DV] = oh * lax.rsqrt(ms + RMS_EPS)


def _sample_out_kernel(x_ref, mod_ref, oatt_ref, ga_ref, oret_ref, rg_ref, mg_ref,
                       wpa_ref, wpr_ref, wout_ref, lng_ref, lnb_ref, y_ref):
    gate_c = mod_ref[:, 2 * D_MODEL:3 * D_MODEL]
    za = _dot((oatt_ref[:, 0, :] * ga_ref[...]).astype(BF16), wpa_ref[...])
    zr = _dot((oret_ref[:, 0, :] * rg_ref[...]).astype(BF16), wpr_ref[...])
    z = mg_ref[:, 0:D_MODEL] * za + mg_ref[:, D_MODEL:2 * D_MODEL] * zr
    u = _dot(z.astype(BF16), wout_ref[...])
    t = ALPHA * x_ref[:, 0, :] + gate_c * u
    mu = jnp.mean(t, axis=-1, keepdims=True)
    d = t - mu
    var = jnp.mean(d * d, axis=-1, keepdims=True)
    y_ref[:, 0, :] = d * lax.rsqrt(var + LN_EPS) * lng_ref[...] + lnb_ref[...]


def _sample_out(layer, x, mod, oatt, ga, oret, rg, mg, w_pa_b, w_pr_b, w_out_b, ln_g, ln_b):
    rows = x.shape[0]
    return pl.pallas_call(
        _sample_out_kernel,
        out_shape=jax.ShapeDtypeStruct(x.shape, F32),
        grid=(1,),
        in_specs=[
            _const_spec((rows, 1, D_MODEL)),
            _layer_spec((rows, 3 * D_MODEL), layer),
            _const_spec((rows, 1, ATT_WIDTH)), _const_spec((rows, ATT_WIDTH)),
            _const_spec((rows, 1, RET_WIDTH)), _const_spec((rows, RET_WIDTH)),
            _const_spec((rows, 2 * D_MODEL)),
            _const_spec((ATT_WIDTH, D_MODEL)),
            _const_spec((RET_WIDTH, D_MODEL)),
            _const_spec((D_MODEL, D_MODEL)),
            _layer_spec((1, D_MODEL), layer), _layer_spec((1, D_MODEL), layer),
        ],
        out_specs=pl.BlockSpec((rows, 1, D_MODEL), lambda i: (0, 0, 0)),
        compiler_params=pltpu.CompilerParams(
            dimension_semantics=("arbitrary",), vmem_limit_bytes=VMEM_LIMIT_BYTES),
        name="sample_out",
    )(x, mod, oatt, ga, oret, rg, mg, w_pa_b, w_pr_b, w_out_b, ln_g, ln_b)


def kernel(x_prompt, x_sample, c_prompt, c_sample, cache_k, cache_v, state_ret, w_in, attn_sinks,
           w_cond, b_cond, w_proj_attn, w_proj_ret, w_out, ln_g, ln_b):
    seq = x_prompt.shape[1]
    nbatch = x_sample.shape[0]
    win = cache_k.shape[2]
    assert seq % PROMPT_ROWS == 0 and nbatch % (seq // PROMPT_ROWS) == 0 and nbatch % SUBLANES == 0

    weights_f32 = (w_in, w_proj_attn, w_proj_ret, w_out)
    weights_b = tuple(w[0].astype(BF16) for w in weights_f32)
    ln_g3 = ln_g.reshape(DEPTH, 1, D_MODEL)
    ln_b3 = ln_b.reshape(DEPTH, 1, D_MODEL)

    c_all = jnp.concatenate([c_sample, c_prompt, jnp.zeros((SUBLANES - 1, D_MODEL), F32)], axis=0)
    mod = _cond_call(c_all, w_cond, b_cond)
    prompt_mod_block = nbatch // SUBLANES

    base, rtab = _prompt_rope_tables(seq, PROMPT_ROWS)
    stab = _sample_rope_table(float(PAST_LEN))
    ret_tabs = _retention_tables()

    def feature_major(c):
        return c.transpose(0, 1, 3, 4, 2).reshape(DEPTH, nbatch, ATT_KV_WIDTH, win)

    def window_major(c):
        return c.reshape(DEPTH, nbatch, ATT_KV_HEADS, ATT_HEAD_DIM, win).transpose(0, 1, 4, 2, 3)

    ck = feature_major(cache_k)
    cv = feature_major(cache_v)

    yp = x_prompt[0]
    ys = x_sample
    kp, vp, sp = [], [], []
    prev = None
    for l in range(DEPTH):
        w_in_b, w_pa_b, w_pr_b, w_out_b = weights_b
        qkv, ga, ret, rg, mg = _sample_proj(l, ys, mod, stab, w_in_b)
        outs = _fused_layer(
            l, yp, mod, prompt_mod_block, base, rtab, weights_b,
            weights_f32 if l + 1 < DEPTH else None,
            ln_g3, ln_b3, attn_sinks, ret_tabs, qkv, ret, ck, cv, state_ret, prev)
        yp, k_new, v_new, s_new, oatt, oret, nk, nv, ns = outs[:N_PROMPT_OUTPUTS]
        kp.append(k_new.reshape(1, WINDOW, ATT_KV_HEADS, ATT_HEAD_DIM))
        vp.append(v_new.reshape(1, WINDOW, ATT_KV_HEADS, ATT_HEAD_DIM))
        sp.append(s_new[None])
        prev = (nk, nv, ns)
        ys = _sample_out(l, ys, mod, oatt, ga, oret, rg, mg, w_pa_b, w_pr_b, w_out_b, ln_g3, ln_b3)
        weights_b = tuple(outs[N_PROMPT_OUTPUTS:])

    nk, nv, ns = prev
    return (yp[None], ys, jnp.stack(kp), jnp.stack(vp), jnp.stack(sp),
            window_major(nk), window_major(nv), ns)
```
